```python
import math
import jax, jax.numpy as jnp
from jax import lax
import numpy as np


D_MODEL = 2048
BATCH = 8
SEQ = 4096
DEPTH = 1

N_META = 16
POOL_WINDOWS = (2, 4, 8, 16)
POOL_GROUP = 256
POOL_WIDTH = POOL_GROUP * len(POOL_WINDOWS)
MLA_HEADS = 16
Q_LORA = 512
KV_LORA = 512
QK_NOPE = 128
QK_ROPE = 64
V_DIM = 128
QK_DIM = QK_NOPE + QK_ROPE
MLA_WIDTH = MLA_HEADS * V_DIM
ROPE_THETA = 10000.0
SOFTMAX_SCALE = QK_DIM ** -0.5
D_FF = 5632
Q_BLOCK = 128
EPS = 1e-6
SPLITS = (POOL_WIDTH,
          POOL_WIDTH + Q_LORA,
          POOL_WIDTH + Q_LORA + KV_LORA,
          POOL_WIDTH + Q_LORA + KV_LORA + QK_ROPE,
          POOL_WIDTH + Q_LORA + KV_LORA + QK_ROPE + D_MODEL)
IN_COLS = POOL_WIDTH + Q_LORA + KV_LORA + QK_ROPE + 2 * D_MODEL

kernel_name = 'hybrid_pool_mla_macaron_block'


def _rmsnorm(x, gain):
    x32 = x.astype(jnp.float32)
    y = x32 * lax.rsqrt(jnp.mean(x32 * x32, axis=-1, keepdims=True) + EPS)
    return (y * gain.astype(jnp.float32)).astype(x.dtype)


def _swiglu(h, w_gu, w_down):
    g, u = jnp.split(h @ w_gu, 2, axis=-1)
    return (jax.nn.silu(g) * u) @ w_down


def _rope_tables(L, dtype):
    pos = jnp.arange(L, dtype=jnp.float32)
    inv = ROPE_THETA ** (-jnp.arange(0, QK_ROPE, 2, dtype=jnp.float32) / QK_ROPE)
    ang = pos[:, None] * inv[None, :]
    ang = jnp.concatenate([ang, ang], axis=-1)
    return jnp.cos(ang).astype(dtype), jnp.sin(ang).astype(dtype)


def _rotate(x, cos, sin):
    x1, x2 = jnp.split(x, 2, axis=-1)
    return x * cos + jnp.concatenate([-x2, x1], axis=-1) * sin


def _multiscale_pool(u, pool_w, pool_scale):
    B, L, _ = u.shape
    u32 = u.astype(jnp.float32)
    cs = jnp.concatenate([jnp.zeros_like(u32[:, :1]), jnp.cumsum(u32, axis=1)], axis=1)
    hi = jnp.arange(1, L + 1)
    outs = []
    for g, w in enumerate(POOL_WINDOWS):
        csg = cs[..., g * POOL_GROUP:(g + 1) * POOL_GROUP]
        lo = jnp.maximum(hi - w, 0)
        cnt = (hi - lo).astype(jnp.float32)[None, :, None]
        mean = (csg[:, hi] - csg[:, lo]) / cnt
        outs.append(mean - u32[..., g * POOL_GROUP:(g + 1) * POOL_GROUP])
    d = jnp.stack(outs, axis=2).astype(u.dtype)
    y = jnp.einsum('blgc,gcd->blgd', d, pool_w).reshape(B, L, POOL_WIDTH)
    return y * pool_scale


def _attend_block(q_blk, q_pos, k, v, k_pos):
    s = jnp.einsum('bqhd,bkhd->bhqk', q_blk, k, preferred_element_type=jnp.float32) * SOFTMAX_SCALE
    mask = k_pos[None, :] <= q_pos[:, None]
    s = jnp.where(mask[None, None], s, jnp.float32(-1e30))
    p = jax.nn.softmax(s, axis=-1).astype(v.dtype)
    return jnp.einsum('bhqk,bkhd->bqhd', p, v)


def _mla(c_q, c_kv, k_rope, q_a_norm, w_q_b, kv_a_norm, w_kv_b, cos, sin):
    B, L, _ = c_q.shape
    q = (_rmsnorm(c_q, q_a_norm) @ w_q_b).reshape(B, L, MLA_HEADS, QK_DIM)
    q_nope, q_pe = jnp.split(q, [QK_NOPE], axis=-1)
    q_pe = _rotate(q_pe, cos[:, None, :], sin[:, None, :])
    kv = (_rmsnorm(c_kv, kv_a_norm) @ w_kv_b).reshape(B, L, MLA_HEADS, QK_NOPE + V_DIM)
    k_nope, v = jnp.split(kv, [QK_NOPE], axis=-1)
    k_pe = _rotate(k_rope, cos, sin)
    q = jnp.concatenate([q_nope, q_pe], axis=-1)
    k = jnp.concatenate([k_nope, jnp.broadcast_to(k_pe[:, :, None, :], (B, L, MLA_HEADS, QK_ROPE))], axis=-1)
    pos = jnp.arange(L)
    o_meta = _attend_block(q[:, :N_META], pos[:N_META], k[:, :N_META], v[:, :N_META], pos[:N_META])
    n_blk = (L - N_META) // Q_BLOCK
    q_real = q[:, N_META:].reshape(B, n_blk, Q_BLOCK, MLA_HEADS, QK_DIM).transpose(1, 0, 2, 3, 4)
    pos_real = pos[N_META:].reshape(n_blk, Q_BLOCK)
    o_real = lax.map(lambda a: _attend_block(a[0], a[1], k, v, pos), (q_real, pos_real))
    o_real = o_real.transpose(1, 0, 2, 3, 4).reshape(B, L - N_META, MLA_HEADS, V_DIM)
    o = jnp.concatenate([o_meta, o_real], axis=1)
    return o.reshape(B, L, MLA_WIDTH)


def _hybrid_mixer(h, w_in, pool_w, pool_scale, w_pool_o, q_a_norm, w_q_b, kv_a_norm, w_kv_b,
                  w_mla_o, w_out, cos, sin):
    z = h @ w_in
    u_pool, c_q, c_kv, k_rope, g_pool, g_mla = jnp.split(z, SPLITS, axis=-1)
    y_pool = _multiscale_pool(u_pool, pool_w, pool_scale) @ w_pool_o
    y_mla = _mla(c_q, c_kv, k_rope, q_a_norm, w_q_b, kv_a_norm, w_kv_b, cos, sin) @ w_mla_o
    y = jax.nn.sigmoid(g_pool) * y_pool + jax.nn.sigmoid(g_mla) * y_mla
    return y @ w_out


def _fwd_setup_inputs(seed: int = 0) -> dict:
    key = jax.random.key(seed)
    ks = jax.random.split(key, 32)

    def dense(k, shape, fan_in):
        return jax.random.normal(k, shape, jnp.float32) * (fan_in ** -0.5)

    def gain(k, shape):
        return 1.0 + 0.1 * jax.random.normal(k, shape, jnp.float32)

    return {
        'x': jax.random.normal(ks[0], (BATCH, SEQ, D_MODEL), jnp.float32),
        'meta_tokens': jax.random.normal(ks[1], (N_META, D_MODEL), jnp.float32),
        'norm_ffn1_pre': gain(ks[2], (DEPTH, D_MODEL)),
        'norm_ffn1_post': gain(ks[3], (DEPTH, D_MODEL)),
        'ffn1_w_gu': dense(ks[4], (DEPTH, D_MODEL, 2 * D_FF), D_MODEL),
        'ffn1_w_down': dense(ks[5], (DEPTH, D_FF, D_MODEL), D_FF),
        'norm_mix_pre': gain(ks[6], (DEPTH, D_MODEL)),
        'norm_mix_post': gain(ks[7], (DEPTH, D_MODEL)),
        'w_in': dense(ks[8], (DEPTH, D_MODEL, IN_COLS), D_MODEL),
        'pool_w': dense(ks[9], (DEPTH, len(POOL_WINDOWS), POOL_GROUP, POOL_GROUP), POOL_GROUP),
        'pool_scale': gain(ks[10], (DEPTH, POOL_WIDTH)),
        'w_pool_o': dense(ks[11], (DEPTH, POOL_WIDTH, D_MODEL), POOL_WIDTH),
        'q_a_norm': gain(ks[12], (DEPTH, Q_LORA)),
        'w_q_b': dense(ks[13], (DEPTH, Q_LORA, MLA_HEADS * QK_DIM), Q_LORA),
        'kv_a_norm': gain(ks[14], (DEPTH, KV_LORA)),
        'w_kv_b': dense(ks[15], (DEPTH, KV_LORA, MLA_HEADS * (QK_NOPE + V_DIM)), KV_LORA),
        'w_mla_o': dense(ks[16], (DEPTH, MLA_WIDTH, D_MODEL), MLA_WIDTH),
        'w_out': dense(ks[17], (DEPTH, D_MODEL, D_MODEL), D_MODEL),
        'norm_ffn2_pre': gain(ks[18], (DEPTH, D_MODEL)),
        'norm_ffn2_post': gain(ks[19], (DEPTH, D_MODEL)),
        'ffn2_w_gu': dense(ks[20], (DEPTH, D_MODEL, 2 * D_FF), D_MODEL),
        'ffn2_w_down': dense(ks[21], (DEPTH, D_FF, D_MODEL), D_FF),
    }


def _fwd_reference(x, meta_tokens, norm_ffn1_pre, norm_ffn1_post, ffn1_w_gu, ffn1_w_down,
              norm_mix_pre, norm_mix_post, w_in, pool_w, pool_scale, w_pool_o,
              q_a_norm, w_q_b, kv_a_norm, w_kv_b, w_mla_o, w_out,
              norm_ffn2_pre, norm_ffn2_post, ffn2_w_gu, ffn2_w_down):
    B = x.shape[0]
    meta = jnp.broadcast_to(meta_tokens.astype(x.dtype)[None], (B, N_META, D_MODEL))
    h = jnp.concatenate([meta, x], axis=1)
    L = h.shape[1]
    cos, sin = _rope_tables(L, h.dtype)
    for i in range(DEPTH):
        h = h + 0.5 * _rmsnorm(_swiglu(_rmsnorm(h, norm_ffn1_pre[i]), ffn1_w_gu[i], ffn1_w_down[i]),
                               norm_ffn1_post[i])
        m = _hybrid_mixer(_rmsnorm(h, norm_mix_pre[i]), w_in[i], pool_w[i], pool_scale[i], w_pool_o[i],
                          q_a_norm[i], w_q_b[i], kv_a_norm[i], w_kv_b[i], w_mla_o[i], w_out[i], cos, sin)
        h = h + _rmsnorm(m, norm_mix_post[i])
        h = h + 0.5 * _rmsnorm(_swiglu(_rmsnorm(h, norm_ffn2_pre[i]), ffn2_w_gu[i], ffn2_w_down[i]),
                               norm_ffn2_post[i])
    return h[:, N_META:]


import jax as _jax
import jax.numpy as _jnp

TWIN_FORMAT = 'train_step'
FWD_PARAMS = ['x', 'meta_tokens', 'norm_ffn1_pre', 'norm_ffn1_post', 'ffn1_w_gu', 'ffn1_w_down', 'norm_mix_pre', 'norm_mix_post', 'w_in', 'pool_w', 'pool_scale', 'w_pool_o', 'q_a_norm', 'w_q_b', 'kv_a_norm', 'w_kv_b', 'w_mla_o', 'w_out', 'norm_ffn2_pre', 'norm_ffn2_post', 'ffn2_w_gu', 'ffn2_w_down']
TWIN_WEIGHTS = ['meta_tokens', 'norm_ffn1_pre', 'norm_ffn1_post', 'ffn1_w_gu', 'ffn1_w_down', 'norm_mix_pre', 'norm_mix_post', 'w_in', 'pool_w', 'pool_scale', 'w_pool_o', 'q_a_norm', 'w_q_b', 'kv_a_norm', 'w_kv_b', 'w_mla_o', 'w_out', 'norm_ffn2_pre', 'norm_ffn2_post', 'ffn2_w_gu', 'ffn2_w_down']
TWIN_DIFF_INPUT = 'x'
TWIN_INPUTS = ['x', 'meta_tokens', 'norm_ffn1_pre', 'norm_ffn1_post', 'ffn1_w_gu', 'ffn1_w_down', 'norm_mix_pre', 'norm_mix_post', 'w_in', 'pool_w', 'pool_scale', 'w_pool_o', 'q_a_norm', 'w_q_b', 'kv_a_norm', 'w_kv_b', 'w_mla_o', 'w_out', 'norm_ffn2_pre', 'norm_ffn2_post', 'ffn2_w_gu', 'ffn2_w_down', 'loss_target', 'm_meta_tokens', 'm_norm_ffn1_pre', 'm_norm_ffn1_post', 'm_ffn1_w_gu', 'm_ffn1_w_down', 'm_norm_mix_pre', 'm_norm_mix_post', 'm_w_in', 'm_pool_w', 'm_pool_scale', 'm_w_pool_o', 'm_q_a_norm', 'm_w_q_b', 'm_kv_a_norm', 'm_w_kv_b', 'm_w_mla_o', 'm_w_out', 'm_norm_ffn2_pre', 'm_norm_ffn2_post', 'm_ffn2_w_gu', 'm_ffn2_w_down', 'v_meta_tokens', 'v_norm_ffn1_pre', 'v_norm_ffn1_post', 'v_ffn1_w_gu', 'v_ffn1_w_down', 'v_norm_mix_pre', 'v_norm_mix_post', 'v_w_in', 'v_pool_w', 'v_pool_scale', 'v_w_pool_o', 'v_q_a_norm', 'v_w_q_b', 'v_kv_a_norm', 'v_w_kv_b', 'v_w_mla_o', 'v_w_out', 'v_norm_ffn2_pre', 'v_norm_ffn2_post', 'v_ffn2_w_gu', 'v_ffn2_w_down']
TWIN_OUTPUTS = ['loss', 'grad_x', 'grad_meta_tokens', 'grad_norm_ffn1_pre', 'grad_norm_ffn1_post', 'grad_ffn1_w_gu', 'grad_ffn1_w_down', 'grad_norm_mix_pre', 'grad_norm_mix_post', 'grad_w_in', 'grad_pool_w', 'grad_pool_scale', 'grad_w_pool_o', 'grad_q_a_norm', 'grad_w_q_b', 'grad_kv_a_norm', 'grad_w_kv_b', 'grad_w_mla_o', 'grad_w_out', 'grad_norm_ffn2_pre', 'grad_norm_ffn2_post', 'grad_ffn2_w_gu', 'grad_ffn2_w_down', 'delta_meta_tokens', 'delta_norm_ffn1_pre', 'delta_norm_ffn1_post', 'delta_ffn1_w_gu', 'delta_ffn1_w_down', 'delta_norm_mix_pre', 'delta_norm_mix_post', 'delta_w_in', 'delta_pool_w', 'delta_pool_scale', 'delta_w_pool_o', 'delta_q_a_norm', 'delta_w_q_b', 'delta_kv_a_norm', 'delta_w_kv_b', 'delta_w_mla_o', 'delta_w_out', 'delta_norm_ffn2_pre', 'delta_norm_ffn2_post', 'delta_ffn2_w_gu', 'delta_ffn2_w_down', 'new_m_meta_tokens', 'new_m_norm_ffn1_pre', 'new_m_norm_ffn1_post', 'new_m_ffn1_w_gu', 'new_m_ffn1_w_down', 'new_m_norm_mix_pre', 'new_m_norm_mix_post', 'new_m_w_in', 'new_m_pool_w', 'new_m_pool_scale', 'new_m_w_pool_o', 'new_m_q_a_norm', 'new_m_w_q_b', 'new_m_kv_a_norm', 'new_m_w_kv_b', 'new_m_w_mla_o', 'new_m_w_out', 'new_m_norm_ffn2_pre', 'new_m_norm_ffn2_post', 'new_m_ffn2_w_gu', 'new_m_ffn2_w_down', 'new_v_meta_tokens', 'new_v_norm_ffn1_pre', 'new_v_norm_ffn1_post', 'new_v_ffn1_w_gu', 'new_v_ffn1_w_down', 'new_v_norm_mix_pre', 'new_v_norm_mix_post', 'new_v_w_in', 'new_v_pool_w', 'new_v_pool_scale', 'new_v_w_pool_o', 'new_v_q_a_norm', 'new_v_w_q_b', 'new_v_kv_a_norm', 'new_v_w_kv_b', 'new_v_w_mla_o', 'new_v_w_out', 'new_v_norm_ffn2_pre', 'new_v_norm_ffn2_post', 'new_v_ffn2_w_gu', 'new_v_ffn2_w_down']
TWIN_LEAF_KINDS = {'loss': 'loss', 'grad_x': 'grad_x', 'grad_meta_tokens': 'grad_w', 'grad_norm_ffn1_pre': 'grad_w', 'grad_norm_ffn1_post': 'grad_w', 'grad_ffn1_w_gu': 'grad_w', 'grad_ffn1_w_down': 'grad_w', 'grad_norm_mix_pre': 'grad_w', 'grad_norm_mix_post': 'grad_w', 'grad_w_in': 'grad_w', 'grad_pool_w': 'grad_w', 'grad_pool_scale': 'grad_w', 'grad_w_pool_o': 'grad_w', 'grad_q_a_norm': 'grad_w', 'grad_w_q_b': 'grad_w', 'grad_kv_a_norm': 'grad_w', 'grad_w_kv_b': 'grad_w', 'grad_w_mla_o': 'grad_w', 'grad_w_out': 'grad_w', 'grad_norm_ffn2_pre': 'grad_w', 'grad_norm_ffn2_post': 'grad_w', 'grad_ffn2_w_gu': 'grad_w', 'grad_ffn2_w_down': 'grad_w', 'delta_meta_tokens': 'delta_w', 'delta_norm_ffn1_pre': 'delta_w', 'delta_norm_ffn1_post': 'delta_w', 'delta_ffn1_w_gu': 'delta_w', 'delta_ffn1_w_down': 'delta_w', 'delta_norm_mix_pre': 'delta_w', 'delta_norm_mix_post': 'delta_w', 'delta_w_in': 'delta_w', 'delta_pool_w': 'delta_w', 'delta_pool_scale': 'delta_w', 'delta_w_pool_o': 'delta_w', 'delta_q_a_norm': 'delta_w', 'delta_w_q_b': 'delta_w', 'delta_kv_a_norm': 'delta_w', 'delta_w_kv_b': 'delta_w', 'delta_w_mla_o': 'delta_w', 'delta_w_out': 'delta_w', 'delta_norm_ffn2_pre': 'delta_w', 'delta_norm_ffn2_post': 'delta_w', 'delta_ffn2_w_gu': 'delta_w', 'delta_ffn2_w_down': 'delta_w', 'new_m_meta_tokens': 'new_m', 'new_m_norm_ffn1_pre': 'new_m', 'new_m_norm_ffn1_post': 'new_m', 'new_m_ffn1_w_gu': 'new_m', 'new_m_ffn1_w_down': 'new_m', 'new_m_norm_mix_pre': 'new_m', 'new_m_norm_mix_post': 'new_m', 'new_m_w_in': 'new_m', 'new_m_pool_w': 'new_m', 'new_m_pool_scale': 'new_m', 'new_m_w_pool_o': 'new_m', 'new_m_q_a_norm': 'new_m', 'new_m_w_q_b': 'new_m', 'new_m_kv_a_norm': 'new_m', 'new_m_w_kv_b': 'new_m', 'new_m_w_mla_o': 'new_m', 'new_m_w_out': 'new_m', 'new_m_norm_ffn2_pre': 'new_m', 'new_m_norm_ffn2_post': 'new_m', 'new_m_ffn2_w_gu': 'new_m', 'new_m_ffn2_w_down': 'new_m', 'new_v_meta_tokens': 'new_v', 'new_v_norm_ffn1_pre': 'new_v', 'new_v_norm_ffn1_post': 'new_v', 'new_v_ffn1_w_gu': 'new_v', 'new_v_ffn1_w_down': 'new_v', 'new_v_norm_mix_pre': 'new_v', 'new_v_norm_mix_post': 'new_v', 'new_v_w_in': 'new_v', 'new_v_pool_w': 'new_v', 'new_v_pool_scale': 'new_v', 'new_v_w_pool_o': 'new_v', 'new_v_q_a_norm': 'new_v', 'new_v_w_q_b': 'new_v', 'new_v_kv_a_norm': 'new_v', 'new_v_w_kv_b': 'new_v', 'new_v_w_mla_o': 'new_v', 'new_v_w_out': 'new_v', 'new_v_norm_ffn2_pre': 'new_v', 'new_v_norm_ffn2_post': 'new_v', 'new_v_ffn2_w_gu': 'new_v', 'new_v_ffn2_w_down': 'new_v'}


def _forward(args):
    return _fwd_reference(*[args[k] for k in FWD_PARAMS])


def _output_shape():
    def fwd():
        inp = _fwd_setup_inputs(0)
        return _fwd_reference(*[inp[k] for k in FWD_PARAMS])
    out = _jax.eval_shape(fwd)
    return out.shape, out.dtype

N_MICROBATCH = 1
ADAM_LR = 0.001
ADAM_B1 = 0.9
ADAM_B2 = 0.999
ADAM_EPS = 1e-08
ADAM_WD = 0.01
ADAM_STEP = 10
PER_EXAMPLE_BATCH_AXIS = {'x': 0, 'loss_target': 0}
SHARED_INPUTS = []
_WEIGHT_DTYPES = {'meta_tokens': _jnp.float32, 'norm_ffn1_pre': _jnp.float32, 'norm_ffn1_post': _jnp.float32, 'ffn1_w_gu': _jnp.float32, 'ffn1_w_down': _jnp.float32, 'norm_mix_pre': _jnp.float32, 'norm_mix_post': _jnp.float32, 'w_in': _jnp.float32, 'pool_w': _jnp.float32, 'pool_scale': _jnp.float32, 'w_pool_o': _jnp.float32, 'q_a_norm': _jnp.float32, 'w_q_b': _jnp.float32, 'kv_a_norm': _jnp.float32, 'w_kv_b': _jnp.float32, 'w_mla_o': _jnp.float32, 'w_out': _jnp.float32, 'norm_ffn2_pre': _jnp.float32, 'norm_ffn2_post': _jnp.float32, 'ffn2_w_gu': _jnp.float32, 'ffn2_w_down': _jnp.float32}
MOMENT_SCALE = {'meta_tokens': 6.893040e-03, 'norm_ffn1_pre': 1.841782e-01, 'norm_ffn1_post': 3.989907e+00, 'ffn1_w_gu': 7.698513e-02, 'ffn1_w_down': 1.391898e-01, 'norm_mix_pre': 2.973112e-01, 'norm_mix_post': 1.610216e+01, 'w_in': 1.769822e-01, 'pool_w': 4.610129e-01, 'pool_scale': 4.773249e-01, 'w_pool_o': 3.374644e-01, 'q_a_norm': 6.290604e-02, 'w_q_b': 2.464564e-02, 'kv_a_norm': 8.580384e-02, 'w_kv_b': 2.925759e-02, 'w_mla_o': 3.276211e-02, 'w_out': 3.436916e-01, 'norm_ffn2_pre': 1.465885e-01, 'norm_ffn2_post': 4.020994e+00, 'ffn2_w_gu': 6.505383e-02, 'ffn2_w_down': 1.310647e-01}


def _to_microbatches(a, axis):
    t = _jnp.moveaxis(a, axis, 0)
    t = t.reshape((N_MICROBATCH, t.shape[0] // N_MICROBATCH) + t.shape[1:])
    return _jnp.moveaxis(t, 1, axis + 1)


def setup_inputs(seed: int = 0) -> dict:
    inp = _fwd_setup_inputs(seed)
    key = _jax.random.fold_in(_jax.random.key(seed), 7919)
    shape, _ = _output_shape()
    out = dict(inp)
    out["loss_target"] = _jax.random.normal(_jax.random.fold_in(key, 0), shape, _jnp.float32)
    for i, name in enumerate(TWIN_WEIGHTS):
        w = inp[name].astype(_jnp.float32)
        if MOMENT_SCALE is None:
            s = _jnp.sqrt(_jnp.mean(_jnp.square(w)) + 1e-30)
        else:
            s = MOMENT_SCALE[name]
        km, kv = _jax.random.split(_jax.random.fold_in(key, i + 1))
        out[name] = w
        out["m_" + name] = s * _jax.random.normal(km, w.shape, _jnp.float32)
        out["v_" + name] = (s * s) * _jax.random.uniform(kv, w.shape, _jnp.float32, 0.5, 1.5)
    if N_MICROBATCH > 1:
        for name, axis in PER_EXAMPLE_BATCH_AXIS.items():
            out[name] = _to_microbatches(out[name], axis)
    return {'x': out['x'], 'meta_tokens': out['meta_tokens'], 'norm_ffn1_pre': out['norm_ffn1_pre'], 'norm_ffn1_post': out['norm_ffn1_post'], 'ffn1_w_gu': out['ffn1_w_gu'], 'ffn1_w_down': out['ffn1_w_down'], 'norm_mix_pre': out['norm_mix_pre'], 'norm_mix_post': out['norm_mix_post'], 'w_in': out['w_in'], 'pool_w': out['pool_w'], 'pool_scale': out['pool_scale'], 'w_pool_o': out['w_pool_o'], 'q_a_norm': out['q_a_norm'], 'w_q_b': out['w_q_b'], 'kv_a_norm': out['kv_a_norm'], 'w_kv_b': out['w_kv_b'], 'w_mla_o': out['w_mla_o'], 'w_out': out['w_out'], 'norm_ffn2_pre': out['norm_ffn2_pre'], 'norm_ffn2_post': out['norm_ffn2_post'], 'ffn2_w_gu': out['ffn2_w_gu'], 'ffn2_w_down': out['ffn2_w_down'], 'loss_target': out['loss_target'], 'm_meta_tokens': out['m_meta_tokens'], 'm_norm_ffn1_pre': out['m_norm_ffn1_pre'], 'm_norm_ffn1_post': out['m_norm_ffn1_post'], 'm_ffn1_w_gu': out['m_ffn1_w_gu'], 'm_ffn1_w_down': out['m_ffn1_w_down'], 'm_norm_mix_pre': out['m_norm_mix_pre'], 'm_norm_mix_post': out['m_norm_mix_post'], 'm_w_in': out['m_w_in'], 'm_pool_w': out['m_pool_w'], 'm_pool_scale': out['m_pool_scale'], 'm_w_pool_o': out['m_w_pool_o'], 'm_q_a_norm': out['m_q_a_norm'], 'm_w_q_b': out['m_w_q_b'], 'm_kv_a_norm': out['m_kv_a_norm'], 'm_w_kv_b': out['m_w_kv_b'], 'm_w_mla_o': out['m_w_mla_o'], 'm_w_out': out['m_w_out'], 'm_norm_ffn2_pre': out['m_norm_ffn2_pre'], 'm_norm_ffn2_post': out['m_norm_ffn2_post'], 'm_ffn2_w_gu': out['m_ffn2_w_gu'], 'm_ffn2_w_down': out['m_ffn2_w_down'], 'v_meta_tokens': out['v_meta_tokens'], 'v_norm_ffn1_pre': out['v_norm_ffn1_pre'], 'v_norm_ffn1_post': out['v_norm_ffn1_post'], 'v_ffn1_w_gu': out['v_ffn1_w_gu'], 'v_ffn1_w_down': out['v_ffn1_w_down'], 'v_norm_mix_pre': out['v_norm_mix_pre'], 'v_norm_mix_post': out['v_norm_mix_post'], 'v_w_in': out['v_w_in'], 'v_pool_w': out['v_pool_w'], 'v_pool_scale': out['v_pool_scale'], 'v_w_pool_o': out['v_w_pool_o'], 'v_q_a_norm': out['v_q_a_norm'], 'v_w_q_b': out['v_w_q_b'], 'v_kv_a_norm': out['v_kv_a_norm'], 'v_w_kv_b': out['v_w_kv_b'], 'v_w_mla_o': out['v_w_mla_o'], 'v_w_out': out['v_w_out'], 'v_norm_ffn2_pre': out['v_norm_ffn2_pre'], 'v_norm_ffn2_post': out['v_norm_ffn2_post'], 'v_ffn2_w_gu': out['v_ffn2_w_gu'], 'v_ffn2_w_down': out['v_ffn2_w_down']}


def _loss(weights, diff, rest, loss_target):
    with _jax.named_scope("forward"):
        args = {**rest, TWIN_DIFF_INPUT: diff, **{k: w.astype(_WEIGHT_DTYPES[k]) for k, w in weights.items()}}
        y = _forward(args)
    with _jax.named_scope("loss_head"):
        err = _jnp.square(y.astype(_jnp.float32) - loss_target)
        return 0.5 * _jnp.sum(_jnp.mean(err, axis=-1)) if err.ndim else 0.5 * err


def _adamw(w, g, m, v):
    m = ADAM_B1 * m + (1.0 - ADAM_B1) * g
    v = ADAM_B2 * v + (1.0 - ADAM_B2) * _jnp.square(g)
    m_hat = m / (1.0 - ADAM_B1 ** ADAM_STEP)
    v_hat = v / (1.0 - ADAM_B2 ** ADAM_STEP)
    delta = -ADAM_LR * (m_hat / (_jnp.sqrt(v_hat) + ADAM_EPS) + ADAM_WD * w)
    return delta, m, v


def reference(x, meta_tokens, norm_ffn1_pre, norm_ffn1_post, ffn1_w_gu, ffn1_w_down, norm_mix_pre, norm_mix_post, w_in, pool_w, pool_scale, w_pool_o, q_a_norm, w_q_b, kv_a_norm, w_kv_b, w_mla_o, w_out, norm_ffn2_pre, norm_ffn2_post, ffn2_w_gu, ffn2_w_down, loss_target, m_meta_tokens, m_norm_ffn1_pre, m_norm_ffn1_post, m_ffn1_w_gu, m_ffn1_w_down, m_norm_mix_pre, m_norm_mix_post, m_w_in, m_pool_w, m_pool_scale, m_w_pool_o, m_q_a_norm, m_w_q_b, m_kv_a_norm, m_w_kv_b, m_w_mla_o, m_w_out, m_norm_ffn2_pre, m_norm_ffn2_post, m_ffn2_w_gu, m_ffn2_w_down, v_meta_tokens, v_norm_ffn1_pre, v_norm_ffn1_post, v_ffn1_w_gu, v_ffn1_w_down, v_norm_mix_pre, v_norm_mix_post, v_w_in, v_pool_w, v_pool_scale, v_w_pool_o, v_q_a_norm, v_w_q_b, v_kv_a_norm, v_w_kv_b, v_w_mla_o, v_w_out, v_norm_ffn2_pre, v_norm_ffn2_post, v_ffn2_w_gu, v_ffn2_w_down):
    given = dict(x=x, meta_tokens=meta_tokens, norm_ffn1_pre=norm_ffn1_pre, norm_ffn1_post=norm_ffn1_post, ffn1_w_gu=ffn1_w_gu, ffn1_w_down=ffn1_w_down, norm_mix_pre=norm_mix_pre, norm_mix_post=norm_mix_post, w_in=w_in, pool_w=pool_w, pool_scale=pool_scale, w_pool_o=w_pool_o, q_a_norm=q_a_norm, w_q_b=w_q_b, kv_a_norm=kv_a_norm, w_kv_b=w_kv_b, w_mla_o=w_mla_o, w_out=w_out, norm_ffn2_pre=norm_ffn2_pre, norm_ffn2_post=norm_ffn2_post, ffn2_w_gu=ffn2_w_gu, ffn2_w_down=ffn2_w_down, loss_target=loss_target, m_meta_tokens=m_meta_tokens, m_norm_ffn1_pre=m_norm_ffn1_pre, m_norm_ffn1_post=m_norm_ffn1_post, m_ffn1_w_gu=m_ffn1_w_gu, m_ffn1_w_down=m_ffn1_w_down, m_norm_mix_pre=m_norm_mix_pre, m_norm_mix_post=m_norm_mix_post, m_w_in=m_w_in, m_pool_w=m_pool_w, m_pool_scale=m_pool_scale, m_w_pool_o=m_w_pool_o, m_q_a_norm=m_q_a_norm, m_w_q_b=m_w_q_b, m_kv_a_norm=m_kv_a_norm, m_w_kv_b=m_w_kv_b, m_w_mla_o=m_w_mla_o, m_w_out=m_w_out, m_norm_ffn2_pre=m_norm_ffn2_pre, m_norm_ffn2_post=m_norm_ffn2_post, m_ffn2_w_gu=m_ffn2_w_gu, m_ffn2_w_down=m_ffn2_w_down, v_meta_tokens=v_meta_tokens, v_norm_ffn1_pre=v_norm_ffn1_pre, v_norm_ffn1_post=v_norm_ffn1_post, v_ffn1_w_gu=v_ffn1_w_gu, v_ffn1_w_down=v_ffn1_w_down, v_norm_mix_pre=v_norm_mix_pre, v_norm_mix_post=v_norm_mix_post, v_w_in=v_w_in, v_pool_w=v_pool_w, v_pool_scale=v_pool_scale, v_w_pool_o=v_w_pool_o, v_q_a_norm=v_q_a_norm, v_w_q_b=v_w_q_b, v_kv_a_norm=v_kv_a_norm, v_w_kv_b=v_w_kv_b, v_w_mla_o=v_w_mla_o, v_w_out=v_w_out, v_norm_ffn2_pre=v_norm_ffn2_pre, v_norm_ffn2_post=v_norm_ffn2_post, v_ffn2_w_gu=v_ffn2_w_gu, v_ffn2_w_down=v_ffn2_w_down)
    weights = {n: given[n] for n in TWIN_WEIGHTS}
    shared = {n: given[n] for n in SHARED_INPUTS}
    per_example = {n: given[n] for n in ['x']}
    grad_fn = _jax.value_and_grad(_loss, argnums=(0, 1))

    def one_microbatch(ex, loss_target):
        ex = dict(ex)
        diff = ex.pop(TWIN_DIFF_INPUT)
        return grad_fn(weights, diff, {**shared, **ex}, loss_target)

    if N_MICROBATCH == 1:
        loss, (grad_w, grad_x) = one_microbatch(per_example, given["loss_target"])
    else:
        def body(carry, xs):
            loss_sum, grad_sum = carry
            l_k, (gw_k, gx_k) = one_microbatch(xs[0], xs[1])
            with _jax.named_scope("update"):
                return (loss_sum + l_k, _jax.tree.map(_jnp.add, grad_sum, gw_k)), gx_k

        init = (_jnp.zeros((), _jnp.float32), _jax.tree.map(_jnp.zeros_like, weights))
        (loss, grad_w), grad_x = _jax.lax.scan(body, init, (per_example, given["loss_target"]))
    with _jax.named_scope("update"):
        delta_w, new_m, new_v = {}, {}, {}
        for n in TWIN_WEIGHTS:
            delta_w[n], new_m[n], new_v[n] = _adamw(weights[n], grad_w[n], given["m_" + n], given["v_" + n])
    return (loss, grad_x, *[grad_w[n] for n in TWIN_WEIGHTS], *[delta_w[n] for n in TWIN_WEIGHTS],
            *[new_m[n] for n in TWIN_WEIGHTS], *[new_v[n] for n in TWIN_WEIGHTS])
```

```python
import functools

import jax
import jax.numpy as jnp
import numpy as np
from jax import lax
from jax.experimental import pallas as pl
from jax.experimental.pallas import tpu as pltpu

F32 = jnp.float32
BF16 = jnp.bfloat16

N_META = 16
POOL_WINDOWS = (2, 4, 8, 16)
POOL_GROUP = 256
HEADS = 16
QK_NOPE = 128
QK_ROPE = 64
V_DIM = 128
ROPE_THETA = 10000.0
SOFTMAX_SCALE = (QK_NOPE + QK_ROPE) ** -0.5
EPS = 1e-6
ADAM_LR = 0.001
ADAM_B1 = 0.9
ADAM_B2 = 0.999
ADAM_EPS = 1e-08
ADAM_WD = 0.01
ADAM_STEP = 10

LANE = 128
FRONT = 128
PAD_ROWS = FRONT - N_META
HEAD_W = 256
GU_TILE = 512
VMEM_LIMIT = 56 * 1024 * 1024
MESH_AXES = ("x", "y", "c")
N_DEV = 8


def _pick(n, cands):
    for c in cands:
        if n % c == 0:
            return c
    raise ValueError(f"no tile for {n} in {cands}")


def _cparams(sem=None):
    kw = dict(vmem_limit_bytes=VMEM_LIMIT)
    if sem is not None:
        kw["dimension_semantics"] = sem
    return pltpu.CompilerParams(**kw)


_DIMS = {"nn": (((1,), (0,)), ((), ())), "nt": (((1,), (1,)), ((), ())), "tn": (((0,), (0,)), ((), ()))}


def _mm(name, form, a, b, tm, tn, outs, epi=None, extras=()):
    if form == "tn":
        k, m = a.shape
        n = b.shape[1]
        a_spec = pl.BlockSpec((k, tm), lambda i, j: (0, i))
        b_spec = pl.BlockSpec((k, tn), lambda i, j: (0, j))
    elif form == "nn":
        m, k = a.shape
        n = b.shape[1]
        a_spec = pl.BlockSpec((tm, k), lambda i, j: (i, 0))
        b_spec = pl.BlockSpec((k, tn), lambda i, j: (0, j))
    else:
        m, k = a.shape
        n = b.shape[0]
        a_spec = pl.BlockSpec((tm, k), lambda i, j: (i, 0))
        b_spec = pl.BlockSpec((tn, k), lambda i, j: (j, 0))
    assert m % tm == 0 and n % tn == 0, (name, m, n, tm, tn)
    n_ex = len(extras)
    dn = _DIMS[form]

    def body(a_ref, b_ref, *rest):
        ex, out_refs = rest[:n_ex], rest[n_ex:]
        acc = lax.dot_general(a_ref[...].astype(BF16), b_ref[...].astype(BF16), dn, preferred_element_type=F32)
        res = epi(acc, *[e[...] for e in ex]) if epi is not None else (acc,)
        for r, o in zip(res, out_refs, strict=True):
            o[...] = r.astype(o.dtype)

    return pl.pallas_call(
        body,
        name=name,
        grid=(m // tm, n // tn),
        in_specs=[a_spec, b_spec] + [pl.BlockSpec(blk, im) for _, blk, im in extras],
        out_specs=[pl.BlockSpec(blk, im) for _, _, blk, im in outs],
        out_shape=[jax.ShapeDtypeStruct(s, d) for s, d, _, _ in outs],
        compiler_params=_cparams(("parallel", "parallel")),
    )(a, b, *[e for e, _, _ in extras])


def _mm_plain(name, form, a, b, tm, tn, out_dtype):
    m = a.shape[1] if form == "tn" else a.shape[0]
    n = b.shape[0] if form == "nt" else b.shape[1]
    return _mm(name, form, a, b, tm, tn, [((m, n), out_dtype, (tm, tn), lambda i, j: (i, j))])[0]


def _rstd(x):
    return lax.rsqrt(jnp.mean(x * x, axis=-1, keepdims=True) + EPS)


def _norm_bwd(dy, x, gain):
    r = _rstd(x)
    dyg = dy * gain
    dx = r * (dyg - x * (r * r) * jnp.mean(dyg * x, axis=-1, keepdims=True))
    dgain = jnp.sum(dy * x * r, axis=0, keepdims=True)
    return dx, dgain


def _row_spec(tr, cols, col_block=0):
    return pl.BlockSpec((tr, cols), lambda i: (i, col_block))


def _vec_spec(cols, col_block=0):
    return pl.BlockSpec((1, cols), lambda i: (0, col_block))


def _row_tile(lp):
    return _pick(lp, (128,))


def prenorm(h, gain):
    lp, d = h.shape
    tr = _row_tile(lp)

    def body(h_ref, g_ref, a_ref):
        x = h_ref[...]
        a_ref[...] = (x * _rstd(x) * g_ref[...]).astype(BF16)

    return pl.pallas_call(
        body, name="prenorm", grid=(lp // tr,),
        in_specs=[_row_spec(tr, d), _vec_spec(d)], out_specs=_row_spec(tr, d),
        out_shape=jax.ShapeDtypeStruct((lp, d), BF16), compiler_params=_cparams(("parallel",)),
    )(h, gain)


def post_pre(name, f, h_in, g_post, coef, g_next):
    lp, d = f.shape
    tr = _row_tile(lp)

    def body(f_ref, h_ref, gp_ref, gn_ref, ho_ref, a_ref):
        fv = f_ref[...]
        ho = h_ref[...] + coef * (fv * _rstd(fv) * gp_ref[...])
        ho_ref[...] = ho
        a_ref[...] = (ho * _rstd(ho) * gn_ref[...]).astype(BF16)

    return pl.pallas_call(
        body, name=name, grid=(lp // tr,),
        in_specs=[_row_spec(tr, d), _row_spec(tr, d), _vec_spec(d), _vec_spec(d)],
        out_specs=[_row_spec(tr, d), _row_spec(tr, d)],
        out_shape=[jax.ShapeDtypeStruct((lp, d), F32), jax.ShapeDtypeStruct((lp, d), BF16)],
        compiler_params=_cparams(("parallel",)),
    )(f, h_in, g_post, g_next)


def post_loss(f, h_in, g_post, coef, target):
    lp, d = f.shape
    tr = _row_tile(lp)
    front_tiles = FRONT // tr

    def body(f_ref, h_ref, gp_ref, t_ref, dh_ref, loss_ref):
        i = pl.program_id(0)

        @pl.when(i == 0)
        def _():
            loss_ref[...] = jnp.zeros_like(loss_ref)

        @pl.when(i < front_tiles)
        def _():
            dh_ref[...] = jnp.zeros_like(dh_ref)

        @pl.when(i >= front_tiles)
        def _():
            fv = f_ref[...]
            ho = h_ref[...] + coef * (fv * _rstd(fv) * gp_ref[...])
            err = ho - t_ref[...]
            dh_ref[...] = err / d
            tok = jnp.mean(err * err, axis=-1, keepdims=True)
            loss_ref[...] += 0.5 * jnp.sum(tok)

    return pl.pallas_call(
        body, name="post_loss", grid=(lp // tr,),
        in_specs=[_row_spec(tr, d), _row_spec(tr, d), _vec_spec(d),
                  pl.BlockSpec((tr, d), lambda i: (jnp.maximum(i - front_tiles, 0), 0))],
        out_specs=[_row_spec(tr, d), pl.BlockSpec((8, LANE), lambda i: (0, 0))],
        out_shape=[jax.ShapeDtypeStruct((lp, d), F32), jax.ShapeDtypeStruct((8, LANE), F32)],
        compiler_params=_cparams(("arbitrary",)),
    )(f, h_in, g_post, target)


def post_bwd(dh_out, f, g_post, coef):
    lp, d = f.shape
    tr = _row_tile(lp)

    def body(dh_ref, f_ref, gp_ref, df_ref, dg_ref):
        @pl.when(pl.program_id(0) == 0)
        def _():
            dg_ref[...] = jnp.zeros_like(dg_ref)

        df, dg = _norm_bwd(coef * dh_ref[...], f_ref[...], gp_ref[...])
        df_ref[...] = df.astype(BF16)
        dg_ref[...] += dg

    return pl.pallas_call(
        body, name="post_bwd", grid=(lp // tr,),
        in_specs=[_row_spec(tr, d), _row_spec(tr, d), _vec_spec(d)],
        out_specs=[_row_spec(tr, d), _vec_spec(d)],
        out_shape=[jax.ShapeDtypeStruct((lp, d), BF16), jax.ShapeDtypeStruct((1, d), F32)],
        compiler_params=_cparams(("arbitrary",)),
    )(dh_out, f, g_post)


def pre_post_bwd(name, da, h_mid, g_pre, dh_out, f_prev, g_post_prev, coef_prev):
    lp, d = da.shape
    tr = _row_tile(lp)

    def body(da_ref, h_ref, gpre_ref, dho_ref, f_ref, gpost_ref, dh_ref, df_ref, dgpre_ref, dgpost_ref):
        @pl.when(pl.program_id(0) == 0)
        def _():
            dgpre_ref[...] = jnp.zeros_like(dgpre_ref)
            dgpost_ref[...] = jnp.zeros_like(dgpost_ref)

        dx, dgpre = _norm_bwd(da_ref[...], h_ref[...], gpre_ref[...])
        dh = dho_ref[...] + dx
        dh_ref[...] = dh
        dgpre_ref[...] += dgpre
        df, dgpost = _norm_bwd(coef_prev * dh, f_ref[...], gpost_ref[...])
        df_ref[...] = df.astype(BF16)
        dgpost_ref[...] += dgpost

    return pl.pallas_call(
        body, name=name, grid=(lp // tr,),
        in_specs=[_row_spec(tr, d), _row_spec(tr, d), _vec_spec(d), _row_spec(tr, d), _row_spec(tr, d), _vec_spec(d)],
        out_specs=[_row_spec(tr, d), _row_spec(tr, d), _vec_spec(d), _vec_spec(d)],
        out_shape=[jax.ShapeDtypeStruct((lp, d), F32), jax.ShapeDtypeStruct((lp, d), BF16),
                   jax.ShapeDtypeStruct((1, d), F32), jax.ShapeDtypeStruct((1, d), F32)],
        compiler_params=_cparams(("arbitrary",)),
    )(da, h_mid, g_pre, dh_out, f_prev, g_post_prev)


def pre_bwd_first(da, h0, g_pre, dh_out):
    lp, d = da.shape
    tr = _row_tile(lp)
    front_tiles = FRONT // tr
    assert front_tiles == 1

    def body(da_ref, h_ref, gpre_ref, dho_ref, gx_ref, front_ref, dgpre_ref):
        i = pl.program_id(0)

        @pl.when(i == 0)
        def _():
            dgpre_ref[...] = jnp.zeros_like(dgpre_ref)

        dx, dgpre = _norm_bwd(da_ref[...], h_ref[...], gpre_ref[...])
        dh = dho_ref[...] + dx
        dgpre_ref[...] += dgpre
        gx_ref[...] = dh

        @pl.when(i == 0)
        def _():
            front_ref[...] = dh

    return pl.pallas_call(
        body, name="pre_bwd_first", grid=(lp // tr,),
        in_specs=[_row_spec(tr, d), _row_spec(tr, d), _vec_spec(d), _row_spec(tr, d)],
        out_specs=[pl.BlockSpec((tr, d), lambda i: (jnp.maximum(i - front_tiles, 0), 0)),
                   pl.BlockSpec((tr, d), lambda i: (0, 0)), _vec_spec(d)],
        out_shape=[jax.ShapeDtypeStruct((lp - FRONT, d), F32), jax.ShapeDtypeStruct((tr, d), F32),
                   jax.ShapeDtypeStruct((1, d), F32)],
        compiler_params=_cparams(("arbitrary",)),
    )(da, h0, g_pre, dh_out)


def _m_tile(lp):
    return _pick(lp, (1056, 512, 256, 128))


def ffn_fwd(tag, a, wgu_p, wd):
    lp, d = a.shape
    f2 = wgu_p.shape[1]
    tm = _m_tile(lp)

    def epi(acc):
        g, u = acc[:, :GU_TILE], acc[:, GU_TILE:]
        return acc, g * jax.nn.sigmoid(g) * u

    gu, s = _mm(tag + "_gu", "nn", a, wgu_p, tm, 2 * GU_TILE,
                [((lp, f2), F32, (tm, 2 * GU_TILE), lambda i, j: (i, j)),
                 ((lp, f2 // 2), BF16, (tm, GU_TILE), lambda i, j: (i, j))], epi=epi)
    f = _mm_plain(tag + "_down", "nn", s, wd, tm, 512, F32)
    return gu, s, f


def ffn_bwd(tag, df, a, gu, s, wgu_p, wd):
    lp, d = df.shape
    f2 = wgu_p.shape[1]
    tm = _m_tile(lp)

    def epi(acc, gu_t):
        g, u = gu_t[:, :GU_TILE], gu_t[:, GU_TILE:]
        sig = jax.nn.sigmoid(g)
        dg = acc * u * (sig * (1.0 + g * (1.0 - sig)))
        du = acc * (g * sig)
        return (jnp.concatenate([dg, du], axis=1),)

    dgu = _mm(tag + "_ds", "nt", df, wd, tm, GU_TILE,
              [((lp, f2), BF16, (tm, 2 * GU_TILE), lambda i, j: (i, j))], epi=epi,
              extras=[(gu, (tm, 2 * GU_TILE), lambda i, j: (i, j))])[0]
    dwd = _mm_plain(tag + "_dwd", "tn", s, df, 512, _pick(d, (1024,)), BF16)
    da = _mm_plain(tag + "_da", "nt", dgu, wgu_p, _pick(lp, (528, 256, 128)), 256, F32)
    dwgu = _mm_plain(tag + "_dwgu", "tn", a, dgu, _pick(d, (1024,)), 1024, BF16)
    return da, dwgu, dwd


Z_U, Z_CQ, Z_CKV, Z_GP, Z_GM, Z_KR, Z_COLS = 0, 1024, 1536, 2048, 4096, 6144, 6400
POOL_W = POOL_GROUP * len(POOL_WINDOWS)
HALO = 16


def _pool_counts(lp, w):
    pos = lax.broadcasted_iota(jnp.int32, (lp, 1), 0) - PAD_ROWS
    return jnp.clip(pos + 1, 1, w).astype(F32)


def _pool_diff(u_ref, pad_ref, lp, w):
    pad_ref[pl.ds(0, HALO), :] = jnp.zeros((HALO, POOL_GROUP), F32)
    pad_ref[pl.ds(HALO, lp), :] = u_ref[...]
    acc = pad_ref[pl.ds(HALO, lp), :]
    for s in range(1, w):
        acc = acc + pad_ref[pl.ds(HALO - s, lp), :]
    return acc / _pool_counts(lp, w) - u_ref[...]


def pool_fwd(z, pool_w, pool_scale):
    lp = z.shape[0]
    ng = len(POOL_WINDOWS)

    def body(u_ref, w_ref, sc_ref, o_ref, pad_ref):
        for g, w in enumerate(POOL_WINDOWS):
            @pl.when(pl.program_id(0) == g)
            def _(w=w):
                dd = _pool_diff(u_ref, pad_ref, lp, w)
                y = jnp.dot(dd.astype(BF16), w_ref[0], preferred_element_type=F32)
                o_ref[...] = (y * sc_ref[...]).astype(BF16)

    return pl.pallas_call(
        body, name="pool_fwd", grid=(ng,),
        in_specs=[pl.BlockSpec((lp, POOL_GROUP), lambda g: (0, g)),
                  pl.BlockSpec((1, POOL_GROUP, POOL_GROUP), lambda g: (g, 0, 0)),
                  pl.BlockSpec((1, POOL_GROUP), lambda g: (0, g))],
        out_specs=pl.BlockSpec((lp, POOL_GROUP), lambda g: (0, g)),
        out_shape=jax.ShapeDtypeStruct((lp, POOL_W), BF16),
        scratch_shapes=[pltpu.VMEM((lp + HALO, POOL_GROUP), F32)],
        compiler_params=_cparams(("parallel",)),
    )(z, pool_w, pool_scale)


def pool_bwd(dyp, z, pool_w, pool_scale):
    lp = z.shape[0]
    ng = len(POOL_WINDOWS)

    def body(dy_ref, u_ref, w_ref, sc_ref, du_ref, dw_ref, dsc_ref, pad_ref):
        for g, w in enumerate(POOL_WINDOWS):
            @pl.when(pl.program_id(0) == g)
            def _(w=w):
                dd = _pool_diff(u_ref, pad_ref, lp, w).astype(BF16)
                wg = w_ref[0]
                ypre = jnp.dot(dd, wg, preferred_element_type=F32)
                dy = dy_ref[...]
                dsc_ref[...] = jnp.sum(dy * ypre, axis=0, keepdims=True)
                dypre = (dy * sc_ref[...]).astype(BF16)
                dw_ref[0] = lax.dot_general(dd, dypre, _DIMS["tn"], preferred_element_type=F32)
                ddd = lax.dot_general(dypre, wg, _DIMS["nt"], preferred_element_type=F32)
                pad_ref[pl.ds(0, lp), :] = ddd / _pool_counts(lp, w)
                pad_ref[pl.ds(lp, HALO), :] = jnp.zeros((HALO, POOL_GROUP), F32)
                acc = -ddd
                for s in range(w):
                    acc = acc + pad_ref[pl.ds(s, lp), :]
                du_ref[...] = acc.astype(BF16)

    return pl.pallas_call(
        body, name="pool_bwd", grid=(ng,),
        in_specs=[pl.BlockSpec((lp, POOL_GROUP), lambda g: (0, g)),
                  pl.BlockSpec((lp, POOL_GROUP), lambda g: (0, g)),
                  pl.BlockSpec((1, POOL_GROUP, POOL_GROUP), lambda g: (g, 0, 0)),
                  pl.BlockSpec((1, POOL_GROUP), lambda g: (0, g))],
        out_specs=[pl.BlockSpec((lp, POOL_GROUP), lambda g: (0, g)),
                   pl.BlockSpec((1, POOL_GROUP, POOL_GROUP), lambda g: (g, 0, 0)),
                   pl.BlockSpec((1, POOL_GROUP), lambda g: (0, g))],
        out_shape=[jax.ShapeDtypeStruct((lp, POOL_W), BF16),
                   jax.ShapeDtypeStruct((ng, POOL_GROUP, POOL_GROUP), F32),
                   jax.ShapeDtypeStruct((1, POOL_W), F32)],
        scratch_shapes=[pltpu.VMEM((lp + HALO, POOL_GROUP), F32)],
        compiler_params=_cparams(("parallel",)),
    )(dyp, z, pool_w, pool_scale)


Q_COLS = HEADS * HEAD_W
ROPE_BLOCK = LANE


def rope_tables(lp):
    pos = jnp.maximum(jnp.arange(lp, dtype=F32) - PAD_ROWS, 0.0)
    inv = ROPE_THETA ** (-jnp.arange(0, QK_ROPE, 2, dtype=F32) / QK_ROPE)
    ang = pos[:, None] * inv[None, :]
    cos, sin, zero = jnp.cos(ang), jnp.sin(ang), jnp.zeros_like(ang)
    return jnp.stack([jnp.concatenate([cos, cos, zero, zero], axis=1),
                      jnp.concatenate([-sin, zero, zero, zero], axis=1),
                      jnp.concatenate([zero, sin, zero, zero], axis=1)])


def _rope(x, tabs):
    return x * tabs[0] + pltpu.roll(x, 96, 1) * tabs[1] + pltpu.roll(x, 32, 1) * tabs[2]


def _rope_bwd(g, tabs):
    return g * tabs[0] + pltpu.roll(g * tabs[1], 32, 1) + pltpu.roll(g * tabs[2], 96, 1)


def _tab_spec(tr):
    return pl.BlockSpec((3, tr, ROPE_BLOCK), lambda i: (0, i, 0))


def mla_prep(z, g_q, g_kv, tabs):
    lp = z.shape[0]
    tr = _row_tile(lp)
    r = g_q.shape[1]

    def body(cq_ref, ckv_ref, kr_ref, gq_ref, gkv_ref, tab_ref, qn_ref, kvn_ref, kpe_ref):
        cq, ckv = cq_ref[...], ckv_ref[...]
        qn_ref[...] = (cq * _rstd(cq) * gq_ref[...]).astype(BF16)
        kvn_ref[...] = (ckv * _rstd(ckv) * gkv_ref[...]).astype(BF16)
        kpe_ref[...] = _rope(kr_ref[...], tab_ref[...]).astype(BF16)

    return pl.pallas_call(
        body, name="mla_prep", grid=(lp // tr,),
        in_specs=[_row_spec(tr, r, Z_CQ // r), _row_spec(tr, r, Z_CKV // r), _row_spec(tr, ROPE_BLOCK, Z_KR // ROPE_BLOCK),
                  _vec_spec(r), _vec_spec(r), _tab_spec(tr)],
        out_specs=[_row_spec(tr, r), _row_spec(tr, r), _row_spec(tr, ROPE_BLOCK)],
        out_shape=[jax.ShapeDtypeStruct((lp, r), BF16), jax.ShapeDtypeStruct((lp, r), BF16),
                   jax.ShapeDtypeStruct((lp, ROPE_BLOCK), BF16)],
        compiler_params=_cparams(("parallel",)),
    )(z, z, z, g_q, g_kv, tabs)


def q_proj(qn, wq_p, tabs):
    lp = qn.shape[0]
    tm = _m_tile(lp)
    tn = 4 * HEAD_W

    def epi(acc, tab):
        parts = []
        for t in range(tn // HEAD_W):
            parts.append(acc[:, t * HEAD_W:t * HEAD_W + QK_NOPE])
            parts.append(_rope(acc[:, t * HEAD_W + QK_NOPE:(t + 1) * HEAD_W], tab))
        return (jnp.concatenate(parts, axis=1),)

    return _mm("q_proj", "nn", qn, wq_p, tm, tn, [((lp, Q_COLS), BF16, (tm, tn), lambda i, j: (i, j))], epi=epi,
               extras=[(tabs, (3, tm, ROPE_BLOCK), lambda i, j: (0, i, 0))])[0]


def kv_proj(kvn, wkv):
    return _mm_plain("kv_proj", "nn", kvn, wkv, _m_tile(kvn.shape[0]), 1024, BF16)


def q_rope_bwd(dq, tabs):
    lp = dq.shape[0]
    tr = _row_tile(lp)

    def body(dq_ref, tab_ref, o_ref):
        tab = tab_ref[...]
        for h in range(HEADS):
            o_ref[:, h * HEAD_W:h * HEAD_W + QK_NOPE] = dq_ref[:, h * HEAD_W:h * HEAD_W + QK_NOPE].astype(BF16)
            o_ref[:, h * HEAD_W + QK_NOPE:(h + 1) * HEAD_W] = _rope_bwd(
                dq_ref[:, h * HEAD_W + QK_NOPE:(h + 1) * HEAD_W], tab).astype(BF16)

    return pl.pallas_call(
        body, name="q_rope_bwd", grid=(lp // tr,),
        in_specs=[_row_spec(tr, Q_COLS), _tab_spec(tr)], out_specs=_row_spec(tr, Q_COLS),
        out_shape=jax.ShapeDtypeStruct((lp, Q_COLS), BF16), compiler_params=_cparams(("parallel",)),
    )(dq, tabs)


def mla_prep_bwd(dqn, dkvn, dkpe_h, z, g_q, g_kv, tabs):
    lp = z.shape[0]
    tr = _row_tile(lp)
    r = g_q.shape[1]

    def body(dqn_ref, dkvn_ref, dkpe_ref, cq_ref, ckv_ref, gq_ref, gkv_ref, tab_ref,
             dcq_ref, dckv_ref, dkr_ref, dgq_ref, dgkv_ref):
        @pl.when(pl.program_id(0) == 0)
        def _():
            dgq_ref[...] = jnp.zeros_like(dgq_ref)
            dgkv_ref[...] = jnp.zeros_like(dgkv_ref)

        dcq, dgq = _norm_bwd(dqn_ref[...], cq_ref[...], gq_ref[...])
        dckv, dgkv = _norm_bwd(dkvn_ref[...], ckv_ref[...], gkv_ref[...])
        dcq_ref[...] = dcq.astype(BF16)
        dckv_ref[...] = dckv.astype(BF16)
        dgq_ref[...] += dgq
        dgkv_ref[...] += dgkv
        dkpe = dkpe_ref[0]
        for h in range(1, HEADS):
            dkpe = dkpe + dkpe_ref[h]
        dkr_ref[...] = _rope_bwd(dkpe, tab_ref[...]).astype(BF16)

    return pl.pallas_call(
        body, name="mla_prep_bwd", grid=(lp // tr,),
        in_specs=[_row_spec(tr, r), _row_spec(tr, r), pl.BlockSpec((HEADS, tr, ROPE_BLOCK), lambda i: (0, i, 0)),
                  _row_spec(tr, r, Z_CQ // r), _row_spec(tr, r, Z_CKV // r), _vec_spec(r), _vec_spec(r), _tab_spec(tr)],
        out_specs=[_row_spec(tr, r), _row_spec(tr, r), _row_spec(tr, ROPE_BLOCK), _vec_spec(r), _vec_spec(r)],
        out_shape=[jax.ShapeDtypeStruct((lp, r), BF16), jax.ShapeDtypeStruct((lp, r), BF16),
                   jax.ShapeDtypeStruct((lp, ROPE_BLOCK), BF16),
                   jax.ShapeDtypeStruct((1, r), F32), jax.ShapeDtypeStruct((1, r), F32)],
        compiler_params=_cparams(("arbitrary",)),
    )(dqn, dkvn, dkpe_h, z, z, g_q, g_kv, tabs)


def _attn_tile(lp):
    return _pick(lp, (384, 128))


def _scores(q, kcat, q_tile, k_tile, t):
    s = lax.dot_general(q, kcat, _DIMS["nt"], preferred_element_type=F32) * SOFTMAX_SCALE
    qpos = q_tile * t + lax.broadcasted_iota(jnp.int32, (t, t), 0)
    kpos = k_tile * t + lax.broadcasted_iota(jnp.int32, (t, t), 1)
    return jnp.where((kpos <= qpos) & (kpos >= PAD_ROWS), s, jnp.float32(-1e30))


def flash_fwd(q, kv, kpe):
    lp = q.shape[0]
    t = _attn_tile(lp)
    nt = lp // t

    def body(q_ref, kv_ref, kpe_ref, o32_ref, o16_ref, lse_ref, m_sc, l_sc, acc_sc):
        qi, ki = pl.program_id(1), pl.program_id(2)

        @pl.when(ki == 0)
        def _():
            m_sc[...] = jnp.full_like(m_sc, -jnp.inf)
            l_sc[...] = jnp.zeros_like(l_sc)
            acc_sc[...] = jnp.zeros_like(acc_sc)

        @pl.when(ki <= qi)
        def _():
            kvt = kv_ref[...]
            kcat = jnp.concatenate([kvt[:, :QK_NOPE], kpe_ref[...]], axis=1)
            s = _scores(q_ref[...], kcat, qi, ki, t)
            m_prev = m_sc[...]
            m_new = jnp.maximum(m_prev, jnp.max(s, axis=1, keepdims=True))
            alpha = jnp.exp(m_prev - m_new)
            p = jnp.exp(s - m_new[:, :1])
            l_sc[...] = alpha * l_sc[...] + jnp.sum(p, axis=1, keepdims=True)
            acc_sc[...] = alpha * acc_sc[...] + jnp.dot(p.astype(BF16), kvt[:, QK_NOPE:], preferred_element_type=F32)
            m_sc[...] = m_new

        @pl.when(ki == qi)
        def _():
            l = l_sc[...]
            o = acc_sc[...] / l
            o32_ref[...] = o
            o16_ref[...] = o.astype(BF16)
            lse_ref[0] = m_sc[...] + jnp.log(l)

    return pl.pallas_call(
        body, name="flash_fwd", grid=(HEADS, nt, nt),
        in_specs=[pl.BlockSpec((t, HEAD_W), lambda h, qi, ki: (qi, h)),
                  pl.BlockSpec((t, HEAD_W), lambda h, qi, ki: (jnp.minimum(ki, qi), h)),
                  pl.BlockSpec((t, ROPE_BLOCK), lambda h, qi, ki: (jnp.minimum(ki, qi), 0))],
        out_specs=[pl.BlockSpec((t, V_DIM), lambda h, qi, ki: (qi, h)),
                   pl.BlockSpec((t, V_DIM), lambda h, qi, ki: (qi, h)),
                   pl.BlockSpec((1, t, LANE), lambda h, qi, ki: (h, qi, 0))],
        out_shape=[jax.ShapeDtypeStruct((lp, HEADS * V_DIM), F32), jax.ShapeDtypeStruct((lp, HEADS * V_DIM), BF16),
                   jax.ShapeDtypeStruct((HEADS, lp, LANE), F32)],
        scratch_shapes=[pltpu.VMEM((t, LANE), F32), pltpu.VMEM((t, LANE), F32), pltpu.VMEM((t, V_DIM), F32)],
        compiler_params=_cparams(("parallel", "parallel", "arbitrary")),
    )(q, kv, kpe)


def flash_bwd(q, kv, kpe, o32, lse, do):
    lp = q.shape[0]
    t = _attn_tile(lp)
    nt = lp // t

    def body(q_ref, kv_ref, kpe_ref, o_ref, lse_ref, do_ref, dq_ref, dkv_ref, dkpe_ref, dk_sc, dv_sc):
        ki, qi = pl.program_id(1), pl.program_id(2)

        @pl.when((ki == 0) & (qi == 0))
        def _():
            dq_ref[...] = jnp.zeros_like(dq_ref)

        @pl.when(qi == 0)
        def _():
            dk_sc[...] = jnp.zeros_like(dk_sc)
            dv_sc[...] = jnp.zeros_like(dv_sc)

        @pl.when(qi >= ki)
        def _():
            qt = q_ref[...]
            kvt = kv_ref[...]
            kcat = jnp.concatenate([kvt[:, :QK_NOPE], kpe_ref[...]], axis=1)
            s = _scores(qt, kcat, qi, ki, t)
            p = jnp.exp(s - lse_ref[0][:, :1])
            do = do_ref[...]
            delta = jnp.sum(do * o_ref[...], axis=1, keepdims=True)
            do16 = do.astype(BF16)
            dv_sc[...] += lax.dot_general(p.astype(BF16), do16, _DIMS["tn"], preferred_element_type=F32)
            dp = lax.dot_general(do16, kvt[:, QK_NOPE:], _DIMS["nt"], preferred_element_type=F32)
            ds = (p * (dp - delta) * SOFTMAX_SCALE).astype(BF16)
            dk_sc[...] += lax.dot_general(ds, qt, _DIMS["tn"], preferred_element_type=F32)
            row = pl.multiple_of(qi * t, t)
            dq_ref[pl.ds(row, t), :] += jnp.dot(ds, kcat, preferred_element_type=F32)

        @pl.when(qi == nt - 1)
        def _():
            dk = dk_sc[...]
            dkv_ref[...] = jnp.concatenate([dk[:, :QK_NOPE], dv_sc[...]], axis=1).astype(BF16)
            dkpe_ref[0] = dk[:, QK_NOPE:]

    qmap = lambda h, ki, qi: (jnp.maximum(qi, ki), h)
    return pl.pallas_call(
        body, name="flash_bwd", grid=(HEADS, nt, nt),
        in_specs=[pl.BlockSpec((t, HEAD_W), qmap),
                  pl.BlockSpec((t, HEAD_W), lambda h, ki, qi: (ki, h)),
                  pl.BlockSpec((t, ROPE_BLOCK), lambda h, ki, qi: (ki, 0)),
                  pl.BlockSpec((t, V_DIM), qmap),
                  pl.BlockSpec((1, t, LANE), lambda h, ki, qi: (h, jnp.maximum(qi, ki), 0)),
                  pl.BlockSpec((t, V_DIM), qmap)],
        out_specs=[pl.BlockSpec((lp, HEAD_W), lambda h, ki, qi: (0, h)),
                   pl.BlockSpec((t, HEAD_W), lambda h, ki, qi: (ki, h)),
                   pl.BlockSpec((1, t, ROPE_BLOCK), lambda h, ki, qi: (h, ki, 0))],
        out_shape=[jax.ShapeDtypeStruct((lp, Q_COLS), F32), jax.ShapeDtypeStruct((lp, Q_COLS), BF16),
                   jax.ShapeDtypeStruct((HEADS, lp, ROPE_BLOCK), F32)],
        scratch_shapes=[pltpu.VMEM((t, HEAD_W), F32), pltpu.VMEM((t, V_DIM), F32)],
        compiler_params=_cparams(("parallel", "arbitrary", "arbitrary")),
    )(q, kv, kpe, o32, lse, do)


def _ij(i, j):
    return (i, j)


def mixer_fwd(a2, w, tabs, pool_scale, g_q, g_kv):
    lp, d = a2.shape
    tm = _m_tile(lp)
    tn = 512
    z = _mm_plain("mix_in", "nn", a2, w["w_in"], tm, 1280, F32)
    yp = pool_fwd(z, w["pool_w"], pool_scale)
    qn, kvn, kpe = mla_prep(z, g_q, g_kv, tabs)
    q = q_proj(qn, w["w_q_b"], tabs)
    kv = kv_proj(kvn, w["w_kv_b"])
    o32, o16, lse = flash_fwd(q, kv, kpe)
    y_pool = _mm_plain("pool_out", "nn", yp, w["w_pool_o"], tm, tn, F32)

    def epi(acc, ypl, gp, gm):
        return jax.nn.sigmoid(gp) * ypl + jax.nn.sigmoid(gm) * acc, acc

    y, y_mla = _mm("mla_out_gate", "nn", o16, w["w_mla_o"], tm, tn,
                   [((lp, d), BF16, (tm, tn), _ij), ((lp, d), F32, (tm, tn), _ij)], epi=epi,
                   extras=[(y_pool, (tm, tn), _ij), (z, (tm, tn), lambda i, j: (i, Z_GP // tn + j)),
                           (z, (tm, tn), lambda i, j: (i, Z_GM // tn + j))])
    m = _mm_plain("mix_out", "nn", y, w["w_out"], tm, tn, F32)
    return m, dict(z=z, yp=yp, qn=qn, kvn=kvn, kpe=kpe, q=q, kv=kv, o32=o32, o16=o16, lse=lse,
                   y_pool=y_pool, y_mla=y_mla, y=y)


def mixer_bwd(dm, a2, sv, w, tabs, pool_scale, g_q, g_kv):
    lp, d = dm.shape
    tm = _m_tile(lp)
    tn = 512
    z = sv["z"]

    def epi(acc, ypl, yml, gp, gm):
        sp, sm = jax.nn.sigmoid(gp), jax.nn.sigmoid(gm)
        return acc * sp, acc * sm, acc * ypl * (sp * (1.0 - sp)), acc * yml * (sm * (1.0 - sm))

    dyp, dym, dgp, dgm = _mm(
        "gate_bwd", "nt", dm, w["w_out"], tm, tn, [((lp, d), BF16, (tm, tn), _ij)] * 4, epi=epi,
        extras=[(sv["y_pool"], (tm, tn), _ij), (sv["y_mla"], (tm, tn), _ij),
                (z, (tm, tn), lambda i, j: (i, Z_GP // tn + j)), (z, (tm, tn), lambda i, j: (i, Z_GM // tn + j))])
    g = {}
    g["w_out"] = _mm_plain("dw_out", "tn", sv["y"], dm, 1024, 1024, BF16)
    g["w_pool_o"] = _mm_plain("dw_pool_o", "tn", sv["yp"], dyp, 512, 1024, BF16)
    dypre = _mm_plain("pool_out_bwd", "nt", dyp, w["w_pool_o"], tm, tn, F32)
    du, g["pool_w"], d_pool_scale = pool_bwd(dypre, z, w["pool_w"], pool_scale)
    g["w_mla_o"] = _mm_plain("dw_mla_o", "tn", sv["o16"], dym, 1024, 1024, BF16)
    do = _mm_plain("mla_out_bwd", "nt", dym, w["w_mla_o"], tm, tn, F32)
    dq, dkv, dkpe_h = flash_bwd(sv["q"], sv["kv"], sv["kpe"], sv["o32"], sv["lse"], do)
    dql = q_rope_bwd(dq, tabs)
    g["w_q_b"] = _mm_plain("dw_q_b", "tn", sv["qn"], dql, 512, 1024, BF16)
    dqn = _mm_plain("q_proj_bwd", "nt", dql, w["w_q_b"], tm, 512, F32)
    g["w_kv_b"] = _mm_plain("dw_kv_b", "tn", sv["kvn"], dkv, 512, 1024, BF16)
    dkvn = _mm_plain("kv_proj_bwd", "nt", dkv, w["w_kv_b"], tm, 512, F32)
    dcq, dckv, dkr, d_gq, d_gkv = mla_prep_bwd(dqn, dkvn, dkpe_h, z, g_q, g_kv, tabs)
    dz = jnp.concatenate([du, dcq, dckv, dgp, dgm, dkr, jnp.zeros((lp, Z_COLS - Z_KR - ROPE_BLOCK), BF16)], axis=1)
    g["w_in"] = _mm_plain("dw_in", "tn", a2, dz, 1024, 1280, BF16)
    da2 = _mm_plain("mix_in_bwd", "nt", dz, w["w_in"], tm, tn, F32)
    return da2, g, dict(pool_scale=d_pool_scale, q_a_norm=d_gq, kv_a_norm=d_gkv)


_ANY = pl.BlockSpec(memory_space=pl.ANY)
_MESH = pl.DeviceIdType.MESH


def _my_pos():
    return lax.axis_index("x"), lax.axis_index("y"), lax.axis_index("c")


def all_gather(name, shards):
    n = len(shards)

    def body(*refs):
        ins, outs = refs[:n], refs[n:2 * n]
        send_sems, recv_sems, local_sems = refs[2 * n:]
        x, y, c = _my_pos()
        me, sibling = (x, y, c), (x, y, 1 - c)
        chips = [(1 - x, y), (x, 1 - y), (1 - x, 1 - y)]

        def blk(a, px, py, pc):
            return outs[a].at[4 * px + 2 * py + pc]

        def copy(a, k, block, to, src=None):
            return pltpu.make_async_remote_copy(
                src_ref=blk(a, *block) if src is None else src, dst_ref=blk(a, *block),
                send_sem=send_sems.at[a, k], recv_sem=recv_sems.at[a, k], device_id=to, device_id_type=_MESH)

        mine = [pltpu.make_async_copy(ins[a], blk(a, *me), local_sems.at[a]) for a in range(n)]
        for cp in mine:
            cp.start()
        first = []
        for a in range(n):
            first.append(copy(a, 0, me, sibling, src=ins[a]))
            first += [copy(a, 1 + j, me, (*chip, c), src=ins[a]) for j, chip in enumerate(chips)]
        for cp in first:
            cp.start()
        passed = []
        for j, chip in enumerate(chips):
            for a in range(n):
                copy(a, 1 + j, (*chip, c), me).wait_recv()
                fwd = copy(a, 4 + j, (*chip, c), sibling)
                fwd.start()
                passed.append(fwd)
        for a in range(n):
            copy(a, 0, sibling, me).wait_recv()
            for j, chip in enumerate(chips):
                copy(a, 4 + j, (*chip, 1 - c), me).wait_recv()
        for cp in first + passed:
            cp.wait_send()
        for cp in mine:
            cp.wait()

    return pl.pallas_call(
        body, name=name,
        in_specs=[_ANY] * n, out_specs=[_ANY] * n,
        out_shape=[jax.ShapeDtypeStruct((N_DEV, *s.shape), s.dtype) for s in shards],
        scratch_shapes=[pltpu.SemaphoreType.DMA((n, 7)), pltpu.SemaphoreType.DMA((n, 7)), pltpu.SemaphoreType.DMA((n,))],
    )(*shards)


def rs_sibling(name, blocks):
    n = len(blocks)

    def body(*refs):
        ins, outs = refs[:n], refs[n:2 * n]
        send_sems, recv_sems = refs[2 * n:]
        x, y, c = _my_pos()
        cps = []
        for a in range(n):
            for k in range(4):
                cp = pltpu.make_async_remote_copy(
                    src_ref=ins[a].at[2 * k + (1 - c)], dst_ref=outs[a].at[k],
                    send_sem=send_sems.at[a, k], recv_sem=recv_sems.at[a, k],
                    device_id=(x, y, 1 - c), device_id_type=_MESH)
                cp.start()
                cps.append(cp)
        for cp in cps:
            cp.wait()

    return pl.pallas_call(
        body, name=name,
        in_specs=[_ANY] * n, out_specs=[_ANY] * n,
        out_shape=[jax.ShapeDtypeStruct((4, *b.shape[1:]), b.dtype) for b in blocks],
        scratch_shapes=[pltpu.SemaphoreType.DMA((n, 4)), pltpu.SemaphoreType.DMA((n, 4))],
    )(*blocks)


def rs_chips(name, sums):
    n = len(sums)

    def body(*refs):
        ins, outs = refs[:n], refs[n:2 * n]
        send_sems, recv_sems = refs[2 * n:]
        x, y, c = _my_pos()
        chips = [(1 - x, y), (x, 1 - y), (1 - x, 1 - y)]
        cps = []
        for a in range(n):
            for j, chip in enumerate(chips):
                cp = pltpu.make_async_remote_copy(
                    src_ref=ins[a].at[2 * chip[0] + chip[1]], dst_ref=outs[a].at[j],
                    send_sem=send_sems.at[a, j], recv_sem=recv_sems.at[a, j],
                    device_id=(*chip, c), device_id_type=_MESH)
                cp.start()
                cps.append(cp)
        for cp in cps:
            cp.wait()

    return pl.pallas_call(
        body, name=name,
        in_specs=[_ANY] * n, out_specs=[_ANY] * n,
        out_shape=[jax.ShapeDtypeStruct((3, *s.shape[1:]), s.dtype) for s in sums],
        scratch_shapes=[pltpu.SemaphoreType.DMA((n, 3)), pltpu.SemaphoreType.DMA((n, 3))],
    )(*sums)


def _ew_rows(r, c):
    for t in (512, 256, 128, 64, 32, 16):
        if r % t == 0 and t * c * 4 <= 768 * 1024:
            return t
    raise ValueError((r, c))


def rs_add(name, blocks, recv, core):
    _, r, c = blocks.shape
    tr = _ew_rows(r, c)

    def body(core_ref, b_ref, r_ref, o_ref):
        o_ref[...] = (b_ref[...].astype(F32) + r_ref[...].astype(F32)).astype(BF16)

    return pl.pallas_call(
        body, name=name,
        grid_spec=pltpu.PrefetchScalarGridSpec(
            num_scalar_prefetch=1, grid=(4, r // tr),
            in_specs=[pl.BlockSpec((1, tr, c), lambda k, i, core_ref: (2 * k + core_ref[0], i, 0)),
                      pl.BlockSpec((1, tr, c), lambda k, i, core_ref: (k, i, 0))],
            out_specs=pl.BlockSpec((1, tr, c), lambda k, i, core_ref: (k, i, 0))),
        out_shape=jax.ShapeDtypeStruct((4, r, c), BF16),
        compiler_params=_cparams(("parallel", "parallel")),
    )(core, blocks, recv)


def _adamw(w, g, m, v):
    m = ADAM_B1 * m + (1.0 - ADAM_B1) * g
    v = ADAM_B2 * v + (1.0 - ADAM_B2) * jnp.square(g)
    m_hat = m / (1.0 - ADAM_B1 ** ADAM_STEP)
    v_hat = v / (1.0 - ADAM_B2 ** ADAM_STEP)
    delta = -ADAM_LR * (m_hat / (jnp.sqrt(v_hat) + ADAM_EPS) + ADAM_WD * w)
    return delta, m, v


def adamw_shard(name, w, m, v, sums, recv, chip):
    r, c = w.shape
    tr = _ew_rows(r, c)

    def body(chip_ref, w_ref, m_ref, v_ref, s_ref, r_ref, g_ref, d_ref, mo_ref, vo_ref):
        g = s_ref[0].astype(F32)
        for j in range(3):
            g = g + r_ref[j].astype(F32)
        d, mn, vn = _adamw(w_ref[...], g, m_ref[...], v_ref[...])
        g_ref[...] = g
        d_ref[...] = d
        mo_ref[...] = mn
        vo_ref[...] = vn

    spec = pl.BlockSpec((tr, c), lambda i, chip_ref: (i, 0))
    return pl.pallas_call(
        body, name=name,
        grid_spec=pltpu.PrefetchScalarGridSpec(
            num_scalar_prefetch=1, grid=(r // tr,),
            in_specs=[spec, spec, spec,
                      pl.BlockSpec((1, tr, c), lambda i, chip_ref: (chip_ref[0], i, 0)),
                      pl.BlockSpec((3, tr, c), lambda i, chip_ref: (0, i, 0))],
            out_specs=[spec] * 4),
        out_shape=[jax.ShapeDtypeStruct((r, c), F32)] * 4,
        compiler_params=_cparams(("parallel",)),
    )(chip, w, m, v, sums, recv)


def reduce_small(gathered):
    _, r, c = gathered.shape

    def body(g_ref, o_ref):
        acc = g_ref[0]
        for k in range(1, N_DEV):
            acc = acc + g_ref[k]
        o_ref[...] = acc

    return pl.pallas_call(body, name="reduce_small", out_shape=jax.ShapeDtypeStruct((r, c), F32))(gathered)


def adamw_small(ws, gs, ms, vs):
    n = len(ws)

    def body(*refs):
        w_r, g_r, m_r, v_r = refs[:n], refs[n:2 * n], refs[2 * n:3 * n], refs[3 * n:4 * n]
        d_o, m_o, v_o = refs[4 * n:5 * n], refs[5 * n:6 * n], refs[6 * n:7 * n]
        for a in range(n):
            d, mn, vn = _adamw(w_r[a][...], g_r[a][...], m_r[a][...], v_r[a][...])
            d_o[a][...] = d
            m_o[a][...] = mn
            v_o[a][...] = vn

    shapes = [jax.ShapeDtypeStruct(w.shape, F32) for w in ws]
    out = pl.pallas_call(body, name="adamw_small", out_shape=shapes * 3)(*ws, *gs, *ms, *vs)
    return out[:n], out[n:2 * n], out[2 * n:]


WEIGHTS = ["meta_tokens", "norm_ffn1_pre", "norm_ffn1_post", "ffn1_w_gu", "ffn1_w_down", "norm_mix_pre",
           "norm_mix_post", "w_in", "pool_w", "pool_scale", "w_pool_o", "q_a_norm", "w_q_b", "kv_a_norm", "w_kv_b",
           "w_mla_o", "w_out", "norm_ffn2_pre", "norm_ffn2_post", "ffn2_w_gu", "ffn2_w_down"]
BIG = ["ffn1_w_gu", "ffn1_w_down", "w_in", "pool_w", "w_pool_o", "w_q_b", "w_kv_b", "w_mla_o", "w_out",
       "ffn2_w_gu", "ffn2_w_down"]
COL_SHARDED = ("ffn1_w_gu", "ffn2_w_gu", "w_in", "w_pool_o", "w_q_b", "w_kv_b")
GAINS = ["norm_ffn1_pre", "norm_ffn1_post", "norm_mix_pre", "norm_mix_post", "norm_ffn2_pre", "norm_ffn2_post"]
SMALL = GAINS + ["pool_scale", "q_a_norm", "kv_a_norm"]
Z_SRC = 1024 + 512 + 512 + QK_ROPE


def _full_from_gathered(name, g):
    _, r, c = g.shape
    if name == "pool_w":
        ng = len(POOL_WINDOWS)
        return g.reshape(N_DEV, ng, r // ng, c).transpose(1, 0, 2, 3).reshape(ng, POOL_GROUP, POOL_GROUP)
    if name in COL_SHARDED:
        return g.transpose(1, 0, 2).reshape(r, N_DEV * c)
    return g.reshape(N_DEV * r, c)


def _blocks_from_full(name, dw):
    if name == "pool_w":
        ng = len(POOL_WINDOWS)
        return dw.reshape(ng, N_DEV, POOL_GROUP // N_DEV, POOL_GROUP).transpose(1, 0, 2, 3).reshape(
            N_DEV, ng * POOL_GROUP // N_DEV, POOL_GROUP)
    k, n = dw.shape
    if name in COL_SHARDED:
        return dw.reshape(k, N_DEV, n // N_DEV).transpose(1, 0, 2)
    return dw.reshape(N_DEV, k // N_DEV, n)


def _to_internal(name, w):
    if name.endswith("w_gu"):
        d, f2 = w.shape
        return w.reshape(d, 2, f2 // (2 * GU_TILE), GU_TILE).transpose(0, 2, 1, 3).reshape(d, f2)
    if name == "w_in":
        d = w.shape[0]
        return jnp.concatenate([w[:, :Z_SRC - QK_ROPE], w[:, Z_SRC:], w[:, Z_SRC - QK_ROPE:Z_SRC],
                                jnp.zeros((d, Z_COLS - Z_KR - QK_ROPE), w.dtype)], axis=1)
    if name == "w_q_b":
        r = w.shape[0]
        w3 = w.reshape(r, HEADS, QK_NOPE + QK_ROPE)
        return jnp.pad(w3, ((0, 0), (0, 0), (0, HEAD_W - QK_NOPE - QK_ROPE))).reshape(r, Q_COLS)
    return w


def _from_internal(name, dw):
    if name.endswith("w_gu"):
        d, f2 = dw.shape
        return dw.reshape(d, f2 // (2 * GU_TILE), 2, GU_TILE).transpose(0, 2, 1, 3).reshape(d, f2)
    if name == "w_in":
        return jnp.concatenate([dw[:, :Z_SRC - QK_ROPE], dw[:, Z_KR:Z_KR + QK_ROPE], dw[:, Z_SRC - QK_ROPE:Z_KR]], axis=1)
    if name == "w_q_b":
        r = dw.shape[0]
        return dw.reshape(r, HEADS, HEAD_W)[:, :, :QK_NOPE + QK_ROPE].reshape(r, HEADS * (QK_NOPE + QK_ROPE))
    return dw


def _shard2d(a):
    return a.reshape(-1, a.shape[-1])


def kernel(x, meta_tokens, norm_ffn1_pre, norm_ffn1_post, ffn1_w_gu, ffn1_w_down, norm_mix_pre, norm_mix_post, w_in, pool_w, pool_scale, w_pool_o, q_a_norm, w_q_b, kv_a_norm, w_kv_b, w_mla_o, w_out, norm_ffn2_pre, norm_ffn2_post, ffn2_w_gu, ffn2_w_down, loss_target, m_meta_tokens, m_norm_ffn1_pre, m_norm_ffn1_post, m_ffn1_w_gu, m_ffn1_w_down, m_norm_mix_pre, m_norm_mix_post, m_w_in, m_pool_w, m_pool_scale, m_w_pool_o, m_q_a_norm, m_w_q_b, m_kv_a_norm, m_w_kv_b, m_w_mla_o, m_w_out, m_norm_ffn2_pre, m_norm_ffn2_post, m_ffn2_w_gu, m_ffn2_w_down, v_meta_tokens, v_norm_ffn1_pre, v_norm_ffn1_post, v_ffn1_w_gu, v_ffn1_w_down, v_norm_mix_pre, v_norm_mix_post, v_w_in, v_pool_w, v_pool_scale, v_w_pool_o, v_q_a_norm, v_w_q_b, v_kv_a_norm, v_w_kv_b, v_w_mla_o, v_w_out, v_norm_ffn2_pre, v_norm_ffn2_post, v_ffn2_w_gu, v_ffn2_w_down):
    given = dict(locals())
    w_in_dev = {n: given[n] for n in WEIGHTS}
    m_in = {n: given["m_" + n] for n in WEIGHTS}
    v_in = {n: given["v_" + n] for n in WEIGHTS}
    xi, yi, ci = _my_pos()
    dev = 4 * xi + 2 * yi + ci
    core = jnp.reshape(ci, (1,)).astype(jnp.int32)
    chip = jnp.reshape(2 * xi + yi, (1,)).astype(jnp.int32)
    d = x.shape[-1]

    shards = {n: _shard2d(w_in_dev[n]) for n in BIG}
    pack = jnp.concatenate([shards[n].astype(BF16).reshape(-1, LANE) for n in BIG], axis=0)
    pack_g, meta_g = all_gather("ag_weights", [pack, meta_tokens])
    wfull, off = {}, 0
    for n in BIG:
        r, c = shards[n].shape
        rows = r * c // LANE
        wfull[n] = _to_internal(n, _full_from_gathered(n, pack_g[:, off:off + rows].reshape(N_DEV, r, c)))
        off += rows
    meta_full = meta_g.transpose(1, 0, 2).reshape(N_META, d)

    gain = {n: given[n] for n in SMALL}
    loss_blk, grad_x, front, gbig, gsmall = local_step(x[0], loss_target[0], meta_full, wfull, gain)

    blocks = [_blocks_from_full(n, _from_internal(n, gbig[n]).astype(BF16)) for n in BIG]
    from_sibling = rs_sibling("rs_sibling", blocks)
    sums = [rs_add("rs_add_" + n, b, r, core) for n, b, r in zip(BIG, blocks, from_sibling, strict=True)]
    from_chips = rs_chips("rs_chips", sums)

    out_g, out_d, out_m, out_v = {}, {}, {}, {}
    for n, s, r in zip(BIG, sums, from_chips, strict=True):
        shp = w_in_dev[n].shape
        res = adamw_shard("adamw_" + n, shards[n], _shard2d(m_in[n]), _shard2d(v_in[n]), s, r, chip)
        out_g[n], out_d[n], out_m[n], out_v[n] = [t.reshape(shp) for t in res]

    tail = jnp.concatenate([gsmall["pool_scale"], gsmall["q_a_norm"], gsmall["kv_a_norm"]], axis=1)
    small = jnp.concatenate([gsmall[n] for n in GAINS] + [tail, jnp.broadcast_to(loss_blk[:1, :1], (1, d)),
                                                         front[PAD_ROWS:]], axis=0)
    (small_g,) = all_gather("ag_small", [small])
    total = reduce_small(small_g)
    ng = len(GAINS)
    for i, n in enumerate(GAINS):
        out_g[n] = total[i:i + 1]
    o = 0
    for n in ("pool_scale", "q_a_norm", "kv_a_norm"):
        wdt = w_in_dev[n].shape[1]
        out_g[n] = total[ng:ng + 1, o:o + wdt]
        o += wdt
    loss = total[ng + 1, 0]
    mcols = meta_tokens.shape[1]
    out_g["meta_tokens"] = lax.dynamic_slice(total[ng + 2:ng + 2 + N_META], (0, dev * mcols), (N_META, mcols))
    names = ["meta_tokens"] + SMALL
    ds_, ms_, vs_ = adamw_small([w_in_dev[n] for n in names], [out_g[n] for n in names],
                                [m_in[n] for n in names], [v_in[n] for n in names])
    for n, dd, mm, vv in zip(names, ds_, ms_, vs_, strict=True):
        out_d[n], out_m[n], out_v[n] = dd, mm, vv

    return (loss, grad_x[None], *[out_g[n] for n in WEIGHTS], *[out_d[n] for n in WEIGHTS],
            *[out_m[n] for n in WEIGHTS], *[out_v[n] for n in WEIGHTS])


def local_step(x, target, meta_full, wfull, gain):
    d = x.shape[-1]
    h0 = jnp.concatenate([jnp.zeros((PAD_ROWS, d), F32), meta_full, x], axis=0)
    lp = h0.shape[0]
    tabs = rope_tables(lp)
    a1 = prenorm(h0, gain["norm_ffn1_pre"])
    gu1, s1, f1 = ffn_fwd("ffn1", a1, wfull["ffn1_w_gu"], wfull["ffn1_w_down"])
    h1, a2 = post_pre("post_pre1", f1, h0, gain["norm_ffn1_post"], 0.5, gain["norm_mix_pre"])
    mix, sv = mixer_fwd(a2, wfull, tabs, gain["pool_scale"], gain["q_a_norm"], gain["kv_a_norm"])
    h2, a3 = post_pre("post_pre2", mix, h1, gain["norm_mix_post"], 1.0, gain["norm_ffn2_pre"])
    gu2, s2, f2 = ffn_fwd("ffn2", a3, wfull["ffn2_w_gu"], wfull["ffn2_w_down"])
    dh3, loss_blk = post_loss(f2, h2, gain["norm_ffn2_post"], 0.5, target)

    gsmall, gbig = {}, {}
    df2, gsmall["norm_ffn2_post"] = post_bwd(dh3, f2, gain["norm_ffn2_post"], 0.5)
    da3, gbig["ffn2_w_gu"], gbig["ffn2_w_down"] = ffn_bwd("ffn2", df2, a3, gu2, s2, wfull["ffn2_w_gu"], wfull["ffn2_w_down"])
    dh2, dmix, gsmall["norm_ffn2_pre"], gsmall["norm_mix_post"] = pre_post_bwd(
        "pre_post_bwd2", da3, h2, gain["norm_ffn2_pre"], dh3, mix, gain["norm_mix_post"], 1.0)
    da2, gmix, gmix_small = mixer_bwd(dmix, a2, sv, wfull, tabs, gain["pool_scale"], gain["q_a_norm"], gain["kv_a_norm"])
    gbig.update(gmix)
    gsmall.update(gmix_small)
    dh1, df1, gsmall["norm_mix_pre"], gsmall["norm_ffn1_post"] = pre_post_bwd(
        "pre_post_bwd1", da2, h1, gain["norm_mix_pre"], dh2, f1, gain["norm_ffn1_post"], 0.5)
    da1, gbig["ffn1_w_gu"], gbig["ffn1_w_down"] = ffn_bwd("ffn1", df1, a1, gu1, s1, wfull["ffn1_w_gu"], wfull["ffn1_w_down"])
    grad_x, front, gsmall["norm_ffn1_pre"] = pre_bwd_first(da1, h0, gain["norm_ffn1_pre"], dh1)
    return loss_blk, grad_x, front, gbig, gsmall
```

```python
import functools

import jax
import jax.numpy as jnp
import numpy as np
from jax import lax
from jax.experimental import pallas as pl
from jax.experimental.pallas import tpu as pltpu

F32 = jnp.float32
BF16 = jnp.bfloat16

N_META = 16
POOL_WINDOWS = (2, 4, 8, 16)
POOL_GROUP = 256
HEADS = 16
QK_NOPE = 128
QK_ROPE = 64
V_DIM = 128
ROPE_THETA = 10000.0
SOFTMAX_SCALE = (QK_NOPE + QK_ROPE) ** -0.5
EPS = 1e-6
ADAM_LR = 0.001
ADAM_B1 = 0.9
ADAM_B2 = 0.999
ADAM_EPS = 1e-08
ADAM_WD = 0.01
ADAM_STEP = 10

LANE = 128
FRONT = 128
PAD_ROWS = FRONT - N_META
HEAD_W = 256
GU_TILE = 1408
VMEM_LIMIT = 56 * 1024 * 1024
MESH_AXES = ("x", "y", "c")
N_DEV = 8


def _pick(n, cands):
    for c in cands:
        if n % c == 0:
            return c
    raise ValueError(f"no tile for {n} in {cands}")


def _cparams(sem=None):
    kw = dict(vmem_limit_bytes=VMEM_LIMIT)
    if sem is not None:
        kw["dimension_semantics"] = sem
    return pltpu.CompilerParams(**kw)


_DIMS = {"nn": (((1,), (0,)), ((), ())), "nt": (((1,), (1,)), ((), ())), "tn": (((0,), (0,)), ((), ()))}


def _mm(name, form, a, b, tm, tn, outs, epi=None, extras=(), n_outer=False):
    if form == "tn":
        k, m = a.shape
        n = b.shape[1]
        a_blk, a_map = (k, tm), lambda i, j: (0, i)
        b_blk, b_map = (k, tn), lambda i, j: (0, j)
    elif form == "nn":
        m, k = a.shape
        n = b.shape[1]
        a_blk, a_map = (tm, k), lambda i, j: (i, 0)
        b_blk, b_map = (k, tn), lambda i, j: (0, j)
    else:
        m, k = a.shape
        n = b.shape[0]
        a_blk, a_map = (tm, k), lambda i, j: (i, 0)
        b_blk, b_map = (tn, k), lambda i, j: (j, 0)
    assert m % tm == 0 and n % tn == 0, (name, m, n, tm, tn)
    n_ex = len(extras)
    dn = _DIMS[form]
    if n_outer:
        grid = (n // tn, m // tm)

        def spec(blk, im):
            return pl.BlockSpec(blk, lambda gj, gi: im(gi, gj))
    else:
        grid = (m // tm, n // tn)
        spec = pl.BlockSpec

    def body(a_ref, b_ref, *rest):
        ex, out_refs = rest[:n_ex], rest[n_ex:]
        acc = lax.dot_general(a_ref[...].astype(BF16), b_ref[...].astype(BF16), dn, preferred_element_type=F32)
        res = epi(acc, *[e[...] for e in ex]) if epi is not None else (acc,)
        for r, o in zip(res, out_refs, strict=True):
            o[...] = r.astype(o.dtype)

    return pl.pallas_call(
        body,
        name=name,
        grid=grid,
        in_specs=[spec(a_blk, a_map), spec(b_blk, b_map)] + [spec(blk, im) for _, blk, im in extras],
        out_specs=[spec(blk, im) for _, _, blk, im in outs],
        out_shape=[jax.ShapeDtypeStruct(s, d) for s, d, _, _ in outs],
        compiler_params=_cparams(("parallel", "parallel")),
    )(a, b, *[e for e, _, _ in extras])


def _mm_plain(name, form, a, b, tm, tn, out_dtype):
    m = a.shape[1] if form == "tn" else a.shape[0]
    n = b.shape[0] if form == "nt" else b.shape[1]
    return _mm(name, form, a, b, tm, tn, [((m, n), out_dtype, (tm, tn), lambda i, j: (i, j))])[0]


def _rstd(x):
    return lax.rsqrt(jnp.mean(x * x, axis=-1, keepdims=True) + EPS)


def _norm_bwd(dy, x, gain):
    r = _rstd(x)
    dyg = dy * gain
    dx = r * (dyg - x * (r * r) * jnp.mean(dyg * x, axis=-1, keepdims=True))
    dgain = jnp.sum(dy * x * r, axis=0, keepdims=True)
    return dx, dgain


def _row_spec(tr, cols, col_block=0):
    return pl.BlockSpec((tr, cols), lambda i: (i, col_block))


def _vec_spec(cols, col_block=0):
    return pl.BlockSpec((1, cols), lambda i: (0, col_block))


def _row_tile(lp):
    return _pick(lp, (128,))


def prenorm(h, gain):
    lp, d = h.shape
    tr = _row_tile(lp)

    def body(h_ref, g_ref, a_ref):
        x = h_ref[...]
        a_ref[...] = (x * _rstd(x) * g_ref[...]).astype(BF16)

    return pl.pallas_call(
        body, name="prenorm", grid=(lp // tr,),
        in_specs=[_row_spec(tr, d), _vec_spec(d)], out_specs=_row_spec(tr, d),
        out_shape=jax.ShapeDtypeStruct((lp, d), BF16), compiler_params=_cparams(("parallel",)),
    )(h, gain)


def post_pre(name, f, h_in, g_post, coef, g_next):
    lp, d = f.shape
    tr = _row_tile(lp)

    def body(f_ref, h_ref, gp_ref, gn_ref, ho_ref, a_ref):
        fv = f_ref[...]
        ho = h_ref[...] + coef * (fv * _rstd(fv) * gp_ref[...])
        ho_ref[...] = ho
        a_ref[...] = (ho * _rstd(ho) * gn_ref[...]).astype(BF16)

    return pl.pallas_call(
        body, name=name, grid=(lp // tr,),
        in_specs=[_row_spec(tr, d), _row_spec(tr, d), _vec_spec(d), _vec_spec(d)],
        out_specs=[_row_spec(tr, d), _row_spec(tr, d)],
        out_shape=[jax.ShapeDtypeStruct((lp, d), F32), jax.ShapeDtypeStruct((lp, d), BF16)],
        compiler_params=_cparams(("parallel",)),
    )(f, h_in, g_post, g_next)


def post_loss(f, h_in, g_post, coef, target):
    lp, d = f.shape
    tr = _row_tile(lp)
    front_tiles = FRONT // tr

    def body(f_ref, h_ref, gp_ref, t_ref, dh_ref, loss_ref):
        i = pl.program_id(0)

        @pl.when(i == 0)
        def _():
            loss_ref[...] = jnp.zeros_like(loss_ref)

        @pl.when(i < front_tiles)
        def _():
            dh_ref[...] = jnp.zeros_like(dh_ref)

        @pl.when(i >= front_tiles)
        def _():
            fv = f_ref[...]
            ho = h_ref[...] + coef * (fv * _rstd(fv) * gp_ref[...])
            err = ho - t_ref[...]
            dh_ref[...] = err / d
            tok = jnp.mean(err * err, axis=-1, keepdims=True)
            loss_ref[...] += 0.5 * jnp.sum(tok)

    return pl.pallas_call(
        body, name="post_loss", grid=(lp // tr,),
        in_specs=[_row_spec(tr, d), _row_spec(tr, d), _vec_spec(d),
                  pl.BlockSpec((tr, d), lambda i: (jnp.maximum(i - front_tiles, 0), 0))],
        out_specs=[_row_spec(tr, d), pl.BlockSpec((8, LANE), lambda i: (0, 0))],
        out_shape=[jax.ShapeDtypeStruct((lp, d), F32), jax.ShapeDtypeStruct((8, LANE), F32)],
        compiler_params=_cparams(("arbitrary",)),
    )(f, h_in, g_post, target)


def post_bwd(dh_out, f, g_post, coef):
    lp, d = f.shape
    tr = _row_tile(lp)

    def body(dh_ref, f_ref, gp_ref, df_ref, dg_ref):
        @pl.when(pl.program_id(0) == 0)
        def _():
            dg_ref[...] = jnp.zeros_like(dg_ref)

        df, dg = _norm_bwd(coef * dh_ref[...], f_ref[...], gp_ref[...])
        df_ref[...] = df.astype(BF16)
        dg_ref[...] += dg

    return pl.pallas_call(
        body, name="post_bwd", grid=(lp // tr,),
        in_specs=[_row_spec(tr, d), _row_spec(tr, d), _vec_spec(d)],
        out_specs=[_row_spec(tr, d), _vec_spec(d)],
        out_shape=[jax.ShapeDtypeStruct((lp, d), BF16), jax.ShapeDtypeStruct((1, d), F32)],
        compiler_params=_cparams(("arbitrary",)),
    )(dh_out, f, g_post)


def pre_post_bwd(name, da, h_mid, g_pre, dh_out, f_prev, g_post_prev, coef_prev):
    lp, d = da.shape
    tr = _row_tile(lp)

    def body(da_ref, h_ref, gpre_ref, dho_ref, f_ref, gpost_ref, dh_ref, df_ref, dgpre_ref, dgpost_ref):
        @pl.when(pl.program_id(0) == 0)
        def _():
            dgpre_ref[...] = jnp.zeros_like(dgpre_ref)
            dgpost_ref[...] = jnp.zeros_like(dgpost_ref)

        dx, dgpre = _norm_bwd(da_ref[...], h_ref[...], gpre_ref[...])
        dh = dho_ref[...] + dx
        dh_ref[...] = dh
        dgpre_ref[...] += dgpre
        df, dgpost = _norm_bwd(coef_prev * dh, f_ref[...], gpost_ref[...])
        df_ref[...] = df.astype(BF16)
        dgpost_ref[...] += dgpost

    return pl.pallas_call(
        body, name=name, grid=(lp // tr,),
        in_specs=[_row_spec(tr, d), _row_spec(tr, d), _vec_spec(d), _row_spec(tr, d), _row_spec(tr, d), _vec_spec(d)],
        out_specs=[_row_spec(tr, d), _row_spec(tr, d), _vec_spec(d), _vec_spec(d)],
        out_shape=[jax.ShapeDtypeStruct((lp, d), F32), jax.ShapeDtypeStruct((lp, d), BF16),
                   jax.ShapeDtypeStruct((1, d), F32), jax.ShapeDtypeStruct((1, d), F32)],
        compiler_params=_cparams(("arbitrary",)),
    )(da, h_mid, g_pre, dh_out, f_prev, g_post_prev)


def pre_bwd_first(da, h0, g_pre, dh_out):
    lp, d = da.shape
    tr = _row_tile(lp)
    front_tiles = FRONT // tr
    assert front_tiles == 1

    def body(da_ref, h_ref, gpre_ref, dho_ref, gx_ref, front_ref, dgpre_ref):
        i = pl.program_id(0)

        @pl.when(i == 0)
        def _():
            dgpre_ref[...] = jnp.zeros_like(dgpre_ref)

        dx, dgpre = _norm_bwd(da_ref[...], h_ref[...], gpre_ref[...])
        dh = dho_ref[...] + dx
        dgpre_ref[...] += dgpre
        gx_ref[...] = dh

        @pl.when(i == 0)
        def _():
            front_ref[...] = dh

    return pl.pallas_call(
        body, name="pre_bwd_first", grid=(lp // tr,),
        in_specs=[_row_spec(tr, d), _row_spec(tr, d), _vec_spec(d), _row_spec(tr, d)],
        out_specs=[pl.BlockSpec((tr, d), lambda i: (jnp.maximum(i - front_tiles, 0), 0)),
                   pl.BlockSpec((tr, d), lambda i: (0, 0)), _vec_spec(d)],
        out_shape=[jax.ShapeDtypeStruct((lp - FRONT, d), F32), jax.ShapeDtypeStruct((tr, d), F32),
                   jax.ShapeDtypeStruct((1, d), F32)],
        compiler_params=_cparams(("arbitrary",)),
    )(da, h0, g_pre, dh_out)


def _m_tile(lp):
    return _pick(lp, (1056, 512, 256, 128))


def ffn_fwd(tag, a, wgu_p, wd):
    lp, d = a.shape
    f2 = wgu_p.shape[1]
    tm = _m_tile(lp)

    def epi(acc):
        g, u = acc[:, :GU_TILE], acc[:, GU_TILE:]
        return acc, g * jax.nn.sigmoid(g) * u

    tg = _pick(lp, (384, 256, 128))
    gu, s = _mm(tag + "_gu", "nn", a, wgu_p, tg, 2 * GU_TILE,
                [((lp, f2), F32, (tg, 2 * GU_TILE), lambda i, j: (i, j)),
                 ((lp, f2 // 2), BF16, (tg, GU_TILE), lambda i, j: (i, j))], epi=epi, n_outer=True)
    f = _mm_plain(tag + "_down", "nn", s, wd, tm, 512, F32)
    return gu, s, f


def ffn_bwd(tag, df, a, gu, s, wgu_p, wd):
    lp, d = df.shape
    f2 = wgu_p.shape[1]
    tm = _m_tile(lp)

    def epi(acc, gu_t):
        g, u = gu_t[:, :GU_TILE], gu_t[:, GU_TILE:]
        sig = jax.nn.sigmoid(g)
        dg = acc * u * (sig * (1.0 + g * (1.0 - sig)))
        du = acc * (g * sig)
        return (jnp.concatenate([dg, du], axis=1),)

    ts = _pick(lp, (528, 256, 128))
    dgu = _mm(tag + "_ds", "nt", df, wd, ts, GU_TILE,
              [((lp, f2), BF16, (ts, 2 * GU_TILE), lambda i, j: (i, j))], epi=epi,
              extras=[(gu, (ts, 2 * GU_TILE), lambda i, j: (i, j))], n_outer=True)[0]
    dwd = _mm_plain(tag + "_dwd", "tn", s, df, 512, _pick(d, (1024,)), BF16)
    da = _mm_plain(tag + "_da", "nt", dgu, wgu_p, _pick(lp, (528, 256, 128)), 256, F32)
    dwgu = _mm_plain(tag + "_dwgu", "tn", a, dgu, _pick(d, (1024,)), 1024, BF16)
    return da, dwgu, dwd


Z_U, Z_CQ, Z_CKV, Z_GP, Z_GM, Z_KR, Z_COLS = 0, 1024, 1536, 2048, 4096, 6144, 6400
POOL_W = POOL_GROUP * len(POOL_WINDOWS)
HALO = 16


def _pool_counts(lp, w):
    pos = lax.broadcasted_iota(jnp.int32, (lp, 1), 0) - PAD_ROWS
    return jnp.clip(pos + 1, 1, w).astype(F32)


def _pool_diff(u_ref, pad_ref, lp, w):
    pad_ref[pl.ds(0, HALO), :] = jnp.zeros((HALO, POOL_GROUP), F32)
    pad_ref[pl.ds(HALO, lp), :] = u_ref[...]
    acc = pad_ref[pl.ds(HALO, lp), :]
    for s in range(1, w):
        acc = acc + pad_ref[pl.ds(HALO - s, lp), :]
    return acc / _pool_counts(lp, w) - u_ref[...]


def pool_fwd(z, pool_w, pool_scale):
    lp = z.shape[0]
    ng = len(POOL_WINDOWS)

    def body(u_ref, w_ref, sc_ref, o_ref, pad_ref):
        for g, w in enumerate(POOL_WINDOWS):
            @pl.when(pl.program_id(0) == g)
            def _(w=w):
                dd = _pool_diff(u_ref, pad_ref, lp, w)
                y = jnp.dot(dd.astype(BF16), w_ref[0], preferred_element_type=F32)
                o_ref[...] = (y * sc_ref[...]).astype(BF16)

    return pl.pallas_call(
        body, name="pool_fwd", grid=(ng,),
        in_specs=[pl.BlockSpec((lp, POOL_GROUP), lambda g: (0, g)),
                  pl.BlockSpec((1, POOL_GROUP, POOL_GROUP), lambda g: (g, 0, 0)),
                  pl.BlockSpec((1, POOL_GROUP), lambda g: (0, g))],
        out_specs=pl.BlockSpec((lp, POOL_GROUP), lambda g: (0, g)),
        out_shape=jax.ShapeDtypeStruct((lp, POOL_W), BF16),
        scratch_shapes=[pltpu.VMEM((lp + HALO, POOL_GROUP), F32)],
        compiler_params=_cparams(("parallel",)),
    )(z, pool_w, pool_scale)


def pool_bwd(dyp, z, pool_w, pool_scale):
    lp = z.shape[0]
    ng = len(POOL_WINDOWS)

    def body(dy_ref, u_ref, w_ref, sc_ref, du_ref, dw_ref, dsc_ref, pad_ref):
        for g, w in enumerate(POOL_WINDOWS):
            @pl.when(pl.program_id(0) == g)
            def _(w=w):
                dd = _pool_diff(u_ref, pad_ref, lp, w).astype(BF16)
                wg = w_ref[0]
                ypre = jnp.dot(dd, wg, preferred_element_type=F32)
                dy = dy_ref[...]
                dsc_ref[...] = jnp.sum(dy * ypre, axis=0, keepdims=True)
                dypre = (dy * sc_ref[...]).astype(BF16)
                dw_ref[0] = lax.dot_general(dd, dypre, _DIMS["tn"], preferred_element_type=F32)
                ddd = lax.dot_general(dypre, wg, _DIMS["nt"], preferred_element_type=F32)
                pad_ref[pl.ds(0, lp), :] = ddd / _pool_counts(lp, w)
                pad_ref[pl.ds(lp, HALO), :] = jnp.zeros((HALO, POOL_GROUP), F32)
                acc = -ddd
                for s in range(w):
                    acc = acc + pad_ref[pl.ds(s, lp), :]
                du_ref[...] = acc.astype(BF16)

    return pl.pallas_call(
        body, name="pool_bwd", grid=(ng,),
        in_specs=[pl.BlockSpec((lp, POOL_GROUP), lambda g: (0, g)),
                  pl.BlockSpec((lp, POOL_GROUP), lambda g: (0, g)),
                  pl.BlockSpec((1, POOL_GROUP, POOL_GROUP), lambda g: (g, 0, 0)),
                  pl.BlockSpec((1, POOL_GROUP), lambda g: (0, g))],
        out_specs=[pl.BlockSpec((lp, POOL_GROUP), lambda g: (0, g)),
                   pl.BlockSpec((1, POOL_GROUP, POOL_GROUP), lambda g: (g, 0, 0)),
                   pl.BlockSpec((1, POOL_GROUP), lambda g: (0, g))],
        out_shape=[jax.ShapeDtypeStruct((lp, POOL_W), BF16),
                   jax.ShapeDtypeStruct((ng, POOL_GROUP, POOL_GROUP), F32),
                   jax.ShapeDtypeStruct((1, POOL_W), F32)],
        scratch_shapes=[pltpu.VMEM((lp + HALO, POOL_GROUP), F32)],
        compiler_params=_cparams(("parallel",)),
    )(dyp, z, pool_w, pool_scale)


Q_COLS = HEADS * HEAD_W
ROPE_BLOCK = LANE


def rope_tables(lp):
    pos = jnp.maximum(jnp.arange(lp, dtype=F32) - PAD_ROWS, 0.0)
    inv = ROPE_THETA ** (-jnp.arange(0, QK_ROPE, 2, dtype=F32) / QK_ROPE)
    ang = pos[:, None] * inv[None, :]
    cos, sin, zero = jnp.cos(ang), jnp.sin(ang), jnp.zeros_like(ang)
    return jnp.stack([jnp.concatenate([cos, cos, zero, zero], axis=1),
                      jnp.concatenate([-sin, zero, zero, zero], axis=1),
                      jnp.concatenate([zero, sin, zero, zero], axis=1)])


def _rope(x, tabs):
    return x * tabs[0] + pltpu.roll(x, 96, 1) * tabs[1] + pltpu.roll(x, 32, 1) * tabs[2]


def _rope_bwd(g, tabs):
    return g * tabs[0] + pltpu.roll(g * tabs[1], 32, 1) + pltpu.roll(g * tabs[2], 96, 1)


def _tab_spec(tr):
    return pl.BlockSpec((3, tr, ROPE_BLOCK), lambda i: (0, i, 0))


def mla_prep(z, g_q, g_kv, tabs):
    lp = z.shape[0]
    tr = _row_tile(lp)
    r = g_q.shape[1]

    def body(cq_ref, ckv_ref, kr_ref, gq_ref, gkv_ref, tab_ref, qn_ref, kvn_ref, kpe_ref):
        cq, ckv = cq_ref[...], ckv_ref[...]
        qn_ref[...] = (cq * _rstd(cq) * gq_ref[...]).astype(BF16)
        kvn_ref[...] = (ckv * _rstd(ckv) * gkv_ref[...]).astype(BF16)
        kpe_ref[...] = _rope(kr_ref[...], tab_ref[...]).astype(BF16)

    return pl.pallas_call(
        body, name="mla_prep", grid=(lp // tr,),
        in_specs=[_row_spec(tr, r, Z_CQ // r), _row_spec(tr, r, Z_CKV // r), _row_spec(tr, ROPE_BLOCK, Z_KR // ROPE_BLOCK),
                  _vec_spec(r), _vec_spec(r), _tab_spec(tr)],
        out_specs=[_row_spec(tr, r), _row_spec(tr, r), _row_spec(tr, ROPE_BLOCK)],
        out_shape=[jax.ShapeDtypeStruct((lp, r), BF16), jax.ShapeDtypeStruct((lp, r), BF16),
                   jax.ShapeDtypeStruct((lp, ROPE_BLOCK), BF16)],
        compiler_params=_cparams(("parallel",)),
    )(z, z, z, g_q, g_kv, tabs)


def q_proj(qn, wq_p, tabs):
    lp = qn.shape[0]
    tm = _m_tile(lp)
    tn = 4 * HEAD_W

    def epi(acc, tab):
        parts = []
        for t in range(tn // HEAD_W):
            parts.append(acc[:, t * HEAD_W:t * HEAD_W + QK_NOPE])
            parts.append(_rope(acc[:, t * HEAD_W + QK_NOPE:(t + 1) * HEAD_W], tab))
        return (jnp.concatenate(parts, axis=1),)

    return _mm("q_proj", "nn", qn, wq_p, tm, tn, [((lp, Q_COLS), BF16, (tm, tn), lambda i, j: (i, j))], epi=epi,
               extras=[(tabs, (3, tm, ROPE_BLOCK), lambda i, j: (0, i, 0))])[0]


def kv_proj(kvn, wkv):
    return _mm_plain("kv_proj", "nn", kvn, wkv, _m_tile(kvn.shape[0]), 1024, BF16)


def q_rope_bwd(dq, tabs):
    lp = dq.shape[0]
    tr = _row_tile(lp)

    def body(dq_ref, tab_ref, o_ref):
        tab = tab_ref[...]
        for h in range(HEADS):
            o_ref[:, h * HEAD_W:h * HEAD_W + QK_NOPE] = dq_ref[:, h * HEAD_W:h * HEAD_W + QK_NOPE].astype(BF16)
            o_ref[:, h * HEAD_W + QK_NOPE:(h + 1) * HEAD_W] = _rope_bwd(
                dq_ref[:, h * HEAD_W + QK_NOPE:(h + 1) * HEAD_W], tab).astype(BF16)

    return pl.pallas_call(
        body, name="q_rope_bwd", grid=(lp // tr,),
        in_specs=[_row_spec(tr, Q_COLS), _tab_spec(tr)], out_specs=_row_spec(tr, Q_COLS),
        out_shape=jax.ShapeDtypeStruct((lp, Q_COLS), BF16), compiler_params=_cparams(("parallel",)),
    )(dq, tabs)


def mla_prep_bwd(dqn, dkvn, dkpe_h, z, g_q, g_kv, tabs):
    lp = z.shape[0]
    tr = _row_tile(lp)
    r = g_q.shape[1]

    def body(dqn_ref, dkvn_ref, dkpe_ref, cq_ref, ckv_ref, gq_ref, gkv_ref, tab_ref,
             dcq_ref, dckv_ref, dkr_ref, dgq_ref, dgkv_ref):
        @pl.when(pl.program_id(0) == 0)
        def _():
            dgq_ref[...] = jnp.zeros_like(dgq_ref)
            dgkv_ref[...] = jnp.zeros_like(dgkv_ref)

        dcq, dgq = _norm_bwd(dqn_ref[...], cq_ref[...], gq_ref[...])
        dckv, dgkv = _norm_bwd(dkvn_ref[...], ckv_ref[...], gkv_ref[...])
        dcq_ref[...] = dcq.astype(BF16)
        dckv_ref[...] = dckv.astype(BF16)
        dgq_ref[...] += dgq
        dgkv_ref[...] += dgkv
        dkpe = dkpe_ref[0]
        for h in range(1, HEADS):
            dkpe = dkpe + dkpe_ref[h]
        dkr_ref[...] = _rope_bwd(dkpe, tab_ref[...]).astype(BF16)

    return pl.pallas_call(
        body, name="mla_prep_bwd", grid=(lp // tr,),
        in_specs=[_row_spec(tr, r), _row_spec(tr, r), pl.BlockSpec((HEADS, tr, ROPE_BLOCK), lambda i: (0, i, 0)),
                  _row_spec(tr, r, Z_CQ // r), _row_spec(tr, r, Z_CKV // r), _vec_spec(r), _vec_spec(r), _tab_spec(tr)],
        out_specs=[_row_spec(tr, r), _row_spec(tr, r), _row_spec(tr, ROPE_BLOCK), _vec_spec(r), _vec_spec(r)],
        out_shape=[jax.ShapeDtypeStruct((lp, r), BF16), jax.ShapeDtypeStruct((lp, r), BF16),
                   jax.ShapeDtypeStruct((lp, ROPE_BLOCK), BF16),
                   jax.ShapeDtypeStruct((1, r), F32), jax.ShapeDtypeStruct((1, r), F32)],
        compiler_params=_cparams(("arbitrary",)),
    )(dqn, dkvn, dkpe_h, z, z, g_q, g_kv, tabs)


def _attn_tile(lp):
    return _pick(lp, (528, 128))


def _causal_pairs(nt, k_major):
    if k_major:
        pairs = [(qi, ki) for ki in range(nt) for qi in range(ki, nt)]
    else:
        pairs = [(qi, ki) for qi in range(nt) for ki in range(qi + 1)]
    return (jnp.asarray([p[0] for p in pairs], jnp.int32), jnp.asarray([p[1] for p in pairs], jnp.int32))


def _scores(q, kcat, q_tile, k_tile, t, masked):
    s = lax.dot_general(q, kcat, _DIMS["nt"], preferred_element_type=F32) * SOFTMAX_SCALE
    if not masked:
        return s
    qpos = q_tile * t + lax.broadcasted_iota(jnp.int32, (t, t), 0)
    kpos = k_tile * t + lax.broadcasted_iota(jnp.int32, (t, t), 1)
    return jnp.where((kpos <= qpos) & (kpos >= PAD_ROWS), s, jnp.float32(-1e30))


def _on_masked_or_not(q_tile, k_tile, fn):
    needs_mask = (q_tile == k_tile) | (k_tile == 0)

    @pl.when(needs_mask)
    def _():
        fn(True)

    @pl.when(jnp.logical_not(needs_mask))
    def _():
        fn(False)


def flash_fwd(q, kv, kpe):
    lp = q.shape[0]
    t = _attn_tile(lp)
    q_tab, k_tab = _causal_pairs(lp // t, k_major=False)

    def body(q_tab_ref, k_tab_ref, q_ref, kv_ref, kpe_ref, o32_ref, o16_ref, lse_ref, m_sc, l_sc, acc_sc):
        pair = pl.program_id(1)
        qi, ki = q_tab_ref[pair], k_tab_ref[pair]

        @pl.when(ki == 0)
        def _():
            m_sc[...] = jnp.full_like(m_sc, -jnp.inf)
            l_sc[...] = jnp.zeros_like(l_sc)
            acc_sc[...] = jnp.zeros_like(acc_sc)

        def step(masked):
            kvt = kv_ref[...]
            kcat = jnp.concatenate([kvt[:, :QK_NOPE], kpe_ref[...]], axis=1)
            s = _scores(q_ref[...], kcat, qi, ki, t, masked)
            m_prev = m_sc[...]
            m_new = jnp.maximum(m_prev, jnp.max(s, axis=1, keepdims=True))
            alpha = jnp.exp(m_prev - m_new)
            p = jnp.exp(s - m_new[:, :1])
            l_sc[...] = alpha * l_sc[...] + jnp.sum(p, axis=1, keepdims=True)
            acc_sc[...] = alpha * acc_sc[...] + jnp.dot(p.astype(BF16), kvt[:, QK_NOPE:], preferred_element_type=F32)
            m_sc[...] = m_new

        _on_masked_or_not(qi, ki, step)

        @pl.when(ki == qi)
        def _():
            l = l_sc[...]
            o = acc_sc[...] / l
            o32_ref[...] = o
            o16_ref[...] = o.astype(BF16)
            lse_ref[0] = m_sc[...] + jnp.log(l)

    return pl.pallas_call(
        body, name="flash_fwd",
        grid_spec=pltpu.PrefetchScalarGridSpec(
            num_scalar_prefetch=2, grid=(HEADS, q_tab.shape[0]),
            in_specs=[pl.BlockSpec((t, HEAD_W), lambda h, p, qt, kt: (qt[p], h)),
                      pl.BlockSpec((t, HEAD_W), lambda h, p, qt, kt: (kt[p], h)),
                      pl.BlockSpec((t, ROPE_BLOCK), lambda h, p, qt, kt: (kt[p], 0))],
            out_specs=[pl.BlockSpec((t, V_DIM), lambda h, p, qt, kt: (qt[p], h)),
                       pl.BlockSpec((t, V_DIM), lambda h, p, qt, kt: (qt[p], h)),
                       pl.BlockSpec((1, t, LANE), lambda h, p, qt, kt: (h, qt[p], 0))],
            scratch_shapes=[pltpu.VMEM((t, LANE), F32), pltpu.VMEM((t, LANE), F32), pltpu.VMEM((t, V_DIM), F32)]),
        out_shape=[jax.ShapeDtypeStruct((lp, HEADS * V_DIM), F32), jax.ShapeDtypeStruct((lp, HEADS * V_DIM), BF16),
                   jax.ShapeDtypeStruct((HEADS, lp, LANE), F32)],
        compiler_params=_cparams(("parallel", "arbitrary")),
    )(q_tab, k_tab, q, kv, kpe)


def flash_bwd(q, kv, kpe, o32, lse, do):
    lp = q.shape[0]
    t = _attn_tile(lp)
    nt = lp // t
    q_tab, k_tab = _causal_pairs(nt, k_major=True)

    def body(q_tab_ref, k_tab_ref, q_ref, kv_ref, kpe_ref, o_ref, lse_ref, do_ref, dq_ref, dkv_ref, dkpe_ref,
             dk_sc, dv_sc):
        pair = pl.program_id(1)
        qi, ki = q_tab_ref[pair], k_tab_ref[pair]

        @pl.when(pair == 0)
        def _():
            dq_ref[...] = jnp.zeros_like(dq_ref)

        @pl.when(qi == ki)
        def _():
            dk_sc[...] = jnp.zeros_like(dk_sc)
            dv_sc[...] = jnp.zeros_like(dv_sc)

        def step(masked):
            qt = q_ref[...]
            kvt = kv_ref[...]
            kcat = jnp.concatenate([kvt[:, :QK_NOPE], kpe_ref[...]], axis=1)
            s = _scores(qt, kcat, qi, ki, t, masked)
            p = jnp.exp(s - lse_ref[0][:, :1])
            do = do_ref[...]
            delta = jnp.sum(do * o_ref[...], axis=1, keepdims=True)
            do16 = do.astype(BF16)
            dv_sc[...] += lax.dot_general(p.astype(BF16), do16, _DIMS["tn"], preferred_element_type=F32)
            dp = lax.dot_general(do16, kvt[:, QK_NOPE:], _DIMS["nt"], preferred_element_type=F32)
            ds = (p * (dp - delta) * SOFTMAX_SCALE).astype(BF16)
            dk_sc[...] += lax.dot_general(ds, qt, _DIMS["tn"], preferred_element_type=F32)
            row = pl.multiple_of(qi * t, t)
            dq_ref[pl.ds(row, t), :] += jnp.dot(ds, kcat, preferred_element_type=F32)

        _on_masked_or_not(qi, ki, step)

        @pl.when(qi == nt - 1)
        def _():
            dk = dk_sc[...]
            dkv_ref[...] = jnp.concatenate([dk[:, :QK_NOPE], dv_sc[...]], axis=1).astype(BF16)
            dkpe_ref[0] = dk[:, QK_NOPE:]

    qmap = lambda h, p, qt, kt: (qt[p], h)
    return pl.pallas_call(
        body, name="flash_bwd",
        grid_spec=pltpu.PrefetchScalarGridSpec(
            num_scalar_prefetch=2, grid=(HEADS, q_tab.shape[0]),
            in_specs=[pl.BlockSpec((t, HEAD_W), qmap),
                      pl.BlockSpec((t, HEAD_W), lambda h, p, qt, kt: (kt[p], h)),
                      pl.BlockSpec((t, ROPE_BLOCK), lambda h, p, qt, kt: (kt[p], 0)),
                      pl.BlockSpec((t, V_DIM), qmap),
                      pl.BlockSpec((1, t, LANE), lambda h, p, qt, kt: (h, qt[p], 0)),
                      pl.BlockSpec((t, V_DIM), qmap)],
            out_specs=[pl.BlockSpec((lp, HEAD_W), lambda h, p, qt, kt: (0, h)),
                       pl.BlockSpec((t, HEAD_W), lambda h, p, qt, kt: (kt[p], h)),
                       pl.BlockSpec((1, t, ROPE_BLOCK), lambda h, p, qt, kt: (h, kt[p], 0))],
            scratch_shapes=[pltpu.VMEM((t, HEAD_W), F32), pltpu.VMEM((t, V_DIM), F32)]),
        out_shape=[jax.ShapeDtypeStruct((lp, Q_COLS), F32), jax.ShapeDtypeStruct((lp, Q_COLS), BF16),
                   jax.ShapeDtypeStruct((HEADS, lp, ROPE_BLOCK), F32)],
        compiler_params=_cparams(("parallel", "arbitrary")),
    )(q_tab, k_tab, q, kv, kpe, o32, lse, do)


def _ij(i, j):
    return (i, j)


def mixer_fwd(a2, w, tabs, pool_scale, g_q, g_kv):
    lp, d = a2.shape
    tm = _m_tile(lp)
    tn = 512
    z = _mm_plain("mix_in", "nn", a2, w["w_in"], tm, 1280, F32)
    yp = pool_fwd(z, w["pool_w"], pool_scale)
    qn, kvn, kpe = mla_prep(z, g_q, g_kv, tabs)
    q = q_proj(qn, w["w_q_b"], tabs)
    kv = kv_proj(kvn, w["w_kv_b"])
    o32, o16, lse = flash_fwd(q, kv, kpe)
    y_pool = _mm_plain("pool_out", "nn", yp, w["w_pool_o"], tm, tn, F32)

    def epi(acc, ypl, gp, gm):
        return jax.nn.sigmoid(gp) * ypl + jax.nn.sigmoid(gm) * acc, acc

    y, y_mla = _mm("mla_out_gate", "nn", o16, w["w_mla_o"], tm, tn,
                   [((lp, d), BF16, (tm, tn), _ij), ((lp, d), F32, (tm, tn), _ij)], epi=epi,
                   extras=[(y_pool, (tm, tn), _ij), (z, (tm, tn), lambda i, j: (i, Z_GP // tn + j)),
                           (z, (tm, tn), lambda i, j: (i, Z_GM // tn + j))])
    m = _mm_plain("mix_out", "nn", y, w["w_out"], tm, tn, F32)
    return m, dict(z=z, yp=yp, qn=qn, kvn=kvn, kpe=kpe, q=q, kv=kv, o32=o32, o16=o16, lse=lse,
                   y_pool=y_pool, y_mla=y_mla, y=y)


def mixer_bwd(dm, a2, sv, w, tabs, pool_scale, g_q, g_kv):
    lp, d = dm.shape
    tm = _m_tile(lp)
    tn = 512
    z = sv["z"]

    def epi(acc, ypl, yml, gp, gm):
        sp, sm = jax.nn.sigmoid(gp), jax.nn.sigmoid(gm)
        return acc * sp, acc * sm, acc * ypl * (sp * (1.0 - sp)), acc * yml * (sm * (1.0 - sm))

    dyp, dym, dgp, dgm = _mm(
        "gate_bwd", "nt", dm, w["w_out"], tm, tn, [((lp, d), BF16, (tm, tn), _ij)] * 4, epi=epi,
        extras=[(sv["y_pool"], (tm, tn), _ij), (sv["y_mla"], (tm, tn), _ij),
                (z, (tm, tn), lambda i, j: (i, Z_GP // tn + j)), (z, (tm, tn), lambda i, j: (i, Z_GM // tn + j))])
    g = {}
    g["w_out"] = _mm_plain("dw_out", "tn", sv["y"], dm, 1024, 1024, BF16)
    g["w_pool_o"] = _mm_plain("dw_pool_o", "tn", sv["yp"], dyp, 512, 1024, BF16)
    dypre = _mm_plain("pool_out_bwd", "nt", dyp, w["w_pool_o"], tm, tn, F32)
    du, g["pool_w"], d_pool_scale = pool_bwd(dypre, z, w["pool_w"], pool_scale)
    g["w_mla_o"] = _mm_plain("dw_mla_o", "tn", sv["o16"], dym, 1024, 1024, BF16)
    do = _mm_plain("mla_out_bwd", "nt", dym, w["w_mla_o"], tm, tn, F32)
    dq, dkv, dkpe_h = flash_bwd(sv["q"], sv["kv"], sv["kpe"], sv["o32"], sv["lse"], do)
    dql = q_rope_bwd(dq, tabs)
    g["w_q_b"] = _mm_plain("dw_q_b", "tn", sv["qn"], dql, 512, 1024, BF16)
    dqn = _mm_plain("q_proj_bwd", "nt", dql, w["w_q_b"], tm, 512, F32)
    g["w_kv_b"] = _mm_plain("dw_kv_b", "tn", sv["kvn"], dkv, 512, 1024, BF16)
    dkvn = _mm_plain("kv_proj_bwd", "nt", dkv, w["w_kv_b"], tm, 512, F32)
    dcq, dckv, dkr, d_gq, d_gkv = mla_prep_bwd(dqn, dkvn, dkpe_h, z, g_q, g_kv, tabs)
    dz = jnp.concatenate([du, dcq, dckv, dgp, dgm, dkr, jnp.zeros((lp, Z_COLS - Z_KR - ROPE_BLOCK), BF16)], axis=1)
    g["w_in"] = _mm_plain("dw_in", "tn", a2, dz, 1024, 1280, BF16)
    da2 = _mm_plain("mix_in_bwd", "nt", dz, w["w_in"], tm, tn, F32)
    return da2, g, dict(pool_scale=d_pool_scale, q_a_norm=d_gq, kv_a_norm=d_gkv)


_ANY = pl.BlockSpec(memory_space=pl.ANY)
_MESH = pl.DeviceIdType.MESH


def _my_pos():
    return lax.axis_index("x"), lax.axis_index("y"), lax.axis_index("c")


LEAD = "lead"
COLS = "cols"
COLS_GU = "cols_gu"


def _col_block(layout, dev):
    return dev if layout == COLS else 2 * (dev % 4) + dev // 4


def _dev_block(ref, layout, dev, cols):
    if layout == LEAD:
        return ref.at[dev]
    return ref.at[:, pl.ds(pl.multiple_of(_col_block(layout, dev) * cols, LANE), cols)]


def _gathered_shape(shard_shape, layout):
    if layout == LEAD:
        return (N_DEV, *shard_shape)
    return (shard_shape[0], N_DEV * shard_shape[1])


def all_gather(name, shards, layouts):
    n = len(shards)

    def body(*refs):
        ins, outs = refs[:n], refs[n:2 * n]
        send_sems, recv_sems, local_sems = refs[2 * n:]
        x, y, c = _my_pos()
        me, sibling = (x, y, c), (x, y, 1 - c)
        chips = [(1 - x, y), (x, 1 - y), (1 - x, 1 - y)]

        def blk(a, px, py, pc):
            return _dev_block(outs[a], layouts[a], 4 * px + 2 * py + pc, shards[a].shape[-1])

        def copy(a, k, block, to, src=None):
            return pltpu.make_async_remote_copy(
                src_ref=blk(a, *block) if src is None else src, dst_ref=blk(a, *block),
                send_sem=send_sems.at[a, k], recv_sem=recv_sems.at[a, k], device_id=to, device_id_type=_MESH)

        mine = [pltpu.make_async_copy(ins[a], blk(a, *me), local_sems.at[a]) for a in range(n)]
        for cp in mine:
            cp.start()
        first = []
        for a in range(n):
            first.append(copy(a, 0, me, sibling, src=ins[a]))
            first += [copy(a, 1 + j, me, (*chip, c), src=ins[a]) for j, chip in enumerate(chips)]
        for cp in first:
            cp.start()
        passed = []
        for j, chip in enumerate(chips):
            for a in range(n):
                copy(a, 1 + j, (*chip, c), me).wait_recv()
                fwd = copy(a, 4 + j, (*chip, c), sibling)
                fwd.start()
                passed.append(fwd)
        for a in range(n):
            copy(a, 0, sibling, me).wait_recv()
            for j, chip in enumerate(chips):
                copy(a, 4 + j, (*chip, 1 - c), me).wait_recv()
        for cp in first + passed:
            cp.wait_send()
        for cp in mine:
            cp.wait()

    return pl.pallas_call(
        body, name=name,
        in_specs=[_ANY] * n, out_specs=[_ANY] * n,
        out_shape=[jax.ShapeDtypeStruct(_gathered_shape(s.shape, lay), s.dtype)
                   for s, lay in zip(shards, layouts, strict=True)],
        scratch_shapes=[pltpu.SemaphoreType.DMA((n, 7)), pltpu.SemaphoreType.DMA((n, 7)), pltpu.SemaphoreType.DMA((n,))],
    )(*shards)


def _shard_shape(grad, layout):
    return grad.shape[1:] if layout == LEAD else (grad.shape[0], grad.shape[1] // N_DEV)


def rs_sibling(name, grads, layouts):
    n = len(grads)

    def body(*refs):
        ins, outs = refs[:n], refs[n:2 * n]
        send_sems, recv_sems = refs[2 * n:]
        x, y, c = _my_pos()
        cps = []
        for a in range(n):
            for k in range(4):
                cp = pltpu.make_async_remote_copy(
                    src_ref=_dev_block(ins[a], layouts[a], 2 * k + (1 - c), outs[a].shape[-1]), dst_ref=outs[a].at[k],
                    send_sem=send_sems.at[a, k], recv_sem=recv_sems.at[a, k],
                    device_id=(x, y, 1 - c), device_id_type=_MESH)
                cp.start()
                cps.append(cp)
        for cp in cps:
            cp.wait()

    return pl.pallas_call(
        body, name=name,
        in_specs=[_ANY] * n, out_specs=[_ANY] * n,
        out_shape=[jax.ShapeDtypeStruct((4, *_shard_shape(g, lay)), g.dtype) for g, lay in zip(grads, layouts, strict=True)],
        scratch_shapes=[pltpu.SemaphoreType.DMA((n, 4)), pltpu.SemaphoreType.DMA((n, 4))],
    )(*grads)


def rs_chips(name, sums):
    n = len(sums)

    def body(*refs):
        ins, outs = refs[:n], refs[n:2 * n]
        send_sems, recv_sems = refs[2 * n:]
        x, y, c = _my_pos()
        chips = [(1 - x, y), (x, 1 - y), (1 - x, 1 - y)]
        cps = []
        for a in range(n):
            for j, chip in enumerate(chips):
                cp = pltpu.make_async_remote_copy(
                    src_ref=ins[a].at[2 * chip[0] + chip[1]], dst_ref=outs[a].at[j],
                    send_sem=send_sems.at[a, j], recv_sem=recv_sems.at[a, j],
                    device_id=(*chip, c), device_id_type=_MESH)
                cp.start()
                cps.append(cp)
        for cp in cps:
            cp.wait()

    return pl.pallas_call(
        body, name=name,
        in_specs=[_ANY] * n, out_specs=[_ANY] * n,
        out_shape=[jax.ShapeDtypeStruct((3, *s.shape[1:]), s.dtype) for s in sums],
        scratch_shapes=[pltpu.SemaphoreType.DMA((n, 3)), pltpu.SemaphoreType.DMA((n, 3))],
    )(*sums)


def _ew_rows(r, c):
    for t in (512, 256, 128, 64, 32, 16):
        if r % t == 0 and t * c * 4 <= 768 * 1024:
            return t
    raise ValueError((r, c))


def rs_add(name, grad, layout, recv, core):
    _, r, c = recv.shape
    tr = _ew_rows(r, c)
    if layout == LEAD:
        g_spec = pl.BlockSpec((None, tr, c), lambda k, i, core_ref: (2 * k + core_ref[0], i, 0))
    else:
        g_spec = pl.BlockSpec((tr, c), lambda k, i, core_ref: (i, _col_block(layout, 2 * k + core_ref[0])))

    def body(core_ref, g_ref, r_ref, o_ref):
        o_ref[...] = (g_ref[...].astype(F32) + r_ref[...].astype(F32)).astype(BF16)

    return pl.pallas_call(
        body, name=name,
        grid_spec=pltpu.PrefetchScalarGridSpec(
            num_scalar_prefetch=1, grid=(4, r // tr),
            in_specs=[g_spec, pl.BlockSpec((None, tr, c), lambda k, i, core_ref: (k, i, 0))],
            out_specs=pl.BlockSpec((None, tr, c), lambda k, i, core_ref: (k, i, 0))),
        out_shape=jax.ShapeDtypeStruct((4, r, c), BF16),
        compiler_params=_cparams(("parallel", "parallel")),
    )(core, grad, recv)


def _adamw(w, g, m, v):
    m = ADAM_B1 * m + (1.0 - ADAM_B1) * g
    v = ADAM_B2 * v + (1.0 - ADAM_B2) * jnp.square(g)
    m_hat = m / (1.0 - ADAM_B1 ** ADAM_STEP)
    v_hat = v / (1.0 - ADAM_B2 ** ADAM_STEP)
    delta = -ADAM_LR * (m_hat / (jnp.sqrt(v_hat) + ADAM_EPS) + ADAM_WD * w)
    return delta, m, v


def adamw_shard(name, w, m, v, sums, recv, chip):
    r, c = w.shape
    tr = _ew_rows(r, c)

    def body(chip_ref, w_ref, m_ref, v_ref, s_ref, r_ref, g_ref, d_ref, mo_ref, vo_ref):
        g = s_ref[0].astype(F32)
        for j in range(3):
            g = g + r_ref[j].astype(F32)
        d, mn, vn = _adamw(w_ref[...], g, m_ref[...], v_ref[...])
        g_ref[...] = g
        d_ref[...] = d
        mo_ref[...] = mn
        vo_ref[...] = vn

    spec = pl.BlockSpec((tr, c), lambda i, chip_ref: (i, 0))
    return pl.pallas_call(
        body, name=name,
        grid_spec=pltpu.PrefetchScalarGridSpec(
            num_scalar_prefetch=1, grid=(r // tr,),
            in_specs=[spec, spec, spec,
                      pl.BlockSpec((1, tr, c), lambda i, chip_ref: (chip_ref[0], i, 0)),
                      pl.BlockSpec((3, tr, c), lambda i, chip_ref: (0, i, 0))],
            out_specs=[spec] * 4),
        out_shape=[jax.ShapeDtypeStruct((r, c), F32)] * 4,
        compiler_params=_cparams(("parallel",)),
    )(chip, w, m, v, sums, recv)


def reduce_small(gathered):
    _, r, c = gathered.shape

    def body(g_ref, o_ref):
        acc = g_ref[0]
        for k in range(1, N_DEV):
            acc = acc + g_ref[k]
        o_ref[...] = acc

    return pl.pallas_call(body, name="reduce_small", out_shape=jax.ShapeDtypeStruct((r, c), F32))(gathered)


def adamw_small(ws, gs, ms, vs):
    n = len(ws)

    def body(*refs):
        w_r, g_r, m_r, v_r = refs[:n], refs[n:2 * n], refs[2 * n:3 * n], refs[3 * n:4 * n]
        d_o, m_o, v_o = refs[4 * n:5 * n], refs[5 * n:6 * n], refs[6 * n:7 * n]
        for a in range(n):
            d, mn, vn = _adamw(w_r[a][...], g_r[a][...], m_r[a][...], v_r[a][...])
            d_o[a][...] = d
            m_o[a][...] = mn
            v_o[a][...] = vn

    shapes = [jax.ShapeDtypeStruct(w.shape, F32) for w in ws]
    out = pl.pallas_call(body, name="adamw_small", out_shape=shapes * 3)(*ws, *gs, *ms, *vs)
    return out[:n], out[n:2 * n], out[2 * n:]


WEIGHTS = ["meta_tokens", "norm_ffn1_pre", "norm_ffn1_post", "ffn1_w_gu", "ffn1_w_down", "norm_mix_pre",
           "norm_mix_post", "w_in", "pool_w", "pool_scale", "w_pool_o", "q_a_norm", "w_q_b", "kv_a_norm", "w_kv_b",
           "w_mla_o", "w_out", "norm_ffn2_pre", "norm_ffn2_post", "ffn2_w_gu", "ffn2_w_down"]
BIG = ["ffn1_w_gu", "ffn1_w_down", "w_in", "pool_w", "w_pool_o", "w_q_b", "w_kv_b", "w_mla_o", "w_out",
       "ffn2_w_gu", "ffn2_w_down"]
COL_SHARDED = ("w_in", "w_q_b")
GATHERED = {"ffn1_w_gu": COLS_GU, "ffn2_w_gu": COLS_GU, "w_pool_o": COLS, "w_kv_b": COLS}
GAINS =["norm_ffn1_pre", "norm_ffn1_post", "norm_mix_pre", "norm_mix_post", "norm_ffn2_pre", "norm_ffn2_post"]
SMALL = GAINS + ["pool_scale", "q_a_norm", "kv_a_norm"]
Z_SRC = 1024 + 512 + 512 + QK_ROPE


def _full_from_gathered(name, g):
    _, r, c = g.shape
    if name == "pool_w":
        ng = len(POOL_WINDOWS)
        return g.reshape(N_DEV, ng, r // ng, c).transpose(1, 0, 2, 3).reshape(ng, POOL_GROUP, POOL_GROUP)
    if name in COL_SHARDED:
        return g.transpose(1, 0, 2).reshape(r, N_DEV * c)
    return g.reshape(N_DEV * r, c)


def _blocks_from_full(name, dw):
    if name == "pool_w":
        ng = len(POOL_WINDOWS)
        return dw.reshape(ng, N_DEV, POOL_GROUP // N_DEV, POOL_GROUP).transpose(1, 0, 2, 3).reshape(
            N_DEV, ng * POOL_GROUP // N_DEV, POOL_GROUP)
    k, n = dw.shape
    if name in COL_SHARDED:
        return dw.reshape(k, N_DEV, n // N_DEV).transpose(1, 0, 2)
    return dw.reshape(N_DEV, k // N_DEV, n)


def _to_internal(name, w):
    if name == "w_in":
        d = w.shape[0]
        return jnp.concatenate([w[:, :Z_SRC - QK_ROPE], w[:, Z_SRC:], w[:, Z_SRC - QK_ROPE:Z_SRC],
                                jnp.zeros((d, Z_COLS - Z_KR - QK_ROPE), w.dtype)], axis=1)
    if name == "w_q_b":
        r = w.shape[0]
        w3 = w.reshape(r, HEADS, QK_NOPE + QK_ROPE)
        return jnp.pad(w3, ((0, 0), (0, 0), (0, HEAD_W - QK_NOPE - QK_ROPE))).reshape(r, Q_COLS)
    return w


def _from_internal(name, dw):
    if name == "w_in":
        return jnp.concatenate([dw[:, :Z_SRC - QK_ROPE], dw[:, Z_KR:Z_KR + QK_ROPE], dw[:, Z_SRC - QK_ROPE:Z_KR]], axis=1)
    if name == "w_q_b":
        r = dw.shape[0]
        return dw.reshape(r, HEADS, HEAD_W)[:, :, :QK_NOPE + QK_ROPE].reshape(r, HEADS * (QK_NOPE + QK_ROPE))
    return dw


def _shard2d(a):
    return a.reshape(-1, a.shape[-1])


def kernel(x, meta_tokens, norm_ffn1_pre, norm_ffn1_post, ffn1_w_gu, ffn1_w_down, norm_mix_pre, norm_mix_post, w_in, pool_w, pool_scale, w_pool_o, q_a_norm, w_q_b, kv_a_norm, w_kv_b, w_mla_o, w_out, norm_ffn2_pre, norm_ffn2_post, ffn2_w_gu, ffn2_w_down, loss_target, m_meta_tokens, m_norm_ffn1_pre, m_norm_ffn1_post, m_ffn1_w_gu, m_ffn1_w_down, m_norm_mix_pre, m_norm_mix_post, m_w_in, m_pool_w, m_pool_scale, m_w_pool_o, m_q_a_norm, m_w_q_b, m_kv_a_norm, m_w_kv_b, m_w_mla_o, m_w_out, m_norm_ffn2_pre, m_norm_ffn2_post, m_ffn2_w_gu, m_ffn2_w_down, v_meta_tokens, v_norm_ffn1_pre, v_norm_ffn1_post, v_ffn1_w_gu, v_ffn1_w_down, v_norm_mix_pre, v_norm_mix_post, v_w_in, v_pool_w, v_pool_scale, v_w_pool_o, v_q_a_norm, v_w_q_b, v_kv_a_norm, v_w_kv_b, v_w_mla_o, v_w_out, v_norm_ffn2_pre, v_norm_ffn2_post, v_ffn2_w_gu, v_ffn2_w_down):
    given = dict(locals())
    w_in_dev = {n: given[n] for n in WEIGHTS}
    m_in = {n: given["m_" + n] for n in WEIGHTS}
    v_in = {n: given["v_" + n] for n in WEIGHTS}
    xi, yi, ci = _my_pos()
    dev = 4 * xi + 2 * yi + ci
    core = jnp.reshape(ci, (1,)).astype(jnp.int32)
    chip = jnp.reshape(2 * xi + yi, (1,)).astype(jnp.int32)
    d = x.shape[-1]

    shards = {n: _shard2d(w_in_dev[n]) for n in BIG}
    layouts = [GATHERED.get(n, LEAD) for n in BIG]
    *gathered, meta_g = all_gather("ag_weights", [shards[n].astype(BF16) for n in BIG] + [meta_tokens], layouts + [LEAD])
    wfull = {n: g if lay != LEAD else _to_internal(n, _full_from_gathered(n, g))
             for n, g, lay in zip(BIG, gathered, layouts, strict=True)}
    meta_full = meta_g.transpose(1, 0, 2).reshape(N_META, d)

    gain = {n: given[n] for n in SMALL}
    loss_blk, grad_x, front, gbig, gsmall = local_step(x[0], loss_target[0], meta_full, wfull, gain)

    grads = [gbig[n] if lay != LEAD else _blocks_from_full(n, _from_internal(n, gbig[n]).astype(BF16))
             for n, lay in zip(BIG, layouts, strict=True)]
    from_sibling = rs_sibling("rs_sibling", grads, layouts)
    sums = [rs_add("rs_add_" + n, g, lay, r, core)
            for n, g, lay, r in zip(BIG, grads, layouts, from_sibling, strict=True)]
    from_chips = rs_chips("rs_chips", sums)

    out_g, out_d, out_m, out_v = {}, {}, {}, {}
    for n, s, r in zip(BIG, sums, from_chips, strict=True):
        shp = w_in_dev[n].shape
        res = adamw_shard("adamw_" + n, shards[n], _shard2d(m_in[n]), _shard2d(v_in[n]), s, r, chip)
        out_g[n], out_d[n], out_m[n], out_v[n] = [t.reshape(shp) for t in res]

    tail = jnp.concatenate([gsmall["pool_scale"], gsmall["q_a_norm"], gsmall["kv_a_norm"]], axis=1)
    small = jnp.concatenate([gsmall[n] for n in GAINS] + [tail, jnp.broadcast_to(loss_blk[:1, :1], (1, d)),
                                                         front[PAD_ROWS:]], axis=0)
    (small_g,) = all_gather("ag_small", [small], [LEAD])
    total = reduce_small(small_g)
    ng = len(GAINS)
    for i, n in enumerate(GAINS):
        out_g[n] = total[i:i + 1]
    o = 0
    for n in ("pool_scale", "q_a_norm", "kv_a_norm"):
        wdt = w_in_dev[n].shape[1]
        out_g[n] = total[ng:ng + 1, o:o + wdt]
        o += wdt
    loss = total[ng + 1, 0]
    mcols = meta_tokens.shape[1]
    out_g["meta_tokens"] = lax.dynamic_slice(total[ng + 2:ng + 2 + N_META], (0, dev * mcols), (N_META, mcols))
    names = ["meta_tokens"] + SMALL
    ds_, ms_, vs_ = adamw_small([w_in_dev[n] for n in names], [out_g[n] for n in names],
                                [m_in[n] for n in names], [v_in[n] for n in names])
    for n, dd, mm, vv in zip(names, ds_, ms_, vs_, strict=True):
        out_d[n], out_m[n], out_v[n] = dd, mm, vv

    return (loss, grad_x[None], *[out_g[n] for n in WEIGHTS], *[out_d[n] for n in WEIGHTS],
            *[out_m[n] for n in WEIGHTS], *[out_v[n] for n in WEIGHTS])


def local_step(x, target, meta_full, wfull, gain):
    d = x.shape[-1]
    h0 = jnp.concatenate([jnp.zeros((PAD_ROWS, d), F32), meta_full, x], axis=0)
    lp = h0.shape[0]
    tabs = rope_tables(lp)
    a1 = prenorm(h0, gain["norm_ffn1_pre"])
    gu1, s1, f1 = ffn_fwd("ffn1", a1, wfull["ffn1_w_gu"], wfull["ffn1_w_down"])
    h1, a2 = post_pre("post_pre1", f1, h0, gain["norm_ffn1_post"], 0.5, gain["norm_mix_pre"])
    mix, sv = mixer_fwd(a2, wfull, tabs, gain["pool_scale"], gain["q_a_norm"], gain["kv_a_norm"])
    h2, a3 = post_pre("post_pre2", mix, h1, gain["norm_mix_post"], 1.0, gain["norm_ffn2_pre"])
    gu2, s2, f2 = ffn_fwd("ffn2", a3, wfull["ffn2_w_gu"], wfull["ffn2_w_down"])
    dh3, loss_blk = post_loss(f2, h2, gain["norm_ffn2_post"], 0.5, target)

    gsmall, gbig = {}, {}
    df2, gsmall["norm_ffn2_post"] = post_bwd(dh3, f2, gain["norm_ffn2_post"], 0.5)
    da3, gbig["ffn2_w_gu"], gbig["ffn2_w_down"] = ffn_bwd("ffn2", df2, a3, gu2, s2, wfull["ffn2_w_gu"], wfull["ffn2_w_down"])
    dh2, dmix, gsmall["norm_ffn2_pre"], gsmall["norm_mix_post"] = pre_post_bwd(
        "pre_post_bwd2", da3, h2, gain["norm_ffn2_pre"], dh3, mix, gain["norm_mix_post"], 1.0)
    da2, gmix, gmix_small = mixer_bwd(dmix, a2, sv, wfull, tabs, gain["pool_scale"], gain["q_a_norm"], gain["kv_a_norm"])
    gbig.update(gmix)
    gsmall.update(gmix_small)
    dh1, df1, gsmall["norm_mix_pre"], gsmall["norm_ffn1_post"] = pre_post_bwd(
        "pre_post_bwd1", da2, h1, gain["norm_mix_pre"], dh2, f1, gain["norm_ffn1_post"], 0.5)
    da1, gbig["ffn1_w_gu"], gbig["ffn1_w_down"] = ffn_bwd("ffn1", df1, a1, gu1, s1, wfull["ffn1_w_gu"], wfull["ffn1_w_down"])
    grad_x, front, gsmall["norm_ffn1_pre"] = pre_bwd_first(da1, h0, gain["norm_ffn1_pre"], dh1)
    return loss_blk, grad_x, front, gbig, gsmall
```

```python
import functools

import jax
import jax.numpy as jnp
import numpy as np
from jax import lax
from jax.experimental import pallas as pl
from jax.experimental.pallas import tpu as pltpu

F32 = jnp.float32
BF16 = jnp.bfloat16

N_META = 16
POOL_WINDOWS = (2, 4, 8, 16)
POOL_GROUP = 256
HEADS = 16
QK_NOPE = 128
QK_ROPE = 64
V_DIM = 128
ROPE_THETA = 10000.0
SOFTMAX_SCALE = (QK_NOPE + QK_ROPE) ** -0.5
EPS = 1e-6
ADAM_LR = 0.001
ADAM_B1 = 0.9
ADAM_B2 = 0.999
ADAM_EPS = 1e-08
ADAM_WD = 0.01
ADAM_STEP = 10

LANE = 128
FRONT = 128
PAD_ROWS = FRONT - N_META
HEAD_W = 256
GU_TILE = 1408
VMEM_LIMIT = 56 * 1024 * 1024
MESH_AXES = ("x", "y", "c")
N_DEV = 8


def _pick(n, cands):
    for c in cands:
        if n % c == 0:
            return c
    raise ValueError(f"no tile for {n} in {cands}")


def _cparams(sem=None):
    kw = dict(vmem_limit_bytes=VMEM_LIMIT)
    if sem is not None:
        kw["dimension_semantics"] = sem
    return pltpu.CompilerParams(**kw)


_DIMS = {"nn": (((1,), (0,)), ((), ())), "nt": (((1,), (1,)), ((), ())), "tn": (((0,), (0,)), ((), ()))}


def _mm(name, form, a, b, tm, tn, outs, epi=None, extras=(), n_outer=False):
    if form == "tn":
        k, m = a.shape
        n = b.shape[1]
        a_blk, a_map = (k, tm), lambda i, j: (0, i)
        b_blk, b_map = (k, tn), lambda i, j: (0, j)
    elif form == "nn":
        m, k = a.shape
        n = b.shape[1]
        a_blk, a_map = (tm, k), lambda i, j: (i, 0)
        b_blk, b_map = (k, tn), lambda i, j: (0, j)
    else:
        m, k = a.shape
        n = b.shape[0]
        a_blk, a_map = (tm, k), lambda i, j: (i, 0)
        b_blk, b_map = (tn, k), lambda i, j: (j, 0)
    assert m % tm == 0 and n % tn == 0, (name, m, n, tm, tn)
    n_ex = len(extras)
    dn = _DIMS[form]
    if n_outer:
        grid = (n // tn, m // tm)

        def spec(blk, im):
            return pl.BlockSpec(blk, lambda gj, gi: im(gi, gj))
    else:
        grid = (m // tm, n // tn)
        spec = pl.BlockSpec

    def body(a_ref, b_ref, *rest):
        ex, out_refs = rest[:n_ex], rest[n_ex:]
        acc = lax.dot_general(a_ref[...].astype(BF16), b_ref[...].astype(BF16), dn, preferred_element_type=F32)
        res = epi(acc, *[e[...] for e in ex]) if epi is not None else (acc,)
        for r, o in zip(res, out_refs, strict=True):
            o[...] = r.astype(o.dtype)

    return pl.pallas_call(
        body,
        name=name,
        grid=grid,
        in_specs=[spec(a_blk, a_map), spec(b_blk, b_map)] + [spec(blk, im) for _, blk, im in extras],
        out_specs=[spec(blk, im) for _, _, blk, im in outs],
        out_shape=[jax.ShapeDtypeStruct(s, d) for s, d, _, _ in outs],
        compiler_params=_cparams(("parallel", "parallel")),
    )(a, b, *[e for e, _, _ in extras])


def _mm_plain(name, form, a, b, tm, tn, out_dtype):
    m = a.shape[1] if form == "tn" else a.shape[0]
    n = b.shape[0] if form == "nt" else b.shape[1]
    return _mm(name, form, a, b, tm, tn, [((m, n), out_dtype, (tm, tn), lambda i, j: (i, j))])[0]


def _rstd(x):
    return lax.rsqrt(jnp.mean(x * x, axis=-1, keepdims=True) + EPS)


def _norm_bwd(dy, x, gain):
    r = _rstd(x)
    dyg = dy * gain
    dx = r * (dyg - x * (r * r) * jnp.mean(dyg * x, axis=-1, keepdims=True))
    dgain = jnp.sum(dy * x * r, axis=0, keepdims=True)
    return dx, dgain


def _row_spec(tr, cols, col_block=0):
    return pl.BlockSpec((tr, cols), lambda i: (i, col_block))


def _vec_spec(cols, col_block=0):
    return pl.BlockSpec((1, cols), lambda i: (0, col_block))


def _row_tile(lp):
    return _pick(lp, (128,))


def prenorm(h, gain):
    lp, d = h.shape
    tr = _row_tile(lp)

    def body(h_ref, g_ref, a_ref):
        x = h_ref[...]
        a_ref[...] = (x * _rstd(x) * g_ref[...]).astype(BF16)

    return pl.pallas_call(
        body, name="prenorm", grid=(lp // tr,),
        in_specs=[_row_spec(tr, d), _vec_spec(d)], out_specs=_row_spec(tr, d),
        out_shape=jax.ShapeDtypeStruct((lp, d), BF16), compiler_params=_cparams(("parallel",)),
    )(h, gain)


def post_pre(name, f, h_in, g_post, coef, g_next):
    lp, d = f.shape
    tr = _row_tile(lp)

    def body(f_ref, h_ref, gp_ref, gn_ref, ho_ref, a_ref):
        fv = f_ref[...]
        ho = h_ref[...] + coef * (fv * _rstd(fv) * gp_ref[...])
        ho_ref[...] = ho
        a_ref[...] = (ho * _rstd(ho) * gn_ref[...]).astype(BF16)

    return pl.pallas_call(
        body, name=name, grid=(lp // tr,),
        in_specs=[_row_spec(tr, d), _row_spec(tr, d), _vec_spec(d), _vec_spec(d)],
        out_specs=[_row_spec(tr, d), _row_spec(tr, d)],
        out_shape=[jax.ShapeDtypeStruct((lp, d), F32), jax.ShapeDtypeStruct((lp, d), BF16)],
        compiler_params=_cparams(("parallel",)),
    )(f, h_in, g_post, g_next)


def post_loss(f, h_in, g_post, coef, target):
    lp, d = f.shape
    tr = _row_tile(lp)
    front_tiles = FRONT // tr

    def body(f_ref, h_ref, gp_ref, t_ref, dh_ref, loss_ref):
        i = pl.program_id(0)

        @pl.when(i == 0)
        def _():
            loss_ref[...] = jnp.zeros_like(loss_ref)

        @pl.when(i < front_tiles)
        def _():
            dh_ref[...] = jnp.zeros_like(dh_ref)

        @pl.when(i >= front_tiles)
        def _():
            fv = f_ref[...]
            ho = h_ref[...] + coef * (fv * _rstd(fv) * gp_ref[...])
            err = ho - t_ref[...]
            dh_ref[...] = err / d
            tok = jnp.mean(err * err, axis=-1, keepdims=True)
            loss_ref[...] += 0.5 * jnp.sum(tok)

    return pl.pallas_call(
        body, name="post_loss", grid=(lp // tr,),
        in_specs=[_row_spec(tr, d), _row_spec(tr, d), _vec_spec(d),
                  pl.BlockSpec((tr, d), lambda i: (jnp.maximum(i - front_tiles, 0), 0))],
        out_specs=[_row_spec(tr, d), pl.BlockSpec((8, LANE), lambda i: (0, 0))],
        out_shape=[jax.ShapeDtypeStruct((lp, d), F32), jax.ShapeDtypeStruct((8, LANE), F32)],
        compiler_params=_cparams(("arbitrary",)),
    )(f, h_in, g_post, target)


def post_bwd(dh_out, f, g_post, coef):
    lp, d = f.shape
    tr = _row_tile(lp)

    def body(dh_ref, f_ref, gp_ref, df_ref, dg_ref):
        @pl.when(pl.program_id(0) == 0)
        def _():
            dg_ref[...] = jnp.zeros_like(dg_ref)

        df, dg = _norm_bwd(coef * dh_ref[...], f_ref[...], gp_ref[...])
        df_ref[...] = df.astype(BF16)
        dg_ref[...] += dg

    return pl.pallas_call(
        body, name="post_bwd", grid=(lp // tr,),
        in_specs=[_row_spec(tr, d), _row_spec(tr, d), _vec_spec(d)],
        out_specs=[_row_spec(tr, d), _vec_spec(d)],
        out_shape=[jax.ShapeDtypeStruct((lp, d), BF16), jax.ShapeDtypeStruct((1, d), F32)],
        compiler_params=_cparams(("arbitrary",)),
    )(dh_out, f, g_post)


def pre_post_bwd(name, da, h_mid, g_pre, dh_out, f_prev, g_post_prev, coef_prev):
    lp, d = da.shape
    tr = _row_tile(lp)

    def body(da_ref, h_ref, gpre_ref, dho_ref, f_ref, gpost_ref, dh_ref, df_ref, dgpre_ref, dgpost_ref):
        @pl.when(pl.program_id(0) == 0)
        def _():
            dgpre_ref[...] = jnp.zeros_like(dgpre_ref)
            dgpost_ref[...] = jnp.zeros_like(dgpost_ref)

        dx, dgpre = _norm_bwd(da_ref[...], h_ref[...], gpre_ref[...])
        dh = dho_ref[...] + dx
        dh_ref[...] = dh
        dgpre_ref[...] += dgpre
        df, dgpost = _norm_bwd(coef_prev * dh, f_ref[...], gpost_ref[...])
        df_ref[...] = df.astype(BF16)
        dgpost_ref[...] += dgpost

    return pl.pallas_call(
        body, name=name, grid=(lp // tr,),
        in_specs=[_row_spec(tr, d), _row_spec(tr, d), _vec_spec(d), _row_spec(tr, d), _row_spec(tr, d), _vec_spec(d)],
        out_specs=[_row_spec(tr, d), _row_spec(tr, d), _vec_spec(d), _vec_spec(d)],
        out_shape=[jax.ShapeDtypeStruct((lp, d), F32), jax.ShapeDtypeStruct((lp, d), BF16),
                   jax.ShapeDtypeStruct((1, d), F32), jax.ShapeDtypeStruct((1, d), F32)],
        compiler_params=_cparams(("arbitrary",)),
    )(da, h_mid, g_pre, dh_out, f_prev, g_post_prev)


def pre_bwd_first(da, h0, g_pre, dh_out):
    lp, d = da.shape
    tr = _row_tile(lp)
    front_tiles = FRONT // tr
    assert front_tiles == 1

    def body(da_ref, h_ref, gpre_ref, dho_ref, gx_ref, front_ref, dgpre_ref):
        i = pl.program_id(0)

        @pl.when(i == 0)
        def _():
            dgpre_ref[...] = jnp.zeros_like(dgpre_ref)

        dx, dgpre = _norm_bwd(da_ref[...], h_ref[...], gpre_ref[...])
        dh = dho_ref[...] + dx
        dgpre_ref[...] += dgpre
        gx_ref[...] = dh

        @pl.when(i == 0)
        def _():
            front_ref[...] = dh

    return pl.pallas_call(
        body, name="pre_bwd_first", grid=(lp // tr,),
        in_specs=[_row_spec(tr, d), _row_spec(tr, d), _vec_spec(d), _row_spec(tr, d)],
        out_specs=[pl.BlockSpec((tr, d), lambda i: (jnp.maximum(i - front_tiles, 0), 0)),
                   pl.BlockSpec((tr, d), lambda i: (0, 0)), _vec_spec(d)],
        out_shape=[jax.ShapeDtypeStruct((lp - FRONT, d), F32), jax.ShapeDtypeStruct((tr, d), F32),
                   jax.ShapeDtypeStruct((1, d), F32)],
        compiler_params=_cparams(("arbitrary",)),
    )(da, h0, g_pre, dh_out)


def _m_tile(lp):
    return _pick(lp, (1056, 512, 256, 128))


def ffn_fwd(tag, a, wgu_p, wd):
    lp, d = a.shape
    f2 = wgu_p.shape[1]
    tm = _m_tile(lp)

    def epi(acc):
        g, u = acc[:, :GU_TILE], acc[:, GU_TILE:]
        return acc, g * jax.nn.sigmoid(g) * u

    tg = _pick(lp, (384, 256, 128))
    gu, s = _mm(tag + "_gu", "nn", a, wgu_p, tg, 2 * GU_TILE,
                [((lp, f2), F32, (tg, 2 * GU_TILE), lambda i, j: (i, j)),
                 ((lp, f2 // 2), BF16, (tg, GU_TILE), lambda i, j: (i, j))], epi=epi, n_outer=True)
    f = _mm_plain(tag + "_down", "nn", s, wd, tm, 512, F32)
    return gu, s, f


def ffn_bwd(tag, df, a, gu, s, wgu_p, wd, emit):
    lp, d = df.shape
    f2 = wgu_p.shape[1]
    tm = _m_tile(lp)

    def epi(acc, gu_t):
        g, u = gu_t[:, :GU_TILE], gu_t[:, GU_TILE:]
        sig = jax.nn.sigmoid(g)
        dg = acc * u * (sig * (1.0 + g * (1.0 - sig)))
        du = acc * (g * sig)
        return (jnp.concatenate([dg, du], axis=1),)

    ts = _pick(lp, (528, 256, 128))
    dgu = _mm(tag + "_ds", "nt", df, wd, ts, GU_TILE,
              [((lp, f2), BF16, (ts, 2 * GU_TILE), lambda i, j: (i, j))], epi=epi,
              extras=[(gu, (ts, 2 * GU_TILE), lambda i, j: (i, j))], n_outer=True)[0]
    dwd = _mm_plain(tag + "_dwd", "tn", s, df, 512, _pick(d, (1024,)), BF16)
    dwgu = _mm_plain(tag + "_dwgu", "tn", a, dgu, _pick(d, (1024,)), 1024, BF16)
    emit({tag + "_w_gu": dwgu, tag + "_w_down": dwd}, dwgu)
    return _mm_plain(tag + "_da", "nt", dgu, wgu_p, _pick(lp, (528, 256, 128)), 256, F32)


Z_U, Z_CQ, Z_CKV, Z_GP, Z_GM, Z_KR, Z_COLS = 0, 1024, 1536, 2048, 4096, 6144, 6400
POOL_W = POOL_GROUP * len(POOL_WINDOWS)
HALO = 16


def _pool_counts(lp, w):
    pos = lax.broadcasted_iota(jnp.int32, (lp, 1), 0) - PAD_ROWS
    return jnp.clip(pos + 1, 1, w).astype(F32)


def _pool_diff(u_ref, pad_ref, lp, w):
    pad_ref[pl.ds(0, HALO), :] = jnp.zeros((HALO, POOL_GROUP), F32)
    pad_ref[pl.ds(HALO, lp), :] = u_ref[...]
    acc = pad_ref[pl.ds(HALO, lp), :]
    for s in range(1, w):
        acc = acc + pad_ref[pl.ds(HALO - s, lp), :]
    return acc / _pool_counts(lp, w) - u_ref[...]


def pool_fwd(z, pool_w, pool_scale):
    lp = z.shape[0]
    ng = len(POOL_WINDOWS)

    def body(u_ref, w_ref, sc_ref, o_ref, pad_ref):
        for g, w in enumerate(POOL_WINDOWS):
            @pl.when(pl.program_id(0) == g)
            def _(w=w):
                dd = _pool_diff(u_ref, pad_ref, lp, w)
                y = jnp.dot(dd.astype(BF16), w_ref[0], preferred_element_type=F32)
                o_ref[...] = (y * sc_ref[...]).astype(BF16)

    return pl.pallas_call(
        body, name="pool_fwd", grid=(ng,),
        in_specs=[pl.BlockSpec((lp, POOL_GROUP), lambda g: (0, g)),
                  pl.BlockSpec((1, POOL_GROUP, POOL_GROUP), lambda g: (g, 0, 0)),
                  pl.BlockSpec((1, POOL_GROUP), lambda g: (0, g))],
        out_specs=pl.BlockSpec((lp, POOL_GROUP), lambda g: (0, g)),
        out_shape=jax.ShapeDtypeStruct((lp, POOL_W), BF16),
        scratch_shapes=[pltpu.VMEM((lp + HALO, POOL_GROUP), F32)],
        compiler_params=_cparams(("parallel",)),
    )(z, pool_w, pool_scale)


def pool_bwd(dyp, z, pool_w, pool_scale):
    lp = z.shape[0]
    ng = len(POOL_WINDOWS)

    def body(dy_ref, u_ref, w_ref, sc_ref, du_ref, dw_ref, dsc_ref, pad_ref):
        for g, w in enumerate(POOL_WINDOWS):
            @pl.when(pl.program_id(0) == g)
            def _(w=w):
                dd = _pool_diff(u_ref, pad_ref, lp, w).astype(BF16)
                wg = w_ref[0]
                ypre = jnp.dot(dd, wg, preferred_element_type=F32)
                dy = dy_ref[...]
                dsc_ref[...] = jnp.sum(dy * ypre, axis=0, keepdims=True)
                dypre = (dy * sc_ref[...]).astype(BF16)
                dw_ref[0] = lax.dot_general(dd, dypre, _DIMS["tn"], preferred_element_type=F32)
                ddd = lax.dot_general(dypre, wg, _DIMS["nt"], preferred_element_type=F32)
                pad_ref[pl.ds(0, lp), :] = ddd / _pool_counts(lp, w)
                pad_ref[pl.ds(lp, HALO), :] = jnp.zeros((HALO, POOL_GROUP), F32)
                acc = -ddd
                for s in range(w):
                    acc = acc + pad_ref[pl.ds(s, lp), :]
                du_ref[...] = acc.astype(BF16)

    return pl.pallas_call(
        body, name="pool_bwd", grid=(ng,),
        in_specs=[pl.BlockSpec((lp, POOL_GROUP), lambda g: (0, g)),
                  pl.BlockSpec((lp, POOL_GROUP), lambda g: (0, g)),
                  pl.BlockSpec((1, POOL_GROUP, POOL_GROUP), lambda g: (g, 0, 0)),
                  pl.BlockSpec((1, POOL_GROUP), lambda g: (0, g))],
        out_specs=[pl.BlockSpec((lp, POOL_GROUP), lambda g: (0, g)),
                   pl.BlockSpec((1, POOL_GROUP, POOL_GROUP), lambda g: (g, 0, 0)),
                   pl.BlockSpec((1, POOL_GROUP), lambda g: (0, g))],
        out_shape=[jax.ShapeDtypeStruct((lp, POOL_W), BF16),
                   jax.ShapeDtypeStruct((ng, POOL_GROUP, POOL_GROUP), F32),
                   jax.ShapeDtypeStruct((1, POOL_W), F32)],
        scratch_shapes=[pltpu.VMEM((lp + HALO, POOL_GROUP), F32)],
        compiler_params=_cparams(("parallel",)),
    )(dyp, z, pool_w, pool_scale)


Q_COLS = HEADS * HEAD_W
ROPE_BLOCK = LANE


def rope_tables(lp):
    pos = jnp.maximum(jnp.arange(lp, dtype=F32) - PAD_ROWS, 0.0)
    inv = ROPE_THETA ** (-jnp.arange(0, QK_ROPE, 2, dtype=F32) / QK_ROPE)
    ang = pos[:, None] * inv[None, :]
    cos, sin, zero = jnp.cos(ang), jnp.sin(ang), jnp.zeros_like(ang)
    return jnp.stack([jnp.concatenate([cos, cos, zero, zero], axis=1),
                      jnp.concatenate([-sin, zero, zero, zero], axis=1),
                      jnp.concatenate([zero, sin, zero, zero], axis=1)])


def _rope(x, tabs):
    return x * tabs[0] + pltpu.roll(x, 96, 1) * tabs[1] + pltpu.roll(x, 32, 1) * tabs[2]


def _rope_bwd(g, tabs):
    return g * tabs[0] + pltpu.roll(g * tabs[1], 32, 1) + pltpu.roll(g * tabs[2], 96, 1)


def _tab_spec(tr):
    return pl.BlockSpec((3, tr, ROPE_BLOCK), lambda i: (0, i, 0))


def mla_prep(z, g_q, g_kv, tabs):
    lp = z.shape[0]
    tr = _row_tile(lp)
    r = g_q.shape[1]

    def body(cq_ref, ckv_ref, kr_ref, gq_ref, gkv_ref, tab_ref, qn_ref, kvn_ref, kpe_ref):
        cq, ckv = cq_ref[...], ckv_ref[...]
        qn_ref[...] = (cq * _rstd(cq) * gq_ref[...]).astype(BF16)
        kvn_ref[...] = (ckv * _rstd(ckv) * gkv_ref[...]).astype(BF16)
        kpe_ref[...] = _rope(kr_ref[...], tab_ref[...]).astype(BF16)

    return pl.pallas_call(
        body, name="mla_prep", grid=(lp // tr,),
        in_specs=[_row_spec(tr, r, Z_CQ // r), _row_spec(tr, r, Z_CKV // r), _row_spec(tr, ROPE_BLOCK, Z_KR // ROPE_BLOCK),
                  _vec_spec(r), _vec_spec(r), _tab_spec(tr)],
        out_specs=[_row_spec(tr, r), _row_spec(tr, r), _row_spec(tr, ROPE_BLOCK)],
        out_shape=[jax.ShapeDtypeStruct((lp, r), BF16), jax.ShapeDtypeStruct((lp, r), BF16),
                   jax.ShapeDtypeStruct((lp, ROPE_BLOCK), BF16)],
        compiler_params=_cparams(("parallel",)),
    )(z, z, z, g_q, g_kv, tabs)


def q_proj(qn, wq_p, tabs):
    lp = qn.shape[0]
    tm = _m_tile(lp)
    tn = 4 * HEAD_W

    def epi(acc, tab):
        parts = []
        for t in range(tn // HEAD_W):
            parts.append(acc[:, t * HEAD_W:t * HEAD_W + QK_NOPE])
            parts.append(_rope(acc[:, t * HEAD_W + QK_NOPE:(t + 1) * HEAD_W], tab))
        return (jnp.concatenate(parts, axis=1),)

    return _mm("q_proj", "nn", qn, wq_p, tm, tn, [((lp, Q_COLS), BF16, (tm, tn), lambda i, j: (i, j))], epi=epi,
               extras=[(tabs, (3, tm, ROPE_BLOCK), lambda i, j: (0, i, 0))])[0]


def kv_proj(kvn, wkv):
    return _mm_plain("kv_proj", "nn", kvn, wkv, _m_tile(kvn.shape[0]), 1024, BF16)


def q_rope_bwd(dq, tabs):
    lp = dq.shape[0]
    tr = _row_tile(lp)

    def body(dq_ref, tab_ref, o_ref):
        tab = tab_ref[...]
        for h in range(HEADS):
            o_ref[:, h * HEAD_W:h * HEAD_W + QK_NOPE] = dq_ref[:, h * HEAD_W:h * HEAD_W + QK_NOPE].astype(BF16)
            o_ref[:, h * HEAD_W + QK_NOPE:(h + 1) * HEAD_W] = _rope_bwd(
                dq_ref[:, h * HEAD_W + QK_NOPE:(h + 1) * HEAD_W], tab).astype(BF16)

    return pl.pallas_call(
        body, name="q_rope_bwd", grid=(lp // tr,),
        in_specs=[_row_spec(tr, Q_COLS), _tab_spec(tr)], out_specs=_row_spec(tr, Q_COLS),
        out_shape=jax.ShapeDtypeStruct((lp, Q_COLS), BF16), compiler_params=_cparams(("parallel",)),
    )(dq, tabs)


def mla_prep_bwd(dqn, dkvn, dkpe_h, z, g_q, g_kv, tabs):
    lp = z.shape[0]
    tr = _row_tile(lp)
    r = g_q.shape[1]

    def body(dqn_ref, dkvn_ref, dkpe_ref, cq_ref, ckv_ref, gq_ref, gkv_ref, tab_ref,
             dcq_ref, dckv_ref, dkr_ref, dgq_ref, dgkv_ref):
        @pl.when(pl.program_id(0) == 0)
        def _():
            dgq_ref[...] = jnp.zeros_like(dgq_ref)
            dgkv_ref[...] = jnp.zeros_like(dgkv_ref)

        dcq, dgq = _norm_bwd(dqn_ref[...], cq_ref[...], gq_ref[...])
        dckv, dgkv = _norm_bwd(dkvn_ref[...], ckv_ref[...], gkv_ref[...])
        dcq_ref[...] = dcq.astype(BF16)
        dckv_ref[...] = dckv.astype(BF16)
        dgq_ref[...] += dgq
        dgkv_ref[...] += dgkv
        dkpe = dkpe_ref[0]
        for h in range(1, HEADS):
            dkpe = dkpe + dkpe_ref[h]
        dkr_ref[...] = _rope_bwd(dkpe, tab_ref[...]).astype(BF16)

    return pl.pallas_call(
        body, name="mla_prep_bwd", grid=(lp // tr,),
        in_specs=[_row_spec(tr, r), _row_spec(tr, r), pl.BlockSpec((HEADS, tr, ROPE_BLOCK), lambda i: (0, i, 0)),
                  _row_spec(tr, r, Z_CQ // r), _row_spec(tr, r, Z_CKV // r), _vec_spec(r), _vec_spec(r), _tab_spec(tr)],
        out_specs=[_row_spec(tr, r), _row_spec(tr, r), _row_spec(tr, ROPE_BLOCK), _vec_spec(r), _vec_spec(r)],
        out_shape=[jax.ShapeDtypeStruct((lp, r), BF16), jax.ShapeDtypeStruct((lp, r), BF16),
                   jax.ShapeDtypeStruct((lp, ROPE_BLOCK), BF16),
                   jax.ShapeDtypeStruct((1, r), F32), jax.ShapeDtypeStruct((1, r), F32)],
        compiler_params=_cparams(("arbitrary",)),
    )(dqn, dkvn, dkpe_h, z, z, g_q, g_kv, tabs)


def _attn_tile(lp):
    return _pick(lp, (528, 128))


def _causal_pairs(nt, k_major):
    if k_major:
        pairs = [(qi, ki) for ki in range(nt) for qi in range(ki, nt)]
    else:
        pairs = [(qi, ki) for qi in range(nt) for ki in range(qi + 1)]
    return (jnp.asarray([p[0] for p in pairs], jnp.int32), jnp.asarray([p[1] for p in pairs], jnp.int32))


def _scores(q, kcat, q_tile, k_tile, t, masked):
    s = lax.dot_general(q, kcat, _DIMS["nt"], preferred_element_type=F32) * SOFTMAX_SCALE
    if not masked:
        return s
    qpos = q_tile * t + lax.broadcasted_iota(jnp.int32, (t, t), 0)
    kpos = k_tile * t + lax.broadcasted_iota(jnp.int32, (t, t), 1)
    return jnp.where((kpos <= qpos) & (kpos >= PAD_ROWS), s, jnp.float32(-1e30))


def _on_masked_or_not(q_tile, k_tile, fn):
    needs_mask = (q_tile == k_tile) | (k_tile == 0)

    @pl.when(needs_mask)
    def _():
        fn(True)

    @pl.when(jnp.logical_not(needs_mask))
    def _():
        fn(False)


def flash_fwd(q, kv, kpe):
    lp = q.shape[0]
    t = _attn_tile(lp)
    q_tab, k_tab = _causal_pairs(lp // t, k_major=False)

    def body(q_tab_ref, k_tab_ref, q_ref, kv_ref, kpe_ref, o32_ref, o16_ref, lse_ref, m_sc, l_sc, acc_sc):
        pair = pl.program_id(1)
        qi, ki = q_tab_ref[pair], k_tab_ref[pair]

        @pl.when(ki == 0)
        def _():
            m_sc[...] = jnp.full_like(m_sc, -jnp.inf)
            l_sc[...] = jnp.zeros_like(l_sc)
            acc_sc[...] = jnp.zeros_like(acc_sc)

        def step(masked):
            kvt = kv_ref[...]
            kcat = jnp.concatenate([kvt[:, :QK_NOPE], kpe_ref[...]], axis=1)
            s = _scores(q_ref[...], kcat, qi, ki, t, masked)
            m_prev = m_sc[...]
            m_new = jnp.maximum(m_prev, jnp.max(s, axis=1, keepdims=True))
            alpha = jnp.exp(m_prev - m_new)
            p = jnp.exp(s - m_new[:, :1])
            l_sc[...] = alpha * l_sc[...] + jnp.sum(p, axis=1, keepdims=True)
            acc_sc[...] = alpha * acc_sc[...] + jnp.dot(p.astype(BF16), kvt[:, QK_NOPE:], preferred_element_type=F32)
            m_sc[...] = m_new

        _on_masked_or_not(qi, ki, step)

        @pl.when(ki == qi)
        def _():
            l = l_sc[...]
            o = acc_sc[...] / l
            o32_ref[...] = o
            o16_ref[...] = o.astype(BF16)
            lse_ref[0] = m_sc[...] + jnp.log(l)

    return pl.pallas_call(
        body, name="flash_fwd",
        grid_spec=pltpu.PrefetchScalarGridSpec(
            num_scalar_prefetch=2, grid=(HEADS, q_tab.shape[0]),
            in_specs=[pl.BlockSpec((t, HEAD_W), lambda h, p, qt, kt: (qt[p], h)),
                      pl.BlockSpec((t, HEAD_W), lambda h, p, qt, kt: (kt[p], h)),
                      pl.BlockSpec((t, ROPE_BLOCK), lambda h, p, qt, kt: (kt[p], 0))],
            out_specs=[pl.BlockSpec((t, V_DIM), lambda h, p, qt, kt: (qt[p], h)),
                       pl.BlockSpec((t, V_DIM), lambda h, p, qt, kt: (qt[p], h)),
                       pl.BlockSpec((1, t, LANE), lambda h, p, qt, kt: (h, qt[p], 0))],
            scratch_shapes=[pltpu.VMEM((t, LANE), F32), pltpu.VMEM((t, LANE), F32), pltpu.VMEM((t, V_DIM), F32)]),
        out_shape=[jax.ShapeDtypeStruct((lp, HEADS * V_DIM), F32), jax.ShapeDtypeStruct((lp, HEADS * V_DIM), BF16),
                   jax.ShapeDtypeStruct((HEADS, lp, LANE), F32)],
        compiler_params=_cparams(("parallel", "arbitrary")),
    )(q_tab, k_tab, q, kv, kpe)


def flash_bwd(q, kv, kpe, o32, lse, do):
    lp = q.shape[0]
    t = _attn_tile(lp)
    nt = lp // t
    q_tab, k_tab = _causal_pairs(nt, k_major=True)

    def body(q_tab_ref, k_tab_ref, q_ref, kv_ref, kpe_ref, o_ref, lse_ref, do_ref, dq_ref, dkv_ref, dkpe_ref,
             dk_sc, dv_sc):
        pair = pl.program_id(1)
        qi, ki = q_tab_ref[pair], k_tab_ref[pair]

        @pl.when(pair == 0)
        def _():
            dq_ref[...] = jnp.zeros_like(dq_ref)

        @pl.when(qi == ki)
        def _():
            dk_sc[...] = jnp.zeros_like(dk_sc)
            dv_sc[...] = jnp.zeros_like(dv_sc)

        def step(masked):
            qt = q_ref[...]
            kvt = kv_ref[...]
            kcat = jnp.concatenate([kvt[:, :QK_NOPE], kpe_ref[...]], axis=1)
            s = _scores(qt, kcat, qi, ki, t, masked)
            p = jnp.exp(s - lse_ref[0][:, :1])
            do = do_ref[...]
            delta = jnp.sum(do * o_ref[...], axis=1, keepdims=True)
            do16 = do.astype(BF16)
            dv_sc[...] += lax.dot_general(p.astype(BF16), do16, _DIMS["tn"], preferred_element_type=F32)
            dp = lax.dot_general(do16, kvt[:, QK_NOPE:], _DIMS["nt"], preferred_element_type=F32)
            ds = (p * (dp - delta) * SOFTMAX_SCALE).astype(BF16)
            dk_sc[...] += lax.dot_general(ds, qt, _DIMS["tn"], preferred_element_type=F32)
            row = pl.multiple_of(qi * t, t)
            dq_ref[pl.ds(row, t), :] += jnp.dot(ds, kcat, preferred_element_type=F32)

        _on_masked_or_not(qi, ki, step)

        @pl.when(qi == nt - 1)
        def _():
            dk = dk_sc[...]
            dkv_ref[...] = jnp.concatenate([dk[:, :QK_NOPE], dv_sc[...]], axis=1).astype(BF16)
            dkpe_ref[0] = dk[:, QK_NOPE:]

    qmap = lambda h, p, qt, kt: (qt[p], h)
    return pl.pallas_call(
        body, name="flash_bwd",
        grid_spec=pltpu.PrefetchScalarGridSpec(
            num_scalar_prefetch=2, grid=(HEADS, q_tab.shape[0]),
            in_specs=[pl.BlockSpec((t, HEAD_W), qmap),
                      pl.BlockSpec((t, HEAD_W), lambda h, p, qt, kt: (kt[p], h)),
                      pl.BlockSpec((t, ROPE_BLOCK), lambda h, p, qt, kt: (kt[p], 0)),
                      pl.BlockSpec((t, V_DIM), qmap),
                      pl.BlockSpec((1, t, LANE), lambda h, p, qt, kt: (h, qt[p], 0)),
                      pl.BlockSpec((t, V_DIM), qmap)],
            out_specs=[pl.BlockSpec((lp, HEAD_W), lambda h, p, qt, kt: (0, h)),
                       pl.BlockSpec((t, HEAD_W), lambda h, p, qt, kt: (kt[p], h)),
                       pl.BlockSpec((1, t, ROPE_BLOCK), lambda h, p, qt, kt: (h, kt[p], 0))],
            scratch_shapes=[pltpu.VMEM((t, HEAD_W), F32), pltpu.VMEM((t, V_DIM), F32)]),
        out_shape=[jax.ShapeDtypeStruct((lp, Q_COLS), F32), jax.ShapeDtypeStruct((lp, Q_COLS), BF16),
                   jax.ShapeDtypeStruct((HEADS, lp, ROPE_BLOCK), F32)],
        compiler_params=_cparams(("parallel", "arbitrary")),
    )(q_tab, k_tab, q, kv, kpe, o32, lse, do)


def _ij(i, j):
    return (i, j)


def mixer_fwd(a2, w, tabs, pool_scale, g_q, g_kv, ex):
    lp, d = a2.shape
    tm = _m_tile(lp)
    tn = 512
    z = _mm_plain("mix_in", "nn", a2, w["w_in"], tm, 1280, F32)
    yp = pool_fwd(z, w["pool_w"], pool_scale)
    qn, kvn, kpe = mla_prep(z, g_q, g_kv, tabs)
    q = q_proj(qn, w["w_q_b"], tabs)
    kv = kv_proj(kvn, w["w_kv_b"])
    o32, o16, lse = flash_fwd(q, kv, kpe)
    ex.point("flash_fwd_done", o16)
    y_pool = _mm_plain("pool_out", "nn", yp, w["w_pool_o"], tm, tn, F32)

    def epi(acc, ypl, gp, gm):
        return jax.nn.sigmoid(gp) * ypl + jax.nn.sigmoid(gm) * acc, acc

    y, y_mla = _mm("mla_out_gate", "nn", o16, w["w_mla_o"], tm, tn,
                   [((lp, d), BF16, (tm, tn), _ij), ((lp, d), F32, (tm, tn), _ij)], epi=epi,
                   extras=[(y_pool, (tm, tn), _ij), (z, (tm, tn), lambda i, j: (i, Z_GP // tn + j)),
                           (z, (tm, tn), lambda i, j: (i, Z_GM // tn + j))])
    m = _mm_plain("mix_out", "nn", y, w["w_out"], tm, tn, F32)
    return m, dict(z=z, yp=yp, qn=qn, kvn=kvn, kpe=kpe, q=q, kv=kv, o32=o32, o16=o16, lse=lse,
                   y_pool=y_pool, y_mla=y_mla, y=y)


def mixer_bwd(dm, a2, sv, w, tabs, pool_scale, g_q, g_kv):
    lp, d = dm.shape
    tm = _m_tile(lp)
    tn = 512
    z = sv["z"]

    def epi(acc, ypl, yml, gp, gm):
        sp, sm = jax.nn.sigmoid(gp), jax.nn.sigmoid(gm)
        return acc * sp, acc * sm, acc * ypl * (sp * (1.0 - sp)), acc * yml * (sm * (1.0 - sm))

    dyp, dym, dgp, dgm = _mm(
        "gate_bwd", "nt", dm, w["w_out"], tm, tn, [((lp, d), BF16, (tm, tn), _ij)] * 4, epi=epi,
        extras=[(sv["y_pool"], (tm, tn), _ij), (sv["y_mla"], (tm, tn), _ij),
                (z, (tm, tn), lambda i, j: (i, Z_GP // tn + j)), (z, (tm, tn), lambda i, j: (i, Z_GM // tn + j))])
    g = {}
    g["w_out"] = _mm_plain("dw_out", "tn", sv["y"], dm, 1024, 1024, BF16)
    g["w_pool_o"] = _mm_plain("dw_pool_o", "tn", sv["yp"], dyp, 512, 1024, BF16)
    dypre = _mm_plain("pool_out_bwd", "nt", dyp, w["w_pool_o"], tm, tn, F32)
    du, g["pool_w"], d_pool_scale = pool_bwd(dypre, z, w["pool_w"], pool_scale)
    g["w_mla_o"] = _mm_plain("dw_mla_o", "tn", sv["o16"], dym, 1024, 1024, BF16)
    do = _mm_plain("mla_out_bwd", "nt", dym, w["w_mla_o"], tm, tn, F32)
    dq, dkv, dkpe_h = flash_bwd(sv["q"], sv["kv"], sv["kpe"], sv["o32"], sv["lse"], do)
    dql = q_rope_bwd(dq, tabs)
    g["w_q_b"] = _mm_plain("dw_q_b", "tn", sv["qn"], dql, 512, 1024, BF16)
    dqn = _mm_plain("q_proj_bwd", "nt", dql, w["w_q_b"], tm, 512, F32)
    g["w_kv_b"] = _mm_plain("dw_kv_b", "tn", sv["kvn"], dkv, 512, 1024, BF16)
    dkvn = _mm_plain("kv_proj_bwd", "nt", dkv, w["w_kv_b"], tm, 512, F32)
    dcq, dckv, dkr, d_gq, d_gkv = mla_prep_bwd(dqn, dkvn, dkpe_h, z, g_q, g_kv, tabs)
    dz = jnp.concatenate([du, dcq, dckv, dgp, dgm, dkr, jnp.zeros((lp, Z_COLS - Z_KR - ROPE_BLOCK), BF16)], axis=1)
    g["w_in"] = _mm_plain("dw_in", "tn", a2, dz, 1024, 1280, BF16)
    da2 = _mm_plain("mix_in_bwd", "nt", dz, w["w_in"], tm, tn, F32)
    return da2, g, dict(pool_scale=d_pool_scale, q_a_norm=d_gq, kv_a_norm=d_gkv)


_ANY = pl.BlockSpec(memory_space=pl.ANY)
_MESH = pl.DeviceIdType.MESH


def _my_pos():
    return lax.axis_index("x"), lax.axis_index("y"), lax.axis_index("c")


LEAD = "lead"
COLS = "cols"
COLS_GU = "cols_gu"


def _col_block(layout, dev):
    return dev if layout == COLS else 2 * (dev % 4) + dev // 4


def _dev_block(ref, layout, dev, cols):
    if layout == LEAD:
        return ref.at[dev]
    return ref.at[:, pl.ds(pl.multiple_of(_col_block(layout, dev) * cols, LANE), cols)]


def _gathered_shape(shard_shape, layout):
    if layout == LEAD:
        return (N_DEV, *shard_shape)
    return (shard_shape[0], N_DEV * shard_shape[1])


def all_gather(name, shards, layouts):
    n = len(shards)

    def body(*refs):
        ins, outs = refs[:n], refs[n:2 * n]
        send_sems, recv_sems, local_sems = refs[2 * n:]
        x, y, c = _my_pos()
        me, sibling = (x, y, c), (x, y, 1 - c)
        chips = [(1 - x, y), (x, 1 - y), (1 - x, 1 - y)]

        def blk(a, px, py, pc):
            return _dev_block(outs[a], layouts[a], 4 * px + 2 * py + pc, shards[a].shape[-1])

        def copy(a, k, block, to, src=None):
            return pltpu.make_async_remote_copy(
                src_ref=blk(a, *block) if src is None else src, dst_ref=blk(a, *block),
                send_sem=send_sems.at[a, k], recv_sem=recv_sems.at[a, k], device_id=to, device_id_type=_MESH)

        mine = [pltpu.make_async_copy(ins[a], blk(a, *me), local_sems.at[a]) for a in range(n)]
        for cp in mine:
            cp.start()
        first = []
        for a in range(n):
            first.append(copy(a, 0, me, sibling, src=ins[a]))
            first += [copy(a, 1 + j, me, (*chip, c), src=ins[a]) for j, chip in enumerate(chips)]
        for cp in first:
            cp.start()
        passed = []
        for j, chip in enumerate(chips):
            for a in range(n):
                copy(a, 1 + j, (*chip, c), me).wait_recv()
                fwd = copy(a, 4 + j, (*chip, c), sibling)
                fwd.start()
                passed.append(fwd)
        for a in range(n):
            copy(a, 0, sibling, me).wait_recv()
            for j, chip in enumerate(chips):
                copy(a, 4 + j, (*chip, 1 - c), me).wait_recv()
        for cp in first + passed:
            cp.wait_send()
        for cp in mine:
            cp.wait()

    return pl.pallas_call(
        body, name=name,
        in_specs=[_ANY] * n, out_specs=[_ANY] * n,
        out_shape=[jax.ShapeDtypeStruct(_gathered_shape(s.shape, lay), s.dtype)
                   for s, lay in zip(shards, layouts, strict=True)],
        scratch_shapes=[pltpu.SemaphoreType.DMA((n, 7)), pltpu.SemaphoreType.DMA((n, 7)), pltpu.SemaphoreType.DMA((n,))],
    )(*shards)


def _shard_shape(grad, layout):
    return grad.shape[1:] if layout == LEAD else (grad.shape[0], grad.shape[1] // N_DEV)


def rs_sibling(name, grads, layouts):
    n = len(grads)

    def body(*refs):
        ins, outs = refs[:n], refs[n:2 * n]
        send_sems, recv_sems = refs[2 * n:]
        x, y, c = _my_pos()
        cps = []
        for a in range(n):
            for k in range(4):
                cp = pltpu.make_async_remote_copy(
                    src_ref=_dev_block(ins[a], layouts[a], 2 * k + (1 - c), outs[a].shape[-1]), dst_ref=outs[a].at[k],
                    send_sem=send_sems.at[a, k], recv_sem=recv_sems.at[a, k],
                    device_id=(x, y, 1 - c), device_id_type=_MESH)
                cp.start()
                cps.append(cp)
        for cp in cps:
            cp.wait()

    return pl.pallas_call(
        body, name=name,
        in_specs=[_ANY] * n, out_specs=[_ANY] * n,
        out_shape=[jax.ShapeDtypeStruct((4, *_shard_shape(g, lay)), g.dtype) for g, lay in zip(grads, layouts, strict=True)],
        scratch_shapes=[pltpu.SemaphoreType.DMA((n, 4)), pltpu.SemaphoreType.DMA((n, 4))],
    )(*grads)


def rs_chips(name, sums):
    n = len(sums)

    def body(*refs):
        ins, outs = refs[:n], refs[n:2 * n]
        send_sems, recv_sems = refs[2 * n:]
        x, y, c = _my_pos()
        chips = [(1 - x, y), (x, 1 - y), (1 - x, 1 - y)]
        cps = []
        for a in range(n):
            for j, chip in enumerate(chips):
                cp = pltpu.make_async_remote_copy(
                    src_ref=ins[a].at[2 * chip[0] + chip[1]], dst_ref=outs[a].at[j],
                    send_sem=send_sems.at[a, j], recv_sem=recv_sems.at[a, j],
                    device_id=(*chip, c), device_id_type=_MESH)
                cp.start()
                cps.append(cp)
        for cp in cps:
            cp.wait()

    return pl.pallas_call(
        body, name=name,
        in_specs=[_ANY] * n, out_specs=[_ANY] * n,
        out_shape=[jax.ShapeDtypeStruct((3, *s.shape[1:]), s.dtype) for s in sums],
        scratch_shapes=[pltpu.SemaphoreType.DMA((n, 3)), pltpu.SemaphoreType.DMA((n, 3))],
    )(*sums)


_HBM = pl.BlockSpec(memory_space=pltpu.HBM)
_SEM = pl.BlockSpec(memory_space=pltpu.SEMAPHORE)
_EFFECT = pltpu.SideEffectType.DATAFLOW_SIDE_EFFECTING


def _in_hbm(a):
    return pltpu.with_memory_space_constraint(a, pltpu.HBM)


def split_start(name, bufs, plan, n_copies, after=None):
    nb = len(bufs)
    extra = [] if after is None else [after]

    def body(*refs):
        buf_refs = refs[:nb]
        send_sems, recv_sems = refs[nb + len(extra)], refs[nb + len(extra) + 1]
        token = refs[-1]
        copies = plan(buf_refs)
        assert len(copies) == n_copies
        for k, (src, dst, to) in enumerate(copies):
            pltpu.make_async_remote_copy(src_ref=src, dst_ref=dst, send_sem=send_sems.at[k], recv_sem=recv_sems.at[k],
                                         device_id=to, device_id_type=_MESH).start()
        token[...] = jnp.zeros_like(token)

    out = pl.pallas_call(
        body, name=name,
        out_shape=(pltpu.SemaphoreType.DMA((n_copies,)), pltpu.SemaphoreType.DMA((n_copies,)),
                   *[pltpu.HBM(b.shape, b.dtype) for b in bufs], jax.ShapeDtypeStruct((8, LANE), F32)),
        in_specs=[_HBM] * nb + [_ANY] * len(extra),
        out_specs=(_SEM, _SEM, *[_HBM] * nb, pl.BlockSpec(memory_space=pltpu.VMEM)),
        input_output_aliases={i: 2 + i for i in range(nb)},
        compiler_params=pltpu.CompilerParams(has_side_effects=_EFFECT),
    )(*[_in_hbm(b) for b in bufs], *extra)
    return out[0], out[1], list(out[2:2 + nb]), out[-1]


def split_wait(name, bufs, send_sems, recv_sems, plan, after):
    nb = len(bufs)

    def body(*refs):
        buf_refs = refs[:nb]
        s_sems, r_sems = refs[nb], refs[nb + 1]
        for k, (src, dst, to) in enumerate(plan(buf_refs)):
            cp = pltpu.make_async_remote_copy(src_ref=src, dst_ref=dst, send_sem=s_sems.at[k], recv_sem=r_sems.at[k],
                                              device_id=to, device_id_type=_MESH)
            cp.wait_send()
            cp.wait_recv()

    out = pl.pallas_call(
        body, name=name,
        out_shape=tuple(pltpu.HBM(b.shape, b.dtype) for b in bufs),
        in_specs=[_HBM] * nb + [_SEM, _SEM, _ANY],
        out_specs=tuple([_HBM] * nb),
        input_output_aliases={i: i for i in range(nb)},
        compiler_params=pltpu.CompilerParams(has_side_effects=_EFFECT),
    )(*bufs, send_sems, recv_sems, after)
    return list(out)


def _ag_own_plan(shapes, layouts):
    n = len(shapes)

    def plan(refs):
        x, y, c = _my_pos()
        targets = [(x, y, 1 - c), (1 - x, y, c), (x, 1 - y, c), (1 - x, 1 - y, c)]
        return [(refs[a], _dev_block(refs[n + a], layouts[a], 4 * x + 2 * y + c, shapes[a][-1]), to)
                for a in range(n) for to in targets]

    return plan, 4 * n


def _ag_pass_plan(shapes, layouts):
    n = len(shapes)

    def plan(refs):
        x, y, c = _my_pos()
        out = []
        for a in range(n):
            for px, py in [(1 - x, y), (x, 1 - y), (1 - x, 1 - y)]:
                blk = _dev_block(refs[a], layouts[a], 4 * px + 2 * py + c, shapes[a][-1])
                out.append((blk, blk, (x, y, 1 - c)))
        return out

    return plan, 3 * n


def _rs_sibling_plan(layouts, n):
    def plan(refs):
        x, y, c = _my_pos()
        return [(_dev_block(refs[a], layouts[a], 2 * k + (1 - c), refs[n + a].shape[-1]), refs[n + a].at[k], (x, y, 1 - c))
                for a in range(n) for k in range(4)]

    return plan, 4 * n


def _rs_chips_plan(n):
    def plan(refs):
        x, y, c = _my_pos()
        return [(refs[a].at[2 * px + py], refs[n + a].at[j], (px, py, c))
                for a in range(n) for j, (px, py) in enumerate([(1 - x, y), (x, 1 - y), (1 - x, 1 - y)])]

    return plan, 3 * n


def _own_block_placed(shard, layout, dev):
    r, c = shard.shape
    land = lax.empty(_gathered_shape(shard.shape, layout), shard.dtype)
    zero = jnp.zeros((), jnp.int32)
    if layout == LEAD:
        return lax.dynamic_update_slice(land, shard[None], (dev.astype(jnp.int32), zero, zero))
    return lax.dynamic_update_slice(land, shard, (zero, (_col_block(layout, dev) * c).astype(jnp.int32)))


def _ew_rows(r, c):
    for t in (512, 256, 128, 64, 32, 16):
        if r % t == 0 and t * c * 4 <= 768 * 1024:
            return t
    raise ValueError((r, c))


def rs_add(name, grad, layout, recv, core):
    _, r, c = recv.shape
    tr = _ew_rows(r, c)
    if layout == LEAD:
        g_spec = pl.BlockSpec((None, tr, c), lambda k, i, core_ref: (2 * k + core_ref[0], i, 0))
    else:
        g_spec = pl.BlockSpec((tr, c), lambda k, i, core_ref: (i, _col_block(layout, 2 * k + core_ref[0])))

    def body(core_ref, g_ref, r_ref, o_ref):
        o_ref[...] = (g_ref[...].astype(F32) + r_ref[...].astype(F32)).astype(BF16)

    return pl.pallas_call(
        body, name=name,
        grid_spec=pltpu.PrefetchScalarGridSpec(
            num_scalar_prefetch=1, grid=(4, r // tr),
            in_specs=[g_spec, pl.BlockSpec((None, tr, c), lambda k, i, core_ref: (k, i, 0))],
            out_specs=pl.BlockSpec((None, tr, c), lambda k, i, core_ref: (k, i, 0))),
        out_shape=jax.ShapeDtypeStruct((4, r, c), BF16),
        compiler_params=_cparams(("parallel", "parallel")),
    )(core, grad, recv)


def _adamw(w, g, m, v):
    m = ADAM_B1 * m + (1.0 - ADAM_B1) * g
    v = ADAM_B2 * v + (1.0 - ADAM_B2) * jnp.square(g)
    m_hat = m / (1.0 - ADAM_B1 ** ADAM_STEP)
    v_hat = v / (1.0 - ADAM_B2 ** ADAM_STEP)
    delta = -ADAM_LR * (m_hat / (jnp.sqrt(v_hat) + ADAM_EPS) + ADAM_WD * w)
    return delta, m, v


def adamw_shard(name, w, m, v, sums, recv, chip):
    r, c = w.shape
    tr = _ew_rows(r, c)

    def body(chip_ref, w_ref, m_ref, v_ref, s_ref, r_ref, g_ref, d_ref, mo_ref, vo_ref):
        g = s_ref[0].astype(F32)
        for j in range(3):
            g = g + r_ref[j].astype(F32)
        d, mn, vn = _adamw(w_ref[...], g, m_ref[...], v_ref[...])
        g_ref[...] = g
        d_ref[...] = d
        mo_ref[...] = mn
        vo_ref[...] = vn

    spec = pl.BlockSpec((tr, c), lambda i, chip_ref: (i, 0))
    return pl.pallas_call(
        body, name=name,
        grid_spec=pltpu.PrefetchScalarGridSpec(
            num_scalar_prefetch=1, grid=(r // tr,),
            in_specs=[spec, spec, spec,
                      pl.BlockSpec((1, tr, c), lambda i, chip_ref: (chip_ref[0], i, 0)),
                      pl.BlockSpec((3, tr, c), lambda i, chip_ref: (0, i, 0))],
            out_specs=[spec] * 4),
        out_shape=[jax.ShapeDtypeStruct((r, c), F32)] * 4,
        compiler_params=_cparams(("parallel",)),
    )(chip, w, m, v, sums, recv)


def reduce_small(gathered):
    _, r, c = gathered.shape

    def body(g_ref, o_ref):
        acc = g_ref[0]
        for k in range(1, N_DEV):
            acc = acc + g_ref[k]
        o_ref[...] = acc

    return pl.pallas_call(body, name="reduce_small", out_shape=jax.ShapeDtypeStruct((r, c), F32))(gathered)


def adamw_small(ws, gs, ms, vs):
    n = len(ws)

    def body(*refs):
        w_r, g_r, m_r, v_r = refs[:n], refs[n:2 * n], refs[2 * n:3 * n], refs[3 * n:4 * n]
        d_o, m_o, v_o = refs[4 * n:5 * n], refs[5 * n:6 * n], refs[6 * n:7 * n]
        for a in range(n):
            d, mn, vn = _adamw(w_r[a][...], g_r[a][...], m_r[a][...], v_r[a][...])
            d_o[a][...] = d
            m_o[a][...] = mn
            v_o[a][...] = vn

    shapes = [jax.ShapeDtypeStruct(w.shape, F32) for w in ws]
    out = pl.pallas_call(body, name="adamw_small", out_shape=shapes * 3)(*ws, *gs, *ms, *vs)
    return out[:n], out[n:2 * n], out[2 * n:]


WEIGHTS = ["meta_tokens", "norm_ffn1_pre", "norm_ffn1_post", "ffn1_w_gu", "ffn1_w_down", "norm_mix_pre",
           "norm_mix_post", "w_in", "pool_w", "pool_scale", "w_pool_o", "q_a_norm", "w_q_b", "kv_a_norm", "w_kv_b",
           "w_mla_o", "w_out", "norm_ffn2_pre", "norm_ffn2_post", "ffn2_w_gu", "ffn2_w_down"]
BIG = ["ffn1_w_gu", "ffn1_w_down", "w_in", "pool_w", "w_pool_o", "w_q_b", "w_kv_b", "w_mla_o", "w_out",
       "ffn2_w_gu", "ffn2_w_down"]
COL_SHARDED = ("w_in", "w_q_b")
GATHERED = {"ffn1_w_gu": COLS_GU, "ffn2_w_gu": COLS_GU, "w_pool_o": COLS, "w_kv_b": COLS}
GAINS =["norm_ffn1_pre", "norm_ffn1_post", "norm_mix_pre", "norm_mix_post", "norm_ffn2_pre", "norm_ffn2_post"]
SMALL = GAINS + ["pool_scale", "q_a_norm", "kv_a_norm"]
Z_SRC = 1024 + 512 + 512 + QK_ROPE


def _full_from_gathered(name, g):
    _, r, c = g.shape
    if name == "pool_w":
        ng = len(POOL_WINDOWS)
        return g.reshape(N_DEV, ng, r // ng, c).transpose(1, 0, 2, 3).reshape(ng, POOL_GROUP, POOL_GROUP)
    if name in COL_SHARDED:
        return g.transpose(1, 0, 2).reshape(r, N_DEV * c)
    return g.reshape(N_DEV * r, c)


def _blocks_from_full(name, dw):
    if name == "pool_w":
        ng = len(POOL_WINDOWS)
        return dw.reshape(ng, N_DEV, POOL_GROUP // N_DEV, POOL_GROUP).transpose(1, 0, 2, 3).reshape(
            N_DEV, ng * POOL_GROUP // N_DEV, POOL_GROUP)
    k, n = dw.shape
    if name in COL_SHARDED:
        return dw.reshape(k, N_DEV, n // N_DEV).transpose(1, 0, 2)
    return dw.reshape(N_DEV, k // N_DEV, n)


def _to_internal(name, w):
    if name == "w_in":
        d = w.shape[0]
        return jnp.concatenate([w[:, :Z_SRC - QK_ROPE], w[:, Z_SRC:], w[:, Z_SRC - QK_ROPE:Z_SRC],
                                jnp.zeros((d, Z_COLS - Z_KR - QK_ROPE), w.dtype)], axis=1)
    if name == "w_q_b":
        r = w.shape[0]
        w3 = w.reshape(r, HEADS, QK_NOPE + QK_ROPE)
        return jnp.pad(w3, ((0, 0), (0, 0), (0, HEAD_W - QK_NOPE - QK_ROPE))).reshape(r, Q_COLS)
    return w


def _from_internal(name, dw):
    if name == "w_in":
        return jnp.concatenate([dw[:, :Z_SRC - QK_ROPE], dw[:, Z_KR:Z_KR + QK_ROPE], dw[:, Z_SRC - QK_ROPE:Z_KR]], axis=1)
    if name == "w_q_b":
        r = dw.shape[0]
        return dw.reshape(r, HEADS, HEAD_W)[:, :, :QK_NOPE + QK_ROPE].reshape(r, HEADS * (QK_NOPE + QK_ROPE))
    return dw


def _shard2d(a):
    return a.reshape(-1, a.shape[-1])


def kernel(x, meta_tokens, norm_ffn1_pre, norm_ffn1_post, ffn1_w_gu, ffn1_w_down, norm_mix_pre, norm_mix_post, w_in, pool_w, pool_scale, w_pool_o, q_a_norm, w_q_b, kv_a_norm, w_kv_b, w_mla_o, w_out, norm_ffn2_pre, norm_ffn2_post, ffn2_w_gu, ffn2_w_down, loss_target, m_meta_tokens, m_norm_ffn1_pre, m_norm_ffn1_post, m_ffn1_w_gu, m_ffn1_w_down, m_norm_mix_pre, m_norm_mix_post, m_w_in, m_pool_w, m_pool_scale, m_w_pool_o, m_q_a_norm, m_w_q_b, m_kv_a_norm, m_w_kv_b, m_w_mla_o, m_w_out, m_norm_ffn2_pre, m_norm_ffn2_post, m_ffn2_w_gu, m_ffn2_w_down, v_meta_tokens, v_norm_ffn1_pre, v_norm_ffn1_post, v_ffn1_w_gu, v_ffn1_w_down, v_norm_mix_pre, v_norm_mix_post, v_w_in, v_pool_w, v_pool_scale, v_w_pool_o, v_q_a_norm, v_w_q_b, v_kv_a_norm, v_w_kv_b, v_w_mla_o, v_w_out, v_norm_ffn2_pre, v_norm_ffn2_post, v_ffn2_w_gu, v_ffn2_w_down):
    given = dict(locals())
    w_in_dev = {n: given[n] for n in WEIGHTS}
    m_in = {n: given["m_" + n] for n in WEIGHTS}
    v_in = {n: given["v_" + n] for n in WEIGHTS}
    xi, yi, ci = _my_pos()
    dev = 4 * xi + 2 * yi + ci
    core = jnp.reshape(ci, (1,)).astype(jnp.int32)
    chip = jnp.reshape(2 * xi + yi, (1,)).astype(jnp.int32)
    d = x.shape[-1]

    shards = {n: _shard2d(w_in_dev[n]) for n in BIG}
    ex = _Exchange(shards, meta_tokens, dev, core)
    gain = {n: given[n] for n in SMALL}
    loss_blk, grad_x, front, gsmall = local_step(x[0], loss_target[0], gain, ex)

    out_g, out_d, out_m, out_v = {}, {}, {}, {}

    def finish(grp, after):
        names, sums, from_chips = ex.finish_grads(grp, after)
        for n, s, r in zip(names, sums, from_chips, strict=True):
            shp = w_in_dev[n].shape
            res = adamw_shard("adamw_" + n, shards[n], _shard2d(m_in[n]), _shard2d(v_in[n]), s, r, chip)
            out_g[n], out_d[n], out_m[n], out_v[n] = [t.reshape(shp) for t in res]
        return res[0]

    finish("B", finish("C", grad_x))

    tail = jnp.concatenate([gsmall["pool_scale"], gsmall["q_a_norm"], gsmall["kv_a_norm"]], axis=1)
    small = jnp.concatenate([gsmall[n] for n in GAINS] + [tail, jnp.broadcast_to(loss_blk[:1, :1], (1, d)),
                                                         front[PAD_ROWS:]], axis=0)
    (small_g,) = all_gather("ag_small", [small], [LEAD])
    total = reduce_small(small_g)
    ng = len(GAINS)
    for i, n in enumerate(GAINS):
        out_g[n] = total[i:i + 1]
    o = 0
    for n in ("pool_scale", "q_a_norm", "kv_a_norm"):
        wdt = w_in_dev[n].shape[1]
        out_g[n] = total[ng:ng + 1, o:o + wdt]
        o += wdt
    loss = total[ng + 1, 0]
    mcols = meta_tokens.shape[1]
    out_g["meta_tokens"] = lax.dynamic_slice(total[ng + 2:ng + 2 + N_META], (0, dev * mcols), (N_META, mcols))
    names = ["meta_tokens"] + SMALL
    ds_, ms_, vs_ = adamw_small([w_in_dev[n] for n in names], [out_g[n] for n in names],
                                [m_in[n] for n in names], [v_in[n] for n in names])
    for n, dd, mm, vv in zip(names, ds_, ms_, vs_, strict=True):
        out_d[n], out_m[n], out_v[n] = dd, mm, vv
    finish("A", ds_[0])

    return (loss, grad_x[None], *[out_g[n] for n in WEIGHTS], *[out_d[n] for n in WEIGHTS],
            *[out_m[n] for n in WEIGHTS], *[out_v[n] for n in WEIGHTS])


GROUPS = {"A": ["ffn1_w_gu", "ffn1_w_down"],
          "B": ["w_in", "pool_w", "w_pool_o", "w_q_b", "w_kv_b", "w_mla_o", "w_out"],
          "C": ["ffn2_w_gu", "ffn2_w_down"]}


class _Exchange:
    def __init__(self, shards, meta_tokens, dev, core):
        self.shards, self.meta_tokens, self.dev, self.core = shards, meta_tokens, dev, core
        self.w, self.meta_full, self.token = {}, None, None
        self._ag, self._rs = {}, {}

    def tie(self, vec):
        return vec if self.token is None else vec + self.token[:1, :1]

    def _ag_start(self, grp, after):
        names = GROUPS[grp]
        arrs = [self.shards[n].astype(BF16) for n in names] + ([self.meta_tokens] if grp == "A" else [])
        lays = [GATHERED.get(n, LEAD) for n in names] + ([LEAD] if grp == "A" else [])
        shapes = [a.shape for a in arrs]
        lands = [_own_block_placed(a, lay, self.dev) for a, lay in zip(arrs, lays, strict=True)]
        plan, cnt = _ag_own_plan(shapes, lays)
        ss, rs, bufs, self.token = split_start(f"ag{grp}_own_start", arrs + lands, plan, cnt, after)
        self._ag[grp] = dict(names=names, lays=lays, shapes=shapes, own=(ss, rs, bufs, plan))

    def _ag_pass(self, grp, after):
        st = self._ag[grp]
        ss, rs, bufs, plan = st["own"]
        lands = split_wait(f"ag{grp}_own_wait", bufs, ss, rs, plan, after)[len(st["shapes"]):]
        plan, cnt = _ag_pass_plan(st["shapes"], st["lays"])
        ss, rs, lands, self.token = split_start(f"ag{grp}_pass_start", lands, plan, cnt)
        st["pass"] = (ss, rs, lands, plan)

    def _ag_finish(self, grp, after):
        st = self._ag[grp]
        ss, rs, lands, plan = st["pass"]
        lands = split_wait(f"ag{grp}_pass_wait", lands, ss, rs, plan, after)
        for n, g, lay in zip(st["names"], lands, st["lays"]):
            self.w[n] = g if lay != LEAD else _to_internal(n, _full_from_gathered(n, g))
        if grp == "A":
            meta_g = lands[-1]
            self.meta_full = meta_g.transpose(1, 0, 2).reshape(N_META, N_DEV * meta_g.shape[-1])

    def grads(self, grp, gbig, after):
        names = GROUPS[grp]
        lays = [GATHERED.get(n, LEAD) for n in names]
        grads = [gbig[n] if lay != LEAD else _blocks_from_full(n, _from_internal(n, gbig[n]).astype(BF16))
                 for n, lay in zip(names, lays, strict=True)]
        lands = [lax.empty((4, *_shard_shape(g, lay)), BF16) for g, lay in zip(grads, lays, strict=True)]
        plan, cnt = _rs_sibling_plan(lays, len(names))
        ss, rs, bufs, self.token = split_start(f"rs{grp}_sibling_start", grads + lands, plan, cnt, after)
        self._rs[grp] = dict(names=names, lays=lays, sib=(ss, rs, bufs, plan))

    def _rs_mid(self, grp, after):
        st = self._rs[grp]
        n = len(st["names"])
        ss, rs, bufs, plan = st["sib"]
        bufs = split_wait(f"rs{grp}_sibling_wait", bufs, ss, rs, plan, after)
        sums = [rs_add(f"rs_add_{name}", g, lay, r, self.core)
                for name, g, lay, r in zip(st["names"], bufs[:n], st["lays"], bufs[n:], strict=True)]
        lands = [lax.empty((3, *s.shape[1:]), BF16) for s in sums]
        plan, cnt = _rs_chips_plan(n)
        ss, rs, bufs, self.token = split_start(f"rs{grp}_chips_start", sums + lands, plan, cnt)
        st["chips"] = (ss, rs, bufs, plan)

    def finish_grads(self, grp, after):
        st = self._rs[grp]
        n = len(st["names"])
        ss, rs, bufs, plan = st["chips"]
        bufs = split_wait(f"rs{grp}_chips_wait", bufs, ss, rs, plan, after)
        return st["names"], bufs[:n], bufs[n:]

    def point(self, name, after=None):
        if name == "start":
            self._ag_start("A", None)
            self._ag_pass("A", self.token)
            self._ag_start("B", self.token)
            self._ag_finish("A", self.token)
        elif name == "ffn1_fwd_done":
            self._ag_pass("B", after)
            self._ag_start("C", self.token)
        elif name == "mix_pre_done":
            self._ag_finish("B", after)
        elif name == "flash_fwd_done":
            self._ag_pass("C", after)
        elif name == "ffn2_pre_done":
            self._ag_finish("C", after)
        elif name in ("rsC_mid", "rsB_mid", "rsA_mid"):
            self._rs_mid(name[2], after)


def local_step(x, target, gain, ex):
    d = x.shape[-1]
    ex.point("start")
    w = ex.w
    h0 = jnp.concatenate([jnp.zeros((PAD_ROWS, d), F32), ex.meta_full, x], axis=0)
    lp = h0.shape[0]
    tabs = rope_tables(lp)
    a1 = prenorm(h0, ex.tie(gain["norm_ffn1_pre"]))
    gu1, s1, f1 = ffn_fwd("ffn1", a1, w["ffn1_w_gu"], w["ffn1_w_down"])
    ex.point("ffn1_fwd_done", f1)
    h1, a2 = post_pre("post_pre1", f1, h0, ex.tie(gain["norm_ffn1_post"]), 0.5, gain["norm_mix_pre"])
    ex.point("mix_pre_done", a2)
    mix, sv = mixer_fwd(a2, w, tabs, gain["pool_scale"], gain["q_a_norm"], gain["kv_a_norm"], ex)
    h2, a3 = post_pre("post_pre2", mix, h1, gain["norm_mix_post"], 1.0, gain["norm_ffn2_pre"])
    ex.point("ffn2_pre_done", a3)
    gu2, s2, f2 = ffn_fwd("ffn2", a3, w["ffn2_w_gu"], w["ffn2_w_down"])
    dh3, loss_blk = post_loss(f2, h2, gain["norm_ffn2_post"], 0.5, target)

    gsmall = {}
    df2, gsmall["norm_ffn2_post"] = post_bwd(dh3, f2, gain["norm_ffn2_post"], 0.5)
    da3 = ffn_bwd("ffn2", df2, a3, gu2, s2, w["ffn2_w_gu"], w["ffn2_w_down"], lambda g, aft: ex.grads("C", g, aft))
    dh2, dmix, gsmall["norm_ffn2_pre"], gsmall["norm_mix_post"] = pre_post_bwd(
        "pre_post_bwd2", da3, h2, gain["norm_ffn2_pre"], dh3, mix, ex.tie(gain["norm_mix_post"]), 1.0)
    ex.point("rsC_mid", dmix)
    da2, gmix, gmix_small = mixer_bwd(dmix, a2, sv, w, tabs, ex.tie(gain["pool_scale"]), gain["q_a_norm"], gain["kv_a_norm"])
    ex.grads("B", gmix, da2)
    gsmall.update(gmix_small)
    dh1, df1, gsmall["norm_mix_pre"], gsmall["norm_ffn1_post"] = pre_post_bwd(
        "pre_post_bwd1", da2, h1, gain["norm_mix_pre"], dh2, f1, ex.tie(gain["norm_ffn1_post"]), 0.5)
    ex.point("rsB_mid", df1)
    da1 = ffn_bwd("ffn1", df1, a1, gu1, s1, w["ffn1_w_gu"], w["ffn1_w_down"], lambda g, aft: ex.grads("A", g, aft))
    ex.point("rsA_mid", da1)
    grad_x, front, gsmall["norm_ffn1_pre"] = pre_bwd_first(da1, h0, ex.tie(gain["norm_ffn1_pre"]), dh1)
    return loss_blk, grad_x, front, gsmall
```

```python
import functools

import jax
import jax.numpy as jnp
import numpy as np
from jax import lax
from jax.experimental import pallas as pl
from jax.experimental.pallas import tpu as pltpu

F32 = jnp.float32
BF16 = jnp.bfloat16

N_META = 16
POOL_WINDOWS = (2, 4, 8, 16)
POOL_GROUP = 256
HEADS = 16
QK_NOPE = 128
QK_ROPE = 64
V_DIM = 128
ROPE_THETA = 10000.0
SOFTMAX_SCALE = (QK_NOPE + QK_ROPE) ** -0.5
EPS = 1e-6
ADAM_LR = 0.001
ADAM_B1 = 0.9
ADAM_B2 = 0.999
ADAM_EPS = 1e-08
ADAM_WD = 0.01
ADAM_STEP = 10

LANE = 128
FRONT = 128
PAD_ROWS = FRONT - N_META
HEAD_W = 256
GU_TILE = 1408
VMEM_LIMIT = 56 * 1024 * 1024
MESH_AXES = ("x", "y", "c")
N_DEV = 8


def _pick(n, cands):
    for c in cands:
        if n % c == 0:
            return c
    raise ValueError(f"no tile for {n} in {cands}")


def _cparams(sem=None):
    kw = dict(vmem_limit_bytes=VMEM_LIMIT)
    if sem is not None:
        kw["dimension_semantics"] = sem
    return pltpu.CompilerParams(**kw)


_DIMS = {"nn": (((1,), (0,)), ((), ())), "nt": (((1,), (1,)), ((), ())), "tn": (((0,), (0,)), ((), ()))}


def _mm(name, form, a, b, tm, tn, outs, epi=None, extras=(), n_outer=False):
    if form == "tn":
        k, m = a.shape
        n = b.shape[1]
        a_blk, a_map = (k, tm), lambda i, j: (0, i)
        b_blk, b_map = (k, tn), lambda i, j: (0, j)
    elif form == "nn":
        m, k = a.shape
        n = b.shape[1]
        a_blk, a_map = (tm, k), lambda i, j: (i, 0)
        b_blk, b_map = (k, tn), lambda i, j: (0, j)
    else:
        m, k = a.shape
        n = b.shape[0]
        a_blk, a_map = (tm, k), lambda i, j: (i, 0)
        b_blk, b_map = (tn, k), lambda i, j: (j, 0)
    assert m % tm == 0 and n % tn == 0, (name, m, n, tm, tn)
    n_ex = len(extras)
    dn = _DIMS[form]
    if n_outer:
        grid = (n // tn, m // tm)

        def spec(blk, im):
            return pl.BlockSpec(blk, lambda gj, gi: im(gi, gj))
    else:
        grid = (m // tm, n // tn)
        spec = pl.BlockSpec

    def body(a_ref, b_ref, *rest):
        ex, out_refs = rest[:n_ex], rest[n_ex:]
        acc = lax.dot_general(a_ref[...].astype(BF16), b_ref[...].astype(BF16), dn, preferred_element_type=F32)
        res = epi(acc, *[e[...] for e in ex]) if epi is not None else (acc,)
        for r, o in zip(res, out_refs, strict=True):
            o[...] = r.astype(o.dtype)

    return pl.pallas_call(
        body,
        name=name,
        grid=grid,
        in_specs=[spec(a_blk, a_map), spec(b_blk, b_map)] + [spec(blk, im) for _, blk, im in extras],
        out_specs=[spec(blk, im) for _, _, blk, im in outs],
        out_shape=[jax.ShapeDtypeStruct(s, d) for s, d, _, _ in outs],
        compiler_params=_cparams(("parallel", "parallel")),
    )(a, b, *[e for e, _, _ in extras])


def _mm_plain(name, form, a, b, tm, tn, out_dtype):
    m = a.shape[1] if form == "tn" else a.shape[0]
    n = b.shape[0] if form == "nt" else b.shape[1]
    return _mm(name, form, a, b, tm, tn, [((m, n), out_dtype, (tm, tn), lambda i, j: (i, j))])[0]


def _rstd(x):
    return lax.rsqrt(jnp.mean(x * x, axis=-1, keepdims=True) + EPS)


def _norm_bwd(dy, x, gain):
    r = _rstd(x)
    dyg = dy * gain
    dx = r * (dyg - x * (r * r) * jnp.mean(dyg * x, axis=-1, keepdims=True))
    dgain = jnp.sum(dy * x * r, axis=0, keepdims=True)
    return dx, dgain


def _row_spec(tr, cols, col_block=0):
    return pl.BlockSpec((tr, cols), lambda i: (i, col_block))


def _vec_spec(cols, col_block=0):
    return pl.BlockSpec((1, cols), lambda i: (0, col_block))


def _row_tile(lp):
    return _pick(lp, (128,))


def prenorm(h, gain):
    lp, d = h.shape
    tr = _row_tile(lp)

    def body(h_ref, g_ref, a_ref):
        x = h_ref[...]
        a_ref[...] = (x * _rstd(x) * g_ref[...]).astype(BF16)

    return pl.pallas_call(
        body, name="prenorm", grid=(lp // tr,),
        in_specs=[_row_spec(tr, d), _vec_spec(d)], out_specs=_row_spec(tr, d),
        out_shape=jax.ShapeDtypeStruct((lp, d), BF16), compiler_params=_cparams(("parallel",)),
    )(h, gain)


def post_pre(name, f, h_in, g_post, coef, g_next):
    lp, d = f.shape
    tr = _row_tile(lp)

    def body(f_ref, h_ref, gp_ref, gn_ref, ho_ref, a_ref):
        fv = f_ref[...]
        ho = h_ref[...] + coef * (fv * _rstd(fv) * gp_ref[...])
        ho_ref[...] = ho
        a_ref[...] = (ho * _rstd(ho) * gn_ref[...]).astype(BF16)

    return pl.pallas_call(
        body, name=name, grid=(lp // tr,),
        in_specs=[_row_spec(tr, d), _row_spec(tr, d), _vec_spec(d), _vec_spec(d)],
        out_specs=[_row_spec(tr, d), _row_spec(tr, d)],
        out_shape=[jax.ShapeDtypeStruct((lp, d), F32), jax.ShapeDtypeStruct((lp, d), BF16)],
        compiler_params=_cparams(("parallel",)),
    )(f, h_in, g_post, g_next)


def post_loss(f, h_in, g_post, coef, target):
    lp, d = f.shape
    tr = _row_tile(lp)
    front_tiles = FRONT // tr

    def body(f_ref, h_ref, gp_ref, t_ref, dh_ref, loss_ref):
        i = pl.program_id(0)

        @pl.when(i == 0)
        def _():
            loss_ref[...] = jnp.zeros_like(loss_ref)

        @pl.when(i < front_tiles)
        def _():
            dh_ref[...] = jnp.zeros_like(dh_ref)

        @pl.when(i >= front_tiles)
        def _():
            fv = f_ref[...]
            ho = h_ref[...] + coef * (fv * _rstd(fv) * gp_ref[...])
            err = ho - t_ref[...]
            dh_ref[...] = err / d
            tok = jnp.mean(err * err, axis=-1, keepdims=True)
            loss_ref[...] += 0.5 * jnp.sum(tok)

    return pl.pallas_call(
        body, name="post_loss", grid=(lp // tr,),
        in_specs=[_row_spec(tr, d), _row_spec(tr, d), _vec_spec(d),
                  pl.BlockSpec((tr, d), lambda i: (jnp.maximum(i - front_tiles, 0), 0))],
        out_specs=[_row_spec(tr, d), pl.BlockSpec((8, LANE), lambda i: (0, 0))],
        out_shape=[jax.ShapeDtypeStruct((lp, d), F32), jax.ShapeDtypeStruct((8, LANE), F32)],
        compiler_params=_cparams(("arbitrary",)),
    )(f, h_in, g_post, target)


def post_bwd(dh_out, f, g_post, coef):
    lp, d = f.shape
    tr = _row_tile(lp)

    def body(dh_ref, f_ref, gp_ref, df_ref, dg_ref):
        @pl.when(pl.program_id(0) == 0)
        def _():
            dg_ref[...] = jnp.zeros_like(dg_ref)

        df, dg = _norm_bwd(coef * dh_ref[...], f_ref[...], gp_ref[...])
        df_ref[...] = df.astype(BF16)
        dg_ref[...] += dg

    return pl.pallas_call(
        body, name="post_bwd", grid=(lp // tr,),
        in_specs=[_row_spec(tr, d), _row_spec(tr, d), _vec_spec(d)],
        out_specs=[_row_spec(tr, d), _vec_spec(d)],
        out_shape=[jax.ShapeDtypeStruct((lp, d), BF16), jax.ShapeDtypeStruct((1, d), F32)],
        compiler_params=_cparams(("arbitrary",)),
    )(dh_out, f, g_post)


def pre_post_bwd(name, da, h_mid, g_pre, dh_out, f_prev, g_post_prev, coef_prev):
    lp, d = da.shape
    tr = _row_tile(lp)

    def body(da_ref, h_ref, gpre_ref, dho_ref, f_ref, gpost_ref, dh_ref, df_ref, dgpre_ref, dgpost_ref):
        @pl.when(pl.program_id(0) == 0)
        def _():
            dgpre_ref[...] = jnp.zeros_like(dgpre_ref)
            dgpost_ref[...] = jnp.zeros_like(dgpost_ref)

        dx, dgpre = _norm_bwd(da_ref[...], h_ref[...], gpre_ref[...])
        dh = dho_ref[...] + dx
        dh_ref[...] = dh
        dgpre_ref[...] += dgpre
        df, dgpost = _norm_bwd(coef_prev * dh, f_ref[...], gpost_ref[...])
        df_ref[...] = df.astype(BF16)
        dgpost_ref[...] += dgpost

    return pl.pallas_call(
        body, name=name, grid=(lp // tr,),
        in_specs=[_row_spec(tr, d), _row_spec(tr, d), _vec_spec(d), _row_spec(tr, d), _row_spec(tr, d), _vec_spec(d)],
        out_specs=[_row_spec(tr, d), _row_spec(tr, d), _vec_spec(d), _vec_spec(d)],
        out_shape=[jax.ShapeDtypeStruct((lp, d), F32), jax.ShapeDtypeStruct((lp, d), BF16),
                   jax.ShapeDtypeStruct((1, d), F32), jax.ShapeDtypeStruct((1, d), F32)],
        compiler_params=_cparams(("arbitrary",)),
    )(da, h_mid, g_pre, dh_out, f_prev, g_post_prev)


def pre_bwd_first(da, h0, g_pre, dh_out):
    lp, d = da.shape
    tr = _row_tile(lp)
    front_tiles = FRONT // tr
    assert front_tiles == 1

    def body(da_ref, h_ref, gpre_ref, dho_ref, gx_ref, front_ref, dgpre_ref):
        i = pl.program_id(0)

        @pl.when(i == 0)
        def _():
            dgpre_ref[...] = jnp.zeros_like(dgpre_ref)

        dx, dgpre = _norm_bwd(da_ref[...], h_ref[...], gpre_ref[...])
        dh = dho_ref[...] + dx
        dgpre_ref[...] += dgpre
        gx_ref[...] = dh

        @pl.when(i == 0)
        def _():
            front_ref[...] = dh

    return pl.pallas_call(
        body, name="pre_bwd_first", grid=(lp // tr,),
        in_specs=[_row_spec(tr, d), _row_spec(tr, d), _vec_spec(d), _row_spec(tr, d)],
        out_specs=[pl.BlockSpec((tr, d), lambda i: (jnp.maximum(i - front_tiles, 0), 0)),
                   pl.BlockSpec((tr, d), lambda i: (0, 0)), _vec_spec(d)],
        out_shape=[jax.ShapeDtypeStruct((lp - FRONT, d), F32), jax.ShapeDtypeStruct((tr, d), F32),
                   jax.ShapeDtypeStruct((1, d), F32)],
        compiler_params=_cparams(("arbitrary",)),
    )(da, h0, g_pre, dh_out)


def _m_tile(lp):
    return _pick(lp, (1056, 512, 256, 128))


def ffn_fwd(tag, a, wgu_p, wd):
    lp, d = a.shape
    f2 = wgu_p.shape[1]
    tm = _m_tile(lp)

    def epi(acc):
        g, u = acc[:, :GU_TILE], acc[:, GU_TILE:]
        return acc, g * jax.nn.sigmoid(g) * u

    tg = _pick(lp, (384, 256, 128))
    gu, s = _mm(tag + "_gu", "nn", a, wgu_p, tg, 2 * GU_TILE,
                [((lp, f2), F32, (tg, 2 * GU_TILE), lambda i, j: (i, j)),
                 ((lp, f2 // 2), BF16, (tg, GU_TILE), lambda i, j: (i, j))], epi=epi, n_outer=True)
    f = _mm_plain(tag + "_down", "nn", s, wd, tm, 512, F32)
    return gu, s, f


def ffn_bwd(tag, df, a, gu, s, wgu_p, wd, emit):
    lp, d = df.shape
    f2 = wgu_p.shape[1]
    tm = _m_tile(lp)

    def epi(acc, gu_t):
        g, u = gu_t[:, :GU_TILE], gu_t[:, GU_TILE:]
        sig = jax.nn.sigmoid(g)
        dg = acc * u * (sig * (1.0 + g * (1.0 - sig)))
        du = acc * (g * sig)
        return (jnp.concatenate([dg, du], axis=1),)

    ts = _pick(lp, (528, 256, 128))
    dgu = _mm(tag + "_ds", "nt", df, wd, ts, GU_TILE,
              [((lp, f2), BF16, (ts, 2 * GU_TILE), lambda i, j: (i, j))], epi=epi,
              extras=[(gu, (ts, 2 * GU_TILE), lambda i, j: (i, j))], n_outer=True)[0]
    dwd = _mm_plain(tag + "_dwd", "tn", s, df, 512, _pick(d, (1024,)), BF16)
    dwgu = _mm_plain(tag + "_dwgu", "tn", a, dgu, _pick(d, (1024,)), 1024, BF16)
    dgu = emit({tag + "_w_gu": dwgu, tag + "_w_down": dwd}, dgu)
    return _mm_plain(tag + "_da", "nt", dgu, wgu_p, _pick(lp, (528, 256, 128)), 256, F32)


Z_U, Z_CQ, Z_CKV, Z_GP, Z_GM, Z_KR, Z_COLS = 0, 1024, 1536, 2048, 4096, 6144, 6400
POOL_W = POOL_GROUP * len(POOL_WINDOWS)
HALO = 16


def _pool_counts(lp, w):
    pos = lax.broadcasted_iota(jnp.int32, (lp, 1), 0) - PAD_ROWS
    return jnp.clip(pos + 1, 1, w).astype(F32)


def _pool_diff(u_ref, pad_ref, lp, w):
    pad_ref[pl.ds(0, HALO), :] = jnp.zeros((HALO, POOL_GROUP), F32)
    pad_ref[pl.ds(HALO, lp), :] = u_ref[...]
    acc = pad_ref[pl.ds(HALO, lp), :]
    for s in range(1, w):
        acc = acc + pad_ref[pl.ds(HALO - s, lp), :]
    return acc / _pool_counts(lp, w) - u_ref[...]


def pool_fwd(z, pool_w, pool_scale):
    lp = z.shape[0]
    ng = len(POOL_WINDOWS)

    def body(u_ref, w_ref, sc_ref, o_ref, pad_ref):
        for g, w in enumerate(POOL_WINDOWS):
            @pl.when(pl.program_id(0) == g)
            def _(w=w):
                dd = _pool_diff(u_ref, pad_ref, lp, w)
                y = jnp.dot(dd.astype(BF16), w_ref[0], preferred_element_type=F32)
                o_ref[...] = (y * sc_ref[...]).astype(BF16)

    return pl.pallas_call(
        body, name="pool_fwd", grid=(ng,),
        in_specs=[pl.BlockSpec((lp, POOL_GROUP), lambda g: (0, g)),
                  pl.BlockSpec((1, POOL_GROUP, POOL_GROUP), lambda g: (g, 0, 0)),
                  pl.BlockSpec((1, POOL_GROUP), lambda g: (0, g))],
        out_specs=pl.BlockSpec((lp, POOL_GROUP), lambda g: (0, g)),
        out_shape=jax.ShapeDtypeStruct((lp, POOL_W), BF16),
        scratch_shapes=[pltpu.VMEM((lp + HALO, POOL_GROUP), F32)],
        compiler_params=_cparams(("parallel",)),
    )(z, pool_w, pool_scale)


def pool_bwd(dyp, z, pool_w, pool_scale):
    lp = z.shape[0]
    ng = len(POOL_WINDOWS)

    def body(dy_ref, u_ref, w_ref, sc_ref, du_ref, dw_ref, dsc_ref, pad_ref):
        for g, w in enumerate(POOL_WINDOWS):
            @pl.when(pl.program_id(0) == g)
            def _(w=w):
                dd = _pool_diff(u_ref, pad_ref, lp, w).astype(BF16)
                wg = w_ref[0]
                ypre = jnp.dot(dd, wg, preferred_element_type=F32)
                dy = dy_ref[...]
                dsc_ref[...] = jnp.sum(dy * ypre, axis=0, keepdims=True)
                dypre = (dy * sc_ref[...]).astype(BF16)
                dw_ref[0] = lax.dot_general(dd, dypre, _DIMS["tn"], preferred_element_type=F32)
                ddd = lax.dot_general(dypre, wg, _DIMS["nt"], preferred_element_type=F32)
                pad_ref[pl.ds(0, lp), :] = ddd / _pool_counts(lp, w)
                pad_ref[pl.ds(lp, HALO), :] = jnp.zeros((HALO, POOL_GROUP), F32)
                acc = -ddd
                for s in range(w):
                    acc = acc + pad_ref[pl.ds(s, lp), :]
                du_ref[...] = acc.astype(BF16)

    return pl.pallas_call(
        body, name="pool_bwd", grid=(ng,),
        in_specs=[pl.BlockSpec((lp, POOL_GROUP), lambda g: (0, g)),
                  pl.BlockSpec((lp, POOL_GROUP), lambda g: (0, g)),
                  pl.BlockSpec((1, POOL_GROUP, POOL_GROUP), lambda g: (g, 0, 0)),
                  pl.BlockSpec((1, POOL_GROUP), lambda g: (0, g))],
        out_specs=[pl.BlockSpec((lp, POOL_GROUP), lambda g: (0, g)),
                   pl.BlockSpec((1, POOL_GROUP, POOL_GROUP), lambda g: (g, 0, 0)),
                   pl.BlockSpec((1, POOL_GROUP), lambda g: (0, g))],
        out_shape=[jax.ShapeDtypeStruct((lp, POOL_W), BF16),
                   jax.ShapeDtypeStruct((ng, POOL_GROUP, POOL_GROUP), F32),
                   jax.ShapeDtypeStruct((1, POOL_W), F32)],
        scratch_shapes=[pltpu.VMEM((lp + HALO, POOL_GROUP), F32)],
        compiler_params=_cparams(("parallel",)),
    )(dyp, z, pool_w, pool_scale)


Q_COLS = HEADS * HEAD_W
ROPE_BLOCK = LANE


def rope_tables(lp):
    pos = jnp.maximum(jnp.arange(lp, dtype=F32) - PAD_ROWS, 0.0)
    inv = ROPE_THETA ** (-jnp.arange(0, QK_ROPE, 2, dtype=F32) / QK_ROPE)
    ang = pos[:, None] * inv[None, :]
    cos, sin, zero = jnp.cos(ang), jnp.sin(ang), jnp.zeros_like(ang)
    return jnp.stack([jnp.concatenate([cos, cos, zero, zero], axis=1),
                      jnp.concatenate([-sin, zero, zero, zero], axis=1),
                      jnp.concatenate([zero, sin, zero, zero], axis=1)])


def _rope(x, tabs):
    return x * tabs[0] + pltpu.roll(x, 96, 1) * tabs[1] + pltpu.roll(x, 32, 1) * tabs[2]


def _rope_bwd(g, tabs):
    return g * tabs[0] + pltpu.roll(g * tabs[1], 32, 1) + pltpu.roll(g * tabs[2], 96, 1)


def _tab_spec(tr):
    return pl.BlockSpec((3, tr, ROPE_BLOCK), lambda i: (0, i, 0))


def mla_prep(z, g_q, g_kv, tabs):
    lp = z.shape[0]
    tr = _row_tile(lp)
    r = g_q.shape[1]

    def body(cq_ref, ckv_ref, kr_ref, gq_ref, gkv_ref, tab_ref, qn_ref, kvn_ref, kpe_ref):
        cq, ckv = cq_ref[...], ckv_ref[...]
        qn_ref[...] = (cq * _rstd(cq) * gq_ref[...]).astype(BF16)
        kvn_ref[...] = (ckv * _rstd(ckv) * gkv_ref[...]).astype(BF16)
        kpe_ref[...] = _rope(kr_ref[...], tab_ref[...]).astype(BF16)

    return pl.pallas_call(
        body, name="mla_prep", grid=(lp // tr,),
        in_specs=[_row_spec(tr, r, Z_CQ // r), _row_spec(tr, r, Z_CKV // r), _row_spec(tr, ROPE_BLOCK, Z_KR // ROPE_BLOCK),
                  _vec_spec(r), _vec_spec(r), _tab_spec(tr)],
        out_specs=[_row_spec(tr, r), _row_spec(tr, r), _row_spec(tr, ROPE_BLOCK)],
        out_shape=[jax.ShapeDtypeStruct((lp, r), BF16), jax.ShapeDtypeStruct((lp, r), BF16),
                   jax.ShapeDtypeStruct((lp, ROPE_BLOCK), BF16)],
        compiler_params=_cparams(("parallel",)),
    )(z, z, z, g_q, g_kv, tabs)


def q_proj(qn, wq_p, tabs):
    lp = qn.shape[0]
    tm = _m_tile(lp)
    tn = 4 * HEAD_W

    def epi(acc, tab):
        parts = []
        for t in range(tn // HEAD_W):
            parts.append(acc[:, t * HEAD_W:t * HEAD_W + QK_NOPE])
            parts.append(_rope(acc[:, t * HEAD_W + QK_NOPE:(t + 1) * HEAD_W], tab))
        return (jnp.concatenate(parts, axis=1),)

    return _mm("q_proj", "nn", qn, wq_p, tm, tn, [((lp, Q_COLS), BF16, (tm, tn), lambda i, j: (i, j))], epi=epi,
               extras=[(tabs, (3, tm, ROPE_BLOCK), lambda i, j: (0, i, 0))])[0]


def kv_proj(kvn, wkv):
    return _mm_plain("kv_proj", "nn", kvn, wkv, _m_tile(kvn.shape[0]), 1024, BF16)


def q_rope_bwd(dq, tabs):
    lp = dq.shape[0]
    tr = _row_tile(lp)

    def body(dq_ref, tab_ref, o_ref):
        tab = tab_ref[...]
        for h in range(HEADS):
            o_ref[:, h * HEAD_W:h * HEAD_W + QK_NOPE] = dq_ref[:, h * HEAD_W:h * HEAD_W + QK_NOPE].astype(BF16)
            o_ref[:, h * HEAD_W + QK_NOPE:(h + 1) * HEAD_W] = _rope_bwd(
                dq_ref[:, h * HEAD_W + QK_NOPE:(h + 1) * HEAD_W], tab).astype(BF16)

    return pl.pallas_call(
        body, name="q_rope_bwd", grid=(lp // tr,),
        in_specs=[_row_spec(tr, Q_COLS), _tab_spec(tr)], out_specs=_row_spec(tr, Q_COLS),
        out_shape=jax.ShapeDtypeStruct((lp, Q_COLS), BF16), compiler_params=_cparams(("parallel",)),
    )(dq, tabs)


def mla_prep_bwd(dqn, dkvn, dkpe_h, z, g_q, g_kv, tabs):
    lp = z.shape[0]
    tr = _row_tile(lp)
    r = g_q.shape[1]

    def body(dqn_ref, dkvn_ref, dkpe_ref, cq_ref, ckv_ref, gq_ref, gkv_ref, tab_ref,
             dcq_ref, dckv_ref, dkr_ref, dgq_ref, dgkv_ref):
        @pl.when(pl.program_id(0) == 0)
        def _():
            dgq_ref[...] = jnp.zeros_like(dgq_ref)
            dgkv_ref[...] = jnp.zeros_like(dgkv_ref)

        dcq, dgq = _norm_bwd(dqn_ref[...], cq_ref[...], gq_ref[...])
        dckv, dgkv = _norm_bwd(dkvn_ref[...], ckv_ref[...], gkv_ref[...])
        dcq_ref[...] = dcq.astype(BF16)
        dckv_ref[...] = dckv.astype(BF16)
        dgq_ref[...] += dgq
        dgkv_ref[...] += dgkv
        dkpe = dkpe_ref[0]
        for h in range(1, HEADS):
            dkpe = dkpe + dkpe_ref[h]
        dkr_ref[...] = _rope_bwd(dkpe, tab_ref[...]).astype(BF16)

    return pl.pallas_call(
        body, name="mla_prep_bwd", grid=(lp // tr,),
        in_specs=[_row_spec(tr, r), _row_spec(tr, r), pl.BlockSpec((HEADS, tr, ROPE_BLOCK), lambda i: (0, i, 0)),
                  _row_spec(tr, r, Z_CQ // r), _row_spec(tr, r, Z_CKV // r), _vec_spec(r), _vec_spec(r), _tab_spec(tr)],
        out_specs=[_row_spec(tr, r), _row_spec(tr, r), _row_spec(tr, ROPE_BLOCK), _vec_spec(r), _vec_spec(r)],
        out_shape=[jax.ShapeDtypeStruct((lp, r), BF16), jax.ShapeDtypeStruct((lp, r), BF16),
                   jax.ShapeDtypeStruct((lp, ROPE_BLOCK), BF16),
                   jax.ShapeDtypeStruct((1, r), F32), jax.ShapeDtypeStruct((1, r), F32)],
        compiler_params=_cparams(("arbitrary",)),
    )(dqn, dkvn, dkpe_h, z, z, g_q, g_kv, tabs)


def _attn_tile(lp):
    return _pick(lp, (528, 128))


def _causal_pairs(nt, k_major):
    if k_major:
        pairs = [(qi, ki) for ki in range(nt) for qi in range(ki, nt)]
    else:
        pairs = [(qi, ki) for qi in range(nt) for ki in range(qi + 1)]
    return (jnp.asarray([p[0] for p in pairs], jnp.int32), jnp.asarray([p[1] for p in pairs], jnp.int32))


def _scores(q, kcat, q_tile, k_tile, t, masked):
    s = lax.dot_general(q, kcat, _DIMS["nt"], preferred_element_type=F32) * SOFTMAX_SCALE
    if not masked:
        return s
    qpos = q_tile * t + lax.broadcasted_iota(jnp.int32, (t, t), 0)
    kpos = k_tile * t + lax.broadcasted_iota(jnp.int32, (t, t), 1)
    return jnp.where((kpos <= qpos) & (kpos >= PAD_ROWS), s, jnp.float32(-1e30))


def _on_masked_or_not(q_tile, k_tile, fn):
    needs_mask = (q_tile == k_tile) | (k_tile == 0)

    @pl.when(needs_mask)
    def _():
        fn(True)

    @pl.when(jnp.logical_not(needs_mask))
    def _():
        fn(False)


def flash_fwd(q, kv, kpe):
    lp = q.shape[0]
    t = _attn_tile(lp)
    q_tab, k_tab = _causal_pairs(lp // t, k_major=False)

    def body(q_tab_ref, k_tab_ref, q_ref, kv_ref, kpe_ref, o32_ref, o16_ref, lse_ref, m_sc, l_sc, acc_sc):
        pair = pl.program_id(1)
        qi, ki = q_tab_ref[pair], k_tab_ref[pair]

        @pl.when(ki == 0)
        def _():
            m_sc[...] = jnp.full_like(m_sc, -jnp.inf)
            l_sc[...] = jnp.zeros_like(l_sc)
            acc_sc[...] = jnp.zeros_like(acc_sc)

        def step(masked):
            kvt = kv_ref[...]
            kcat = jnp.concatenate([kvt[:, :QK_NOPE], kpe_ref[...]], axis=1)
            s = _scores(q_ref[...], kcat, qi, ki, t, masked)
            m_prev = m_sc[...]
            m_new = jnp.maximum(m_prev, jnp.max(s, axis=1, keepdims=True))
            alpha = jnp.exp(m_prev - m_new)
            p = jnp.exp(s - m_new[:, :1])
            l_sc[...] = alpha * l_sc[...] + jnp.sum(p, axis=1, keepdims=True)
            acc_sc[...] = alpha * acc_sc[...] + jnp.dot(p.astype(BF16), kvt[:, QK_NOPE:], preferred_element_type=F32)
            m_sc[...] = m_new

        _on_masked_or_not(qi, ki, step)

        @pl.when(ki == qi)
        def _():
            l = l_sc[...]
            o = acc_sc[...] / l
            o32_ref[...] = o
            o16_ref[...] = o.astype(BF16)
            lse_ref[0] = m_sc[...] + jnp.log(l)

    return pl.pallas_call(
        body, name="flash_fwd",
        grid_spec=pltpu.PrefetchScalarGridSpec(
            num_scalar_prefetch=2, grid=(HEADS, q_tab.shape[0]),
            in_specs=[pl.BlockSpec((t, HEAD_W), lambda h, p, qt, kt: (qt[p], h)),
                      pl.BlockSpec((t, HEAD_W), lambda h, p, qt, kt: (kt[p], h)),
                      pl.BlockSpec((t, ROPE_BLOCK), lambda h, p, qt, kt: (kt[p], 0))],
            out_specs=[pl.BlockSpec((t, V_DIM), lambda h, p, qt, kt: (qt[p], h)),
                       pl.BlockSpec((t, V_DIM), lambda h, p, qt, kt: (qt[p], h)),
                       pl.BlockSpec((1, t, LANE), lambda h, p, qt, kt: (h, qt[p], 0))],
            scratch_shapes=[pltpu.VMEM((t, LANE), F32), pltpu.VMEM((t, LANE), F32), pltpu.VMEM((t, V_DIM), F32)]),
        out_shape=[jax.ShapeDtypeStruct((lp, HEADS * V_DIM), F32), jax.ShapeDtypeStruct((lp, HEADS * V_DIM), BF16),
                   jax.ShapeDtypeStruct((HEADS, lp, LANE), F32)],
        compiler_params=_cparams(("parallel", "arbitrary")),
    )(q_tab, k_tab, q, kv, kpe)


def flash_bwd(q, kv, kpe, o32, lse, do):
    lp = q.shape[0]
    t = _attn_tile(lp)
    nt = lp // t
    q_tab, k_tab = _causal_pairs(nt, k_major=True)

    def body(q_tab_ref, k_tab_ref, q_ref, kv_ref, kpe_ref, o_ref, lse_ref, do_ref, dq_ref, dkv_ref, dkpe_ref,
             dk_sc, dv_sc):
        pair = pl.program_id(1)
        qi, ki = q_tab_ref[pair], k_tab_ref[pair]

        @pl.when(pair == 0)
        def _():
            dq_ref[...] = jnp.zeros_like(dq_ref)

        @pl.when(qi == ki)
        def _():
            dk_sc[...] = jnp.zeros_like(dk_sc)
            dv_sc[...] = jnp.zeros_like(dv_sc)

        def step(masked):
            qt = q_ref[...]
            kvt = kv_ref[...]
            kcat = jnp.concatenate([kvt[:, :QK_NOPE], kpe_ref[...]], axis=1)
            s = _scores(qt, kcat, qi, ki, t, masked)
            p = jnp.exp(s - lse_ref[0][:, :1])
            do = do_ref[...]
            delta = jnp.sum(do * o_ref[...], axis=1, keepdims=True)
            do16 = do.astype(BF16)
            dv_sc[...] += lax.dot_general(p.astype(BF16), do16, _DIMS["tn"], preferred_element_type=F32)
            dp = lax.dot_general(do16, kvt[:, QK_NOPE:], _DIMS["nt"], preferred_element_type=F32)
            ds = (p * (dp - delta) * SOFTMAX_SCALE).astype(BF16)
            dk_sc[...] += lax.dot_general(ds, qt, _DIMS["tn"], preferred_element_type=F32)
            row = pl.multiple_of(qi * t, t)
            dq_ref[pl.ds(row, t), :] += jnp.dot(ds, kcat, preferred_element_type=F32)

        _on_masked_or_not(qi, ki, step)

        @pl.when(qi == nt - 1)
        def _():
            dk = dk_sc[...]
            dkv_ref[...] = jnp.concatenate([dk[:, :QK_NOPE], dv_sc[...]], axis=1).astype(BF16)
            dkpe_ref[0] = dk[:, QK_NOPE:]

    qmap = lambda h, p, qt, kt: (qt[p], h)
    return pl.pallas_call(
        body, name="flash_bwd",
        grid_spec=pltpu.PrefetchScalarGridSpec(
            num_scalar_prefetch=2, grid=(HEADS, q_tab.shape[0]),
            in_specs=[pl.BlockSpec((t, HEAD_W), qmap),
                      pl.BlockSpec((t, HEAD_W), lambda h, p, qt, kt: (kt[p], h)),
                      pl.BlockSpec((t, ROPE_BLOCK), lambda h, p, qt, kt: (kt[p], 0)),
                      pl.BlockSpec((t, V_DIM), qmap),
                      pl.BlockSpec((1, t, LANE), lambda h, p, qt, kt: (h, qt[p], 0)),
                      pl.BlockSpec((t, V_DIM), qmap)],
            out_specs=[pl.BlockSpec((lp, HEAD_W), lambda h, p, qt, kt: (0, h)),
                       pl.BlockSpec((t, HEAD_W), lambda h, p, qt, kt: (kt[p], h)),
                       pl.BlockSpec((1, t, ROPE_BLOCK), lambda h, p, qt, kt: (h, kt[p], 0))],
            scratch_shapes=[pltpu.VMEM((t, HEAD_W), F32), pltpu.VMEM((t, V_DIM), F32)]),
        out_shape=[jax.ShapeDtypeStruct((lp, Q_COLS), F32), jax.ShapeDtypeStruct((lp, Q_COLS), BF16),
                   jax.ShapeDtypeStruct((HEADS, lp, ROPE_BLOCK), F32)],
        compiler_params=_cparams(("parallel", "arbitrary")),
    )(q_tab, k_tab, q, kv, kpe, o32, lse, do)


def _ij(i, j):
    return (i, j)


def mixer_fwd(a2, w, tabs, pool_scale, g_q, g_kv, ex):
    lp, d = a2.shape
    tm = _m_tile(lp)
    tn = 512
    z = _mm_plain("mix_in", "nn", a2, w["w_in"], tm, 1280, F32)
    yp = pool_fwd(z, w["pool_w"], pool_scale)
    qn, kvn, kpe = mla_prep(z, g_q, g_kv, tabs)
    q = q_proj(qn, w["w_q_b"], tabs)
    kv = kv_proj(kvn, w["w_kv_b"])
    o32, o16, lse = flash_fwd(q, kv, kpe)
    ex.point("flash_fwd_done", o16)
    o16 = ex.tie(o16)
    y_pool =_mm_plain("pool_out", "nn", yp, w["w_pool_o"], tm, tn, F32)

    def epi(acc, ypl, gp, gm):
        return jax.nn.sigmoid(gp) * ypl + jax.nn.sigmoid(gm) * acc, acc

    y, y_mla = _mm("mla_out_gate", "nn", o16, w["w_mla_o"], tm, tn,
                   [((lp, d), BF16, (tm, tn), _ij), ((lp, d), F32, (tm, tn), _ij)], epi=epi,
                   extras=[(y_pool, (tm, tn), _ij), (z, (tm, tn), lambda i, j: (i, Z_GP // tn + j)),
                           (z, (tm, tn), lambda i, j: (i, Z_GM // tn + j))])
    m = _mm_plain("mix_out", "nn", y, w["w_out"], tm, tn, F32)
    return m, dict(z=z, yp=yp, qn=qn, kvn=kvn, kpe=kpe, q=q, kv=kv, o32=o32, o16=o16, lse=lse,
                   y_pool=y_pool, y_mla=y_mla, y=y)


def mixer_bwd(dm, a2, sv, w, tabs, pool_scale, g_q, g_kv):
    lp, d = dm.shape
    tm = _m_tile(lp)
    tn = 512
    z = sv["z"]

    def epi(acc, ypl, yml, gp, gm):
        sp, sm = jax.nn.sigmoid(gp), jax.nn.sigmoid(gm)
        return acc * sp, acc * sm, acc * ypl * (sp * (1.0 - sp)), acc * yml * (sm * (1.0 - sm))

    dyp, dym, dgp, dgm = _mm(
        "gate_bwd", "nt", dm, w["w_out"], tm, tn, [((lp, d), BF16, (tm, tn), _ij)] * 4, epi=epi,
        extras=[(sv["y_pool"], (tm, tn), _ij), (sv["y_mla"], (tm, tn), _ij),
                (z, (tm, tn), lambda i, j: (i, Z_GP // tn + j)), (z, (tm, tn), lambda i, j: (i, Z_GM // tn + j))])
    g = {}
    g["w_out"] = _mm_plain("dw_out", "tn", sv["y"], dm, 1024, 1024, BF16)
    g["w_pool_o"] = _mm_plain("dw_pool_o", "tn", sv["yp"], dyp, 512, 1024, BF16)
    dypre = _mm_plain("pool_out_bwd", "nt", dyp, w["w_pool_o"], tm, tn, F32)
    du, g["pool_w"], d_pool_scale = pool_bwd(dypre, z, w["pool_w"], pool_scale)
    g["w_mla_o"] = _mm_plain("dw_mla_o", "tn", sv["o16"], dym, 1024, 1024, BF16)
    do = _mm_plain("mla_out_bwd", "nt", dym, w["w_mla_o"], tm, tn, F32)
    dq, dkv, dkpe_h = flash_bwd(sv["q"], sv["kv"], sv["kpe"], sv["o32"], sv["lse"], do)
    dql = q_rope_bwd(dq, tabs)
    g["w_q_b"] = _mm_plain("dw_q_b", "tn", sv["qn"], dql, 512, 1024, BF16)
    dqn = _mm_plain("q_proj_bwd", "nt", dql, w["w_q_b"], tm, 512, F32)
    g["w_kv_b"] = _mm_plain("dw_kv_b", "tn", sv["kvn"], dkv, 512, 1024, BF16)
    dkvn = _mm_plain("kv_proj_bwd", "nt", dkv, w["w_kv_b"], tm, 512, F32)
    dcq, dckv, dkr, d_gq, d_gkv = mla_prep_bwd(dqn, dkvn, dkpe_h, z, g_q, g_kv, tabs)
    dz = jnp.concatenate([du, dcq, dckv, dgp, dgm, dkr, jnp.zeros((lp, Z_COLS - Z_KR - ROPE_BLOCK), BF16)], axis=1)
    g["w_in"] = _mm_plain("dw_in", "tn", a2, dz, 1024, 1280, BF16)
    da2 = _mm_plain("mix_in_bwd", "nt", dz, w["w_in"], tm, tn, F32)
    return da2, g, dict(pool_scale=d_pool_scale, q_a_norm=d_gq, kv_a_norm=d_gkv)


_ANY = pl.BlockSpec(memory_space=pl.ANY)
_MESH = pl.DeviceIdType.MESH


def _my_pos():
    return lax.axis_index("x"), lax.axis_index("y"), lax.axis_index("c")


LEAD = "lead"
COLS = "cols"
COLS_GU = "cols_gu"


def _col_block(layout, dev):
    return dev if layout == COLS else 2 * (dev % 4) + dev // 4


def _dev_block(ref, layout, dev, cols):
    if layout == LEAD:
        return ref.at[dev]
    return ref.at[:, pl.ds(pl.multiple_of(_col_block(layout, dev) * cols, LANE), cols)]


def _gathered_shape(shard_shape, layout):
    if layout == LEAD:
        return (N_DEV, *shard_shape)
    return (shard_shape[0], N_DEV * shard_shape[1])


def all_gather(name, shards, layouts):
    n = len(shards)

    def body(*refs):
        ins, outs = refs[:n], refs[n:2 * n]
        send_sems, recv_sems, local_sems = refs[2 * n:]
        x, y, c = _my_pos()
        me, sibling = (x, y, c), (x, y, 1 - c)
        chips = [(1 - x, y), (x, 1 - y), (1 - x, 1 - y)]

        def blk(a, px, py, pc):
            return _dev_block(outs[a], layouts[a], 4 * px + 2 * py + pc, shards[a].shape[-1])

        def copy(a, k, block, to, src=None):
            return pltpu.make_async_remote_copy(
                src_ref=blk(a, *block) if src is None else src, dst_ref=blk(a, *block),
                send_sem=send_sems.at[a, k], recv_sem=recv_sems.at[a, k], device_id=to, device_id_type=_MESH)

        mine = [pltpu.make_async_copy(ins[a], blk(a, *me), local_sems.at[a]) for a in range(n)]
        for cp in mine:
            cp.start()
        first = []
        for a in range(n):
            first.append(copy(a, 0, me, sibling, src=ins[a]))
            first += [copy(a, 1 + j, me, (*chip, c), src=ins[a]) for j, chip in enumerate(chips)]
        for cp in first:
            cp.start()
        passed = []
        for j, chip in enumerate(chips):
            for a in range(n):
                copy(a, 1 + j, (*chip, c), me).wait_recv()
                fwd = copy(a, 4 + j, (*chip, c), sibling)
                fwd.start()
                passed.append(fwd)
        for a in range(n):
            copy(a, 0, sibling, me).wait_recv()
            for j, chip in enumerate(chips):
                copy(a, 4 + j, (*chip, 1 - c), me).wait_recv()
        for cp in first + passed:
            cp.wait_send()
        for cp in mine:
            cp.wait()

    return pl.pallas_call(
        body, name=name,
        in_specs=[_ANY] * n, out_specs=[_ANY] * n,
        out_shape=[jax.ShapeDtypeStruct(_gathered_shape(s.shape, lay), s.dtype)
                   for s, lay in zip(shards, layouts, strict=True)],
        scratch_shapes=[pltpu.SemaphoreType.DMA((n, 7)), pltpu.SemaphoreType.DMA((n, 7)), pltpu.SemaphoreType.DMA((n,))],
    )(*shards)


def _shard_shape(grad, layout):
    return grad.shape[1:] if layout == LEAD else (grad.shape[0], grad.shape[1] // N_DEV)


def rs_sibling(name, grads, layouts):
    n = len(grads)

    def body(*refs):
        ins, outs = refs[:n], refs[n:2 * n]
        send_sems, recv_sems = refs[2 * n:]
        x, y, c = _my_pos()
        cps = []
        for a in range(n):
            for k in range(4):
                cp = pltpu.make_async_remote_copy(
                    src_ref=_dev_block(ins[a], layouts[a], 2 * k + (1 - c), outs[a].shape[-1]), dst_ref=outs[a].at[k],
                    send_sem=send_sems.at[a, k], recv_sem=recv_sems.at[a, k],
                    device_id=(x, y, 1 - c), device_id_type=_MESH)
                cp.start()
                cps.append(cp)
        for cp in cps:
            cp.wait()

    return pl.pallas_call(
        body, name=name,
        in_specs=[_ANY] * n, out_specs=[_ANY] * n,
        out_shape=[jax.ShapeDtypeStruct((4, *_shard_shape(g, lay)), g.dtype) for g, lay in zip(grads, layouts, strict=True)],
        scratch_shapes=[pltpu.SemaphoreType.DMA((n, 4)), pltpu.SemaphoreType.DMA((n, 4))],
    )(*grads)


def rs_chips(name, sums):
    n = len(sums)

    def body(*refs):
        ins, outs = refs[:n], refs[n:2 * n]
        send_sems, recv_sems = refs[2 * n:]
        x, y, c = _my_pos()
        chips = [(1 - x, y), (x, 1 - y), (1 - x, 1 - y)]
        cps = []
        for a in range(n):
            for j, chip in enumerate(chips):
                cp = pltpu.make_async_remote_copy(
                    src_ref=ins[a].at[2 * chip[0] + chip[1]], dst_ref=outs[a].at[j],
                    send_sem=send_sems.at[a, j], recv_sem=recv_sems.at[a, j],
                    device_id=(*chip, c), device_id_type=_MESH)
                cp.start()
                cps.append(cp)
        for cp in cps:
            cp.wait()

    return pl.pallas_call(
        body, name=name,
        in_specs=[_ANY] * n, out_specs=[_ANY] * n,
        out_shape=[jax.ShapeDtypeStruct((3, *s.shape[1:]), s.dtype) for s in sums],
        scratch_shapes=[pltpu.SemaphoreType.DMA((n, 3)), pltpu.SemaphoreType.DMA((n, 3))],
    )(*sums)


_HBM = pl.BlockSpec(memory_space=pltpu.HBM)
_SEM = pl.BlockSpec(memory_space=pltpu.SEMAPHORE)
_EFFECT = pltpu.SideEffectType.DATAFLOW_SIDE_EFFECTING


def _in_hbm(a):
    return pltpu.with_memory_space_constraint(a, pltpu.HBM)


def split_start(name, bufs, plan, n_copies, after=None):
    nb = len(bufs)
    extra = [] if after is None else [after]

    def body(*refs):
        buf_refs = refs[:nb]
        send_sems, recv_sems = refs[nb + len(extra)], refs[nb + len(extra) + 1]
        token = refs[-1]
        copies = plan(buf_refs)
        assert len(copies) == n_copies
        for k, (src, dst, to) in enumerate(copies):
            pltpu.make_async_remote_copy(src_ref=src, dst_ref=dst, send_sem=send_sems.at[k], recv_sem=recv_sems.at[k],
                                         device_id=to, device_id_type=_MESH).start()
        token[...] = jnp.zeros_like(token)

    out = pl.pallas_call(
        body, name=name,
        out_shape=(pltpu.SemaphoreType.DMA((n_copies,)), pltpu.SemaphoreType.DMA((n_copies,)),
                   *[pltpu.HBM(b.shape, b.dtype) for b in bufs], jax.ShapeDtypeStruct((8, LANE), F32)),
        in_specs=[_HBM] * nb + [_ANY] * len(extra),
        out_specs=(_SEM, _SEM, *[_HBM] * nb, pl.BlockSpec(memory_space=pltpu.VMEM)),
        input_output_aliases={i: 2 + i for i in range(nb)},
        compiler_params=pltpu.CompilerParams(has_side_effects=_EFFECT),
    )(*[_in_hbm(b) for b in bufs], *extra)
    return out[0], out[1], list(out[2:2 + nb]), out[-1]


def split_wait(name, bufs, send_sems, recv_sems, plan, after):
    nb = len(bufs)

    def body(*refs):
        buf_refs = refs[:nb]
        s_sems, r_sems = refs[nb], refs[nb + 1]
        for k, (src, dst, to) in enumerate(plan(buf_refs)):
            cp = pltpu.make_async_remote_copy(src_ref=src, dst_ref=dst, send_sem=s_sems.at[k], recv_sem=r_sems.at[k],
                                              device_id=to, device_id_type=_MESH)
            cp.wait_send()
            cp.wait_recv()

    out = pl.pallas_call(
        body, name=name,
        out_shape=tuple(pltpu.HBM(b.shape, b.dtype) for b in bufs),
        in_specs=[_HBM] * nb + [_SEM, _SEM, _ANY],
        out_specs=tuple([_HBM] * nb),
        input_output_aliases={i: i for i in range(nb)},
        compiler_params=pltpu.CompilerParams(has_side_effects=_EFFECT),
    )(*bufs, send_sems, recv_sems, after)
    return list(out)


def _ag_own_plan(shapes, layouts):
    n = len(shapes)

    def plan(refs):
        x, y, c = _my_pos()
        targets = [(x, y, 1 - c), (1 - x, y, c), (x, 1 - y, c), (1 - x, 1 - y, c)]
        out = []
        for a in range(n):
            blk = _dev_block(refs[a], layouts[a], 4 * x + 2 * y + c, shapes[a][-1])
            out += [(blk, blk, to) for to in targets]
        return out

    return plan, 4 * n


def _ag_pass_plan(shapes, layouts):
    n = len(shapes)

    def plan(refs):
        x, y, c = _my_pos()
        out = []
        for a in range(n):
            for px, py in [(1 - x, y), (x, 1 - y), (1 - x, 1 - y)]:
                blk = _dev_block(refs[a], layouts[a], 4 * px + 2 * py + c, shapes[a][-1])
                out.append((blk, blk, (x, y, 1 - c)))
        return out

    return plan, 3 * n


def _rs_sibling_plan(layouts, n):
    def plan(refs):
        x, y, c = _my_pos()
        return [(_dev_block(refs[a], layouts[a], 2 * k + (1 - c), refs[n + a].shape[-1]), refs[n + a].at[k], (x, y, 1 - c))
                for a in range(n) for k in range(4)]

    return plan, 4 * n


def _rs_chips_plan(n):
    def plan(refs):
        x, y, c = _my_pos()
        return [(refs[a].at[2 * px + py], refs[n + a].at[j], (px, py, c))
                for a in range(n) for j, (px, py) in enumerate([(1 - x, y), (x, 1 - y), (1 - x, 1 - y)])]

    return plan, 3 * n


def place_own(name, shard, layout, dtype, dev, after):
    r, c = shard.shape
    tr = _ew_rows(r, c)
    if layout == LEAD:
        o_spec = pl.BlockSpec((None, tr, c), lambda i, dev_ref: (dev_ref[0], i, 0))
    else:
        o_spec = pl.BlockSpec((tr, c), lambda i, dev_ref: (i, _col_block(layout, dev_ref[0])))
    extra = [] if after is None else [after]

    def body(dev_ref, s_ref, *rest):
        rest[-1][...] = s_ref[...].astype(dtype)

    return pl.pallas_call(
        body, name=name,
        grid_spec=pltpu.PrefetchScalarGridSpec(
            num_scalar_prefetch=1, grid=(r // tr,),
            in_specs=[pl.BlockSpec((tr, c), lambda i, dev_ref: (i, 0))] + [_ANY] * len(extra),
            out_specs=o_spec),
        out_shape=jax.ShapeDtypeStruct(_gathered_shape(shard.shape, layout), dtype),
        compiler_params=_cparams(("parallel",)),
    )(dev, shard, *extra)


def _ew_rows(r, c):
    for t in (512, 256, 128, 64, 32, 16):
        if r % t == 0 and t * c * 4 <= 768 * 1024:
            return t
    raise ValueError((r, c))


def rs_add(name, grad, layout, recv, core):
    _, r, c = recv.shape
    tr = _ew_rows(r, c)
    if layout == LEAD:
        g_spec = pl.BlockSpec((None, tr, c), lambda k, i, core_ref: (2 * k + core_ref[0], i, 0))
    else:
        g_spec = pl.BlockSpec((tr, c), lambda k, i, core_ref: (i, _col_block(layout, 2 * k + core_ref[0])))

    def body(core_ref, g_ref, r_ref, o_ref):
        o_ref[...] = (g_ref[...].astype(F32) + r_ref[...].astype(F32)).astype(BF16)

    return pl.pallas_call(
        body, name=name,
        grid_spec=pltpu.PrefetchScalarGridSpec(
            num_scalar_prefetch=1, grid=(4, r // tr),
            in_specs=[g_spec, pl.BlockSpec((None, tr, c), lambda k, i, core_ref: (k, i, 0))],
            out_specs=pl.BlockSpec((None, tr, c), lambda k, i, core_ref: (k, i, 0))),
        out_shape=jax.ShapeDtypeStruct((4, r, c), BF16),
        compiler_params=_cparams(("parallel", "parallel")),
    )(core, grad, recv)


def _adamw(w, g, m, v):
    m = ADAM_B1 * m + (1.0 - ADAM_B1) * g
    v = ADAM_B2 * v + (1.0 - ADAM_B2) * jnp.square(g)
    m_hat = m / (1.0 - ADAM_B1 ** ADAM_STEP)
    v_hat = v / (1.0 - ADAM_B2 ** ADAM_STEP)
    delta = -ADAM_LR * (m_hat / (jnp.sqrt(v_hat) + ADAM_EPS) + ADAM_WD * w)
    return delta, m, v


def adamw_shard(name, w, m, v, sums, recv, chip):
    r, c = w.shape
    tr = _ew_rows(r, c)

    def body(chip_ref, w_ref, m_ref, v_ref, s_ref, r_ref, g_ref, d_ref, mo_ref, vo_ref):
        g = s_ref[0].astype(F32)
        for j in range(3):
            g = g + r_ref[j].astype(F32)
        d, mn, vn = _adamw(w_ref[...], g, m_ref[...], v_ref[...])
        g_ref[...] = g
        d_ref[...] = d
        mo_ref[...] = mn
        vo_ref[...] = vn

    spec = pl.BlockSpec((tr, c), lambda i, chip_ref: (i, 0))
    return pl.pallas_call(
        body, name=name,
        grid_spec=pltpu.PrefetchScalarGridSpec(
            num_scalar_prefetch=1, grid=(r // tr,),
            in_specs=[spec, spec, spec,
                      pl.BlockSpec((1, tr, c), lambda i, chip_ref: (chip_ref[0], i, 0)),
                      pl.BlockSpec((3, tr, c), lambda i, chip_ref: (0, i, 0))],
            out_specs=[spec] * 4),
        out_shape=[jax.ShapeDtypeStruct((r, c), F32)] * 4,
        compiler_params=_cparams(("parallel",)),
    )(chip, w, m, v, sums, recv)


def reduce_small(gathered):
    _, r, c = gathered.shape

    def body(g_ref, o_ref):
        acc = g_ref[0]
        for k in range(1, N_DEV):
            acc = acc + g_ref[k]
        o_ref[...] = acc

    return pl.pallas_call(body, name="reduce_small", out_shape=jax.ShapeDtypeStruct((r, c), F32))(gathered)


def adamw_small(ws, gs, ms, vs):
    n = len(ws)

    def body(*refs):
        w_r, g_r, m_r, v_r = refs[:n], refs[n:2 * n], refs[2 * n:3 * n], refs[3 * n:4 * n]
        d_o, m_o, v_o = refs[4 * n:5 * n], refs[5 * n:6 * n], refs[6 * n:7 * n]
        for a in range(n):
            d, mn, vn = _adamw(w_r[a][...], g_r[a][...], m_r[a][...], v_r[a][...])
            d_o[a][...] = d
            m_o[a][...] = mn
            v_o[a][...] = vn

    shapes = [jax.ShapeDtypeStruct(w.shape, F32) for w in ws]
    out = pl.pallas_call(body, name="adamw_small", out_shape=shapes * 3)(*ws, *gs, *ms, *vs)
    return out[:n], out[n:2 * n], out[2 * n:]


WEIGHTS = ["meta_tokens", "norm_ffn1_pre", "norm_ffn1_post", "ffn1_w_gu", "ffn1_w_down", "norm_mix_pre",
           "norm_mix_post", "w_in", "pool_w", "pool_scale", "w_pool_o", "q_a_norm", "w_q_b", "kv_a_norm", "w_kv_b",
           "w_mla_o", "w_out", "norm_ffn2_pre", "norm_ffn2_post", "ffn2_w_gu", "ffn2_w_down"]
BIG = ["ffn1_w_gu", "ffn1_w_down", "w_in", "pool_w", "w_pool_o", "w_q_b", "w_kv_b", "w_mla_o", "w_out",
       "ffn2_w_gu", "ffn2_w_down"]
COL_SHARDED = ("w_in", "w_q_b")
GATHERED = {"ffn1_w_gu": COLS_GU, "ffn2_w_gu": COLS_GU, "w_pool_o": COLS, "w_kv_b": COLS}
GAINS =["norm_ffn1_pre", "norm_ffn1_post", "norm_mix_pre", "norm_mix_post", "norm_ffn2_pre", "norm_ffn2_post"]
SMALL = GAINS + ["pool_scale", "q_a_norm", "kv_a_norm"]
Z_SRC = 1024 + 512 + 512 + QK_ROPE


def _full_from_gathered(name, g):
    _, r, c = g.shape
    if name == "pool_w":
        ng = len(POOL_WINDOWS)
        return g.reshape(N_DEV, ng, r // ng, c).transpose(1, 0, 2, 3).reshape(ng, POOL_GROUP, POOL_GROUP)
    if name in COL_SHARDED:
        return g.transpose(1, 0, 2).reshape(r, N_DEV * c)
    return g.reshape(N_DEV * r, c)


def _blocks_from_full(name, dw):
    if name == "pool_w":
        ng = len(POOL_WINDOWS)
        return dw.reshape(ng, N_DEV, POOL_GROUP // N_DEV, POOL_GROUP).transpose(1, 0, 2, 3).reshape(
            N_DEV, ng * POOL_GROUP // N_DEV, POOL_GROUP)
    k, n = dw.shape
    if name in COL_SHARDED:
        return dw.reshape(k, N_DEV, n // N_DEV).transpose(1, 0, 2)
    return dw.reshape(N_DEV, k // N_DEV, n)


def _to_internal(name, w):
    if name == "w_in":
        d = w.shape[0]
        return jnp.concatenate([w[:, :Z_SRC - QK_ROPE], w[:, Z_SRC:], w[:, Z_SRC - QK_ROPE:Z_SRC],
                                jnp.zeros((d, Z_COLS - Z_KR - QK_ROPE), w.dtype)], axis=1)
    if name == "w_q_b":
        r = w.shape[0]
        w3 = w.reshape(r, HEADS, QK_NOPE + QK_ROPE)
        return jnp.pad(w3, ((0, 0), (0, 0), (0, HEAD_W - QK_NOPE - QK_ROPE))).reshape(r, Q_COLS)
    return w


def _from_internal(name, dw):
    if name == "w_in":
        return jnp.concatenate([dw[:, :Z_SRC - QK_ROPE], dw[:, Z_KR:Z_KR + QK_ROPE], dw[:, Z_SRC - QK_ROPE:Z_KR]], axis=1)
    if name == "w_q_b":
        r = dw.shape[0]
        return dw.reshape(r, HEADS, HEAD_W)[:, :, :QK_NOPE + QK_ROPE].reshape(r, HEADS * (QK_NOPE + QK_ROPE))
    return dw


def _shard2d(a):
    return a.reshape(-1, a.shape[-1])


def kernel(x, meta_tokens, norm_ffn1_pre, norm_ffn1_post, ffn1_w_gu, ffn1_w_down, norm_mix_pre, norm_mix_post, w_in, pool_w, pool_scale, w_pool_o, q_a_norm, w_q_b, kv_a_norm, w_kv_b, w_mla_o, w_out, norm_ffn2_pre, norm_ffn2_post, ffn2_w_gu, ffn2_w_down, loss_target, m_meta_tokens, m_norm_ffn1_pre, m_norm_ffn1_post, m_ffn1_w_gu, m_ffn1_w_down, m_norm_mix_pre, m_norm_mix_post, m_w_in, m_pool_w, m_pool_scale, m_w_pool_o, m_q_a_norm, m_w_q_b, m_kv_a_norm, m_w_kv_b, m_w_mla_o, m_w_out, m_norm_ffn2_pre, m_norm_ffn2_post, m_ffn2_w_gu, m_ffn2_w_down, v_meta_tokens, v_norm_ffn1_pre, v_norm_ffn1_post, v_ffn1_w_gu, v_ffn1_w_down, v_norm_mix_pre, v_norm_mix_post, v_w_in, v_pool_w, v_pool_scale, v_w_pool_o, v_q_a_norm, v_w_q_b, v_kv_a_norm, v_w_kv_b, v_w_mla_o, v_w_out, v_norm_ffn2_pre, v_norm_ffn2_post, v_ffn2_w_gu, v_ffn2_w_down):
    given = dict(locals())
    w_in_dev = {n: given[n] for n in WEIGHTS}
    m_in = {n: given["m_" + n] for n in WEIGHTS}
    v_in = {n: given["v_" + n] for n in WEIGHTS}
    xi, yi, ci = _my_pos()
    dev = 4 * xi + 2 * yi + ci
    core = jnp.reshape(ci, (1,)).astype(jnp.int32)
    chip = jnp.reshape(2 * xi + yi, (1,)).astype(jnp.int32)
    d = x.shape[-1]

    shards = {n: _shard2d(w_in_dev[n]) for n in BIG}
    ex = _Exchange(shards, meta_tokens, dev, core)
    gain = {n: given[n] for n in SMALL}
    loss_blk, grad_x, front, gsmall = local_step(x[0], loss_target[0], gain, ex)

    out_g, out_d, out_m, out_v = {}, {}, {}, {}

    def finish(grp, after):
        names, sums, from_chips = ex.finish_grads(grp, after)
        for n, s, r in zip(names, sums, from_chips, strict=True):
            shp = w_in_dev[n].shape
            res = adamw_shard("adamw_" + n, shards[n], _shard2d(m_in[n]), _shard2d(v_in[n]), s, r, chip)
            out_g[n], out_d[n], out_m[n], out_v[n] = [t.reshape(shp) for t in res]
        return res[0]

    finish("B", finish("C", grad_x))

    tail = jnp.concatenate([gsmall["pool_scale"], gsmall["q_a_norm"], gsmall["kv_a_norm"]], axis=1)
    small = jnp.concatenate([gsmall[n] for n in GAINS] + [tail, jnp.broadcast_to(loss_blk[:1, :1], (1, d)),
                                                         front[PAD_ROWS:]], axis=0)
    (small_g,) = all_gather("ag_small", [small], [LEAD])
    total = reduce_small(small_g)
    ng = len(GAINS)
    for i, n in enumerate(GAINS):
        out_g[n] = total[i:i + 1]
    o = 0
    for n in ("pool_scale", "q_a_norm", "kv_a_norm"):
        wdt = w_in_dev[n].shape[1]
        out_g[n] = total[ng:ng + 1, o:o + wdt]
        o += wdt
    loss = total[ng + 1, 0]
    mcols = meta_tokens.shape[1]
    out_g["meta_tokens"] = lax.dynamic_slice(total[ng + 2:ng + 2 + N_META], (0, dev * mcols), (N_META, mcols))
    names = ["meta_tokens"] + SMALL
    ds_, ms_, vs_ = adamw_small([w_in_dev[n] for n in names], [out_g[n] for n in names],
                                [m_in[n] for n in names], [v_in[n] for n in names])
    for n, dd, mm, vv in zip(names, ds_, ms_, vs_, strict=True):
        out_d[n], out_m[n], out_v[n] = dd, mm, vv
    finish("A", ds_[0])

    return (loss, grad_x[None], *[out_g[n] for n in WEIGHTS], *[out_d[n] for n in WEIGHTS],
            *[out_m[n] for n in WEIGHTS], *[out_v[n] for n in WEIGHTS])


GROUPS = {"A": ["ffn1_w_gu", "ffn1_w_down"],
          "B": ["w_in", "pool_w", "w_pool_o", "w_q_b", "w_kv_b", "w_mla_o", "w_out"],
          "C": ["ffn2_w_gu", "ffn2_w_down"]}


class _Exchange:
    def __init__(self, shards, meta_tokens, dev, core):
        self.shards, self.meta_tokens, self.core = shards, meta_tokens, core
        self.dev1 = jnp.reshape(dev, (1,)).astype(jnp.int32)
        self.w, self.meta_full, self.token = {}, None, None
        self._ag, self._rs = {}, {}

    def tie(self, arr):
        return arr if self.token is None else lax.optimization_barrier((arr, self.token))[0]

    def _ag_start(self, grp, after):
        names = GROUPS[grp] + (["meta_tokens"] if grp == "A" else [])
        srcs = [self.meta_tokens if n == "meta_tokens" else self.shards[n] for n in names]
        lays = [GATHERED.get(n, LEAD) for n in names]
        shapes = [a.shape for a in srcs]
        lands = [place_own(f"place_{n}", a, lay, F32 if n == "meta_tokens" else BF16, self.dev1, after)
                 for n, a, lay in zip(names, srcs, lays, strict=True)]
        plan, cnt = _ag_own_plan(shapes, lays)
        ss, rs, bufs, self.token = split_start(f"ag{grp}_own_start", lands, plan, cnt)
        self._ag[grp] = dict(names=names, lays=lays, shapes=shapes, own=(ss, rs, bufs, plan))

    def _ag_pass(self, grp, after):
        st = self._ag[grp]
        ss, rs, bufs, plan = st["own"]
        lands = split_wait(f"ag{grp}_own_wait", bufs, ss, rs, plan, after)
        plan, cnt = _ag_pass_plan(st["shapes"], st["lays"])
        ss, rs, lands, self.token = split_start(f"ag{grp}_pass_start", lands, plan, cnt)
        st["pass"] = (ss, rs, lands, plan)

    def _ag_finish(self, grp, after):
        st = self._ag[grp]
        ss, rs, lands, plan = st["pass"]
        lands = split_wait(f"ag{grp}_pass_wait", lands, ss, rs, plan, after)
        for n, g, lay in zip(st["names"], lands, st["lays"]):
            if n == "meta_tokens":
                self.meta_full = g.transpose(1, 0, 2).reshape(N_META, N_DEV * g.shape[-1])
            else:
                self.w[n] = g if lay != LEAD else _to_internal(n, _full_from_gathered(n, g))

    def grads(self, grp, gbig, carry):
        after = None
        names = GROUPS[grp]
        lays = [GATHERED.get(n, LEAD) for n in names]
        grads = [gbig[n] if lay != LEAD else _blocks_from_full(n, _from_internal(n, gbig[n]).astype(BF16))
                 for n, lay in zip(names, lays, strict=True)]
        lands = [lax.empty((4, *_shard_shape(g, lay)), BF16) for g, lay in zip(grads, lays, strict=True)]
        plan, cnt = _rs_sibling_plan(lays, len(names))
        ss, rs, bufs, self.token = split_start(f"rs{grp}_sibling_start", grads + lands, plan, cnt, after)
        self._rs[grp] = dict(names=names, lays=lays, sib=(ss, rs, bufs, plan))
        return self.tie(carry)

    def _rs_mid(self, grp, after):
        st = self._rs[grp]
        n = len(st["names"])
        ss, rs, bufs, plan = st["sib"]
        bufs = split_wait(f"rs{grp}_sibling_wait", bufs, ss, rs, plan, after)
        sums = [rs_add(f"rs_add_{name}", g, lay, r, self.core)
                for name, g, lay, r in zip(st["names"], bufs[:n], st["lays"], bufs[n:], strict=True)]
        lands = [lax.empty((3, *s.shape[1:]), BF16) for s in sums]
        plan, cnt = _rs_chips_plan(n)
        ss, rs, bufs, self.token = split_start(f"rs{grp}_chips_start", sums + lands, plan, cnt)
        st["chips"] = (ss, rs, bufs, plan)

    def finish_grads(self, grp, after):
        st = self._rs[grp]
        n = len(st["names"])
        ss, rs, bufs, plan = st["chips"]
        bufs = split_wait(f"rs{grp}_chips_wait", bufs, ss, rs, plan, after)
        return st["names"], bufs[:n], bufs[n:]

    def point(self, name, after=None):
        if name == "start":
            self._ag_start("A", None)
            self._ag_pass("A", self.token)
            self._ag_start("B", self.token)
            self._ag_finish("A", self.token)
        elif name == "ffn1_fwd_done":
            self._ag_pass("B", after)
            self._ag_start("C", self.token)
        elif name == "mix_pre_done":
            self._ag_finish("B", after)
        elif name == "flash_fwd_done":
            self._ag_pass("C", after)
        elif name == "ffn2_pre_done":
            self._ag_finish("C", after)
        elif name in ("rsC_mid", "rsB_mid", "rsA_mid"):
            self._rs_mid(name[2], after)


def local_step(x, target, gain, ex):
    d = x.shape[-1]
    ex.point("start")
    w = ex.w
    h0 = jnp.concatenate([jnp.zeros((PAD_ROWS, d), F32), ex.meta_full, x], axis=0)
    lp = h0.shape[0]
    tabs = rope_tables(lp)
    a1 = prenorm(ex.tie(h0), gain["norm_ffn1_pre"])
    gu1, s1, f1 = ffn_fwd("ffn1", a1, w["ffn1_w_gu"], w["ffn1_w_down"])
    ex.point("ffn1_fwd_done", f1)
    h1, a2 = post_pre("post_pre1", ex.tie(f1), h0, gain["norm_ffn1_post"], 0.5, gain["norm_mix_pre"])
    ex.point("mix_pre_done", a2)
    mix, sv = mixer_fwd(a2, w, tabs, gain["pool_scale"], gain["q_a_norm"], gain["kv_a_norm"], ex)
    h2, a3 = post_pre("post_pre2", mix, h1, gain["norm_mix_post"], 1.0, gain["norm_ffn2_pre"])
    ex.point("ffn2_pre_done", a3)
    gu2, s2, f2 = ffn_fwd("ffn2", a3, w["ffn2_w_gu"], w["ffn2_w_down"])
    dh3, loss_blk = post_loss(f2, h2, gain["norm_ffn2_post"], 0.5, target)

    gsmall = {}
    df2, gsmall["norm_ffn2_post"] = post_bwd(dh3, f2, gain["norm_ffn2_post"], 0.5)
    da3 = ffn_bwd("ffn2", df2, a3, gu2, s2, w["ffn2_w_gu"], w["ffn2_w_down"], lambda g, dgu: ex.grads("C", g, dgu))
    dh2, dmix, gsmall["norm_ffn2_pre"], gsmall["norm_mix_post"] = pre_post_bwd(
        "pre_post_bwd2", da3, h2, gain["norm_ffn2_pre"], dh3, mix, gain["norm_mix_post"], 1.0)
    ex.point("rsC_mid", dmix)
    da2, gmix, gmix_small = mixer_bwd(ex.tie(dmix), a2, sv, w, tabs, gain["pool_scale"], gain["q_a_norm"], gain["kv_a_norm"])
    da2 = ex.grads("B", gmix, da2)
    gsmall.update(gmix_small)
    dh1, df1, gsmall["norm_mix_pre"], gsmall["norm_ffn1_post"] = pre_post_bwd(
        "pre_post_bwd1", da2, h1, gain["norm_mix_pre"], dh2, f1, gain["norm_ffn1_post"], 0.5)
    ex.point("rsB_mid", df1)
    da1 = ffn_bwd("ffn1", ex.tie(df1), a1, gu1, s1, w["ffn1_w_gu"], w["ffn1_w_down"], lambda g, dgu: ex.grads("A", g, dgu))
    ex.point("rsA_mid", da1)
    grad_x, front, gsmall["norm_ffn1_pre"] = pre_bwd_first(ex.tie(da1), h0, gain["norm_ffn1_pre"], dh1)
    return loss_blk, grad_x, front, gsmall
```

```python
import functools

import jax
import jax.numpy as jnp
import numpy as np
from jax import lax
from jax.experimental import pallas as pl
from jax.experimental.pallas import tpu as pltpu

F32 = jnp.float32
BF16 = jnp.bfloat16

N_META = 16
POOL_WINDOWS = (2, 4, 8, 16)
POOL_GROUP = 256
HEADS = 16
QK_NOPE = 128
QK_ROPE = 64
V_DIM = 128
ROPE_THETA = 10000.0
SOFTMAX_SCALE = (QK_NOPE + QK_ROPE) ** -0.5
EPS = 1e-6
ADAM_LR = 0.001
ADAM_B1 = 0.9
ADAM_B2 = 0.999
ADAM_EPS = 1e-08
ADAM_WD = 0.01
ADAM_STEP = 10

LANE = 128
FRONT = 128
PAD_ROWS = FRONT - N_META
HEAD_W = 256
GU_TILE = 1408
VMEM_LIMIT = 56 * 1024 * 1024
MESH_AXES = ("x", "y", "c")
N_DEV = 8


def _pick(n, cands):
    for c in cands:
        if n % c == 0:
            return c
    raise ValueError(f"no tile for {n} in {cands}")


def _cparams(sem=None):
    kw = dict(vmem_limit_bytes=VMEM_LIMIT)
    if sem is not None:
        kw["dimension_semantics"] = sem
    return pltpu.CompilerParams(**kw)


_DIMS = {"nn": (((1,), (0,)), ((), ())), "nt": (((1,), (1,)), ((), ())), "tn": (((0,), (0,)), ((), ()))}


def _behind(after):
    return [] if after is None else [after]


def _behind_specs(after):
    return [] if after is None else [pl.BlockSpec(memory_space=pl.ANY)]


def _mm(name, form, a, b, tm, tn, outs, epi=None, extras=(), n_outer=False, after=None):
    if form == "tn":
        k, m = a.shape
        n = b.shape[1]
        a_blk, a_map = (k, tm), lambda i, j: (0, i)
        b_blk, b_map = (k, tn), lambda i, j: (0, j)
    elif form == "nn":
        m, k = a.shape
        n = b.shape[1]
        a_blk, a_map = (tm, k), lambda i, j: (i, 0)
        b_blk, b_map = (k, tn), lambda i, j: (0, j)
    else:
        m, k = a.shape
        n = b.shape[0]
        a_blk, a_map = (tm, k), lambda i, j: (i, 0)
        b_blk, b_map = (tn, k), lambda i, j: (j, 0)
    assert m % tm == 0 and n % tn == 0, (name, m, n, tm, tn)
    n_ex = len(extras)
    dn = _DIMS[form]
    if n_outer:
        grid = (n // tn, m // tm)

        def spec(blk, im):
            return pl.BlockSpec(blk, lambda gj, gi: im(gi, gj))
    else:
        grid = (m // tm, n // tn)
        spec = pl.BlockSpec

    behind = _behind(after)

    def body(a_ref, b_ref, *rest):
        ex, out_refs = rest[:n_ex], rest[n_ex + len(behind):]
        acc = lax.dot_general(a_ref[...].astype(BF16), b_ref[...].astype(BF16), dn, preferred_element_type=F32)
        res = epi(acc, *[e[...] for e in ex]) if epi is not None else (acc,)
        for r, o in zip(res, out_refs, strict=True):
            o[...] = r.astype(o.dtype)

    return pl.pallas_call(
        body,
        name=name,
        grid=grid,
        in_specs=[spec(a_blk, a_map), spec(b_blk, b_map)] + [spec(blk, im) for _, blk, im in extras] + _behind_specs(after),
        out_specs=[spec(blk, im) for _, _, blk, im in outs],
        out_shape=[jax.ShapeDtypeStruct(s, d) for s, d, _, _ in outs],
        compiler_params=_cparams(("parallel", "parallel")),
    )(a, b, *[e for e, _, _ in extras], *behind)


def _mm_plain(name, form, a, b, tm, tn, out_dtype, after=None):
    m = a.shape[1] if form == "tn" else a.shape[0]
    n = b.shape[0] if form == "nt" else b.shape[1]
    return _mm(name, form, a, b, tm, tn, [((m, n), out_dtype, (tm, tn), lambda i, j: (i, j))], after=after)[0]


def _rstd(x):
    return lax.rsqrt(jnp.mean(x * x, axis=-1, keepdims=True) + EPS)


def _norm_bwd(dy, x, gain):
    r = _rstd(x)
    dyg = dy * gain
    dx = r * (dyg - x * (r * r) * jnp.mean(dyg * x, axis=-1, keepdims=True))
    dgain = jnp.sum(dy * x * r, axis=0, keepdims=True)
    return dx, dgain


def _row_spec(tr, cols, col_block=0):
    return pl.BlockSpec((tr, cols), lambda i: (i, col_block))


def _vec_spec(cols, col_block=0):
    return pl.BlockSpec((1, cols), lambda i: (0, col_block))


def _row_tile(lp):
    return _pick(lp, (128,))


def _skip_behind(after, body):
    return body if after is None else (lambda _unread, *refs: body(*refs))


def prenorm(h, gain, after=None):
    lp, d = h.shape
    tr = _row_tile(lp)

    def body(h_ref, g_ref, a_ref):
        x = h_ref[...]
        a_ref[...] = (x * _rstd(x) * g_ref[...]).astype(BF16)

    return pl.pallas_call(
        _skip_behind(after, body), name="prenorm", grid=(lp // tr,),
        in_specs=_behind_specs(after) + [_row_spec(tr, d), _vec_spec(d)], out_specs=_row_spec(tr, d),
        out_shape=jax.ShapeDtypeStruct((lp, d), BF16), compiler_params=_cparams(("parallel",)),
    )(*_behind(after), h, gain)


def post_pre(name, f, h_in, g_post, coef, g_next, after=None):
    lp, d = f.shape
    tr = _row_tile(lp)

    def body(f_ref, h_ref, gp_ref, gn_ref, ho_ref, a_ref):
        fv = f_ref[...]
        ho = h_ref[...] + coef * (fv * _rstd(fv) * gp_ref[...])
        ho_ref[...] = ho
        a_ref[...] = (ho * _rstd(ho) * gn_ref[...]).astype(BF16)

    return pl.pallas_call(
        _skip_behind(after, body), name=name, grid=(lp // tr,),
        in_specs=_behind_specs(after) + [_row_spec(tr, d), _row_spec(tr, d), _vec_spec(d), _vec_spec(d)],
        out_specs=[_row_spec(tr, d), _row_spec(tr, d)],
        out_shape=[jax.ShapeDtypeStruct((lp, d), F32), jax.ShapeDtypeStruct((lp, d), BF16)],
        compiler_params=_cparams(("parallel",)),
    )(*_behind(after), f, h_in, g_post, g_next)


def post_loss(f, h_in, g_post, coef, target):
    lp, d = f.shape
    tr = _row_tile(lp)
    front_tiles = FRONT // tr

    def body(f_ref, h_ref, gp_ref, t_ref, dh_ref, loss_ref):
        i = pl.program_id(0)

        @pl.when(i == 0)
        def _():
            loss_ref[...] = jnp.zeros_like(loss_ref)

        @pl.when(i < front_tiles)
        def _():
            dh_ref[...] = jnp.zeros_like(dh_ref)

        @pl.when(i >= front_tiles)
        def _():
            fv = f_ref[...]
            ho = h_ref[...] + coef * (fv * _rstd(fv) * gp_ref[...])
            err = ho - t_ref[...]
            dh_ref[...] = err / d
            tok = jnp.mean(err * err, axis=-1, keepdims=True)
            loss_ref[...] += 0.5 * jnp.sum(tok)

    return pl.pallas_call(
        body, name="post_loss", grid=(lp // tr,),
        in_specs=[_row_spec(tr, d), _row_spec(tr, d), _vec_spec(d),
                  pl.BlockSpec((tr, d), lambda i: (jnp.maximum(i - front_tiles, 0), 0))],
        out_specs=[_row_spec(tr, d), pl.BlockSpec((8, LANE), lambda i: (0, 0))],
        out_shape=[jax.ShapeDtypeStruct((lp, d), F32), jax.ShapeDtypeStruct((8, LANE), F32)],
        compiler_params=_cparams(("arbitrary",)),
    )(f, h_in, g_post, target)


def post_bwd(dh_out, f, g_post, coef):
    lp, d = f.shape
    tr = _row_tile(lp)

    def body(dh_ref, f_ref, gp_ref, df_ref, dg_ref):
        @pl.when(pl.program_id(0) == 0)
        def _():
            dg_ref[...] = jnp.zeros_like(dg_ref)

        df, dg = _norm_bwd(coef * dh_ref[...], f_ref[...], gp_ref[...])
        df_ref[...] = df.astype(BF16)
        dg_ref[...] += dg

    return pl.pallas_call(
        body, name="post_bwd", grid=(lp // tr,),
        in_specs=[_row_spec(tr, d), _row_spec(tr, d), _vec_spec(d)],
        out_specs=[_row_spec(tr, d), _vec_spec(d)],
        out_shape=[jax.ShapeDtypeStruct((lp, d), BF16), jax.ShapeDtypeStruct((1, d), F32)],
        compiler_params=_cparams(("arbitrary",)),
    )(dh_out, f, g_post)


def pre_post_bwd(name, da, h_mid, g_pre, dh_out, f_prev, g_post_prev, coef_prev, after=None):
    lp, d = da.shape
    tr = _row_tile(lp)

    def body(da_ref, h_ref, gpre_ref, dho_ref, f_ref, gpost_ref, dh_ref, df_ref, dgpre_ref, dgpost_ref):
        @pl.when(pl.program_id(0) == 0)
        def _():
            dgpre_ref[...] = jnp.zeros_like(dgpre_ref)
            dgpost_ref[...] = jnp.zeros_like(dgpost_ref)

        dx, dgpre = _norm_bwd(da_ref[...], h_ref[...], gpre_ref[...])
        dh = dho_ref[...] + dx
        dh_ref[...] = dh
        dgpre_ref[...] += dgpre
        df, dgpost = _norm_bwd(coef_prev * dh, f_ref[...], gpost_ref[...])
        df_ref[...] = df.astype(BF16)
        dgpost_ref[...] += dgpost

    return pl.pallas_call(
        _skip_behind(after, body), name=name, grid=(lp // tr,),
        in_specs=_behind_specs(after) + [_row_spec(tr, d), _row_spec(tr, d), _vec_spec(d), _row_spec(tr, d),
                                         _row_spec(tr, d), _vec_spec(d)],
        out_specs=[_row_spec(tr, d), _row_spec(tr, d), _vec_spec(d), _vec_spec(d)],
        out_shape=[jax.ShapeDtypeStruct((lp, d), F32), jax.ShapeDtypeStruct((lp, d), BF16),
                   jax.ShapeDtypeStruct((1, d), F32), jax.ShapeDtypeStruct((1, d), F32)],
        compiler_params=_cparams(("arbitrary",)),
    )(*_behind(after), da, h_mid, g_pre, dh_out, f_prev, g_post_prev)


def pre_bwd_first(da, h0, g_pre, dh_out, after=None):
    lp, d = da.shape
    tr = _row_tile(lp)
    front_tiles = FRONT // tr
    assert front_tiles == 1

    def body(da_ref, h_ref, gpre_ref, dho_ref, gx_ref, front_ref, dgpre_ref):
        i = pl.program_id(0)

        @pl.when(i == 0)
        def _():
            dgpre_ref[...] = jnp.zeros_like(dgpre_ref)

        dx, dgpre = _norm_bwd(da_ref[...], h_ref[...], gpre_ref[...])
        dh = dho_ref[...] + dx
        dgpre_ref[...] += dgpre
        gx_ref[...] = dh

        @pl.when(i == 0)
        def _():
            front_ref[...] = dh

    return pl.pallas_call(
        _skip_behind(after, body), name="pre_bwd_first", grid=(lp // tr,),
        in_specs=_behind_specs(after) + [_row_spec(tr, d), _row_spec(tr, d), _vec_spec(d), _row_spec(tr, d)],
        out_specs=[pl.BlockSpec((tr, d), lambda i: (jnp.maximum(i - front_tiles, 0), 0)),
                   pl.BlockSpec((tr, d), lambda i: (0, 0)), _vec_spec(d)],
        out_shape=[jax.ShapeDtypeStruct((lp - FRONT, d), F32), jax.ShapeDtypeStruct((tr, d), F32),
                   jax.ShapeDtypeStruct((1, d), F32)],
        compiler_params=_cparams(("arbitrary",)),
    )(*_behind(after), da, h0, g_pre, dh_out)


def _m_tile(lp):
    return _pick(lp, (1056, 512, 256, 128))


def ffn_fwd(tag, a, wgu_p, wd):
    lp, d = a.shape
    f2 = wgu_p.shape[1]
    tm = _m_tile(lp)

    def epi(acc):
        g, u = acc[:, :GU_TILE], acc[:, GU_TILE:]
        return acc, g * jax.nn.sigmoid(g) * u

    tg = _pick(lp, (384, 256, 128))
    gu, s = _mm(tag + "_gu", "nn", a, wgu_p, tg, 2 * GU_TILE,
                [((lp, f2), F32, (tg, 2 * GU_TILE), lambda i, j: (i, j)),
                 ((lp, f2 // 2), BF16, (tg, GU_TILE), lambda i, j: (i, j))], epi=epi, n_outer=True)
    f = _mm_plain(tag + "_down", "nn", s, wd, tm, 512, F32)
    return gu, s, f


def ffn_bwd(tag, df, a, gu, s, wgu_p, wd, emit, after=None):
    lp, d = df.shape
    f2 = wgu_p.shape[1]
    tm = _m_tile(lp)

    def epi(acc, gu_t):
        g, u = gu_t[:, :GU_TILE], gu_t[:, GU_TILE:]
        sig = jax.nn.sigmoid(g)
        dg = acc * u * (sig * (1.0 + g * (1.0 - sig)))
        du = acc * (g * sig)
        return (jnp.concatenate([dg, du], axis=1),)

    ts = _pick(lp, (528, 256, 128))
    dgu = _mm(tag + "_ds", "nt", df, wd, ts, GU_TILE,
              [((lp, f2), BF16, (ts, 2 * GU_TILE), lambda i, j: (i, j))], epi=epi,
              extras=[(gu, (ts, 2 * GU_TILE), lambda i, j: (i, j))], n_outer=True, after=after)[0]
    dwd = _mm_plain(tag + "_dwd", "tn", s, df, 512, _pick(d, (1024,)), BF16, after=after)
    dwgu = _mm_plain(tag + "_dwgu", "tn", a, dgu, _pick(d, (1024,)), 1024, BF16)
    started = emit({tag + "_w_gu": dwgu, tag + "_w_down": dwd})
    return _mm_plain(tag + "_da", "nt", dgu, wgu_p, _pick(lp, (528, 256, 128)), 256, F32, after=started)


Z_U, Z_CQ, Z_CKV, Z_GP, Z_GM, Z_KR, Z_COLS = 0, 1024, 1536, 2048, 4096, 6144, 6400
POOL_W = POOL_GROUP * len(POOL_WINDOWS)
HALO = 16


def _pool_counts(lp, w):
    pos = lax.broadcasted_iota(jnp.int32, (lp, 1), 0) - PAD_ROWS
    return jnp.clip(pos + 1, 1, w).astype(F32)


def _pool_diff(u_ref, pad_ref, lp, w):
    pad_ref[pl.ds(0, HALO), :] = jnp.zeros((HALO, POOL_GROUP), F32)
    pad_ref[pl.ds(HALO, lp), :] = u_ref[...]
    acc = pad_ref[pl.ds(HALO, lp), :]
    for s in range(1, w):
        acc = acc + pad_ref[pl.ds(HALO - s, lp), :]
    return acc / _pool_counts(lp, w) - u_ref[...]


def pool_fwd(z, pool_w, pool_scale):
    lp = z.shape[0]
    ng = len(POOL_WINDOWS)

    def body(u_ref, w_ref, sc_ref, o_ref, pad_ref):
        for g, w in enumerate(POOL_WINDOWS):
            @pl.when(pl.program_id(0) == g)
            def _(w=w):
                dd = _pool_diff(u_ref, pad_ref, lp, w)
                y = jnp.dot(dd.astype(BF16), w_ref[0], preferred_element_type=F32)
                o_ref[...] = (y * sc_ref[...]).astype(BF16)

    return pl.pallas_call(
        body, name="pool_fwd", grid=(ng,),
        in_specs=[pl.BlockSpec((lp, POOL_GROUP), lambda g: (0, g)),
                  pl.BlockSpec((1, POOL_GROUP, POOL_GROUP), lambda g: (g, 0, 0)),
                  pl.BlockSpec((1, POOL_GROUP), lambda g: (0, g))],
        out_specs=pl.BlockSpec((lp, POOL_GROUP), lambda g: (0, g)),
        out_shape=jax.ShapeDtypeStruct((lp, POOL_W), BF16),
        scratch_shapes=[pltpu.VMEM((lp + HALO, POOL_GROUP), F32)],
        compiler_params=_cparams(("parallel",)),
    )(z, pool_w, pool_scale)


def pool_bwd(dyp, z, pool_w, pool_scale):
    lp = z.shape[0]
    ng = len(POOL_WINDOWS)

    def body(dy_ref, u_ref, w_ref, sc_ref, du_ref, dw_ref, dsc_ref, pad_ref):
        for g, w in enumerate(POOL_WINDOWS):
            @pl.when(pl.program_id(0) == g)
            def _(w=w):
                dd = _pool_diff(u_ref, pad_ref, lp, w).astype(BF16)
                wg = w_ref[0]
                ypre = jnp.dot(dd, wg, preferred_element_type=F32)
                dy = dy_ref[...]
                dsc_ref[...] = jnp.sum(dy * ypre, axis=0, keepdims=True)
                dypre = (dy * sc_ref[...]).astype(BF16)
                dw_ref[0] = lax.dot_general(dd, dypre, _DIMS["tn"], preferred_element_type=F32)
                ddd = lax.dot_general(dypre, wg, _DIMS["nt"], preferred_element_type=F32)
                pad_ref[pl.ds(0, lp), :] = ddd / _pool_counts(lp, w)
                pad_ref[pl.ds(lp, HALO), :] = jnp.zeros((HALO, POOL_GROUP), F32)
                acc = -ddd
                for s in range(w):
                    acc = acc + pad_ref[pl.ds(s, lp), :]
                du_ref[...] = acc.astype(BF16)

    return pl.pallas_call(
        body, name="pool_bwd", grid=(ng,),
        in_specs=[pl.BlockSpec((lp, POOL_GROUP), lambda g: (0, g)),
                  pl.BlockSpec((lp, POOL_GROUP), lambda g: (0, g)),
                  pl.BlockSpec((1, POOL_GROUP, POOL_GROUP), lambda g: (g, 0, 0)),
                  pl.BlockSpec((1, POOL_GROUP), lambda g: (0, g))],
        out_specs=[pl.BlockSpec((lp, POOL_GROUP), lambda g: (0, g)),
                   pl.BlockSpec((1, POOL_GROUP, POOL_GROUP), lambda g: (g, 0, 0)),
                   pl.BlockSpec((1, POOL_GROUP), lambda g: (0, g))],
        out_shape=[jax.ShapeDtypeStruct((lp, POOL_W), BF16),
                   jax.ShapeDtypeStruct((ng, POOL_GROUP, POOL_GROUP), F32),
                   jax.ShapeDtypeStruct((1, POOL_W), F32)],
        scratch_shapes=[pltpu.VMEM((lp + HALO, POOL_GROUP), F32)],
        compiler_params=_cparams(("parallel",)),
    )(dyp, z, pool_w, pool_scale)


Q_COLS = HEADS * HEAD_W
ROPE_BLOCK = LANE


def rope_tables(lp):
    pos = jnp.maximum(jnp.arange(lp, dtype=F32) - PAD_ROWS, 0.0)
    inv = ROPE_THETA ** (-jnp.arange(0, QK_ROPE, 2, dtype=F32) / QK_ROPE)
    ang = pos[:, None] * inv[None, :]
    cos, sin, zero = jnp.cos(ang), jnp.sin(ang), jnp.zeros_like(ang)
    return jnp.stack([jnp.concatenate([cos, cos, zero, zero], axis=1),
                      jnp.concatenate([-sin, zero, zero, zero], axis=1),
                      jnp.concatenate([zero, sin, zero, zero], axis=1)])


def _rope(x, tabs):
    return x * tabs[0] + pltpu.roll(x, 96, 1) * tabs[1] + pltpu.roll(x, 32, 1) * tabs[2]


def _rope_bwd(g, tabs):
    return g * tabs[0] + pltpu.roll(g * tabs[1], 32, 1) + pltpu.roll(g * tabs[2], 96, 1)


def _tab_spec(tr):
    return pl.BlockSpec((3, tr, ROPE_BLOCK), lambda i: (0, i, 0))


def mla_prep(z, g_q, g_kv, tabs):
    lp = z.shape[0]
    tr = _row_tile(lp)
    r = g_q.shape[1]

    def body(cq_ref, ckv_ref, kr_ref, gq_ref, gkv_ref, tab_ref, qn_ref, kvn_ref, kpe_ref):
        cq, ckv = cq_ref[...], ckv_ref[...]
        qn_ref[...] = (cq * _rstd(cq) * gq_ref[...]).astype(BF16)
        kvn_ref[...] = (ckv * _rstd(ckv) * gkv_ref[...]).astype(BF16)
        kpe_ref[...] = _rope(kr_ref[...], tab_ref[...]).astype(BF16)

    return pl.pallas_call(
        body, name="mla_prep", grid=(lp // tr,),
        in_specs=[_row_spec(tr, r, Z_CQ // r), _row_spec(tr, r, Z_CKV // r), _row_spec(tr, ROPE_BLOCK, Z_KR // ROPE_BLOCK),
                  _vec_spec(r), _vec_spec(r), _tab_spec(tr)],
        out_specs=[_row_spec(tr, r), _row_spec(tr, r), _row_spec(tr, ROPE_BLOCK)],
        out_shape=[jax.ShapeDtypeStruct((lp, r), BF16), jax.ShapeDtypeStruct((lp, r), BF16),
                   jax.ShapeDtypeStruct((lp, ROPE_BLOCK), BF16)],
        compiler_params=_cparams(("parallel",)),
    )(z, z, z, g_q, g_kv, tabs)


def q_proj(qn, wq_p, tabs):
    lp = qn.shape[0]
    tm = _m_tile(lp)
    tn = 4 * HEAD_W

    def epi(acc, tab):
        parts = []
        for t in range(tn // HEAD_W):
            parts.append(acc[:, t * HEAD_W:t * HEAD_W + QK_NOPE])
            parts.append(_rope(acc[:, t * HEAD_W + QK_NOPE:(t + 1) * HEAD_W], tab))
        return (jnp.concatenate(parts, axis=1),)

    return _mm("q_proj", "nn", qn, wq_p, tm, tn, [((lp, Q_COLS), BF16, (tm, tn), lambda i, j: (i, j))], epi=epi,
               extras=[(tabs, (3, tm, ROPE_BLOCK), lambda i, j: (0, i, 0))])[0]


def kv_proj(kvn, wkv):
    return _mm_plain("kv_proj", "nn", kvn, wkv, _m_tile(kvn.shape[0]), 1024, BF16)


def q_rope_bwd(dq, tabs):
    lp = dq.shape[0]
    tr = _row_tile(lp)

    def body(dq_ref, tab_ref, o_ref):
        tab = tab_ref[...]
        for h in range(HEADS):
            o_ref[:, h * HEAD_W:h * HEAD_W + QK_NOPE] = dq_ref[:, h * HEAD_W:h * HEAD_W + QK_NOPE].astype(BF16)
            o_ref[:, h * HEAD_W + QK_NOPE:(h + 1) * HEAD_W] = _rope_bwd(
                dq_ref[:, h * HEAD_W + QK_NOPE:(h + 1) * HEAD_W], tab).astype(BF16)

    return pl.pallas_call(
        body, name="q_rope_bwd", grid=(lp // tr,),
        in_specs=[_row_spec(tr, Q_COLS), _tab_spec(tr)], out_specs=_row_spec(tr, Q_COLS),
        out_shape=jax.ShapeDtypeStruct((lp, Q_COLS), BF16), compiler_params=_cparams(("parallel",)),
    )(dq, tabs)


def mla_prep_bwd(dqn, dkvn, dkpe_h, z, g_q, g_kv, tabs):
    lp = z.shape[0]
    tr = _row_tile(lp)
    r = g_q.shape[1]

    def body(dqn_ref, dkvn_ref, dkpe_ref, cq_ref, ckv_ref, gq_ref, gkv_ref, tab_ref,
             dcq_ref, dckv_ref, dkr_ref, dgq_ref, dgkv_ref):
        @pl.when(pl.program_id(0) == 0)
        def _():
            dgq_ref[...] = jnp.zeros_like(dgq_ref)
            dgkv_ref[...] = jnp.zeros_like(dgkv_ref)

        dcq, dgq = _norm_bwd(dqn_ref[...], cq_ref[...], gq_ref[...])
        dckv, dgkv = _norm_bwd(dkvn_ref[...], ckv_ref[...], gkv_ref[...])
        dcq_ref[...] = dcq.astype(BF16)
        dckv_ref[...] = dckv.astype(BF16)
        dgq_ref[...] += dgq
        dgkv_ref[...] += dgkv
        dkpe = dkpe_ref[0]
        for h in range(1, HEADS):
            dkpe = dkpe + dkpe_ref[h]
        dkr_ref[...] = _rope_bwd(dkpe, tab_ref[...]).astype(BF16)

    return pl.pallas_call(
        body, name="mla_prep_bwd", grid=(lp // tr,),
        in_specs=[_row_spec(tr, r), _row_spec(tr, r), pl.BlockSpec((HEADS, tr, ROPE_BLOCK), lambda i: (0, i, 0)),
                  _row_spec(tr, r, Z_CQ // r), _row_spec(tr, r, Z_CKV // r), _vec_spec(r), _vec_spec(r), _tab_spec(tr)],
        out_specs=[_row_spec(tr, r), _row_spec(tr, r), _row_spec(tr, ROPE_BLOCK), _vec_spec(r), _vec_spec(r)],
        out_shape=[jax.ShapeDtypeStruct((lp, r), BF16), jax.ShapeDtypeStruct((lp, r), BF16),
                   jax.ShapeDtypeStruct((lp, ROPE_BLOCK), BF16),
                   jax.ShapeDtypeStruct((1, r), F32), jax.ShapeDtypeStruct((1, r), F32)],
        compiler_params=_cparams(("arbitrary",)),
    )(dqn, dkvn, dkpe_h, z, z, g_q, g_kv, tabs)


def _attn_tile(lp):
    return _pick(lp, (528, 128))


def _causal_pairs(nt, k_major):
    if k_major:
        pairs = [(qi, ki) for ki in range(nt) for qi in range(ki, nt)]
    else:
        pairs = [(qi, ki) for qi in range(nt) for ki in range(qi + 1)]
    return (jnp.asarray([p[0] for p in pairs], jnp.int32), jnp.asarray([p[1] for p in pairs], jnp.int32))


def _scores(q, kcat, q_tile, k_tile, t, masked):
    s = lax.dot_general(q, kcat, _DIMS["nt"], preferred_element_type=F32) * SOFTMAX_SCALE
    if not masked:
        return s
    qpos = q_tile * t + lax.broadcasted_iota(jnp.int32, (t, t), 0)
    kpos = k_tile * t + lax.broadcasted_iota(jnp.int32, (t, t), 1)
    return jnp.where((kpos <= qpos) & (kpos >= PAD_ROWS), s, jnp.float32(-1e30))


def _on_masked_or_not(q_tile, k_tile, fn):
    needs_mask = (q_tile == k_tile) | (k_tile == 0)

    @pl.when(needs_mask)
    def _():
        fn(True)

    @pl.when(jnp.logical_not(needs_mask))
    def _():
        fn(False)


def flash_fwd(q, kv, kpe):
    lp = q.shape[0]
    t = _attn_tile(lp)
    q_tab, k_tab = _causal_pairs(lp // t, k_major=False)

    def body(q_tab_ref, k_tab_ref, q_ref, kv_ref, kpe_ref, o32_ref, o16_ref, lse_ref, m_sc, l_sc, acc_sc):
        pair = pl.program_id(1)
        qi, ki = q_tab_ref[pair], k_tab_ref[pair]

        @pl.when(ki == 0)
        def _():
            m_sc[...] = jnp.full_like(m_sc, -jnp.inf)
            l_sc[...] = jnp.zeros_like(l_sc)
            acc_sc[...] = jnp.zeros_like(acc_sc)

        def step(masked):
            kvt = kv_ref[...]
            kcat = jnp.concatenate([kvt[:, :QK_NOPE], kpe_ref[...]], axis=1)
            s = _scores(q_ref[...], kcat, qi, ki, t, masked)
            m_prev = m_sc[...]
            m_new = jnp.maximum(m_prev, jnp.max(s, axis=1, keepdims=True))
            alpha = jnp.exp(m_prev - m_new)
            p = jnp.exp(s - m_new[:, :1])
            l_sc[...] = alpha * l_sc[...] + jnp.sum(p, axis=1, keepdims=True)
            acc_sc[...] = alpha * acc_sc[...] + jnp.dot(p.astype(BF16), kvt[:, QK_NOPE:], preferred_element_type=F32)
            m_sc[...] = m_new

        _on_masked_or_not(qi, ki, step)

        @pl.when(ki == qi)
        def _():
            l = l_sc[...]
            o = acc_sc[...] / l
            o32_ref[...] = o
            o16_ref[...] = o.astype(BF16)
            lse_ref[0] = m_sc[...] + jnp.log(l)

    return pl.pallas_call(
        body, name="flash_fwd",
        grid_spec=pltpu.PrefetchScalarGridSpec(
            num_scalar_prefetch=2, grid=(HEADS, q_tab.shape[0]),
            in_specs=[pl.BlockSpec((t, HEAD_W), lambda h, p, qt, kt: (qt[p], h)),
                      pl.BlockSpec((t, HEAD_W), lambda h, p, qt, kt: (kt[p], h)),
                      pl.BlockSpec((t, ROPE_BLOCK), lambda h, p, qt, kt: (kt[p], 0))],
            out_specs=[pl.BlockSpec((t, V_DIM), lambda h, p, qt, kt: (qt[p], h)),
                       pl.BlockSpec((t, V_DIM), lambda h, p, qt, kt: (qt[p], h)),
                       pl.BlockSpec((1, t, LANE), lambda h, p, qt, kt: (h, qt[p], 0))],
            scratch_shapes=[pltpu.VMEM((t, LANE), F32), pltpu.VMEM((t, LANE), F32), pltpu.VMEM((t, V_DIM), F32)]),
        out_shape=[jax.ShapeDtypeStruct((lp, HEADS * V_DIM), F32), jax.ShapeDtypeStruct((lp, HEADS * V_DIM), BF16),
                   jax.ShapeDtypeStruct((HEADS, lp, LANE), F32)],
        compiler_params=_cparams(("parallel", "arbitrary")),
    )(q_tab, k_tab, q, kv, kpe)


def flash_bwd(q, kv, kpe, o32, lse, do):
    lp = q.shape[0]
    t = _attn_tile(lp)
    nt = lp // t
    q_tab, k_tab = _causal_pairs(nt, k_major=True)

    def body(q_tab_ref, k_tab_ref, q_ref, kv_ref, kpe_ref, o_ref, lse_ref, do_ref, dq_ref, dkv_ref, dkpe_ref,
             dk_sc, dv_sc):
        pair = pl.program_id(1)
        qi, ki = q_tab_ref[pair], k_tab_ref[pair]

        @pl.when(pair == 0)
        def _():
            dq_ref[...] = jnp.zeros_like(dq_ref)

        @pl.when(qi == ki)
        def _():
            dk_sc[...] = jnp.zeros_like(dk_sc)
            dv_sc[...] = jnp.zeros_like(dv_sc)

        def step(masked):
            qt = q_ref[...]
            kvt = kv_ref[...]
            kcat = jnp.concatenate([kvt[:, :QK_NOPE], kpe_ref[...]], axis=1)
            s = _scores(qt, kcat, qi, ki, t, masked)
            p = jnp.exp(s - lse_ref[0][:, :1])
            do = do_ref[...]
            delta = jnp.sum(do * o_ref[...], axis=1, keepdims=True)
            do16 = do.astype(BF16)
            dv_sc[...] += lax.dot_general(p.astype(BF16), do16, _DIMS["tn"], preferred_element_type=F32)
            dp = lax.dot_general(do16, kvt[:, QK_NOPE:], _DIMS["nt"], preferred_element_type=F32)
            ds = (p * (dp - delta) * SOFTMAX_SCALE).astype(BF16)
            dk_sc[...] += lax.dot_general(ds, qt, _DIMS["tn"], preferred_element_type=F32)
            row = pl.multiple_of(qi * t, t)
            dq_ref[pl.ds(row, t), :] += jnp.dot(ds, kcat, preferred_element_type=F32)

        _on_masked_or_not(qi, ki, step)

        @pl.when(qi == nt - 1)
        def _():
            dk = dk_sc[...]
            dkv_ref[...] = jnp.concatenate([dk[:, :QK_NOPE], dv_sc[...]], axis=1).astype(BF16)
            dkpe_ref[0] = dk[:, QK_NOPE:]

    qmap = lambda h, p, qt, kt: (qt[p], h)
    return pl.pallas_call(
        body, name="flash_bwd",
        grid_spec=pltpu.PrefetchScalarGridSpec(
            num_scalar_prefetch=2, grid=(HEADS, q_tab.shape[0]),
            in_specs=[pl.BlockSpec((t, HEAD_W), qmap),
                      pl.BlockSpec((t, HEAD_W), lambda h, p, qt, kt: (kt[p], h)),
                      pl.BlockSpec((t, ROPE_BLOCK), lambda h, p, qt, kt: (kt[p], 0)),
                      pl.BlockSpec((t, V_DIM), qmap),
                      pl.BlockSpec((1, t, LANE), lambda h, p, qt, kt: (h, qt[p], 0)),
                      pl.BlockSpec((t, V_DIM), qmap)],
            out_specs=[pl.BlockSpec((lp, HEAD_W), lambda h, p, qt, kt: (0, h)),
                       pl.BlockSpec((t, HEAD_W), lambda h, p, qt, kt: (kt[p], h)),
                       pl.BlockSpec((1, t, ROPE_BLOCK), lambda h, p, qt, kt: (h, kt[p], 0))],
            scratch_shapes=[pltpu.VMEM((t, HEAD_W), F32), pltpu.VMEM((t, V_DIM), F32)]),
        out_shape=[jax.ShapeDtypeStruct((lp, Q_COLS), F32), jax.ShapeDtypeStruct((lp, Q_COLS), BF16),
                   jax.ShapeDtypeStruct((HEADS, lp, ROPE_BLOCK), F32)],
        compiler_params=_cparams(("parallel", "arbitrary")),
    )(q_tab, k_tab, q, kv, kpe, o32, lse, do)


def _ij(i, j):
    return (i, j)


def mixer_fwd(a2, w, tabs, pool_scale, g_q, g_kv, ex):
    lp, d = a2.shape
    tm = _m_tile(lp)
    tn = 512
    z = _mm_plain("mix_in", "nn", a2, w["w_in"], tm, 1280, F32)
    yp = pool_fwd(z, w["pool_w"], pool_scale)
    qn, kvn, kpe = mla_prep(z, g_q, g_kv, tabs)
    q = q_proj(qn, w["w_q_b"], tabs)
    kv = kv_proj(kvn, w["w_kv_b"])
    o32, o16, lse = flash_fwd(q, kv, kpe)
    ex.point("flash_fwd_done", o16)
    y_pool = _mm_plain("pool_out", "nn", yp, w["w_pool_o"], tm, tn, F32, after=ex.token)

    def epi(acc, ypl, gp, gm):
        return jax.nn.sigmoid(gp) * ypl + jax.nn.sigmoid(gm) * acc, acc

    y, y_mla = _mm("mla_out_gate", "nn", o16, w["w_mla_o"], tm, tn,
                   [((lp, d), BF16, (tm, tn), _ij), ((lp, d), F32, (tm, tn), _ij)], epi=epi,
                   extras=[(y_pool, (tm, tn), _ij), (z, (tm, tn), lambda i, j: (i, Z_GP // tn + j)),
                           (z, (tm, tn), lambda i, j: (i, Z_GM // tn + j))])
    m = _mm_plain("mix_out", "nn", y, w["w_out"], tm, tn, F32)
    return m, dict(z=z, yp=yp, qn=qn, kvn=kvn, kpe=kpe, q=q, kv=kv, o32=o32, o16=o16, lse=lse,
                   y_pool=y_pool, y_mla=y_mla, y=y)


def mixer_bwd(dm, a2, sv, w, tabs, pool_scale, g_q, g_kv, after=None):
    lp, d = dm.shape
    tm = _m_tile(lp)
    tn = 512
    z = sv["z"]

    def epi(acc, ypl, yml, gp, gm):
        sp, sm = jax.nn.sigmoid(gp), jax.nn.sigmoid(gm)
        return acc * sp, acc * sm, acc * ypl * (sp * (1.0 - sp)), acc * yml * (sm * (1.0 - sm))

    dyp, dym, dgp, dgm = _mm(
        "gate_bwd", "nt", dm, w["w_out"], tm, tn, [((lp, d), BF16, (tm, tn), _ij)] * 4, epi=epi,
        extras=[(sv["y_pool"], (tm, tn), _ij), (sv["y_mla"], (tm, tn), _ij),
                (z, (tm, tn), lambda i, j: (i, Z_GP // tn + j)), (z, (tm, tn), lambda i, j: (i, Z_GM // tn + j))],
        after=after)
    g = {}
    g["w_out"] = _mm_plain("dw_out", "tn", sv["y"], dm, 1024, 1024, BF16, after=after)
    g["w_pool_o"] = _mm_plain("dw_pool_o", "tn", sv["yp"], dyp, 512, 1024, BF16)
    dypre = _mm_plain("pool_out_bwd", "nt", dyp, w["w_pool_o"], tm, tn, F32)
    du, g["pool_w"], d_pool_scale = pool_bwd(dypre, z, w["pool_w"], pool_scale)
    g["w_mla_o"] = _mm_plain("dw_mla_o", "tn", sv["o16"], dym, 1024, 1024, BF16)
    do = _mm_plain("mla_out_bwd", "nt", dym, w["w_mla_o"], tm, tn, F32)
    dq, dkv, dkpe_h = flash_bwd(sv["q"], sv["kv"], sv["kpe"], sv["o32"], sv["lse"], do)
    dql = q_rope_bwd(dq, tabs)
    g["w_q_b"] = _mm_plain("dw_q_b", "tn", sv["qn"], dql, 512, 1024, BF16)
    dqn = _mm_plain("q_proj_bwd", "nt", dql, w["w_q_b"], tm, 512, F32)
    g["w_kv_b"] = _mm_plain("dw_kv_b", "tn", sv["kvn"], dkv, 512, 1024, BF16)
    dkvn = _mm_plain("kv_proj_bwd", "nt", dkv, w["w_kv_b"], tm, 512, F32)
    dcq, dckv, dkr, d_gq, d_gkv = mla_prep_bwd(dqn, dkvn, dkpe_h, z, g_q, g_kv, tabs)
    dz = jnp.concatenate([du, dcq, dckv, dgp, dgm, dkr, jnp.zeros((lp, Z_COLS - Z_KR - ROPE_BLOCK), BF16)], axis=1)
    g["w_in"] = _mm_plain("dw_in", "tn", a2, dz, 1024, 1280, BF16)
    da2 = _mm_plain("mix_in_bwd", "nt", dz, w["w_in"], tm, tn, F32)
    return da2, g, dict(pool_scale=d_pool_scale, q_a_norm=d_gq, kv_a_norm=d_gkv)


_ANY = pl.BlockSpec(memory_space=pl.ANY)
_MESH = pl.DeviceIdType.MESH


def _my_pos():
    return lax.axis_index("x"), lax.axis_index("y"), lax.axis_index("c")


LEAD = "lead"
COLS = "cols"
COLS_GU = "cols_gu"


def _col_block(layout, dev):
    return dev if layout == COLS else 2 * (dev % 4) + dev // 4


def _dev_block(ref, layout, dev, cols):
    if layout == LEAD:
        return ref.at[dev]
    return ref.at[:, pl.ds(pl.multiple_of(_col_block(layout, dev) * cols, LANE), cols)]


def _gathered_shape(shard_shape, layout):
    if layout == LEAD:
        return (N_DEV, *shard_shape)
    return (shard_shape[0], N_DEV * shard_shape[1])


def all_gather(name, shards, layouts):
    n = len(shards)

    def body(*refs):
        ins, outs = refs[:n], refs[n:2 * n]
        send_sems, recv_sems, local_sems = refs[2 * n:]
        x, y, c = _my_pos()
        me, sibling = (x, y, c), (x, y, 1 - c)
        chips = [(1 - x, y), (x, 1 - y), (1 - x, 1 - y)]

        def blk(a, px, py, pc):
            return _dev_block(outs[a], layouts[a], 4 * px + 2 * py + pc, shards[a].shape[-1])

        def copy(a, k, block, to, src=None):
            return pltpu.make_async_remote_copy(
                src_ref=blk(a, *block) if src is None else src, dst_ref=blk(a, *block),
                send_sem=send_sems.at[a, k], recv_sem=recv_sems.at[a, k], device_id=to, device_id_type=_MESH)

        mine = [pltpu.make_async_copy(ins[a], blk(a, *me), local_sems.at[a]) for a in range(n)]
        for cp in mine:
            cp.start()
        first = []
        for a in range(n):
            first.append(copy(a, 0, me, sibling, src=ins[a]))
            first += [copy(a, 1 + j, me, (*chip, c), src=ins[a]) for j, chip in enumerate(chips)]
        for cp in first:
            cp.start()
        passed = []
        for j, chip in enumerate(chips):
            for a in range(n):
                copy(a, 1 + j, (*chip, c), me).wait_recv()
                fwd = copy(a, 4 + j, (*chip, c), sibling)
                fwd.start()
                passed.append(fwd)
        for a in range(n):
            copy(a, 0, sibling, me).wait_recv()
            for j, chip in enumerate(chips):
                copy(a, 4 + j, (*chip, 1 - c), me).wait_recv()
        for cp in first + passed:
            cp.wait_send()
        for cp in mine:
            cp.wait()

    return pl.pallas_call(
        body, name=name,
        in_specs=[_ANY] * n, out_specs=[_ANY] * n,
        out_shape=[jax.ShapeDtypeStruct(_gathered_shape(s.shape, lay), s.dtype)
                   for s, lay in zip(shards, layouts, strict=True)],
        scratch_shapes=[pltpu.SemaphoreType.DMA((n, 7)), pltpu.SemaphoreType.DMA((n, 7)), pltpu.SemaphoreType.DMA((n,))],
    )(*shards)


def _shard_shape(grad, layout):
    return grad.shape[1:] if layout == LEAD else (grad.shape[0], grad.shape[1] // N_DEV)


def rs_sibling(name, grads, layouts):
    n = len(grads)

    def body(*refs):
        ins, outs = refs[:n], refs[n:2 * n]
        send_sems, recv_sems = refs[2 * n:]
        x, y, c = _my_pos()
        cps = []
        for a in range(n):
            for k in range(4):
                cp = pltpu.make_async_remote_copy(
                    src_ref=_dev_block(ins[a], layouts[a], 2 * k + (1 - c), outs[a].shape[-1]), dst_ref=outs[a].at[k],
                    send_sem=send_sems.at[a, k], recv_sem=recv_sems.at[a, k],
                    device_id=(x, y, 1 - c), device_id_type=_MESH)
                cp.start()
                cps.append(cp)
        for cp in cps:
            cp.wait()

    return pl.pallas_call(
        body, name=name,
        in_specs=[_ANY] * n, out_specs=[_ANY] * n,
        out_shape=[jax.ShapeDtypeStruct((4, *_shard_shape(g, lay)), g.dtype) for g, lay in zip(grads, layouts, strict=True)],
        scratch_shapes=[pltpu.SemaphoreType.DMA((n, 4)), pltpu.SemaphoreType.DMA((n, 4))],
    )(*grads)


def rs_chips(name, sums):
    n = len(sums)

    def body(*refs):
        ins, outs = refs[:n], refs[n:2 * n]
        send_sems, recv_sems = refs[2 * n:]
        x, y, c = _my_pos()
        chips = [(1 - x, y), (x, 1 - y), (1 - x, 1 - y)]
        cps = []
        for a in range(n):
            for j, chip in enumerate(chips):
                cp = pltpu.make_async_remote_copy(
                    src_ref=ins[a].at[2 * chip[0] + chip[1]], dst_ref=outs[a].at[j],
                    send_sem=send_sems.at[a, j], recv_sem=recv_sems.at[a, j],
                    device_id=(*chip, c), device_id_type=_MESH)
                cp.start()
                cps.append(cp)
        for cp in cps:
            cp.wait()

    return pl.pallas_call(
        body, name=name,
        in_specs=[_ANY] * n, out_specs=[_ANY] * n,
        out_shape=[jax.ShapeDtypeStruct((3, *s.shape[1:]), s.dtype) for s in sums],
        scratch_shapes=[pltpu.SemaphoreType.DMA((n, 3)), pltpu.SemaphoreType.DMA((n, 3))],
    )(*sums)


_HBM = pl.BlockSpec(memory_space=pltpu.HBM)
_SEM = pl.BlockSpec(memory_space=pltpu.SEMAPHORE)
_EFFECT = pltpu.SideEffectType.DATAFLOW_SIDE_EFFECTING


def _in_hbm(a):
    return pltpu.with_memory_space_constraint(a, pltpu.HBM)


def split_start(name, bufs, plan, n_copies, after=None):
    nb = len(bufs)
    extra = [] if after is None else [after]

    def body(*refs):
        buf_refs = refs[:nb]
        send_sems, recv_sems = refs[nb + len(extra)], refs[nb + len(extra) + 1]
        token = refs[-1]
        copies = plan(buf_refs)
        assert len(copies) == n_copies
        for k, (src, dst, to) in enumerate(copies):
            pltpu.make_async_remote_copy(src_ref=src, dst_ref=dst, send_sem=send_sems.at[k], recv_sem=recv_sems.at[k],
                                         device_id=to, device_id_type=_MESH).start()
        token[...] = jnp.zeros_like(token)

    out = pl.pallas_call(
        body, name=name,
        out_shape=(pltpu.SemaphoreType.DMA((n_copies,)), pltpu.SemaphoreType.DMA((n_copies,)),
                   *[pltpu.HBM(b.shape, b.dtype) for b in bufs], jax.ShapeDtypeStruct((8, LANE), F32)),
        in_specs=[_HBM] * nb + [_ANY] * len(extra),
        out_specs=(_SEM, _SEM, *[_HBM] * nb, pl.BlockSpec(memory_space=pltpu.VMEM)),
        input_output_aliases={i: 2 + i for i in range(nb)},
        compiler_params=pltpu.CompilerParams(has_side_effects=_EFFECT),
    )(*[_in_hbm(b) for b in bufs], *extra)
    return out[0], out[1], list(out[2:2 + nb]), out[-1]


def split_wait(name, bufs, send_sems, recv_sems, plan, after):
    nb = len(bufs)

    def body(*refs):
        buf_refs = refs[:nb]
        s_sems, r_sems = refs[nb], refs[nb + 1]
        for k, (src, dst, to) in enumerate(plan(buf_refs)):
            cp = pltpu.make_async_remote_copy(src_ref=src, dst_ref=dst, send_sem=s_sems.at[k], recv_sem=r_sems.at[k],
                                              device_id=to, device_id_type=_MESH)
            cp.wait_send()
            cp.wait_recv()

    out = pl.pallas_call(
        body, name=name,
        out_shape=tuple(pltpu.HBM(b.shape, b.dtype) for b in bufs),
        in_specs=[_HBM] * nb + [_SEM, _SEM, _ANY],
        out_specs=tuple([_HBM] * nb),
        input_output_aliases={i: i for i in range(nb)},
        compiler_params=pltpu.CompilerParams(has_side_effects=_EFFECT),
    )(*bufs, send_sems, recv_sems, after)
    return list(out)


def _ag_own_plan(shapes, layouts):
    n = len(shapes)

    def plan(refs):
        x, y, c = _my_pos()
        targets = [(x, y, 1 - c), (1 - x, y, c), (x, 1 - y, c), (1 - x, 1 - y, c)]
        out = []
        for a in range(n):
            blk = _dev_block(refs[a], layouts[a], 4 * x + 2 * y + c, shapes[a][-1])
            out += [(blk, blk, to) for to in targets]
        return out

    return plan, 4 * n


def _ag_pass_plan(shapes, layouts):
    n = len(shapes)

    def plan(refs):
        x, y, c = _my_pos()
        out = []
        for a in range(n):
            for px, py in [(1 - x, y), (x, 1 - y), (1 - x, 1 - y)]:
                blk = _dev_block(refs[a], layouts[a], 4 * px + 2 * py + c, shapes[a][-1])
                out.append((blk, blk, (x, y, 1 - c)))
        return out

    return plan, 3 * n


def _rs_sibling_plan(layouts, n):
    def plan(refs):
        x, y, c = _my_pos()
        return [(_dev_block(refs[a], layouts[a], 2 * k + (1 - c), refs[n + a].shape[-1]), refs[n + a].at[k], (x, y, 1 - c))
                for a in range(n) for k in range(4)]

    return plan, 4 * n


def _rs_chips_plan(n):
    def plan(refs):
        x, y, c = _my_pos()
        return [(refs[a].at[2 * px + py], refs[n + a].at[j], (px, py, c))
                for a in range(n) for j, (px, py) in enumerate([(1 - x, y), (x, 1 - y), (1 - x, 1 - y)])]

    return plan, 3 * n


def place_own(name, shard, layout, dtype, dev, after):
    r, c = shard.shape
    tr = _ew_rows(r, c)
    if layout == LEAD:
        o_spec = pl.BlockSpec((None, tr, c), lambda i, dev_ref: (dev_ref[0], i, 0))
    else:
        o_spec = pl.BlockSpec((tr, c), lambda i, dev_ref: (i, _col_block(layout, dev_ref[0])))
    extra = [] if after is None else [after]

    def body(dev_ref, s_ref, *rest):
        rest[-1][...] = s_ref[...].astype(dtype)

    return pl.pallas_call(
        body, name=name,
        grid_spec=pltpu.PrefetchScalarGridSpec(
            num_scalar_prefetch=1, grid=(r // tr,),
            in_specs=[pl.BlockSpec((tr, c), lambda i, dev_ref: (i, 0))] + [_ANY] * len(extra),
            out_specs=o_spec),
        out_shape=jax.ShapeDtypeStruct(_gathered_shape(shard.shape, layout), dtype),
        compiler_params=_cparams(("parallel",)),
    )(dev, shard, *extra)


def _ew_rows(r, c):
    for t in (512, 256, 128, 64, 32, 16):
        if r % t == 0 and t * c * 4 <= 768 * 1024:
            return t
    raise ValueError((r, c))


def rs_add(name, grad, layout, recv, core):
    _, r, c = recv.shape
    tr = _ew_rows(r, c)
    if layout == LEAD:
        g_spec = pl.BlockSpec((None, tr, c), lambda k, i, core_ref: (2 * k + core_ref[0], i, 0))
    else:
        g_spec = pl.BlockSpec((tr, c), lambda k, i, core_ref: (i, _col_block(layout, 2 * k + core_ref[0])))

    def body(core_ref, g_ref, r_ref, o_ref):
        o_ref[...] = (g_ref[...].astype(F32) + r_ref[...].astype(F32)).astype(BF16)

    return pl.pallas_call(
        body, name=name,
        grid_spec=pltpu.PrefetchScalarGridSpec(
            num_scalar_prefetch=1, grid=(4, r // tr),
            in_specs=[g_spec, pl.BlockSpec((None, tr, c), lambda k, i, core_ref: (k, i, 0))],
            out_specs=pl.BlockSpec((None, tr, c), lambda k, i, core_ref: (k, i, 0))),
        out_shape=jax.ShapeDtypeStruct((4, r, c), BF16),
        compiler_params=_cparams(("parallel", "parallel")),
    )(core, grad, recv)


def _adamw(w, g, m, v):
    m = ADAM_B1 * m + (1.0 - ADAM_B1) * g
    v = ADAM_B2 * v + (1.0 - ADAM_B2) * jnp.square(g)
    m_hat = m / (1.0 - ADAM_B1 ** ADAM_STEP)
    v_hat = v / (1.0 - ADAM_B2 ** ADAM_STEP)
    delta = -ADAM_LR * (m_hat / (jnp.sqrt(v_hat) + ADAM_EPS) + ADAM_WD * w)
    return delta, m, v


def adamw_shard(name, w, m, v, sums, recv, chip):
    r, c = w.shape
    tr = _ew_rows(r, c)

    def body(chip_ref, w_ref, m_ref, v_ref, s_ref, r_ref, g_ref, d_ref, mo_ref, vo_ref):
        g = s_ref[0].astype(F32)
        for j in range(3):
            g = g + r_ref[j].astype(F32)
        d, mn, vn = _adamw(w_ref[...], g, m_ref[...], v_ref[...])
        g_ref[...] = g
        d_ref[...] = d
        mo_ref[...] = mn
        vo_ref[...] = vn

    spec = pl.BlockSpec((tr, c), lambda i, chip_ref: (i, 0))
    return pl.pallas_call(
        body, name=name,
        grid_spec=pltpu.PrefetchScalarGridSpec(
            num_scalar_prefetch=1, grid=(r // tr,),
            in_specs=[spec, spec, spec,
                      pl.BlockSpec((1, tr, c), lambda i, chip_ref: (chip_ref[0], i, 0)),
                      pl.BlockSpec((3, tr, c), lambda i, chip_ref: (0, i, 0))],
            out_specs=[spec] * 4),
        out_shape=[jax.ShapeDtypeStruct((r, c), F32)] * 4,
        compiler_params=_cparams(("parallel",)),
    )(chip, w, m, v, sums, recv)


def reduce_small(gathered):
    _, r, c = gathered.shape

    def body(g_ref, o_ref):
        acc = g_ref[0]
        for k in range(1, N_DEV):
            acc = acc + g_ref[k]
        o_ref[...] = acc

    return pl.pallas_call(body, name="reduce_small", out_shape=jax.ShapeDtypeStruct((r, c), F32))(gathered)


def adamw_small(ws, gs, ms, vs):
    n = len(ws)

    def body(*refs):
        w_r, g_r, m_r, v_r = refs[:n], refs[n:2 * n], refs[2 * n:3 * n], refs[3 * n:4 * n]
        d_o, m_o, v_o = refs[4 * n:5 * n], refs[5 * n:6 * n], refs[6 * n:7 * n]
        for a in range(n):
            d, mn, vn = _adamw(w_r[a][...], g_r[a][...], m_r[a][...], v_r[a][...])
            d_o[a][...] = d
            m_o[a][...] = mn
            v_o[a][...] = vn

    shapes = [jax.ShapeDtypeStruct(w.shape, F32) for w in ws]
    out = pl.pallas_call(body, name="adamw_small", out_shape=shapes * 3)(*ws, *gs, *ms, *vs)
    return out[:n], out[n:2 * n], out[2 * n:]


WEIGHTS = ["meta_tokens", "norm_ffn1_pre", "norm_ffn1_post", "ffn1_w_gu", "ffn1_w_down", "norm_mix_pre",
           "norm_mix_post", "w_in", "pool_w", "pool_scale", "w_pool_o", "q_a_norm", "w_q_b", "kv_a_norm", "w_kv_b",
           "w_mla_o", "w_out", "norm_ffn2_pre", "norm_ffn2_post", "ffn2_w_gu", "ffn2_w_down"]
BIG = ["ffn1_w_gu", "ffn1_w_down", "w_in", "pool_w", "w_pool_o", "w_q_b", "w_kv_b", "w_mla_o", "w_out",
       "ffn2_w_gu", "ffn2_w_down"]
COL_SHARDED = ("w_in", "w_q_b")
GATHERED = {"ffn1_w_gu": COLS_GU, "ffn2_w_gu": COLS_GU, "w_pool_o": COLS, "w_kv_b": COLS}
GAINS =["norm_ffn1_pre", "norm_ffn1_post", "norm_mix_pre", "norm_mix_post", "norm_ffn2_pre", "norm_ffn2_post"]
SMALL = GAINS + ["pool_scale", "q_a_norm", "kv_a_norm"]
Z_SRC = 1024 + 512 + 512 + QK_ROPE


def _full_from_gathered(name, g):
    _, r, c = g.shape
    if name == "pool_w":
        ng = len(POOL_WINDOWS)
        return g.reshape(N_DEV, ng, r // ng, c).transpose(1, 0, 2, 3).reshape(ng, POOL_GROUP, POOL_GROUP)
    if name in COL_SHARDED:
        return g.transpose(1, 0, 2).reshape(r, N_DEV * c)
    return g.reshape(N_DEV * r, c)


def _blocks_from_full(name, dw):
    if name == "pool_w":
        ng = len(POOL_WINDOWS)
        return dw.reshape(ng, N_DEV, POOL_GROUP // N_DEV, POOL_GROUP).transpose(1, 0, 2, 3).reshape(
            N_DEV, ng * POOL_GROUP // N_DEV, POOL_GROUP)
    k, n = dw.shape
    if name in COL_SHARDED:
        return dw.reshape(k, N_DEV, n // N_DEV).transpose(1, 0, 2)
    return dw.reshape(N_DEV, k // N_DEV, n)


def _to_internal(name, w):
    if name == "w_in":
        d = w.shape[0]
        return jnp.concatenate([w[:, :Z_SRC - QK_ROPE], w[:, Z_SRC:], w[:, Z_SRC - QK_ROPE:Z_SRC],
                                jnp.zeros((d, Z_COLS - Z_KR - QK_ROPE), w.dtype)], axis=1)
    if name == "w_q_b":
        r = w.shape[0]
        w3 = w.reshape(r, HEADS, QK_NOPE + QK_ROPE)
        return jnp.pad(w3, ((0, 0), (0, 0), (0, HEAD_W - QK_NOPE - QK_ROPE))).reshape(r, Q_COLS)
    return w


def _from_internal(name, dw):
    if name == "w_in":
        return jnp.concatenate([dw[:, :Z_SRC - QK_ROPE], dw[:, Z_KR:Z_KR + QK_ROPE], dw[:, Z_SRC - QK_ROPE:Z_KR]], axis=1)
    if name == "w_q_b":
        r = dw.shape[0]
        return dw.reshape(r, HEADS, HEAD_W)[:, :, :QK_NOPE + QK_ROPE].reshape(r, HEADS * (QK_NOPE + QK_ROPE))
    return dw


def _shard2d(a):
    return a.reshape(-1, a.shape[-1])


def kernel(x, meta_tokens, norm_ffn1_pre, norm_ffn1_post, ffn1_w_gu, ffn1_w_down, norm_mix_pre, norm_mix_post, w_in, pool_w, pool_scale, w_pool_o, q_a_norm, w_q_b, kv_a_norm, w_kv_b, w_mla_o, w_out, norm_ffn2_pre, norm_ffn2_post, ffn2_w_gu, ffn2_w_down, loss_target, m_meta_tokens, m_norm_ffn1_pre, m_norm_ffn1_post, m_ffn1_w_gu, m_ffn1_w_down, m_norm_mix_pre, m_norm_mix_post, m_w_in, m_pool_w, m_pool_scale, m_w_pool_o, m_q_a_norm, m_w_q_b, m_kv_a_norm, m_w_kv_b, m_w_mla_o, m_w_out, m_norm_ffn2_pre, m_norm_ffn2_post, m_ffn2_w_gu, m_ffn2_w_down, v_meta_tokens, v_norm_ffn1_pre, v_norm_ffn1_post, v_ffn1_w_gu, v_ffn1_w_down, v_norm_mix_pre, v_norm_mix_post, v_w_in, v_pool_w, v_pool_scale, v_w_pool_o, v_q_a_norm, v_w_q_b, v_kv_a_norm, v_w_kv_b, v_w_mla_o, v_w_out, v_norm_ffn2_pre, v_norm_ffn2_post, v_ffn2_w_gu, v_ffn2_w_down):
    given = dict(locals())
    w_in_dev = {n: given[n] for n in WEIGHTS}
    m_in = {n: given["m_" + n] for n in WEIGHTS}
    v_in = {n: given["v_" + n] for n in WEIGHTS}
    xi, yi, ci = _my_pos()
    dev = 4 * xi + 2 * yi + ci
    core = jnp.reshape(ci, (1,)).astype(jnp.int32)
    chip = jnp.reshape(2 * xi + yi, (1,)).astype(jnp.int32)
    d = x.shape[-1]

    shards = {n: _shard2d(w_in_dev[n]) for n in BIG}
    ex = _Exchange(shards, meta_tokens, dev, core)
    gain = {n: given[n] for n in SMALL}
    loss_blk, grad_x, front, gsmall = local_step(x[0], loss_target[0], gain, ex)

    out_g, out_d, out_m, out_v = {}, {}, {}, {}

    def finish(grp, after):
        names, sums, from_chips = ex.finish_grads(grp, after)
        for n, s, r in zip(names, sums, from_chips, strict=True):
            shp = w_in_dev[n].shape
            res = adamw_shard("adamw_" + n, shards[n], _shard2d(m_in[n]), _shard2d(v_in[n]), s, r, chip)
            out_g[n], out_d[n], out_m[n], out_v[n] = [t.reshape(shp) for t in res]
        return res[0]

    finish("B", finish("C", grad_x))

    tail = jnp.concatenate([gsmall["pool_scale"], gsmall["q_a_norm"], gsmall["kv_a_norm"]], axis=1)
    small = jnp.concatenate([gsmall[n] for n in GAINS] + [tail, jnp.broadcast_to(loss_blk[:1, :1], (1, d)),
                                                         front[PAD_ROWS:]], axis=0)
    (small_g,) = all_gather("ag_small", [small], [LEAD])
    total = reduce_small(small_g)
    ng = len(GAINS)
    for i, n in enumerate(GAINS):
        out_g[n] = total[i:i + 1]
    o = 0
    for n in ("pool_scale", "q_a_norm", "kv_a_norm"):
        wdt = w_in_dev[n].shape[1]
        out_g[n] = total[ng:ng + 1, o:o + wdt]
        o += wdt
    loss = total[ng + 1, 0]
    mcols = meta_tokens.shape[1]
    out_g["meta_tokens"] = lax.dynamic_slice(total[ng + 2:ng + 2 + N_META], (0, dev * mcols), (N_META, mcols))
    names = ["meta_tokens"] + SMALL
    ds_, ms_, vs_ = adamw_small([w_in_dev[n] for n in names], [out_g[n] for n in names],
                                [m_in[n] for n in names], [v_in[n] for n in names])
    for n, dd, mm, vv in zip(names, ds_, ms_, vs_, strict=True):
        out_d[n], out_m[n], out_v[n] = dd, mm, vv
    finish("A", ds_[0])

    return (loss, grad_x[None], *[out_g[n] for n in WEIGHTS], *[out_d[n] for n in WEIGHTS],
            *[out_m[n] for n in WEIGHTS], *[out_v[n] for n in WEIGHTS])


GROUPS = {"A": ["ffn1_w_gu", "ffn1_w_down"],
          "B": ["w_in", "pool_w", "w_pool_o", "w_q_b", "w_kv_b", "w_mla_o", "w_out"],
          "C": ["ffn2_w_gu", "ffn2_w_down"]}


class _Exchange:
    def __init__(self, shards, meta_tokens, dev, core):
        self.shards, self.meta_tokens, self.core = shards, meta_tokens, core
        self.dev1 = jnp.reshape(dev, (1,)).astype(jnp.int32)
        self.w, self.meta_full, self.token = {}, None, None
        self._ag, self._rs = {}, {}

    def _ag_start(self, grp, after):
        names = GROUPS[grp] + (["meta_tokens"] if grp == "A" else [])
        srcs = [self.meta_tokens if n == "meta_tokens" else self.shards[n] for n in names]
        lays = [GATHERED.get(n, LEAD) for n in names]
        shapes = [a.shape for a in srcs]
        lands = [place_own(f"place_{n}", a, lay, F32 if n == "meta_tokens" else BF16, self.dev1, after)
                 for n, a, lay in zip(names, srcs, lays, strict=True)]
        plan, cnt = _ag_own_plan(shapes, lays)
        ss, rs, bufs, self.token = split_start(f"ag{grp}_own_start", lands, plan, cnt)
        self._ag[grp] = dict(names=names, lays=lays, shapes=shapes, own=(ss, rs, bufs, plan))

    def _ag_pass(self, grp, after):
        st = self._ag[grp]
        ss, rs, bufs, plan = st["own"]
        lands = split_wait(f"ag{grp}_own_wait", bufs, ss, rs, plan, after)
        plan, cnt = _ag_pass_plan(st["shapes"], st["lays"])
        ss, rs, lands, self.token = split_start(f"ag{grp}_pass_start", lands, plan, cnt)
        st["pass"] = (ss, rs, lands, plan)

    def _ag_finish(self, grp, after):
        st = self._ag[grp]
        ss, rs, lands, plan = st["pass"]
        lands = split_wait(f"ag{grp}_pass_wait", lands, ss, rs, plan, after)
        for n, g, lay in zip(st["names"], lands, st["lays"]):
            if n == "meta_tokens":
                self.meta_full = g.transpose(1, 0, 2).reshape(N_META, N_DEV * g.shape[-1])
            else:
                self.w[n] = g if lay != LEAD else _to_internal(n, _full_from_gathered(n, g))

    def grads(self, grp, gbig):
        after = None
        names = GROUPS[grp]
        lays = [GATHERED.get(n, LEAD) for n in names]
        grads = [gbig[n] if lay != LEAD else _blocks_from_full(n, _from_internal(n, gbig[n]).astype(BF16))
                 for n, lay in zip(names, lays, strict=True)]
        lands = [lax.empty((4, *_shard_shape(g, lay)), BF16) for g, lay in zip(grads, lays, strict=True)]
        plan, cnt = _rs_sibling_plan(lays, len(names))
        ss, rs, bufs, self.token = split_start(f"rs{grp}_sibling_start", grads + lands, plan, cnt, after)
        self._rs[grp] = dict(names=names, lays=lays, sib=(ss, rs, bufs, plan))
        return self.token

    def _rs_mid(self, grp, after):
        st = self._rs[grp]
        n = len(st["names"])
        ss, rs, bufs, plan = st["sib"]
        bufs = split_wait(f"rs{grp}_sibling_wait", bufs, ss, rs, plan, after)
        sums = [rs_add(f"rs_add_{name}", g, lay, r, self.core)
                for name, g, lay, r in zip(st["names"], bufs[:n], st["lays"], bufs[n:], strict=True)]
        lands = [lax.empty((3, *s.shape[1:]), BF16) for s in sums]
        plan, cnt = _rs_chips_plan(n)
        ss, rs, bufs, self.token = split_start(f"rs{grp}_chips_start", sums + lands, plan, cnt)
        st["chips"] = (ss, rs, bufs, plan)

    def finish_grads(self, grp, after):
        st = self._rs[grp]
        n = len(st["names"])
        ss, rs, bufs, plan = st["chips"]
        bufs = split_wait(f"rs{grp}_chips_wait", bufs, ss, rs, plan, after)
        return st["names"], bufs[:n], bufs[n:]

    def point(self, name, after=None):
        if name == "start":
            self._ag_start("A", None)
            self._ag_pass("A", self.token)
            self._ag_start("B", self.token)
            self._ag_finish("A", self.token)
        elif name == "ffn1_fwd_done":
            self._ag_pass("B", after)
            self._ag_start("C", self.token)
        elif name == "mix_pre_done":
            self._ag_finish("B", after)
        elif name == "flash_fwd_done":
            self._ag_pass("C", after)
        elif name == "ffn2_pre_done":
            self._ag_finish("C", after)
        elif name in ("rsC_mid", "rsB_mid", "rsA_mid"):
            self._rs_mid(name[2], after)


def local_step(x, target, gain, ex):
    d = x.shape[-1]
    ex.point("start")
    w = ex.w
    h0 = jnp.concatenate([jnp.zeros((PAD_ROWS, d), F32), ex.meta_full, x], axis=0)
    lp = h0.shape[0]
    tabs = rope_tables(lp)
    a1 = prenorm(h0, gain["norm_ffn1_pre"], after=ex.token)
    gu1, s1, f1 = ffn_fwd("ffn1", a1, w["ffn1_w_gu"], w["ffn1_w_down"])
    ex.point("ffn1_fwd_done", f1)
    h1, a2 = post_pre("post_pre1", f1, h0, gain["norm_ffn1_post"], 0.5, gain["norm_mix_pre"], after=ex.token)
    ex.point("mix_pre_done", a2)
    mix, sv = mixer_fwd(a2, w, tabs, gain["pool_scale"], gain["q_a_norm"], gain["kv_a_norm"], ex)
    h2, a3 = post_pre("post_pre2", mix, h1, gain["norm_mix_post"], 1.0, gain["norm_ffn2_pre"])
    ex.point("ffn2_pre_done", a3)
    gu2, s2, f2 = ffn_fwd("ffn2", a3, w["ffn2_w_gu"], w["ffn2_w_down"])
    dh3, loss_blk = post_loss(f2, h2, gain["norm_ffn2_post"], 0.5, target)

    gsmall = {}
    df2, gsmall["norm_ffn2_post"] = post_bwd(dh3, f2, gain["norm_ffn2_post"], 0.5)
    da3 = ffn_bwd("ffn2", df2, a3, gu2, s2, w["ffn2_w_gu"], w["ffn2_w_down"], lambda g: ex.grads("C", g))
    dh2, dmix, gsmall["norm_ffn2_pre"], gsmall["norm_mix_post"] = pre_post_bwd(
        "pre_post_bwd2", da3, h2, gain["norm_ffn2_pre"], dh3, mix, gain["norm_mix_post"], 1.0)
    ex.point("rsC_mid", dmix)
    da2, gmix, gmix_small = mixer_bwd(dmix, a2, sv, w, tabs, gain["pool_scale"], gain["q_a_norm"], gain["kv_a_norm"],
                                      after=ex.token)
    gsmall.update(gmix_small)
    dh1, df1, gsmall["norm_mix_pre"], gsmall["norm_ffn1_post"] = pre_post_bwd(
        "pre_post_bwd1", da2, h1, gain["norm_mix_pre"], dh2, f1, gain["norm_ffn1_post"], 0.5,
        after=ex.grads("B", gmix))
    ex.point("rsB_mid", df1)
    da1 = ffn_bwd("ffn1", df1, a1, gu1, s1, w["ffn1_w_gu"], w["ffn1_w_down"], lambda g: ex.grads("A", g),
                  after=ex.token)
    ex.point("rsA_mid", da1)
    grad_x, front, gsmall["norm_ffn1_pre"] = pre_bwd_first(da1, h0, gain["norm_ffn1_pre"], dh1, after=ex.token)
    return loss_blk, grad_x, front, gsmall
```

```python
import functools

import jax
import jax.numpy as jnp
import numpy as np
from jax import lax
from jax.experimental import pallas as pl
from jax.experimental.pallas import tpu as pltpu

F32 = jnp.float32
BF16 = jnp.bfloat16

N_META = 16
POOL_WINDOWS = (2, 4, 8, 16)
POOL_GROUP = 256
HEADS = 16
QK_NOPE = 128
QK_ROPE = 64
V_DIM = 128
ROPE_THETA = 10000.0
SOFTMAX_SCALE = (QK_NOPE + QK_ROPE) ** -0.5
EPS = 1e-6
ADAM_LR = 0.001
ADAM_B1 = 0.9
ADAM_B2 = 0.999
ADAM_EPS = 1e-08
ADAM_WD = 0.01
ADAM_STEP = 10

LANE = 128
FRONT = 128
PAD_ROWS = FRONT - N_META
HEAD_W = 256
GU_TILE = 1408
VMEM_LIMIT = 56 * 1024 * 1024
MESH_AXES = ("x", "y", "c")
N_DEV = 8


def _pick(n, cands):
    for c in cands:
        if n % c == 0:
            return c
    raise ValueError(f"no tile for {n} in {cands}")


def _cparams(sem=None):
    kw = dict(vmem_limit_bytes=VMEM_LIMIT)
    if sem is not None:
        kw["dimension_semantics"] = sem
    return pltpu.CompilerParams(**kw)


_DIMS = {"nn": (((1,), (0,)), ((), ())), "nt": (((1,), (1,)), ((), ())), "tn": (((0,), (0,)), ((), ()))}


def _behind(after):
    return [] if after is None else [after]


def _behind_specs(after):
    return [] if after is None else [pl.BlockSpec(memory_space=pl.ANY)]


def _mm(name, form, a, b, tm, tn, outs, epi=None, extras=(), n_outer=False, after=None):
    if form == "tn":
        k, m = a.shape
        n = b.shape[1]
        a_blk, a_map = (k, tm), lambda i, j: (0, i)
        b_blk, b_map = (k, tn), lambda i, j: (0, j)
    elif form == "nn":
        m, k = a.shape
        n = b.shape[1]
        a_blk, a_map = (tm, k), lambda i, j: (i, 0)
        b_blk, b_map = (k, tn), lambda i, j: (0, j)
    else:
        m, k = a.shape
        n = b.shape[0]
        a_blk, a_map = (tm, k), lambda i, j: (i, 0)
        b_blk, b_map = (tn, k), lambda i, j: (j, 0)
    assert m % tm == 0 and n % tn == 0, (name, m, n, tm, tn)
    n_ex = len(extras)
    dn = _DIMS[form]
    if n_outer:
        grid = (n // tn, m // tm)

        def spec(blk, im):
            return pl.BlockSpec(blk, lambda gj, gi: im(gi, gj))
    else:
        grid = (m // tm, n // tn)
        spec = pl.BlockSpec

    behind = _behind(after)

    def body(a_ref, b_ref, *rest):
        ex, out_refs = rest[:n_ex], rest[n_ex + len(behind):]
        acc = lax.dot_general(a_ref[...].astype(BF16), b_ref[...].astype(BF16), dn, preferred_element_type=F32)
        res = epi(acc, *[e[...] for e in ex]) if epi is not None else (acc,)
        for r, o in zip(res, out_refs, strict=True):
            o[...] = r.astype(o.dtype)

    return pl.pallas_call(
        body,
        name=name,
        grid=grid,
        in_specs=[spec(a_blk, a_map), spec(b_blk, b_map)] + [spec(blk, im) for _, blk, im in extras] + _behind_specs(after),
        out_specs=[spec(blk, im) for _, _, blk, im in outs],
        out_shape=[jax.ShapeDtypeStruct(s, d) for s, d, _, _ in outs],
        compiler_params=_cparams(("parallel", "parallel")),
    )(a, b, *[e for e, _, _ in extras], *behind)


def _mm_plain(name, form, a, b, tm, tn, out_dtype, after=None):
    m = a.shape[1] if form == "tn" else a.shape[0]
    n = b.shape[0] if form == "nt" else b.shape[1]
    return _mm(name, form, a, b, tm, tn, [((m, n), out_dtype, (tm, tn), lambda i, j: (i, j))], after=after)[0]


def _rstd(x):
    return lax.rsqrt(jnp.mean(x * x, axis=-1, keepdims=True) + EPS)


def _norm_bwd(dy, x, gain):
    r = _rstd(x)
    dyg = dy * gain
    dx = r * (dyg - x * (r * r) * jnp.mean(dyg * x, axis=-1, keepdims=True))
    dgain = jnp.sum(dy * x * r, axis=0, keepdims=True)
    return dx, dgain


def _row_spec(tr, cols, col_block=0):
    return pl.BlockSpec((tr, cols), lambda i: (i, col_block))


def _vec_spec(cols, col_block=0):
    return pl.BlockSpec((1, cols), lambda i: (0, col_block))


def _row_tile(lp):
    return _pick(lp, (128,))


def _skip_behind(after, body):
    return body if after is None else (lambda _unread, *refs: body(*refs))


def prenorm(h, gain, after=None):
    lp, d = h.shape
    tr = _row_tile(lp)

    def body(h_ref, g_ref, a_ref):
        x = h_ref[...]
        a_ref[...] = (x * _rstd(x) * g_ref[...]).astype(BF16)

    return pl.pallas_call(
        _skip_behind(after, body), name="prenorm", grid=(lp // tr,),
        in_specs=_behind_specs(after) + [_row_spec(tr, d), _vec_spec(d)], out_specs=_row_spec(tr, d),
        out_shape=jax.ShapeDtypeStruct((lp, d), BF16), compiler_params=_cparams(("parallel",)),
    )(*_behind(after), h, gain)


def post_pre(name, f, h_in, g_post, coef, g_next, after=None):
    lp, d = f.shape
    tr = _row_tile(lp)

    def body(f_ref, h_ref, gp_ref, gn_ref, ho_ref, a_ref):
        fv = f_ref[...]
        ho = h_ref[...] + coef * (fv * _rstd(fv) * gp_ref[...])
        ho_ref[...] = ho
        a_ref[...] = (ho * _rstd(ho) * gn_ref[...]).astype(BF16)

    return pl.pallas_call(
        _skip_behind(after, body), name=name, grid=(lp // tr,),
        in_specs=_behind_specs(after) + [_row_spec(tr, d), _row_spec(tr, d), _vec_spec(d), _vec_spec(d)],
        out_specs=[_row_spec(tr, d), _row_spec(tr, d)],
        out_shape=[jax.ShapeDtypeStruct((lp, d), F32), jax.ShapeDtypeStruct((lp, d), BF16)],
        compiler_params=_cparams(("parallel",)),
    )(*_behind(after), f, h_in, g_post, g_next)


def post_loss(f, h_in, g_post, coef, target):
    lp, d = f.shape
    tr = _row_tile(lp)
    front_tiles = FRONT // tr

    def body(f_ref, h_ref, gp_ref, t_ref, dh_ref, loss_ref):
        i = pl.program_id(0)

        @pl.when(i == 0)
        def _():
            loss_ref[...] = jnp.zeros_like(loss_ref)

        @pl.when(i < front_tiles)
        def _():
            dh_ref[...] = jnp.zeros_like(dh_ref)

        @pl.when(i >= front_tiles)
        def _():
            fv = f_ref[...]
            ho = h_ref[...] + coef * (fv * _rstd(fv) * gp_ref[...])
            err = ho - t_ref[...]
            dh_ref[...] = err / d
            tok = jnp.mean(err * err, axis=-1, keepdims=True)
            loss_ref[...] += 0.5 * jnp.sum(tok)

    return pl.pallas_call(
        body, name="post_loss", grid=(lp // tr,),
        in_specs=[_row_spec(tr, d), _row_spec(tr, d), _vec_spec(d),
                  pl.BlockSpec((tr, d), lambda i: (jnp.maximum(i - front_tiles, 0), 0))],
        out_specs=[_row_spec(tr, d), pl.BlockSpec((8, LANE), lambda i: (0, 0))],
        out_shape=[jax.ShapeDtypeStruct((lp, d), F32), jax.ShapeDtypeStruct((8, LANE), F32)],
        compiler_params=_cparams(("arbitrary",)),
    )(f, h_in, g_post, target)


def post_bwd(dh_out, f, g_post, coef):
    lp, d = f.shape
    tr = _row_tile(lp)

    def body(dh_ref, f_ref, gp_ref, df_ref, dg_ref):
        @pl.when(pl.program_id(0) == 0)
        def _():
            dg_ref[...] = jnp.zeros_like(dg_ref)

        df, dg = _norm_bwd(coef * dh_ref[...], f_ref[...], gp_ref[...])
        df_ref[...] = df.astype(BF16)
        dg_ref[...] += dg

    return pl.pallas_call(
        body, name="post_bwd", grid=(lp // tr,),
        in_specs=[_row_spec(tr, d), _row_spec(tr, d), _vec_spec(d)],
        out_specs=[_row_spec(tr, d), _vec_spec(d)],
        out_shape=[jax.ShapeDtypeStruct((lp, d), BF16), jax.ShapeDtypeStruct((1, d), F32)],
        compiler_params=_cparams(("arbitrary",)),
    )(dh_out, f, g_post)


def pre_post_bwd(name, da, h_mid, g_pre, dh_out, f_prev, g_post_prev, coef_prev, after=None):
    lp, d = da.shape
    tr = _row_tile(lp)

    def body(da_ref, h_ref, gpre_ref, dho_ref, f_ref, gpost_ref, dh_ref, df_ref, dgpre_ref, dgpost_ref):
        @pl.when(pl.program_id(0) == 0)
        def _():
            dgpre_ref[...] = jnp.zeros_like(dgpre_ref)
            dgpost_ref[...] = jnp.zeros_like(dgpost_ref)

        dx, dgpre = _norm_bwd(da_ref[...], h_ref[...], gpre_ref[...])
        dh = dho_ref[...] + dx
        dh_ref[...] = dh
        dgpre_ref[...] += dgpre
        df, dgpost = _norm_bwd(coef_prev * dh, f_ref[...], gpost_ref[...])
        df_ref[...] = df.astype(BF16)
        dgpost_ref[...] += dgpost

    return pl.pallas_call(
        _skip_behind(after, body), name=name, grid=(lp // tr,),
        in_specs=_behind_specs(after) + [_row_spec(tr, d), _row_spec(tr, d), _vec_spec(d), _row_spec(tr, d),
                                         _row_spec(tr, d), _vec_spec(d)],
        out_specs=[_row_spec(tr, d), _row_spec(tr, d), _vec_spec(d), _vec_spec(d)],
        out_shape=[jax.ShapeDtypeStruct((lp, d), F32), jax.ShapeDtypeStruct((lp, d), BF16),
                   jax.ShapeDtypeStruct((1, d), F32), jax.ShapeDtypeStruct((1, d), F32)],
        compiler_params=_cparams(("arbitrary",)),
    )(*_behind(after), da, h_mid, g_pre, dh_out, f_prev, g_post_prev)


def pre_bwd_first(da, h0, g_pre, dh_out, after=None):
    lp, d = da.shape
    tr = _row_tile(lp)
    front_tiles = FRONT // tr
    assert front_tiles == 1

    def body(da_ref, h_ref, gpre_ref, dho_ref, gx_ref, front_ref, dgpre_ref):
        i = pl.program_id(0)

        @pl.when(i == 0)
        def _():
            dgpre_ref[...] = jnp.zeros_like(dgpre_ref)

        dx, dgpre = _norm_bwd(da_ref[...], h_ref[...], gpre_ref[...])
        dh = dho_ref[...] + dx
        dgpre_ref[...] += dgpre
        gx_ref[...] = dh

        @pl.when(i == 0)
        def _():
            front_ref[...] = dh

    return pl.pallas_call(
        _skip_behind(after, body), name="pre_bwd_first", grid=(lp // tr,),
        in_specs=_behind_specs(after) + [_row_spec(tr, d), _row_spec(tr, d), _vec_spec(d), _row_spec(tr, d)],
        out_specs=[pl.BlockSpec((tr, d), lambda i: (jnp.maximum(i - front_tiles, 0), 0)),
                   pl.BlockSpec((tr, d), lambda i: (0, 0)), _vec_spec(d)],
        out_shape=[jax.ShapeDtypeStruct((lp - FRONT, d), F32), jax.ShapeDtypeStruct((tr, d), F32),
                   jax.ShapeDtypeStruct((1, d), F32)],
        compiler_params=_cparams(("arbitrary",)),
    )(*_behind(after), da, h0, g_pre, dh_out)


def _m_tile(lp):
    return _pick(lp, (1056, 512, 256, 128))


def ffn_gu(tag, a, wgu_p):
    lp, d = a.shape
    f2 = wgu_p.shape[1]

    def epi(acc):
        g, u = acc[:, :GU_TILE], acc[:, GU_TILE:]
        return acc, g * jax.nn.sigmoid(g) * u

    tg = _pick(lp, (384, 256, 128))
    return _mm(tag + "_gu", "nn", a, wgu_p, tg, 2 * GU_TILE,
               [((lp, f2), F32, (tg, 2 * GU_TILE), lambda i, j: (i, j)),
                ((lp, f2 // 2), BF16, (tg, GU_TILE), lambda i, j: (i, j))], epi=epi, n_outer=True)


def ffn_down(tag, s, wd):
    return _mm_plain(tag + "_down", "nn", s, wd, _m_tile(s.shape[0]), 512, F32)


def ffn_bwd(tag, df, a, gu, s, wgu_p, wd, on_dwd, on_dwgu, after=None):
    lp, d = df.shape
    f2 = wgu_p.shape[1]
    tm = _m_tile(lp)

    def epi(acc, gu_t):
        g, u = gu_t[:, :GU_TILE], gu_t[:, GU_TILE:]
        sig = jax.nn.sigmoid(g)
        dg = acc * u * (sig * (1.0 + g * (1.0 - sig)))
        du = acc * (g * sig)
        return (jnp.concatenate([dg, du], axis=1),)

    ts = _pick(lp, (528, 256, 128))
    dgu = _mm(tag + "_ds", "nt", df, wd, ts, GU_TILE,
              [((lp, f2), BF16, (ts, 2 * GU_TILE), lambda i, j: (i, j))], epi=epi,
              extras=[(gu, (ts, 2 * GU_TILE), lambda i, j: (i, j))], n_outer=True, after=after)[0]
    dwd = _mm_plain(tag + "_dwd", "tn", s, df, 512, _pick(d, (1024,)), BF16, after=after)
    dwgu = _mm_plain(tag + "_dwgu", "tn", a, dgu, _pick(d, (1024,)), 1024, BF16, after=on_dwd(dwd))
    return _mm_plain(tag + "_da", "nt", dgu, wgu_p, _pick(lp, (528, 256, 128)), 256, F32, after=on_dwgu(dwgu, dwd))


Z_U, Z_CQ, Z_CKV, Z_GP, Z_GM, Z_KR, Z_COLS = 0, 1024, 1536, 2048, 4096, 6144, 6400
POOL_W = POOL_GROUP * len(POOL_WINDOWS)
HALO = 16


def _pool_counts(lp, w):
    pos = lax.broadcasted_iota(jnp.int32, (lp, 1), 0) - PAD_ROWS
    return jnp.clip(pos + 1, 1, w).astype(F32)


def _pool_diff(u_ref, pad_ref, lp, w):
    pad_ref[pl.ds(0, HALO), :] = jnp.zeros((HALO, POOL_GROUP), F32)
    pad_ref[pl.ds(HALO, lp), :] = u_ref[...]
    acc = pad_ref[pl.ds(HALO, lp), :]
    for s in range(1, w):
        acc = acc + pad_ref[pl.ds(HALO - s, lp), :]
    return acc / _pool_counts(lp, w) - u_ref[...]


def pool_fwd(z, pool_w, pool_scale):
    lp = z.shape[0]
    ng = len(POOL_WINDOWS)

    def body(u_ref, w_ref, sc_ref, o_ref, pad_ref):
        for g, w in enumerate(POOL_WINDOWS):
            @pl.when(pl.program_id(0) == g)
            def _(w=w):
                dd = _pool_diff(u_ref, pad_ref, lp, w)
                y = jnp.dot(dd.astype(BF16), w_ref[0], preferred_element_type=F32)
                o_ref[...] = (y * sc_ref[...]).astype(BF16)

    return pl.pallas_call(
        body, name="pool_fwd", grid=(ng,),
        in_specs=[pl.BlockSpec((lp, POOL_GROUP), lambda g: (0, g)),
                  pl.BlockSpec((1, POOL_GROUP, POOL_GROUP), lambda g: (g, 0, 0)),
                  pl.BlockSpec((1, POOL_GROUP), lambda g: (0, g))],
        out_specs=pl.BlockSpec((lp, POOL_GROUP), lambda g: (0, g)),
        out_shape=jax.ShapeDtypeStruct((lp, POOL_W), BF16),
        scratch_shapes=[pltpu.VMEM((lp + HALO, POOL_GROUP), F32)],
        compiler_params=_cparams(("parallel",)),
    )(z, pool_w, pool_scale)


def pool_bwd(dyp, z, pool_w, pool_scale):
    lp = z.shape[0]
    ng = len(POOL_WINDOWS)

    def body(dy_ref, u_ref, w_ref, sc_ref, du_ref, dw_ref, dsc_ref, pad_ref):
        for g, w in enumerate(POOL_WINDOWS):
            @pl.when(pl.program_id(0) == g)
            def _(w=w):
                dd = _pool_diff(u_ref, pad_ref, lp, w).astype(BF16)
                wg = w_ref[0]
                ypre = jnp.dot(dd, wg, preferred_element_type=F32)
                dy = dy_ref[...]
                dsc_ref[...] = jnp.sum(dy * ypre, axis=0, keepdims=True)
                dypre = (dy * sc_ref[...]).astype(BF16)
                dw_ref[0] = lax.dot_general(dd, dypre, _DIMS["tn"], preferred_element_type=F32)
                ddd = lax.dot_general(dypre, wg, _DIMS["nt"], preferred_element_type=F32)
                pad_ref[pl.ds(0, lp), :] = ddd / _pool_counts(lp, w)
                pad_ref[pl.ds(lp, HALO), :] = jnp.zeros((HALO, POOL_GROUP), F32)
                acc = -ddd
                for s in range(w):
                    acc = acc + pad_ref[pl.ds(s, lp), :]
                du_ref[...] = acc.astype(BF16)

    return pl.pallas_call(
        body, name="pool_bwd", grid=(ng,),
        in_specs=[pl.BlockSpec((lp, POOL_GROUP), lambda g: (0, g)),
                  pl.BlockSpec((lp, POOL_GROUP), lambda g: (0, g)),
                  pl.BlockSpec((1, POOL_GROUP, POOL_GROUP), lambda g: (g, 0, 0)),
                  pl.BlockSpec((1, POOL_GROUP), lambda g: (0, g))],
        out_specs=[pl.BlockSpec((lp, POOL_GROUP), lambda g: (0, g)),
                   pl.BlockSpec((1, POOL_GROUP, POOL_GROUP), lambda g: (g, 0, 0)),
                   pl.BlockSpec((1, POOL_GROUP), lambda g: (0, g))],
        out_shape=[jax.ShapeDtypeStruct((lp, POOL_W), BF16),
                   jax.ShapeDtypeStruct((ng, POOL_GROUP, POOL_GROUP), F32),
                   jax.ShapeDtypeStruct((1, POOL_W), F32)],
        scratch_shapes=[pltpu.VMEM((lp + HALO, POOL_GROUP), F32)],
        compiler_params=_cparams(("parallel",)),
    )(dyp, z, pool_w, pool_scale)


Q_COLS = HEADS * HEAD_W
ROPE_BLOCK = LANE


def rope_tables(lp):
    pos = jnp.maximum(jnp.arange(lp, dtype=F32) - PAD_ROWS, 0.0)
    inv = ROPE_THETA ** (-jnp.arange(0, QK_ROPE, 2, dtype=F32) / QK_ROPE)
    ang = pos[:, None] * inv[None, :]
    cos, sin, zero = jnp.cos(ang), jnp.sin(ang), jnp.zeros_like(ang)
    return jnp.stack([jnp.concatenate([cos, cos, zero, zero], axis=1),
                      jnp.concatenate([-sin, zero, zero, zero], axis=1),
                      jnp.concatenate([zero, sin, zero, zero], axis=1)])


def _rope(x, tabs):
    return x * tabs[0] + pltpu.roll(x, 96, 1) * tabs[1] + pltpu.roll(x, 32, 1) * tabs[2]


def _rope_bwd(g, tabs):
    return g * tabs[0] + pltpu.roll(g * tabs[1], 32, 1) + pltpu.roll(g * tabs[2], 96, 1)


def _tab_spec(tr):
    return pl.BlockSpec((3, tr, ROPE_BLOCK), lambda i: (0, i, 0))


def mla_prep(z, g_q, g_kv, tabs):
    lp = z.shape[0]
    tr = _row_tile(lp)
    r = g_q.shape[1]

    def body(cq_ref, ckv_ref, kr_ref, gq_ref, gkv_ref, tab_ref, qn_ref, kvn_ref, kpe_ref):
        cq, ckv = cq_ref[...], ckv_ref[...]
        qn_ref[...] = (cq * _rstd(cq) * gq_ref[...]).astype(BF16)
        kvn_ref[...] = (ckv * _rstd(ckv) * gkv_ref[...]).astype(BF16)
        kpe_ref[...] = _rope(kr_ref[...], tab_ref[...]).astype(BF16)

    return pl.pallas_call(
        body, name="mla_prep", grid=(lp // tr,),
        in_specs=[_row_spec(tr, r, Z_CQ // r), _row_spec(tr, r, Z_CKV // r), _row_spec(tr, ROPE_BLOCK, Z_KR // ROPE_BLOCK),
                  _vec_spec(r), _vec_spec(r), _tab_spec(tr)],
        out_specs=[_row_spec(tr, r), _row_spec(tr, r), _row_spec(tr, ROPE_BLOCK)],
        out_shape=[jax.ShapeDtypeStruct((lp, r), BF16), jax.ShapeDtypeStruct((lp, r), BF16),
                   jax.ShapeDtypeStruct((lp, ROPE_BLOCK), BF16)],
        compiler_params=_cparams(("parallel",)),
    )(z, z, z, g_q, g_kv, tabs)


def q_proj(qn, wq_p, tabs):
    lp = qn.shape[0]
    tm = _m_tile(lp)
    tn = 4 * HEAD_W

    def epi(acc, tab):
        parts = []
        for t in range(tn // HEAD_W):
            parts.append(acc[:, t * HEAD_W:t * HEAD_W + QK_NOPE])
            parts.append(_rope(acc[:, t * HEAD_W + QK_NOPE:(t + 1) * HEAD_W], tab))
        return (jnp.concatenate(parts, axis=1),)

    return _mm("q_proj", "nn", qn, wq_p, tm, tn, [((lp, Q_COLS), BF16, (tm, tn), lambda i, j: (i, j))], epi=epi,
               extras=[(tabs, (3, tm, ROPE_BLOCK), lambda i, j: (0, i, 0))])[0]


def kv_proj(kvn, wkv):
    return _mm_plain("kv_proj", "nn", kvn, wkv, _m_tile(kvn.shape[0]), 1024, BF16)


def q_rope_bwd(dq, tabs):
    lp = dq.shape[0]
    tr = _row_tile(lp)

    def body(dq_ref, tab_ref, o_ref):
        tab = tab_ref[...]
        for h in range(HEADS):
            o_ref[:, h * HEAD_W:h * HEAD_W + QK_NOPE] = dq_ref[:, h * HEAD_W:h * HEAD_W + QK_NOPE].astype(BF16)
            o_ref[:, h * HEAD_W + QK_NOPE:(h + 1) * HEAD_W] = _rope_bwd(
                dq_ref[:, h * HEAD_W + QK_NOPE:(h + 1) * HEAD_W], tab).astype(BF16)

    return pl.pallas_call(
        body, name="q_rope_bwd", grid=(lp // tr,),
        in_specs=[_row_spec(tr, Q_COLS), _tab_spec(tr)], out_specs=_row_spec(tr, Q_COLS),
        out_shape=jax.ShapeDtypeStruct((lp, Q_COLS), BF16), compiler_params=_cparams(("parallel",)),
    )(dq, tabs)


def mla_prep_bwd(dqn, dkvn, dkpe_h, z, g_q, g_kv, tabs):
    lp = z.shape[0]
    tr = _row_tile(lp)
    r = g_q.shape[1]

    def body(dqn_ref, dkvn_ref, dkpe_ref, cq_ref, ckv_ref, gq_ref, gkv_ref, tab_ref,
             dcq_ref, dckv_ref, dkr_ref, dgq_ref, dgkv_ref):
        @pl.when(pl.program_id(0) == 0)
        def _():
            dgq_ref[...] = jnp.zeros_like(dgq_ref)
            dgkv_ref[...] = jnp.zeros_like(dgkv_ref)

        dcq, dgq = _norm_bwd(dqn_ref[...], cq_ref[...], gq_ref[...])
        dckv, dgkv = _norm_bwd(dkvn_ref[...], ckv_ref[...], gkv_ref[...])
        dcq_ref[...] = dcq.astype(BF16)
        dckv_ref[...] = dckv.astype(BF16)
        dgq_ref[...] += dgq
        dgkv_ref[...] += dgkv
        dkpe = dkpe_ref[0]
        for h in range(1, HEADS):
            dkpe = dkpe + dkpe_ref[h]
        dkr_ref[...] = _rope_bwd(dkpe, tab_ref[...]).astype(BF16)

    return pl.pallas_call(
        body, name="mla_prep_bwd", grid=(lp // tr,),
        in_specs=[_row_spec(tr, r), _row_spec(tr, r), pl.BlockSpec((HEADS, tr, ROPE_BLOCK), lambda i: (0, i, 0)),
                  _row_spec(tr, r, Z_CQ // r), _row_spec(tr, r, Z_CKV // r), _vec_spec(r), _vec_spec(r), _tab_spec(tr)],
        out_specs=[_row_spec(tr, r), _row_spec(tr, r), _row_spec(tr, ROPE_BLOCK), _vec_spec(r), _vec_spec(r)],
        out_shape=[jax.ShapeDtypeStruct((lp, r), BF16), jax.ShapeDtypeStruct((lp, r), BF16),
                   jax.ShapeDtypeStruct((lp, ROPE_BLOCK), BF16),
                   jax.ShapeDtypeStruct((1, r), F32), jax.ShapeDtypeStruct((1, r), F32)],
        compiler_params=_cparams(("arbitrary",)),
    )(dqn, dkvn, dkpe_h, z, z, g_q, g_kv, tabs)


def _attn_tile(lp):
    return _pick(lp, (528, 128))


def _scores(q, kcat, q_tile, k_tile, t, masked):
    s = lax.dot_general(q, kcat, _DIMS["nt"], preferred_element_type=F32) * SOFTMAX_SCALE
    if not masked:
        return s
    qpos = q_tile * t + lax.broadcasted_iota(jnp.int32, (t, t), 0)
    kpos = k_tile * t + lax.broadcasted_iota(jnp.int32, (t, t), 1)
    return jnp.where((kpos <= qpos) & (kpos >= PAD_ROWS), s, jnp.float32(-1e30))


def _causal_pairs(nt, k_major):
    if k_major:
        pairs = [(qi, ki) for ki in range(nt) for qi in range(ki, nt)]
    else:
        pairs = [(qi, ki) for qi in range(nt) for ki in range(qi + 1)]
    return (jnp.asarray([p[0] for p in pairs], jnp.int32), jnp.asarray([p[1] for p in pairs], jnp.int32))


def _on_masked_or_not(q_tile, k_tile, fn):
    needs_mask = (q_tile == k_tile) | (k_tile == 0)

    @pl.when(needs_mask)
    def _():
        fn(True)

    @pl.when(jnp.logical_not(needs_mask))
    def _():
        fn(False)


def flash_fwd(q, kv, kpe):
    lp = q.shape[0]
    t = _attn_tile(lp)
    q_tab, k_tab = _causal_pairs(lp // t, k_major=False)

    def body(q_tab_ref, k_tab_ref, q_ref, kv_ref, kpe_ref, o32_ref, o16_ref, lse_ref, m_sc, l_sc, acc_sc):
        pair = pl.program_id(1)
        qi, ki = q_tab_ref[pair], k_tab_ref[pair]

        @pl.when(ki == 0)
        def _():
            m_sc[...] = jnp.full_like(m_sc, -jnp.inf)
            l_sc[...] = jnp.zeros_like(l_sc)
            acc_sc[...] = jnp.zeros_like(acc_sc)

        def step(masked):
            kvt = kv_ref[...]
            kcat = jnp.concatenate([kvt[:, :QK_NOPE], kpe_ref[...]], axis=1)
            s = _scores(q_ref[...], kcat, qi, ki, t, masked)
            m_prev = m_sc[...]
            m_new = jnp.maximum(m_prev, jnp.max(s, axis=1, keepdims=True))
            alpha = jnp.exp(m_prev - m_new)
            p = jnp.exp(s - m_new[:, :1])
            v_ones = jnp.concatenate([kvt[:, QK_NOPE:], jnp.ones((t, LANE), BF16)], axis=1)
            pv = jnp.dot(p.astype(BF16), v_ones, preferred_element_type=F32)
            l_sc[...] = alpha * l_sc[...] + pv[:, V_DIM:]
            acc_sc[...] = alpha * acc_sc[...] + pv[:, :V_DIM]
            m_sc[...] = m_new

        _on_masked_or_not(qi, ki, step)

        @pl.when(ki == qi)
        def _():
            l = l_sc[...]
            o = acc_sc[...] / l
            o32_ref[...] = o
            o16_ref[...] = o.astype(BF16)
            lse_ref[0] = m_sc[...] + jnp.log(l)

    return pl.pallas_call(
        body, name="flash_fwd",
        grid_spec=pltpu.PrefetchScalarGridSpec(
            num_scalar_prefetch=2, grid=(HEADS, q_tab.shape[0]),
            in_specs=[pl.BlockSpec((t, HEAD_W), lambda h, p, qt, kt: (qt[p], h)),
                      pl.BlockSpec((t, HEAD_W), lambda h, p, qt, kt: (kt[p], h)),
                      pl.BlockSpec((t, ROPE_BLOCK), lambda h, p, qt, kt: (kt[p], 0))],
            out_specs=[pl.BlockSpec((t, V_DIM), lambda h, p, qt, kt: (qt[p], h)),
                       pl.BlockSpec((t, V_DIM), lambda h, p, qt, kt: (qt[p], h)),
                       pl.BlockSpec((1, t, LANE), lambda h, p, qt, kt: (h, qt[p], 0))],
            scratch_shapes=[pltpu.VMEM((t, LANE), F32), pltpu.VMEM((t, LANE), F32), pltpu.VMEM((t, V_DIM), F32)]),
        out_shape=[jax.ShapeDtypeStruct((lp, HEADS * V_DIM), F32), jax.ShapeDtypeStruct((lp, HEADS * V_DIM), BF16),
                   jax.ShapeDtypeStruct((HEADS, lp, LANE), F32)],
        compiler_params=_cparams(("parallel", "arbitrary")),
    )(q_tab, k_tab, q, kv, kpe)


def flash_bwd(q, kv, kpe, o32, lse, do):
    lp = q.shape[0]
    t = _attn_tile(lp)
    nt = lp // t
    q_tab, k_tab = _causal_pairs(nt, k_major=True)

    def body(q_tab_ref, k_tab_ref, q_ref, kv_ref, kpe_ref, o_ref, lse_ref, do_ref, dq_ref, dkv_ref, dkpe_ref,
             dk_sc, dv_sc):
        pair = pl.program_id(1)
        qi, ki = q_tab_ref[pair], k_tab_ref[pair]

        @pl.when(pair == 0)
        def _():
            dq_ref[...] = jnp.zeros_like(dq_ref)

        @pl.when(qi == ki)
        def _():
            dk_sc[...] = jnp.zeros_like(dk_sc)
            dv_sc[...] = jnp.zeros_like(dv_sc)

        def step(masked):
            qt = q_ref[...]
            kvt = kv_ref[...]
            kcat = jnp.concatenate([kvt[:, :QK_NOPE], kpe_ref[...]], axis=1)
            s = _scores(qt, kcat, qi, ki, t, masked)
            p = jnp.exp(s - lse_ref[0][:, :1])
            do = do_ref[...]
            delta = jnp.sum(do * o_ref[...], axis=1, keepdims=True)
            do16 = do.astype(BF16)
            dv_sc[...] += lax.dot_general(p.astype(BF16), do16, _DIMS["tn"], preferred_element_type=F32)
            dp = lax.dot_general(do16, kvt[:, QK_NOPE:], _DIMS["nt"], preferred_element_type=F32)
            ds = (p * (dp - delta) * SOFTMAX_SCALE).astype(BF16)
            dk_sc[...] += lax.dot_general(ds, qt, _DIMS["tn"], preferred_element_type=F32)
            row = pl.multiple_of(qi * t, t)
            dq_ref[pl.ds(row, t), :] += jnp.dot(ds, kcat, preferred_element_type=F32)

        _on_masked_or_not(qi, ki, step)

        @pl.when(qi == nt - 1)
        def _():
            dk = dk_sc[...]
            dkv_ref[...] = jnp.concatenate([dk[:, :QK_NOPE], dv_sc[...]], axis=1).astype(BF16)
            dkpe_ref[0] = dk[:, QK_NOPE:]

    qmap = lambda h, p, qt, kt: (qt[p], h)
    return pl.pallas_call(
        body, name="flash_bwd",
        grid_spec=pltpu.PrefetchScalarGridSpec(
            num_scalar_prefetch=2, grid=(HEADS, q_tab.shape[0]),
            in_specs=[pl.BlockSpec((t, HEAD_W), qmap),
                      pl.BlockSpec((t, HEAD_W), lambda h, p, qt, kt: (kt[p], h)),
                      pl.BlockSpec((t, ROPE_BLOCK), lambda h, p, qt, kt: (kt[p], 0)),
                      pl.BlockSpec((t, V_DIM), qmap),
                      pl.BlockSpec((1, t, LANE), lambda h, p, qt, kt: (h, qt[p], 0)),
                      pl.BlockSpec((t, V_DIM), qmap)],
            out_specs=[pl.BlockSpec((lp, HEAD_W), lambda h, p, qt, kt: (0, h)),
                       pl.BlockSpec((t, HEAD_W), lambda h, p, qt, kt: (kt[p], h)),
                       pl.BlockSpec((1, t, ROPE_BLOCK), lambda h, p, qt, kt: (h, kt[p], 0))],
            scratch_shapes=[pltpu.VMEM((t, HEAD_W), F32), pltpu.VMEM((t, V_DIM), F32)]),
        out_shape=[jax.ShapeDtypeStruct((lp, Q_COLS), F32), jax.ShapeDtypeStruct((lp, Q_COLS), BF16),
                   jax.ShapeDtypeStruct((HEADS, lp, ROPE_BLOCK), F32)],
        compiler_params=_cparams(("parallel", "arbitrary")),
    )(q_tab, k_tab, q, kv, kpe, o32, lse, do)


def _ij(i, j):
    return (i, j)


def mixer_fwd(a2, w, tabs, pool_scale, g_q, g_kv, ex):
    lp, d = a2.shape
    tm = _m_tile(lp)
    tn = 512
    z = _mm_plain("mix_in", "nn", a2, w["w_in"], tm, 1280, F32)
    yp = pool_fwd(z, w["pool_w"], pool_scale)
    qn, kvn, kpe = mla_prep(z, g_q, g_kv, tabs)
    q = q_proj(qn, w["w_q_b"], tabs)
    kv = kv_proj(kvn, w["w_kv_b"])
    o32, o16, lse = flash_fwd(q, kv, kpe)
    ex.point("flash_fwd_done", o16)
    y_pool = _mm_plain("pool_out", "nn", yp, w["w_pool_o"], tm, tn, F32, after=ex.token)

    def epi(acc, ypl, gp, gm):
        return jax.nn.sigmoid(gp) * ypl + jax.nn.sigmoid(gm) * acc, acc

    y, y_mla = _mm("mla_out_gate", "nn", o16, w["w_mla_o"], tm, tn,
                   [((lp, d), BF16, (tm, tn), _ij), ((lp, d), F32, (tm, tn), _ij)], epi=epi,
                   extras=[(y_pool, (tm, tn), _ij), (z, (tm, tn), lambda i, j: (i, Z_GP // tn + j)),
                           (z, (tm, tn), lambda i, j: (i, Z_GM // tn + j))])
    m = _mm_plain("mix_out", "nn", y, w["w_out"], tm, tn, F32)
    return m, dict(z=z, yp=yp, qn=qn, kvn=kvn, kpe=kpe, q=q, kv=kv, o32=o32, o16=o16, lse=lse,
                   y_pool=y_pool, y_mla=y_mla, y=y)


def mixer_bwd(dm, a2, sv, w, tabs, pool_scale, g_q, g_kv, after=None):
    lp, d = dm.shape
    tm = _m_tile(lp)
    tn = 512
    z = sv["z"]

    def epi(acc, ypl, yml, gp, gm):
        sp, sm = jax.nn.sigmoid(gp), jax.nn.sigmoid(gm)
        return acc * sp, acc * sm, acc * ypl * (sp * (1.0 - sp)), acc * yml * (sm * (1.0 - sm))

    dyp, dym, dgp, dgm = _mm(
        "gate_bwd", "nt", dm, w["w_out"], tm, tn, [((lp, d), BF16, (tm, tn), _ij)] * 4, epi=epi,
        extras=[(sv["y_pool"], (tm, tn), _ij), (sv["y_mla"], (tm, tn), _ij),
                (z, (tm, tn), lambda i, j: (i, Z_GP // tn + j)), (z, (tm, tn), lambda i, j: (i, Z_GM // tn + j))],
        after=after)
    g = {}
    g["w_out"] = _mm_plain("dw_out", "tn", sv["y"], dm, 1024, 1024, BF16, after=after)
    g["w_pool_o"] = _mm_plain("dw_pool_o", "tn", sv["yp"], dyp, 512, 1024, BF16)
    dypre = _mm_plain("pool_out_bwd", "nt", dyp, w["w_pool_o"], tm, tn, F32)
    du, g["pool_w"], d_pool_scale = pool_bwd(dypre, z, w["pool_w"], pool_scale)
    g["w_mla_o"] = _mm_plain("dw_mla_o", "tn", sv["o16"], dym, 1024, 1024, BF16)
    do = _mm_plain("mla_out_bwd", "nt", dym, w["w_mla_o"], tm, tn, F32)
    dq, dkv, dkpe_h = flash_bwd(sv["q"], sv["kv"], sv["kpe"], sv["o32"], sv["lse"], do)
    dql = q_rope_bwd(dq, tabs)
    g["w_q_b"] = _mm_plain("dw_q_b", "tn", sv["qn"], dql, 512, 1024, BF16)
    dqn = _mm_plain("q_proj_bwd", "nt", dql, w["w_q_b"], tm, 512, F32)
    g["w_kv_b"] = _mm_plain("dw_kv_b", "tn", sv["kvn"], dkv, 512, 1024, BF16)
    dkvn = _mm_plain("kv_proj_bwd", "nt", dkv, w["w_kv_b"], tm, 512, F32)
    dcq, dckv, dkr, d_gq, d_gkv = mla_prep_bwd(dqn, dkvn, dkpe_h, z, g_q, g_kv, tabs)
    dz = jnp.concatenate([du, dcq, dckv, dgp, dgm, dkr, jnp.zeros((lp, Z_COLS - Z_KR - ROPE_BLOCK), BF16)], axis=1)
    g["w_in"] = _mm_plain("dw_in", "tn", a2, dz, 1024, 1280, BF16)
    da2 = _mm_plain("mix_in_bwd", "nt", dz, w["w_in"], tm, tn, F32)
    return da2, g, dict(pool_scale=d_pool_scale, q_a_norm=d_gq, kv_a_norm=d_gkv)


_ANY = pl.BlockSpec(memory_space=pl.ANY)
_MESH = pl.DeviceIdType.MESH


def _my_pos():
    return lax.axis_index("x"), lax.axis_index("y"), lax.axis_index("c")


LEAD = "lead"
COLS = "cols"
COLS_GU = "cols_gu"


def _col_block(layout, dev):
    return dev if layout == COLS else 2 * (dev % 4) + dev // 4


def _dev_block(ref, layout, dev, cols):
    if layout == LEAD:
        return ref.at[dev]
    return ref.at[:, pl.ds(pl.multiple_of(_col_block(layout, dev) * cols, LANE), cols)]


def _gathered_shape(shard_shape, layout):
    if layout == LEAD:
        return (N_DEV, *shard_shape)
    return (shard_shape[0], N_DEV * shard_shape[1])


def all_gather(name, shards, layouts):
    n = len(shards)

    def body(*refs):
        ins, outs = refs[:n], refs[n:2 * n]
        send_sems, recv_sems, local_sems = refs[2 * n:]
        x, y, c = _my_pos()
        me, sibling = (x, y, c), (x, y, 1 - c)
        chips = [(1 - x, y), (x, 1 - y), (1 - x, 1 - y)]

        def blk(a, px, py, pc):
            return _dev_block(outs[a], layouts[a], 4 * px + 2 * py + pc, shards[a].shape[-1])

        def copy(a, k, block, to, src=None):
            return pltpu.make_async_remote_copy(
                src_ref=blk(a, *block) if src is None else src, dst_ref=blk(a, *block),
                send_sem=send_sems.at[a, k], recv_sem=recv_sems.at[a, k], device_id=to, device_id_type=_MESH)

        mine = [pltpu.make_async_copy(ins[a], blk(a, *me), local_sems.at[a]) for a in range(n)]
        for cp in mine:
            cp.start()
        first = []
        for a in range(n):
            first.append(copy(a, 0, me, sibling, src=ins[a]))
            first += [copy(a, 1 + j, me, (*chip, c), src=ins[a]) for j, chip in enumerate(chips)]
        for cp in first:
            cp.start()
        passed = []
        for j, chip in enumerate(chips):
            for a in range(n):
                copy(a, 1 + j, (*chip, c), me).wait_recv()
                fwd = copy(a, 4 + j, (*chip, c), sibling)
                fwd.start()
                passed.append(fwd)
        for a in range(n):
            copy(a, 0, sibling, me).wait_recv()
            for j, chip in enumerate(chips):
                copy(a, 4 + j, (*chip, 1 - c), me).wait_recv()
        for cp in first + passed:
            cp.wait_send()
        for cp in mine:
            cp.wait()

    return pl.pallas_call(
        body, name=name,
        in_specs=[_ANY] * n, out_specs=[_ANY] * n,
        out_shape=[jax.ShapeDtypeStruct(_gathered_shape(s.shape, lay), s.dtype)
                   for s, lay in zip(shards, layouts, strict=True)],
        scratch_shapes=[pltpu.SemaphoreType.DMA((n, 7)), pltpu.SemaphoreType.DMA((n, 7)), pltpu.SemaphoreType.DMA((n,))],
    )(*shards)


def _shard_shape(grad, layout):
    return grad.shape[1:] if layout == LEAD else (grad.shape[0], grad.shape[1] // N_DEV)


def rs_sibling(name, grads, layouts):
    n = len(grads)

    def body(*refs):
        ins, outs = refs[:n], refs[n:2 * n]
        send_sems, recv_sems = refs[2 * n:]
        x, y, c = _my_pos()
        cps = []
        for a in range(n):
            for k in range(4):
                cp = pltpu.make_async_remote_copy(
                    src_ref=_dev_block(ins[a], layouts[a], 2 * k + (1 - c), outs[a].shape[-1]), dst_ref=outs[a].at[k],
                    send_sem=send_sems.at[a, k], recv_sem=recv_sems.at[a, k],
                    device_id=(x, y, 1 - c), device_id_type=_MESH)
                cp.start()
                cps.append(cp)
        for cp in cps:
            cp.wait()

    return pl.pallas_call(
        body, name=name,
        in_specs=[_ANY] * n, out_specs=[_ANY] * n,
        out_shape=[jax.ShapeDtypeStruct((4, *_shard_shape(g, lay)), g.dtype) for g, lay in zip(grads, layouts, strict=True)],
        scratch_shapes=[pltpu.SemaphoreType.DMA((n, 4)), pltpu.SemaphoreType.DMA((n, 4))],
    )(*grads)


def rs_chips(name, sums):
    n = len(sums)

    def body(*refs):
        ins, outs = refs[:n], refs[n:2 * n]
        send_sems, recv_sems = refs[2 * n:]
        x, y, c = _my_pos()
        chips = [(1 - x, y), (x, 1 - y), (1 - x, 1 - y)]
        cps = []
        for a in range(n):
            for j, chip in enumerate(chips):
                cp = pltpu.make_async_remote_copy(
                    src_ref=ins[a].at[2 * chip[0] + chip[1]], dst_ref=outs[a].at[j],
                    send_sem=send_sems.at[a, j], recv_sem=recv_sems.at[a, j],
                    device_id=(*chip, c), device_id_type=_MESH)
                cp.start()
                cps.append(cp)
        for cp in cps:
            cp.wait()

    return pl.pallas_call(
        body, name=name,
        in_specs=[_ANY] * n, out_specs=[_ANY] * n,
        out_shape=[jax.ShapeDtypeStruct((3, *s.shape[1:]), s.dtype) for s in sums],
        scratch_shapes=[pltpu.SemaphoreType.DMA((n, 3)), pltpu.SemaphoreType.DMA((n, 3))],
    )(*sums)


_HBM = pl.BlockSpec(memory_space=pltpu.HBM)
_SEM = pl.BlockSpec(memory_space=pltpu.SEMAPHORE)
_EFFECT = pltpu.SideEffectType.DATAFLOW_SIDE_EFFECTING


def _in_hbm(a):
    return pltpu.with_memory_space_constraint(a, pltpu.HBM)


def split_start(name, bufs, plan, n_copies, after=None):
    nb = len(bufs)
    extra = [] if after is None else [after]

    def body(*refs):
        buf_refs = refs[:nb]
        send_sems, recv_sems = refs[nb + len(extra)], refs[nb + len(extra) + 1]
        token = refs[-1]
        copies = plan(buf_refs)
        assert len(copies) == n_copies
        for k, (src, dst, to) in enumerate(copies):
            pltpu.make_async_remote_copy(src_ref=src, dst_ref=dst, send_sem=send_sems.at[k], recv_sem=recv_sems.at[k],
                                         device_id=to, device_id_type=_MESH).start()
        token[...] = jnp.zeros_like(token)

    out = pl.pallas_call(
        body, name=name,
        out_shape=(pltpu.SemaphoreType.DMA((n_copies,)), pltpu.SemaphoreType.DMA((n_copies,)),
                   *[pltpu.HBM(b.shape, b.dtype) for b in bufs], jax.ShapeDtypeStruct((8, LANE), F32)),
        in_specs=[_HBM] * nb + [_ANY] * len(extra),
        out_specs=(_SEM, _SEM, *[_HBM] * nb, pl.BlockSpec(memory_space=pltpu.VMEM)),
        input_output_aliases={i: 2 + i for i in range(nb)},
        compiler_params=pltpu.CompilerParams(has_side_effects=_EFFECT),
    )(*[_in_hbm(b) for b in bufs], *extra)
    return out[0], out[1], list(out[2:2 + nb]), out[-1]


def split_wait(name, bufs, send_sems, recv_sems, plan, after):
    nb = len(bufs)

    def body(*refs):
        buf_refs = refs[:nb]
        s_sems, r_sems = refs[nb], refs[nb + 1]
        for k, (src, dst, to) in enumerate(plan(buf_refs)):
            cp = pltpu.make_async_remote_copy(src_ref=src, dst_ref=dst, send_sem=s_sems.at[k], recv_sem=r_sems.at[k],
                                              device_id=to, device_id_type=_MESH)
            cp.wait_send()
            cp.wait_recv()

    out = pl.pallas_call(
        body, name=name,
        out_shape=tuple(pltpu.HBM(b.shape, b.dtype) for b in bufs),
        in_specs=[_HBM] * nb + [_SEM, _SEM, _ANY],
        out_specs=tuple([_HBM] * nb),
        input_output_aliases={i: i for i in range(nb)},
        compiler_params=pltpu.CompilerParams(has_side_effects=_EFFECT),
    )(*bufs, send_sems, recv_sems, after)
    return list(out)


def _ag_own_plan(shapes, layouts):
    n = len(shapes)

    def plan(refs):
        x, y, c = _my_pos()
        targets = [(x, y, 1 - c), (1 - x, y, c), (x, 1 - y, c), (1 - x, 1 - y, c)]
        out = []
        for a in range(n):
            blk = _dev_block(refs[a], layouts[a], 4 * x + 2 * y + c, shapes[a][-1])
            out += [(blk, blk, to) for to in targets]
        return out

    return plan, 4 * n


def _ag_pass_plan(shapes, layouts):
    n = len(shapes)

    def plan(refs):
        x, y, c = _my_pos()
        out = []
        for a in range(n):
            for px, py in [(1 - x, y), (x, 1 - y), (1 - x, 1 - y)]:
                blk = _dev_block(refs[a], layouts[a], 4 * px + 2 * py + c, shapes[a][-1])
                out.append((blk, blk, (x, y, 1 - c)))
        return out

    return plan, 3 * n


def _rs_sibling_plan(layouts, n):
    def plan(refs):
        x, y, c = _my_pos()
        return [(_dev_block(refs[a], layouts[a], 2 * k + (1 - c), refs[n + a].shape[-1]), refs[n + a].at[k], (x, y, 1 - c))
                for a in range(n) for k in range(4)]

    return plan, 4 * n


def _rs_chips_plan(n):
    def plan(refs):
        x, y, c = _my_pos()
        return [(refs[a].at[2 * px + py], refs[n + a].at[j], (px, py, c))
                for a in range(n) for j, (px, py) in enumerate([(1 - x, y), (x, 1 - y), (1 - x, 1 - y)])]

    return plan, 3 * n


def place_own(name, shard, layout, dtype, dev, after):
    r, c = shard.shape
    tr = _ew_rows(r, c)
    if layout == LEAD:
        o_spec = pl.BlockSpec((None, tr, c), lambda i, dev_ref: (dev_ref[0], i, 0))
    else:
        o_spec = pl.BlockSpec((tr, c), lambda i, dev_ref: (i, _col_block(layout, dev_ref[0])))
    extra = [] if after is None else [after]

    def body(dev_ref, s_ref, *rest):
        rest[-1][...] = s_ref[...].astype(dtype)

    return pl.pallas_call(
        body, name=name,
        grid_spec=pltpu.PrefetchScalarGridSpec(
            num_scalar_prefetch=1, grid=(r // tr,),
            in_specs=[pl.BlockSpec((tr, c), lambda i, dev_ref: (i, 0))] + [_ANY] * len(extra),
            out_specs=o_spec),
        out_shape=jax.ShapeDtypeStruct(_gathered_shape(shard.shape, layout), dtype),
        compiler_params=_cparams(("parallel",)),
    )(dev, shard, *extra)


def _ew_rows(r, c):
    for t in (512, 256, 128, 64, 32, 16):
        if r % t == 0 and t * c * 4 <= 768 * 1024:
            return t
    raise ValueError((r, c))


def rs_add(name, grad, layout, recv, core):
    _, r, c = recv.shape
    tr = _ew_rows(r, c)
    if layout == LEAD:
        g_spec = pl.BlockSpec((None, tr, c), lambda k, i, core_ref: (2 * k + core_ref[0], i, 0))
    else:
        g_spec = pl.BlockSpec((tr, c), lambda k, i, core_ref: (i, _col_block(layout, 2 * k + core_ref[0])))

    def body(core_ref, g_ref, r_ref, o_ref):
        o_ref[...] = (g_ref[...].astype(F32) + r_ref[...].astype(F32)).astype(BF16)

    return pl.pallas_call(
        body, name=name,
        grid_spec=pltpu.PrefetchScalarGridSpec(
            num_scalar_prefetch=1, grid=(4, r // tr),
            in_specs=[g_spec, pl.BlockSpec((None, tr, c), lambda k, i, core_ref: (k, i, 0))],
            out_specs=pl.BlockSpec((None, tr, c), lambda k, i, core_ref: (k, i, 0))),
        out_shape=jax.ShapeDtypeStruct((4, r, c), BF16),
        compiler_params=_cparams(("parallel", "parallel")),
    )(core, grad, recv)


def _adamw(w, g, m, v):
    m = ADAM_B1 * m + (1.0 - ADAM_B1) * g
    v = ADAM_B2 * v + (1.0 - ADAM_B2) * jnp.square(g)
    m_hat = m / (1.0 - ADAM_B1 ** ADAM_STEP)
    v_hat = v / (1.0 - ADAM_B2 ** ADAM_STEP)
    delta = -ADAM_LR * (m_hat / (jnp.sqrt(v_hat) + ADAM_EPS) + ADAM_WD * w)
    return delta, m, v


def adamw_shard(name, w, m, v, sums, recv, chip):
    r, c = w.shape
    tr = _ew_rows(r, c)

    def body(chip_ref, w_ref, m_ref, v_ref, s_ref, r_ref, g_ref, d_ref, mo_ref, vo_ref):
        g = s_ref[0].astype(F32)
        for j in range(3):
            g = g + r_ref[j].astype(F32)
        d, mn, vn = _adamw(w_ref[...], g, m_ref[...], v_ref[...])
        g_ref[...] = g
        d_ref[...] = d
        mo_ref[...] = mn
        vo_ref[...] = vn

    spec = pl.BlockSpec((tr, c), lambda i, chip_ref: (i, 0))
    return pl.pallas_call(
        body, name=name,
        grid_spec=pltpu.PrefetchScalarGridSpec(
            num_scalar_prefetch=1, grid=(r // tr,),
            in_specs=[spec, spec, spec,
                      pl.BlockSpec((1, tr, c), lambda i, chip_ref: (chip_ref[0], i, 0)),
                      pl.BlockSpec((3, tr, c), lambda i, chip_ref: (0, i, 0))],
            out_specs=[spec] * 4),
        out_shape=[jax.ShapeDtypeStruct((r, c), F32)] * 4,
        compiler_params=_cparams(("parallel",)),
    )(chip, w, m, v, sums, recv)


def reduce_small(gathered):
    _, r, c = gathered.shape

    def body(g_ref, o_ref):
        acc = g_ref[0]
        for k in range(1, N_DEV):
            acc = acc + g_ref[k]
        o_ref[...] = acc

    return pl.pallas_call(body, name="reduce_small", out_shape=jax.ShapeDtypeStruct((r, c), F32))(gathered)


def adamw_small(ws, gs, ms, vs):
    n = len(ws)

    def body(*refs):
        w_r, g_r, m_r, v_r = refs[:n], refs[n:2 * n], refs[2 * n:3 * n], refs[3 * n:4 * n]
        d_o, m_o, v_o = refs[4 * n:5 * n], refs[5 * n:6 * n], refs[6 * n:7 * n]
        for a in range(n):
            d, mn, vn = _adamw(w_r[a][...], g_r[a][...], m_r[a][...], v_r[a][...])
            d_o[a][...] = d
            m_o[a][...] = mn
            v_o[a][...] = vn

    shapes = [jax.ShapeDtypeStruct(w.shape, F32) for w in ws]
    out = pl.pallas_call(body, name="adamw_small", out_shape=shapes * 3)(*ws, *gs, *ms, *vs)
    return out[:n], out[n:2 * n], out[2 * n:]


WEIGHTS = ["meta_tokens", "norm_ffn1_pre", "norm_ffn1_post", "ffn1_w_gu", "ffn1_w_down", "norm_mix_pre",
           "norm_mix_post", "w_in", "pool_w", "pool_scale", "w_pool_o", "q_a_norm", "w_q_b", "kv_a_norm", "w_kv_b",
           "w_mla_o", "w_out", "norm_ffn2_pre", "norm_ffn2_post", "ffn2_w_gu", "ffn2_w_down"]
BIG = ["ffn1_w_gu", "ffn1_w_down", "w_in", "pool_w", "w_pool_o", "w_q_b", "w_kv_b", "w_mla_o", "w_out",
       "ffn2_w_gu", "ffn2_w_down"]
COL_SHARDED = ("w_in", "w_q_b")
GATHERED = {"ffn1_w_gu": COLS_GU, "ffn2_w_gu": COLS_GU, "w_pool_o": COLS, "w_kv_b": COLS}
GAINS =["norm_ffn1_pre", "norm_ffn1_post", "norm_mix_pre", "norm_mix_post", "norm_ffn2_pre", "norm_ffn2_post"]
SMALL = GAINS + ["pool_scale", "q_a_norm", "kv_a_norm"]
Z_SRC = 1024 + 512 + 512 + QK_ROPE


def _full_from_gathered(name, g):
    _, r, c = g.shape
    if name == "pool_w":
        ng = len(POOL_WINDOWS)
        return g.reshape(N_DEV, ng, r // ng, c).transpose(1, 0, 2, 3).reshape(ng, POOL_GROUP, POOL_GROUP)
    if name in COL_SHARDED:
        return g.transpose(1, 0, 2).reshape(r, N_DEV * c)
    return g.reshape(N_DEV * r, c)


def _blocks_from_full(name, dw):
    if name == "pool_w":
        ng = len(POOL_WINDOWS)
        return dw.reshape(ng, N_DEV, POOL_GROUP // N_DEV, POOL_GROUP).transpose(1, 0, 2, 3).reshape(
            N_DEV, ng * POOL_GROUP // N_DEV, POOL_GROUP)
    k, n = dw.shape
    if name in COL_SHARDED:
        return dw.reshape(k, N_DEV, n // N_DEV).transpose(1, 0, 2)
    return dw.reshape(N_DEV, k // N_DEV, n)


def _to_internal(name, w):
    if name == "w_in":
        d = w.shape[0]
        return jnp.concatenate([w[:, :Z_SRC - QK_ROPE], w[:, Z_SRC:], w[:, Z_SRC - QK_ROPE:Z_SRC],
                                jnp.zeros((d, Z_COLS - Z_KR - QK_ROPE), w.dtype)], axis=1)
    if name == "w_q_b":
        r = w.shape[0]
        w3 = w.reshape(r, HEADS, QK_NOPE + QK_ROPE)
        return jnp.pad(w3, ((0, 0), (0, 0), (0, HEAD_W - QK_NOPE - QK_ROPE))).reshape(r, Q_COLS)
    return w


def _from_internal(name, dw):
    if name == "w_in":
        return jnp.concatenate([dw[:, :Z_SRC - QK_ROPE], dw[:, Z_KR:Z_KR + QK_ROPE], dw[:, Z_SRC - QK_ROPE:Z_KR]], axis=1)
    if name == "w_q_b":
        r = dw.shape[0]
        return dw.reshape(r, HEADS, HEAD_W)[:, :, :QK_NOPE + QK_ROPE].reshape(r, HEADS * (QK_NOPE + QK_ROPE))
    return dw


def _shard2d(a):
    return a.reshape(-1, a.shape[-1])


def kernel(x, meta_tokens, norm_ffn1_pre, norm_ffn1_post, ffn1_w_gu, ffn1_w_down, norm_mix_pre, norm_mix_post, w_in, pool_w, pool_scale, w_pool_o, q_a_norm, w_q_b, kv_a_norm, w_kv_b, w_mla_o, w_out, norm_ffn2_pre, norm_ffn2_post, ffn2_w_gu, ffn2_w_down, loss_target, m_meta_tokens, m_norm_ffn1_pre, m_norm_ffn1_post, m_ffn1_w_gu, m_ffn1_w_down, m_norm_mix_pre, m_norm_mix_post, m_w_in, m_pool_w, m_pool_scale, m_w_pool_o, m_q_a_norm, m_w_q_b, m_kv_a_norm, m_w_kv_b, m_w_mla_o, m_w_out, m_norm_ffn2_pre, m_norm_ffn2_post, m_ffn2_w_gu, m_ffn2_w_down, v_meta_tokens, v_norm_ffn1_pre, v_norm_ffn1_post, v_ffn1_w_gu, v_ffn1_w_down, v_norm_mix_pre, v_norm_mix_post, v_w_in, v_pool_w, v_pool_scale, v_w_pool_o, v_q_a_norm, v_w_q_b, v_kv_a_norm, v_w_kv_b, v_w_mla_o, v_w_out, v_norm_ffn2_pre, v_norm_ffn2_post, v_ffn2_w_gu, v_ffn2_w_down):
    given = dict(locals())
    w_in_dev = {n: given[n] for n in WEIGHTS}
    m_in = {n: given["m_" + n] for n in WEIGHTS}
    v_in = {n: given["v_" + n] for n in WEIGHTS}
    xi, yi, ci = _my_pos()
    dev = 4 * xi + 2 * yi + ci
    core = jnp.reshape(ci, (1,)).astype(jnp.int32)
    chip = jnp.reshape(2 * xi + yi, (1,)).astype(jnp.int32)
    d = x.shape[-1]

    shards = {n: _shard2d(w_in_dev[n]) for n in BIG}
    ex = _Exchange(shards, meta_tokens, dev, core)
    gain = {n: given[n] for n in SMALL}
    loss_blk, grad_x, front, gsmall = local_step(x[0], loss_target[0], gain, ex)

    out_g, out_d, out_m, out_v = {}, {}, {}, {}

    def finish(grp, after):
        names, sums, from_chips = ex.finish_grads(grp, after)
        for n, s, r in zip(names, sums, from_chips, strict=True):
            shp = w_in_dev[n].shape
            res = adamw_shard("adamw_" + n, shards[n], _shard2d(m_in[n]), _shard2d(v_in[n]), s, r, chip)
            out_g[n], out_d[n], out_m[n], out_v[n] = [t.reshape(shp) for t in res]
        return res[0]

    finish("A2", finish("B", finish("C", grad_x)))

    tail = jnp.concatenate([gsmall["pool_scale"], gsmall["q_a_norm"], gsmall["kv_a_norm"]], axis=1)
    small = jnp.concatenate([gsmall[n] for n in GAINS] + [tail, jnp.broadcast_to(loss_blk[:1, :1], (1, d)),
                                                         front[PAD_ROWS:]], axis=0)
    (small_g,) = all_gather("ag_small", [small], [LEAD])
    total = reduce_small(small_g)
    ng = len(GAINS)
    for i, n in enumerate(GAINS):
        out_g[n] = total[i:i + 1]
    o = 0
    for n in ("pool_scale", "q_a_norm", "kv_a_norm"):
        wdt = w_in_dev[n].shape[1]
        out_g[n] = total[ng:ng + 1, o:o + wdt]
        o += wdt
    loss = total[ng + 1, 0]
    mcols = meta_tokens.shape[1]
    out_g["meta_tokens"] = lax.dynamic_slice(total[ng + 2:ng + 2 + N_META], (0, dev * mcols), (N_META, mcols))
    names = ["meta_tokens"] + SMALL
    ds_, ms_, vs_ = adamw_small([w_in_dev[n] for n in names], [out_g[n] for n in names],
                                [m_in[n] for n in names], [v_in[n] for n in names])
    for n, dd, mm, vv in zip(names, ds_, ms_, vs_, strict=True):
        out_d[n], out_m[n], out_v[n] = dd, mm, vv
    finish("A1", ds_[0])

    return (loss, grad_x[None], *[out_g[n] for n in WEIGHTS], *[out_d[n] for n in WEIGHTS],
            *[out_m[n] for n in WEIGHTS], *[out_v[n] for n in WEIGHTS])


GROUPS = {"A1": ["ffn1_w_gu"], "A2": ["ffn1_w_down"],
          "B": ["w_in", "pool_w", "w_pool_o", "w_q_b", "w_kv_b", "w_mla_o", "w_out"],
          "C": ["ffn2_w_gu", "ffn2_w_down"]}


class _Exchange:
    def __init__(self, shards, meta_tokens, dev, core):
        self.shards, self.meta_tokens, self.core = shards, meta_tokens, core
        self.dev1 = jnp.reshape(dev, (1,)).astype(jnp.int32)
        self.w, self.meta_full, self.token = {}, None, None
        self._ag, self._rs = {}, {}

    def _ag_start(self, grp, after):
        names = GROUPS[grp] + (["meta_tokens"] if grp == "A1" else [])
        srcs = [self.meta_tokens if n == "meta_tokens" else self.shards[n] for n in names]
        lays = [GATHERED.get(n, LEAD) for n in names]
        shapes = [a.shape for a in srcs]
        lands = [place_own(f"place_{n}", a, lay, F32 if n == "meta_tokens" else BF16, self.dev1, after)
                 for n, a, lay in zip(names, srcs, lays, strict=True)]
        plan, cnt = _ag_own_plan(shapes, lays)
        ss, rs, bufs, self.token = split_start(f"ag{grp}_own_start", lands, plan, cnt)
        self._ag[grp] = dict(names=names, lays=lays, shapes=shapes, own=(ss, rs, bufs, plan))

    def _ag_pass(self, grp, after):
        st = self._ag[grp]
        ss, rs, bufs, plan = st["own"]
        lands = split_wait(f"ag{grp}_own_wait", bufs, ss, rs, plan, after)
        plan, cnt = _ag_pass_plan(st["shapes"], st["lays"])
        ss, rs, lands, self.token = split_start(f"ag{grp}_pass_start", lands, plan, cnt)
        st["pass"] = (ss, rs, lands, plan)

    def _ag_finish(self, grp, after):
        st = self._ag[grp]
        ss, rs, lands, plan = st["pass"]
        lands = split_wait(f"ag{grp}_pass_wait", lands, ss, rs, plan, after)
        for n, g, lay in zip(st["names"], lands, st["lays"]):
            if n == "meta_tokens":
                self.meta_full = g.transpose(1, 0, 2).reshape(N_META, N_DEV * g.shape[-1])
            else:
                self.w[n] = g if lay != LEAD else _to_internal(n, _full_from_gathered(n, g))

    def grads(self, grp, gbig, after=None):
        names = GROUPS[grp]
        lays = [GATHERED.get(n, LEAD) for n in names]
        grads = [gbig[n] if lay != LEAD else _blocks_from_full(n, _from_internal(n, gbig[n]).astype(BF16))
                 for n, lay in zip(names, lays, strict=True)]
        lands = [lax.empty((4, *_shard_shape(g, lay)), BF16) for g, lay in zip(grads, lays, strict=True)]
        plan, cnt = _rs_sibling_plan(lays, len(names))
        ss, rs, bufs, self.token = split_start(f"rs{grp}_sibling_start", grads + lands, plan, cnt, after)
        self._rs[grp] = dict(names=names, lays=lays, sib=(ss, rs, bufs, plan))
        return self.token

    def _rs_mid(self, grp, after):
        st = self._rs[grp]
        n = len(st["names"])
        ss, rs, bufs, plan = st["sib"]
        bufs = split_wait(f"rs{grp}_sibling_wait", bufs, ss, rs, plan, after)
        sums = [rs_add(f"rs_add_{name}", g, lay, r, self.core)
                for name, g, lay, r in zip(st["names"], bufs[:n], st["lays"], bufs[n:], strict=True)]
        lands = [lax.empty((3, *s.shape[1:]), BF16) for s in sums]
        plan, cnt = _rs_chips_plan(n)
        ss, rs, bufs, self.token = split_start(f"rs{grp}_chips_start", sums + lands, plan, cnt)
        st["chips"] = (ss, rs, bufs, plan)

    def finish_grads(self, grp, after):
        st = self._rs[grp]
        n = len(st["names"])
        ss, rs, bufs, plan = st["chips"]
        bufs = split_wait(f"rs{grp}_chips_wait", bufs, ss, rs, plan, after)
        return st["names"], bufs[:n], bufs[n:]

    def point(self, name, after=None):
        if name == "start":
            self._ag_start("A1", None)
            self._ag_pass("A1", self.token)
            self._ag_start("A2", self.token)
            self._ag_start("B", self.token)
            self._ag_finish("A1", self.token)
        elif name == "ffn1_gu_done":
            self._ag_pass("A2", after)
            self._ag_finish("A2", self.token)
        elif name == "ffn1_fwd_done":
            self._ag_pass("B", after)
            self._ag_start("C", self.token)
        elif name == "mix_pre_done":
            self._ag_finish("B", after)
        elif name == "flash_fwd_done":
            self._ag_pass("C", after)
        elif name == "ffn2_pre_done":
            self._ag_finish("C", after)
        elif name.startswith("rs") and name.endswith("_mid"):
            self._rs_mid(name[2:-4], after)


def local_step(x, target, gain, ex):
    d = x.shape[-1]
    ex.point("start")
    w = ex.w
    h0 = jnp.concatenate([jnp.zeros((PAD_ROWS, d), F32), ex.meta_full, x], axis=0)
    lp = h0.shape[0]
    tabs = rope_tables(lp)
    a1 = prenorm(h0, gain["norm_ffn1_pre"], after=ex.token)
    gu1, s1 = ffn_gu("ffn1", a1, w["ffn1_w_gu"])
    ex.point("ffn1_gu_done", s1)
    f1 = ffn_down("ffn1", s1, w["ffn1_w_down"])
    ex.point("ffn1_fwd_done", f1)
    h1, a2 = post_pre("post_pre1", f1, h0, gain["norm_ffn1_post"], 0.5, gain["norm_mix_pre"], after=ex.token)
    ex.point("mix_pre_done", a2)
    mix, sv = mixer_fwd(a2, w, tabs, gain["pool_scale"], gain["q_a_norm"], gain["kv_a_norm"], ex)
    h2, a3 = post_pre("post_pre2", mix, h1, gain["norm_mix_post"], 1.0, gain["norm_ffn2_pre"])
    ex.point("ffn2_pre_done", a3)
    gu2, s2 = ffn_gu("ffn2", a3, w["ffn2_w_gu"])
    f2 = ffn_down("ffn2", s2, w["ffn2_w_down"])
    dh3, loss_blk = post_loss(f2, h2, gain["norm_ffn2_post"], 0.5, target)

    gsmall = {}
    df2, gsmall["norm_ffn2_post"] = post_bwd(dh3, f2, gain["norm_ffn2_post"], 0.5)
    da3 = ffn_bwd("ffn2", df2, a3, gu2, s2, w["ffn2_w_gu"], w["ffn2_w_down"], lambda dwd: None,
                  lambda dwgu, dwd: ex.grads("C", {"ffn2_w_gu": dwgu, "ffn2_w_down": dwd}))
    dh2, dmix, gsmall["norm_ffn2_pre"], gsmall["norm_mix_post"] = pre_post_bwd(
        "pre_post_bwd2", da3, h2, gain["norm_ffn2_pre"], dh3, mix, gain["norm_mix_post"], 1.0)
    ex.point("rsC_mid", dmix)
    da2, gmix, gmix_small = mixer_bwd(dmix, a2, sv, w, tabs, gain["pool_scale"], gain["q_a_norm"], gain["kv_a_norm"],
                                      after=ex.token)
    gsmall.update(gmix_small)
    dh1, df1, gsmall["norm_mix_pre"], gsmall["norm_ffn1_post"] = pre_post_bwd(
        "pre_post_bwd1", da2, h1, gain["norm_mix_pre"], dh2, f1, gain["norm_ffn1_post"], 0.5,
        after=ex.grads("B", gmix))
    ex.point("rsB_mid", df1)
    def on_dwgu1(dwgu, dwd):
        ex.point("rsA2_mid", dwgu)
        return ex.grads("A1", {"ffn1_w_gu": dwgu}, after=ex.token)

    da1 = ffn_bwd("ffn1", df1, a1, gu1, s1, w["ffn1_w_gu"], w["ffn1_w_down"],
                  lambda dwd: ex.grads("A2", {"ffn1_w_down": dwd}), on_dwgu1, after=ex.token)
    ex.point("rsA1_mid", da1)
    grad_x, front, gsmall["norm_ffn1_pre"] = pre_bwd_first(da1, h0, gain["norm_ffn1_pre"], dh1, after=ex.token)
    return loss_blk, grad_x, front, gsmall
```

```python
import functools

import jax
import jax.numpy as jnp
import numpy as np
from jax import lax
from jax.experimental import pallas as pl
from jax.experimental.pallas import tpu as pltpu

F32 = jnp.float32
BF16 = jnp.bfloat16

N_META = 16
POOL_WINDOWS = (2, 4, 8, 16)
POOL_GROUP = 256
HEADS = 16
QK_NOPE = 128
QK_ROPE = 64
V_DIM = 128
ROPE_THETA = 10000.0
SOFTMAX_SCALE = (QK_NOPE + QK_ROPE) ** -0.5
EPS = 1e-6
ADAM_LR = 0.001
ADAM_B1 = 0.9
ADAM_B2 = 0.999
ADAM_EPS = 1e-08
ADAM_WD = 0.01
ADAM_STEP = 10

LANE = 128
FRONT = 128
PAD_ROWS = FRONT - N_META
HEAD_W = 256
GU_TILE = 1408
VMEM_LIMIT = 56 * 1024 * 1024
MESH_AXES = ("x", "y", "c")
N_DEV = 8


def _pick(n, cands):
    for c in cands:
        if n % c == 0:
            return c
    raise ValueError(f"no tile for {n} in {cands}")


def _cparams(sem=None):
    kw = dict(vmem_limit_bytes=VMEM_LIMIT)
    if sem is not None:
        kw["dimension_semantics"] = sem
    return pltpu.CompilerParams(**kw)


_DIMS = {"nn": (((1,), (0,)), ((), ())), "nt": (((1,), (1,)), ((), ())), "tn": (((0,), (0,)), ((), ()))}


def _behind(after):
    return [] if after is None else list(after) if isinstance(after, (list, tuple)) else [after]


def _behind_specs(after):
    return [pl.BlockSpec(memory_space=pl.ANY)] * len(_behind(after))


def _mm(name, form, a, b, tm, tn, outs, epi=None, extras=(), n_outer=False, after=None):
    if form == "tn":
        k, m = a.shape
        n = b.shape[1]
        a_blk, a_map = (k, tm), lambda i, j: (0, i)
        b_blk, b_map = (k, tn), lambda i, j: (0, j)
    elif form == "nn":
        m, k = a.shape
        n = b.shape[1]
        a_blk, a_map = (tm, k), lambda i, j: (i, 0)
        b_blk, b_map = (k, tn), lambda i, j: (0, j)
    else:
        m, k = a.shape
        n = b.shape[0]
        a_blk, a_map = (tm, k), lambda i, j: (i, 0)
        b_blk, b_map = (tn, k), lambda i, j: (j, 0)
    assert m % tm == 0 and n % tn == 0, (name, m, n, tm, tn)
    n_ex = len(extras)
    dn = _DIMS[form]
    if n_outer:
        grid = (n // tn, m // tm)

        def spec(blk, im):
            return pl.BlockSpec(blk, lambda gj, gi: im(gi, gj))
    else:
        grid = (m // tm, n // tn)
        spec = pl.BlockSpec

    behind = _behind(after)

    def body(a_ref, b_ref, *rest):
        ex, out_refs = rest[:n_ex], rest[n_ex + len(behind):]
        acc = lax.dot_general(a_ref[...].astype(BF16), b_ref[...].astype(BF16), dn, preferred_element_type=F32)
        res = epi(acc, *[e[...] for e in ex]) if epi is not None else (acc,)
        for r, o in zip(res, out_refs, strict=True):
            o[...] = r.astype(o.dtype)

    return pl.pallas_call(
        body,
        name=name,
        grid=grid,
        in_specs=[spec(a_blk, a_map), spec(b_blk, b_map)] + [spec(blk, im) for _, blk, im in extras] + _behind_specs(after),
        out_specs=[spec(blk, im) for _, _, blk, im in outs],
        out_shape=[jax.ShapeDtypeStruct(s, d) for s, d, _, _ in outs],
        compiler_params=_cparams(("parallel", "parallel")),
    )(a, b, *[e for e, _, _ in extras], *behind)


def _mm_plain(name, form, a, b, tm, tn, out_dtype, after=None):
    m = a.shape[1] if form == "tn" else a.shape[0]
    n = b.shape[0] if form == "nt" else b.shape[1]
    return _mm(name, form, a, b, tm, tn, [((m, n), out_dtype, (tm, tn), lambda i, j: (i, j))], after=after)[0]


def _rstd(x):
    return lax.rsqrt(jnp.mean(x * x, axis=-1, keepdims=True) + EPS)


def _norm_bwd(dy, x, gain):
    r = _rstd(x)
    dyg = dy * gain
    dx = r * (dyg - x * (r * r) * jnp.mean(dyg * x, axis=-1, keepdims=True))
    dgain = jnp.sum(dy * x * r, axis=0, keepdims=True)
    return dx, dgain


def _row_spec(tr, cols, col_block=0):
    return pl.BlockSpec((tr, cols), lambda i: (i, col_block))


def _vec_spec(cols, col_block=0):
    return pl.BlockSpec((1, cols), lambda i: (0, col_block))


def _row_tile(lp):
    return _pick(lp, (128,))


def _skip_behind(after, body):
    n = len(_behind(after))
    return body if n == 0 else (lambda *refs: body(*refs[n:]))


def prenorm(h, gain, after=None):
    lp, d = h.shape
    tr = _row_tile(lp)

    def body(h_ref, g_ref, a_ref):
        x = h_ref[...]
        a_ref[...] = (x * _rstd(x) * g_ref[...]).astype(BF16)

    return pl.pallas_call(
        _skip_behind(after, body), name="prenorm", grid=(lp // tr,),
        in_specs=_behind_specs(after) + [_row_spec(tr, d), _vec_spec(d)], out_specs=_row_spec(tr, d),
        out_shape=jax.ShapeDtypeStruct((lp, d), BF16), compiler_params=_cparams(("parallel",)),
    )(*_behind(after), h, gain)


def post_pre(name, f, h_in, g_post, coef, g_next, after=None):
    lp, d = f.shape
    tr = _row_tile(lp)

    def body(f_ref, h_ref, gp_ref, gn_ref, ho_ref, a_ref):
        fv = f_ref[...]
        ho = h_ref[...] + coef * (fv * _rstd(fv) * gp_ref[...])
        ho_ref[...] = ho
        a_ref[...] = (ho * _rstd(ho) * gn_ref[...]).astype(BF16)

    return pl.pallas_call(
        _skip_behind(after, body), name=name, grid=(lp // tr,),
        in_specs=_behind_specs(after) + [_row_spec(tr, d), _row_spec(tr, d), _vec_spec(d), _vec_spec(d)],
        out_specs=[_row_spec(tr, d), _row_spec(tr, d)],
        out_shape=[jax.ShapeDtypeStruct((lp, d), F32), jax.ShapeDtypeStruct((lp, d), BF16)],
        compiler_params=_cparams(("parallel",)),
    )(*_behind(after), f, h_in, g_post, g_next)


def post_loss(f, h_in, g_post, coef, target):
    lp, d = f.shape
    tr = _row_tile(lp)
    front_tiles = FRONT // tr

    def body(f_ref, h_ref, gp_ref, t_ref, dh_ref, loss_ref):
        i = pl.program_id(0)

        @pl.when(i == 0)
        def _():
            loss_ref[...] = jnp.zeros_like(loss_ref)

        @pl.when(i < front_tiles)
        def _():
            dh_ref[...] = jnp.zeros_like(dh_ref)

        @pl.when(i >= front_tiles)
        def _():
            fv = f_ref[...]
            ho = h_ref[...] + coef * (fv * _rstd(fv) * gp_ref[...])
            err = ho - t_ref[...]
            dh_ref[...] = err / d
            tok = jnp.mean(err * err, axis=-1, keepdims=True)
            loss_ref[...] += 0.5 * jnp.sum(tok)

    return pl.pallas_call(
        body, name="post_loss", grid=(lp // tr,),
        in_specs=[_row_spec(tr, d), _row_spec(tr, d), _vec_spec(d),
                  pl.BlockSpec((tr, d), lambda i: (jnp.maximum(i - front_tiles, 0), 0))],
        out_specs=[_row_spec(tr, d), pl.BlockSpec((8, LANE), lambda i: (0, 0))],
        out_shape=[jax.ShapeDtypeStruct((lp, d), F32), jax.ShapeDtypeStruct((8, LANE), F32)],
        compiler_params=_cparams(("arbitrary",)),
    )(f, h_in, g_post, target)


def post_bwd(dh_out, f, g_post, coef):
    lp, d = f.shape
    tr = _row_tile(lp)

    def body(dh_ref, f_ref, gp_ref, df_ref, dg_ref):
        @pl.when(pl.program_id(0) == 0)
        def _():
            dg_ref[...] = jnp.zeros_like(dg_ref)

        df, dg = _norm_bwd(coef * dh_ref[...], f_ref[...], gp_ref[...])
        df_ref[...] = df.astype(BF16)
        dg_ref[...] += dg

    return pl.pallas_call(
        body, name="post_bwd", grid=(lp // tr,),
        in_specs=[_row_spec(tr, d), _row_spec(tr, d), _vec_spec(d)],
        out_specs=[_row_spec(tr, d), _vec_spec(d)],
        out_shape=[jax.ShapeDtypeStruct((lp, d), BF16), jax.ShapeDtypeStruct((1, d), F32)],
        compiler_params=_cparams(("arbitrary",)),
    )(dh_out, f, g_post)


def pre_post_bwd(name, da, h_mid, g_pre, dh_out, f_prev, g_post_prev, coef_prev, after=None):
    lp, d = da.shape
    tr = _row_tile(lp)

    def body(da_ref, h_ref, gpre_ref, dho_ref, f_ref, gpost_ref, dh_ref, df_ref, dgpre_ref, dgpost_ref):
        @pl.when(pl.program_id(0) == 0)
        def _():
            dgpre_ref[...] = jnp.zeros_like(dgpre_ref)
            dgpost_ref[...] = jnp.zeros_like(dgpost_ref)

        dx, dgpre = _norm_bwd(da_ref[...], h_ref[...], gpre_ref[...])
        dh = dho_ref[...] + dx
        dh_ref[...] = dh
        dgpre_ref[...] += dgpre
        df, dgpost = _norm_bwd(coef_prev * dh, f_ref[...], gpost_ref[...])
        df_ref[...] = df.astype(BF16)
        dgpost_ref[...] += dgpost

    return pl.pallas_call(
        _skip_behind(after, body), name=name, grid=(lp // tr,),
        in_specs=_behind_specs(after) + [_row_spec(tr, d), _row_spec(tr, d), _vec_spec(d), _row_spec(tr, d),
                                         _row_spec(tr, d), _vec_spec(d)],
        out_specs=[_row_spec(tr, d), _row_spec(tr, d), _vec_spec(d), _vec_spec(d)],
        out_shape=[jax.ShapeDtypeStruct((lp, d), F32), jax.ShapeDtypeStruct((lp, d), BF16),
                   jax.ShapeDtypeStruct((1, d), F32), jax.ShapeDtypeStruct((1, d), F32)],
        compiler_params=_cparams(("arbitrary",)),
    )(*_behind(after), da, h_mid, g_pre, dh_out, f_prev, g_post_prev)


def pre_bwd_first(da, h0, g_pre, dh_out, after=None):
    lp, d = da.shape
    tr = _row_tile(lp)
    front_tiles = FRONT // tr
    assert front_tiles == 1

    def body(da_ref, h_ref, gpre_ref, dho_ref, gx_ref, front_ref, dgpre_ref):
        i = pl.program_id(0)

        @pl.when(i == 0)
        def _():
            dgpre_ref[...] = jnp.zeros_like(dgpre_ref)

        dx, dgpre = _norm_bwd(da_ref[...], h_ref[...], gpre_ref[...])
        dh = dho_ref[...] + dx
        dgpre_ref[...] += dgpre
        gx_ref[...] = dh

        @pl.when(i == 0)
        def _():
            front_ref[...] = dh

    return pl.pallas_call(
        _skip_behind(after, body), name="pre_bwd_first", grid=(lp // tr,),
        in_specs=_behind_specs(after) + [_row_spec(tr, d), _row_spec(tr, d), _vec_spec(d), _row_spec(tr, d)],
        out_specs=[pl.BlockSpec((tr, d), lambda i: (jnp.maximum(i - front_tiles, 0), 0)),
                   pl.BlockSpec((tr, d), lambda i: (0, 0)), _vec_spec(d)],
        out_shape=[jax.ShapeDtypeStruct((lp - FRONT, d), F32), jax.ShapeDtypeStruct((tr, d), F32),
                   jax.ShapeDtypeStruct((1, d), F32)],
        compiler_params=_cparams(("arbitrary",)),
    )(*_behind(after), da, h0, g_pre, dh_out)


def _m_tile(lp):
    return _pick(lp, (1056, 512, 256, 128))


def ffn_gu(tag, a, wgu_p):
    lp, d = a.shape
    f2 = wgu_p.shape[1]

    def epi(acc):
        g, u = acc[:, :GU_TILE], acc[:, GU_TILE:]
        return acc, g * jax.nn.sigmoid(g) * u

    tg = _pick(lp, (384, 256, 128))
    return _mm(tag + "_gu", "nn", a, wgu_p, tg, 2 * GU_TILE,
               [((lp, f2), F32, (tg, 2 * GU_TILE), lambda i, j: (i, j)),
                ((lp, f2 // 2), BF16, (tg, GU_TILE), lambda i, j: (i, j))], epi=epi, n_outer=True)


def ffn_down(tag, s, wd):
    return _mm_plain(tag + "_down", "nn", s, wd, _m_tile(s.shape[0]), 512, F32)


def ffn_bwd(tag, df, a, gu, s, wgu_p, wd, on_dwd, on_dwgu, after=None):
    lp, d = df.shape
    f2 = wgu_p.shape[1]
    tm = _m_tile(lp)

    def epi(acc, gu_t):
        g, u = gu_t[:, :GU_TILE], gu_t[:, GU_TILE:]
        sig = jax.nn.sigmoid(g)
        dg = acc * u * (sig * (1.0 + g * (1.0 - sig)))
        du = acc * (g * sig)
        return (jnp.concatenate([dg, du], axis=1),)

    ts = _pick(lp, (528, 256, 128))
    dgu = _mm(tag + "_ds", "nt", df, wd, ts, GU_TILE,
              [((lp, f2), BF16, (ts, 2 * GU_TILE), lambda i, j: (i, j))], epi=epi,
              extras=[(gu, (ts, 2 * GU_TILE), lambda i, j: (i, j))], n_outer=True, after=after)[0]
    dwd = _mm_plain(tag + "_dwd", "tn", s, df, 512, _pick(d, (1024,)), BF16, after=after)
    dwgu = _mm_plain(tag + "_dwgu", "tn", a, dgu, _pick(d, (1024,)), 1024, BF16, after=on_dwd(dwd))
    return _mm_plain(tag + "_da", "nt", dgu, wgu_p, _pick(lp, (528, 256, 128)), 256, F32, after=on_dwgu(dwgu, dwd))


Z_U, Z_CQ, Z_CKV, Z_GP, Z_GM, Z_KR, Z_COLS = 0, 1024, 1536, 2048, 4096, 6144, 6400
POOL_W = POOL_GROUP * len(POOL_WINDOWS)
HALO = 16


def _pool_counts(lp, w):
    pos = lax.broadcasted_iota(jnp.int32, (lp, 1), 0) - PAD_ROWS
    return jnp.clip(pos + 1, 1, w).astype(F32)


def _pool_diff(u_ref, pad_ref, lp, w):
    pad_ref[pl.ds(0, HALO), :] = jnp.zeros((HALO, POOL_GROUP), F32)
    pad_ref[pl.ds(HALO, lp), :] = u_ref[...]
    acc = pad_ref[pl.ds(HALO, lp), :]
    for s in range(1, w):
        acc = acc + pad_ref[pl.ds(HALO - s, lp), :]
    return acc / _pool_counts(lp, w) - u_ref[...]


def pool_fwd(z, pool_w, pool_scale):
    lp = z.shape[0]
    ng = len(POOL_WINDOWS)

    def body(u_ref, w_ref, sc_ref, o_ref, pad_ref):
        for g, w in enumerate(POOL_WINDOWS):
            @pl.when(pl.program_id(0) == g)
            def _(w=w):
                dd = _pool_diff(u_ref, pad_ref, lp, w)
                y = jnp.dot(dd.astype(BF16), w_ref[0], preferred_element_type=F32)
                o_ref[...] = (y * sc_ref[...]).astype(BF16)

    return pl.pallas_call(
        body, name="pool_fwd", grid=(ng,),
        in_specs=[pl.BlockSpec((lp, POOL_GROUP), lambda g: (0, g)),
                  pl.BlockSpec((1, POOL_GROUP, POOL_GROUP), lambda g: (g, 0, 0)),
                  pl.BlockSpec((1, POOL_GROUP), lambda g: (0, g))],
        out_specs=pl.BlockSpec((lp, POOL_GROUP), lambda g: (0, g)),
        out_shape=jax.ShapeDtypeStruct((lp, POOL_W), BF16),
        scratch_shapes=[pltpu.VMEM((lp + HALO, POOL_GROUP), F32)],
        compiler_params=_cparams(("parallel",)),
    )(z, pool_w, pool_scale)


def pool_bwd(dyp, z, pool_w, pool_scale):
    lp = z.shape[0]
    ng = len(POOL_WINDOWS)

    def body(dy_ref, u_ref, w_ref, sc_ref, du_ref, dw_ref, dsc_ref, pad_ref):
        for g, w in enumerate(POOL_WINDOWS):
            @pl.when(pl.program_id(0) == g)
            def _(w=w):
                dd = _pool_diff(u_ref, pad_ref, lp, w).astype(BF16)
                wg = w_ref[0]
                ypre = jnp.dot(dd, wg, preferred_element_type=F32)
                dy = dy_ref[...]
                dsc_ref[...] = jnp.sum(dy * ypre, axis=0, keepdims=True)
                dypre = (dy * sc_ref[...]).astype(BF16)
                dw_ref[0] = lax.dot_general(dd, dypre, _DIMS["tn"], preferred_element_type=F32)
                ddd = lax.dot_general(dypre, wg, _DIMS["nt"], preferred_element_type=F32)
                pad_ref[pl.ds(0, lp), :] = ddd / _pool_counts(lp, w)
                pad_ref[pl.ds(lp, HALO), :] = jnp.zeros((HALO, POOL_GROUP), F32)
                acc = -ddd
                for s in range(w):
                    acc = acc + pad_ref[pl.ds(s, lp), :]
                du_ref[...] = acc.astype(BF16)

    return pl.pallas_call(
        body, name="pool_bwd", grid=(ng,),
        in_specs=[pl.BlockSpec((lp, POOL_GROUP), lambda g: (0, g)),
                  pl.BlockSpec((lp, POOL_GROUP), lambda g: (0, g)),
                  pl.BlockSpec((1, POOL_GROUP, POOL_GROUP), lambda g: (g, 0, 0)),
                  pl.BlockSpec((1, POOL_GROUP), lambda g: (0, g))],
        out_specs=[pl.BlockSpec((lp, POOL_GROUP), lambda g: (0, g)),
                   pl.BlockSpec((1, POOL_GROUP, POOL_GROUP), lambda g: (g, 0, 0)),
                   pl.BlockSpec((1, POOL_GROUP), lambda g: (0, g))],
        out_shape=[jax.ShapeDtypeStruct((lp, POOL_W), BF16),
                   jax.ShapeDtypeStruct((ng, POOL_GROUP, POOL_GROUP), F32),
                   jax.ShapeDtypeStruct((1, POOL_W), F32)],
        scratch_shapes=[pltpu.VMEM((lp + HALO, POOL_GROUP), F32)],
        compiler_params=_cparams(("parallel",)),
    )(dyp, z, pool_w, pool_scale)


Q_COLS = HEADS * HEAD_W
ROPE_BLOCK = LANE


def rope_tables(lp):
    pos = jnp.maximum(jnp.arange(lp, dtype=F32) - PAD_ROWS, 0.0)
    inv = ROPE_THETA ** (-jnp.arange(0, QK_ROPE, 2, dtype=F32) / QK_ROPE)
    ang = pos[:, None] * inv[None, :]
    cos, sin, zero = jnp.cos(ang), jnp.sin(ang), jnp.zeros_like(ang)
    return jnp.stack([jnp.concatenate([cos, cos, zero, zero], axis=1),
                      jnp.concatenate([-sin, zero, zero, zero], axis=1),
                      jnp.concatenate([zero, sin, zero, zero], axis=1)])


def _rope(x, tabs):
    return x * tabs[0] + pltpu.roll(x, 96, 1) * tabs[1] + pltpu.roll(x, 32, 1) * tabs[2]


def _rope_bwd(g, tabs):
    return g * tabs[0] + pltpu.roll(g * tabs[1], 32, 1) + pltpu.roll(g * tabs[2], 96, 1)


def _tab_spec(tr):
    return pl.BlockSpec((3, tr, ROPE_BLOCK), lambda i: (0, i, 0))


def mla_prep(z, g_q, g_kv, tabs):
    lp = z.shape[0]
    tr = _row_tile(lp)
    r = g_q.shape[1]

    def body(cq_ref, ckv_ref, kr_ref, gq_ref, gkv_ref, tab_ref, qn_ref, kvn_ref, kpe_ref):
        cq, ckv = cq_ref[...], ckv_ref[...]
        qn_ref[...] = (cq * _rstd(cq) * gq_ref[...]).astype(BF16)
        kvn_ref[...] = (ckv * _rstd(ckv) * gkv_ref[...]).astype(BF16)
        kpe_ref[...] = _rope(kr_ref[...], tab_ref[...]).astype(BF16)

    return pl.pallas_call(
        body, name="mla_prep", grid=(lp // tr,),
        in_specs=[_row_spec(tr, r, Z_CQ // r), _row_spec(tr, r, Z_CKV // r), _row_spec(tr, ROPE_BLOCK, Z_KR // ROPE_BLOCK),
                  _vec_spec(r), _vec_spec(r), _tab_spec(tr)],
        out_specs=[_row_spec(tr, r), _row_spec(tr, r), _row_spec(tr, ROPE_BLOCK)],
        out_shape=[jax.ShapeDtypeStruct((lp, r), BF16), jax.ShapeDtypeStruct((lp, r), BF16),
                   jax.ShapeDtypeStruct((lp, ROPE_BLOCK), BF16)],
        compiler_params=_cparams(("parallel",)),
    )(z, z, z, g_q, g_kv, tabs)


def q_proj(qn, wq_p, tabs):
    lp = qn.shape[0]
    tm = _m_tile(lp)
    tn = 4 * HEAD_W

    def epi(acc, tab):
        parts = []
        for t in range(tn // HEAD_W):
            parts.append(acc[:, t * HEAD_W:t * HEAD_W + QK_NOPE])
            parts.append(_rope(acc[:, t * HEAD_W + QK_NOPE:(t + 1) * HEAD_W], tab))
        return (jnp.concatenate(parts, axis=1),)

    return _mm("q_proj", "nn", qn, wq_p, tm, tn, [((lp, Q_COLS), BF16, (tm, tn), lambda i, j: (i, j))], epi=epi,
               extras=[(tabs, (3, tm, ROPE_BLOCK), lambda i, j: (0, i, 0))])[0]


def kv_proj(kvn, wkv):
    return _mm_plain("kv_proj", "nn", kvn, wkv, _m_tile(kvn.shape[0]), 1024, BF16)


def q_rope_bwd(dq, tabs):
    lp = dq.shape[0]
    tr = _row_tile(lp)

    def body(dq_ref, tab_ref, o_ref):
        tab = tab_ref[...]
        for h in range(HEADS):
            o_ref[:, h * HEAD_W:h * HEAD_W + QK_NOPE] = dq_ref[:, h * HEAD_W:h * HEAD_W + QK_NOPE].astype(BF16)
            o_ref[:, h * HEAD_W + QK_NOPE:(h + 1) * HEAD_W] = _rope_bwd(
                dq_ref[:, h * HEAD_W + QK_NOPE:(h + 1) * HEAD_W], tab).astype(BF16)

    return pl.pallas_call(
        body, name="q_rope_bwd", grid=(lp // tr,),
        in_specs=[_row_spec(tr, Q_COLS), _tab_spec(tr)], out_specs=_row_spec(tr, Q_COLS),
        out_shape=jax.ShapeDtypeStruct((lp, Q_COLS), BF16), compiler_params=_cparams(("parallel",)),
    )(dq, tabs)


def mla_prep_bwd(dqn, dkvn, dkpe_h, z, g_q, g_kv, tabs):
    lp = z.shape[0]
    tr = _row_tile(lp)
    r = g_q.shape[1]

    def body(dqn_ref, dkvn_ref, dkpe_ref, cq_ref, ckv_ref, gq_ref, gkv_ref, tab_ref,
             dcq_ref, dckv_ref, dkr_ref, dgq_ref, dgkv_ref):
        @pl.when(pl.program_id(0) == 0)
        def _():
            dgq_ref[...] = jnp.zeros_like(dgq_ref)
            dgkv_ref[...] = jnp.zeros_like(dgkv_ref)

        dcq, dgq = _norm_bwd(dqn_ref[...], cq_ref[...], gq_ref[...])
        dckv, dgkv = _norm_bwd(dkvn_ref[...], ckv_ref[...], gkv_ref[...])
        dcq_ref[...] = dcq.astype(BF16)
        dckv_ref[...] = dckv.astype(BF16)
        dgq_ref[...] += dgq
        dgkv_ref[...] += dgkv
        dkpe = dkpe_ref[0]
        for h in range(1, HEADS):
            dkpe = dkpe + dkpe_ref[h]
        dkr_ref[...] = _rope_bwd(dkpe, tab_ref[...]).astype(BF16)

    return pl.pallas_call(
        body, name="mla_prep_bwd", grid=(lp // tr,),
        in_specs=[_row_spec(tr, r), _row_spec(tr, r), pl.BlockSpec((HEADS, tr, ROPE_BLOCK), lambda i: (0, i, 0)),
                  _row_spec(tr, r, Z_CQ // r), _row_spec(tr, r, Z_CKV // r), _vec_spec(r), _vec_spec(r), _tab_spec(tr)],
        out_specs=[_row_spec(tr, r), _row_spec(tr, r), _row_spec(tr, ROPE_BLOCK), _vec_spec(r), _vec_spec(r)],
        out_shape=[jax.ShapeDtypeStruct((lp, r), BF16), jax.ShapeDtypeStruct((lp, r), BF16),
                   jax.ShapeDtypeStruct((lp, ROPE_BLOCK), BF16),
                   jax.ShapeDtypeStruct((1, r), F32), jax.ShapeDtypeStruct((1, r), F32)],
        compiler_params=_cparams(("arbitrary",)),
    )(dqn, dkvn, dkpe_h, z, z, g_q, g_kv, tabs)


def _attn_tile(lp):
    return _pick(lp, (528, 128))


def _scores(q, kcat, q_tile, k_tile, t, masked):
    s = lax.dot_general(q, kcat, _DIMS["nt"], preferred_element_type=F32) * SOFTMAX_SCALE
    if not masked:
        return s
    qpos = q_tile * t + lax.broadcasted_iota(jnp.int32, (t, t), 0)
    kpos = k_tile * t + lax.broadcasted_iota(jnp.int32, (t, t), 1)
    return jnp.where((kpos <= qpos) & (kpos >= PAD_ROWS), s, jnp.float32(-1e30))


def _causal_pairs(nt, k_major):
    if k_major:
        pairs = [(qi, ki) for ki in range(nt) for qi in range(ki, nt)]
    else:
        pairs = [(qi, ki) for qi in range(nt) for ki in range(qi + 1)]
    return (jnp.asarray([p[0] for p in pairs], jnp.int32), jnp.asarray([p[1] for p in pairs], jnp.int32))


def _on_masked_or_not(q_tile, k_tile, fn):
    needs_mask = (q_tile == k_tile) | (k_tile == 0)

    @pl.when(needs_mask)
    def _():
        fn(True)

    @pl.when(jnp.logical_not(needs_mask))
    def _():
        fn(False)


def flash_fwd(q, kv, kpe):
    lp = q.shape[0]
    t = _attn_tile(lp)
    q_tab, k_tab = _causal_pairs(lp // t, k_major=False)

    def body(q_tab_ref, k_tab_ref, q_ref, kv_ref, kpe_ref, o32_ref, o16_ref, lse_ref, m_sc, l_sc, acc_sc):
        pair = pl.program_id(1)
        qi, ki = q_tab_ref[pair], k_tab_ref[pair]

        @pl.when(ki == 0)
        def _():
            m_sc[...] = jnp.full_like(m_sc, -jnp.inf)
            l_sc[...] = jnp.zeros_like(l_sc)
            acc_sc[...] = jnp.zeros_like(acc_sc)

        def step(masked):
            kvt = kv_ref[...]
            kcat = jnp.concatenate([kvt[:, :QK_NOPE], kpe_ref[...]], axis=1)
            s = _scores(q_ref[...], kcat, qi, ki, t, masked)
            m_prev = m_sc[...]
            m_new = jnp.maximum(m_prev, jnp.max(s, axis=1, keepdims=True))
            alpha = jnp.exp(m_prev - m_new)
            p = jnp.exp(s - m_new[:, :1])
            l_sc[...] = alpha * l_sc[...] + jnp.sum(p, axis=1, keepdims=True)
            acc_sc[...] = alpha * acc_sc[...] + jnp.dot(p.astype(BF16), kvt[:, QK_NOPE:], preferred_element_type=F32)
            m_sc[...] = m_new

        _on_masked_or_not(qi, ki, step)

        @pl.when(ki == qi)
        def _():
            l = l_sc[...]
            o = acc_sc[...] / l
            o32_ref[...] = o
            o16_ref[...] = o.astype(BF16)
            lse_ref[0] = m_sc[...] + jnp.log(l)

    return pl.pallas_call(
        body, name="flash_fwd",
        grid_spec=pltpu.PrefetchScalarGridSpec(
            num_scalar_prefetch=2, grid=(HEADS, q_tab.shape[0]),
            in_specs=[pl.BlockSpec((t, HEAD_W), lambda h, p, qt, kt: (qt[p], h)),
                      pl.BlockSpec((t, HEAD_W), lambda h, p, qt, kt: (kt[p], h)),
                      pl.BlockSpec((t, ROPE_BLOCK), lambda h, p, qt, kt: (kt[p], 0))],
            out_specs=[pl.BlockSpec((t, V_DIM), lambda h, p, qt, kt: (qt[p], h)),
                       pl.BlockSpec((t, V_DIM), lambda h, p, qt, kt: (qt[p], h)),
                       pl.BlockSpec((1, t, LANE), lambda h, p, qt, kt: (h, qt[p], 0))],
            scratch_shapes=[pltpu.VMEM((t, LANE), F32), pltpu.VMEM((t, LANE), F32), pltpu.VMEM((t, V_DIM), F32)]),
        out_shape=[jax.ShapeDtypeStruct((lp, HEADS * V_DIM), F32), jax.ShapeDtypeStruct((lp, HEADS * V_DIM), BF16),
                   jax.ShapeDtypeStruct((HEADS, lp, LANE), F32)],
        compiler_params=_cparams(("parallel", "arbitrary")),
    )(q_tab, k_tab, q, kv, kpe)


def flash_bwd(q, kv, kpe, o32, lse, do):
    lp = q.shape[0]
    t = _attn_tile(lp)
    nt = lp // t
    q_tab, k_tab = _causal_pairs(nt, k_major=True)

    def body(q_tab_ref, k_tab_ref, q_ref, kv_ref, kpe_ref, o_ref, lse_ref, do_ref, dq_ref, dkv_ref, dkpe_ref,
             dk_sc, dv_sc):
        pair = pl.program_id(1)
        qi, ki = q_tab_ref[pair], k_tab_ref[pair]

        @pl.when(pair == 0)
        def _():
            dq_ref[...] = jnp.zeros_like(dq_ref)

        @pl.when(qi == ki)
        def _():
            dk_sc[...] = jnp.zeros_like(dk_sc)
            dv_sc[...] = jnp.zeros_like(dv_sc)

        def step(masked):
            qt = q_ref[...]
            kvt = kv_ref[...]
            kcat = jnp.concatenate([kvt[:, :QK_NOPE], kpe_ref[...]], axis=1)
            s = _scores(qt, kcat, qi, ki, t, masked)
            p = jnp.exp(s - lse_ref[0][:, :1])
            do = do_ref[...]
            delta = jnp.sum(do * o_ref[...], axis=1, keepdims=True)
            do16 = do.astype(BF16)
            dv_sc[...] += lax.dot_general(p.astype(BF16), do16, _DIMS["tn"], preferred_element_type=F32)
            dp = lax.dot_general(do16, kvt[:, QK_NOPE:], _DIMS["nt"], preferred_element_type=F32)
            ds = (p * (dp - delta) * SOFTMAX_SCALE).astype(BF16)
            dk_sc[...] += lax.dot_general(ds, qt, _DIMS["tn"], preferred_element_type=F32)
            row = pl.multiple_of(qi * t, t)
            dq_ref[pl.ds(row, t), :] += jnp.dot(ds, kcat, preferred_element_type=F32)

        _on_masked_or_not(qi, ki, step)

        @pl.when(qi == nt - 1)
        def _():
            dk = dk_sc[...]
            dkv_ref[...] = jnp.concatenate([dk[:, :QK_NOPE], dv_sc[...]], axis=1).astype(BF16)
            dkpe_ref[0] = dk[:, QK_NOPE:]

    qmap = lambda h, p, qt, kt: (qt[p], h)
    return pl.pallas_call(
        body, name="flash_bwd",
        grid_spec=pltpu.PrefetchScalarGridSpec(
            num_scalar_prefetch=2, grid=(HEADS, q_tab.shape[0]),
            in_specs=[pl.BlockSpec((t, HEAD_W), qmap),
                      pl.BlockSpec((t, HEAD_W), lambda h, p, qt, kt: (kt[p], h)),
                      pl.BlockSpec((t, ROPE_BLOCK), lambda h, p, qt, kt: (kt[p], 0)),
                      pl.BlockSpec((t, V_DIM), qmap),
                      pl.BlockSpec((1, t, LANE), lambda h, p, qt, kt: (h, qt[p], 0)),
                      pl.BlockSpec((t, V_DIM), qmap)],
            out_specs=[pl.BlockSpec((lp, HEAD_W), lambda h, p, qt, kt: (0, h)),
                       pl.BlockSpec((t, HEAD_W), lambda h, p, qt, kt: (kt[p], h)),
                       pl.BlockSpec((1, t, ROPE_BLOCK), lambda h, p, qt, kt: (h, kt[p], 0))],
            scratch_shapes=[pltpu.VMEM((t, HEAD_W), F32), pltpu.VMEM((t, V_DIM), F32)]),
        out_shape=[jax.ShapeDtypeStruct((lp, Q_COLS), F32), jax.ShapeDtypeStruct((lp, Q_COLS), BF16),
                   jax.ShapeDtypeStruct((HEADS, lp, ROPE_BLOCK), F32)],
        compiler_params=_cparams(("parallel", "arbitrary")),
    )(q_tab, k_tab, q, kv, kpe, o32, lse, do)


def _ij(i, j):
    return (i, j)


def mixer_fwd(a2, w, tabs, pool_scale, g_q, g_kv, ex):
    lp, d = a2.shape
    tm = _m_tile(lp)
    tn = 512
    z = _mm_plain("mix_in", "nn", a2, w["w_in"], tm, 1280, F32)
    yp = pool_fwd(z, w["pool_w"], pool_scale)
    qn, kvn, kpe = mla_prep(z, g_q, g_kv, tabs)
    q = q_proj(qn, w["w_q_b"], tabs)
    kv = kv_proj(kvn, w["w_kv_b"])
    o32, o16, lse = flash_fwd(q, kv, kpe)
    ex.point("flash_fwd_done", o16)
    y_pool = _mm_plain("pool_out", "nn", yp, w["w_pool_o"], tm, tn, F32, after=ex.token)

    def epi(acc, ypl, gp, gm):
        return jax.nn.sigmoid(gp) * ypl + jax.nn.sigmoid(gm) * acc, acc

    y, y_mla = _mm("mla_out_gate", "nn", o16, w["w_mla_o"], tm, tn,
                   [((lp, d), BF16, (tm, tn), _ij), ((lp, d), F32, (tm, tn), _ij)], epi=epi,
                   extras=[(y_pool, (tm, tn), _ij), (z, (tm, tn), lambda i, j: (i, Z_GP // tn + j)),
                           (z, (tm, tn), lambda i, j: (i, Z_GM // tn + j))])
    m = _mm_plain("mix_out", "nn", y, w["w_out"], tm, tn, F32)
    return m, dict(z=z, yp=yp, qn=qn, kvn=kvn, kpe=kpe, q=q, kv=kv, o32=o32, o16=o16, lse=lse,
                   y_pool=y_pool, y_mla=y_mla, y=y)


def mixer_bwd(dm, a2, sv, w, tabs, pool_scale, g_q, g_kv, after=None):
    lp, d = dm.shape
    tm = _m_tile(lp)
    tn = 512
    z = sv["z"]

    def epi(acc, ypl, yml, gp, gm):
        sp, sm = jax.nn.sigmoid(gp), jax.nn.sigmoid(gm)
        return acc * sp, acc * sm, acc * ypl * (sp * (1.0 - sp)), acc * yml * (sm * (1.0 - sm))

    dyp, dym, dgp, dgm = _mm(
        "gate_bwd", "nt", dm, w["w_out"], tm, tn, [((lp, d), BF16, (tm, tn), _ij)] * 4, epi=epi,
        extras=[(sv["y_pool"], (tm, tn), _ij), (sv["y_mla"], (tm, tn), _ij),
                (z, (tm, tn), lambda i, j: (i, Z_GP // tn + j)), (z, (tm, tn), lambda i, j: (i, Z_GM // tn + j))],
        after=after)
    g = {}
    g["w_out"] = _mm_plain("dw_out", "tn", sv["y"], dm, 1024, 1024, BF16, after=after)
    g["w_pool_o"] = _mm_plain("dw_pool_o", "tn", sv["yp"], dyp, 512, 1024, BF16)
    dypre = _mm_plain("pool_out_bwd", "nt", dyp, w["w_pool_o"], tm, tn, F32)
    du, g["pool_w"], d_pool_scale = pool_bwd(dypre, z, w["pool_w"], pool_scale)
    g["w_mla_o"] = _mm_plain("dw_mla_o", "tn", sv["o16"], dym, 1024, 1024, BF16)
    do = _mm_plain("mla_out_bwd", "nt", dym, w["w_mla_o"], tm, tn, F32)
    dq, dkv, dkpe_h = flash_bwd(sv["q"], sv["kv"], sv["kpe"], sv["o32"], sv["lse"], do)
    dql = q_rope_bwd(dq, tabs)
    g["w_q_b"] = _mm_plain("dw_q_b", "tn", sv["qn"], dql, 512, 1024, BF16)
    dqn = _mm_plain("q_proj_bwd", "nt", dql, w["w_q_b"], tm, 512, F32)
    g["w_kv_b"] = _mm_plain("dw_kv_b", "tn", sv["kvn"], dkv, 512, 1024, BF16)
    dkvn = _mm_plain("kv_proj_bwd", "nt", dkv, w["w_kv_b"], tm, 512, F32)
    dcq, dckv, dkr, d_gq, d_gkv = mla_prep_bwd(dqn, dkvn, dkpe_h, z, g_q, g_kv, tabs)
    dz = jnp.concatenate([du, dcq, dckv, dgp, dgm, dkr, jnp.zeros((lp, Z_COLS - Z_KR - ROPE_BLOCK), BF16)], axis=1)
    g["w_in"] = _mm_plain("dw_in", "tn", a2, dz, 1024, 1280, BF16)
    da2 = _mm_plain("mix_in_bwd", "nt", dz, w["w_in"], tm, tn, F32)
    return da2, g, dict(pool_scale=d_pool_scale, q_a_norm=d_gq, kv_a_norm=d_gkv)


_ANY = pl.BlockSpec(memory_space=pl.ANY)
_MESH = pl.DeviceIdType.MESH


def _my_pos():
    return lax.axis_index("x"), lax.axis_index("y"), lax.axis_index("c")


LEAD = "lead"
COLS = "cols"
COLS_GU = "cols_gu"


def _col_block(layout, dev):
    return dev if layout == COLS else 2 * (dev % 4) + dev // 4


def _dev_block(ref, layout, dev, cols):
    if layout == LEAD:
        return ref.at[dev]
    return ref.at[:, pl.ds(pl.multiple_of(_col_block(layout, dev) * cols, LANE), cols)]


def _gathered_shape(shard_shape, layout):
    if layout == LEAD:
        return (N_DEV, *shard_shape)
    return (shard_shape[0], N_DEV * shard_shape[1])


def all_gather(name, shards, layouts, after=None):
    n = len(shards)
    behind = _behind(after)

    def body(*refs):
        ins, outs = refs[:n], refs[n + len(behind):2 * n + len(behind)]
        send_sems, recv_sems, local_sems = refs[2 * n + len(behind):]
        x, y, c = _my_pos()
        me, sibling = (x, y, c), (x, y, 1 - c)
        chips = [(1 - x, y), (x, 1 - y), (1 - x, 1 - y)]

        def blk(a, px, py, pc):
            return _dev_block(outs[a], layouts[a], 4 * px + 2 * py + pc, shards[a].shape[-1])

        def copy(a, k, block, to, src=None):
            return pltpu.make_async_remote_copy(
                src_ref=blk(a, *block) if src is None else src, dst_ref=blk(a, *block),
                send_sem=send_sems.at[a, k], recv_sem=recv_sems.at[a, k], device_id=to, device_id_type=_MESH)

        mine = [pltpu.make_async_copy(ins[a], blk(a, *me), local_sems.at[a]) for a in range(n)]
        for cp in mine:
            cp.start()
        first = []
        for a in range(n):
            first.append(copy(a, 0, me, sibling, src=ins[a]))
            first += [copy(a, 1 + j, me, (*chip, c), src=ins[a]) for j, chip in enumerate(chips)]
        for cp in first:
            cp.start()
        passed = []
        for j, chip in enumerate(chips):
            for a in range(n):
                copy(a, 1 + j, (*chip, c), me).wait_recv()
                fwd = copy(a, 4 + j, (*chip, c), sibling)
                fwd.start()
                passed.append(fwd)
        for a in range(n):
            copy(a, 0, sibling, me).wait_recv()
            for j, chip in enumerate(chips):
                copy(a, 4 + j, (*chip, 1 - c), me).wait_recv()
        for cp in first + passed:
            cp.wait_send()
        for cp in mine:
            cp.wait()

    return pl.pallas_call(
        body, name=name,
        in_specs=[_ANY] * (n + len(behind)), out_specs=[_ANY] * n,
        out_shape=[jax.ShapeDtypeStruct(_gathered_shape(s.shape, lay), s.dtype)
                   for s, lay in zip(shards, layouts, strict=True)],
        scratch_shapes=[pltpu.SemaphoreType.DMA((n, 7)), pltpu.SemaphoreType.DMA((n, 7)), pltpu.SemaphoreType.DMA((n,))],
    )(*shards, *behind)


def _shard_shape(grad, layout):
    return grad.shape[1:] if layout == LEAD else (grad.shape[0], grad.shape[1] // N_DEV)


def rs_sibling(name, grads, layouts):
    n = len(grads)

    def body(*refs):
        ins, outs = refs[:n], refs[n:2 * n]
        send_sems, recv_sems = refs[2 * n:]
        x, y, c = _my_pos()
        cps = []
        for a in range(n):
            for k in range(4):
                cp = pltpu.make_async_remote_copy(
                    src_ref=_dev_block(ins[a], layouts[a], 2 * k + (1 - c), outs[a].shape[-1]), dst_ref=outs[a].at[k],
                    send_sem=send_sems.at[a, k], recv_sem=recv_sems.at[a, k],
                    device_id=(x, y, 1 - c), device_id_type=_MESH)
                cp.start()
                cps.append(cp)
        for cp in cps:
            cp.wait()

    return pl.pallas_call(
        body, name=name,
        in_specs=[_ANY] * n, out_specs=[_ANY] * n,
        out_shape=[jax.ShapeDtypeStruct((4, *_shard_shape(g, lay)), g.dtype) for g, lay in zip(grads, layouts, strict=True)],
        scratch_shapes=[pltpu.SemaphoreType.DMA((n, 4)), pltpu.SemaphoreType.DMA((n, 4))],
    )(*grads)


def rs_chips(name, sums):
    n = len(sums)

    def body(*refs):
        ins, outs = refs[:n], refs[n:2 * n]
        send_sems, recv_sems = refs[2 * n:]
        x, y, c = _my_pos()
        chips = [(1 - x, y), (x, 1 - y), (1 - x, 1 - y)]
        cps = []
        for a in range(n):
            for j, chip in enumerate(chips):
                cp = pltpu.make_async_remote_copy(
                    src_ref=ins[a].at[2 * chip[0] + chip[1]], dst_ref=outs[a].at[j],
                    send_sem=send_sems.at[a, j], recv_sem=recv_sems.at[a, j],
                    device_id=(*chip, c), device_id_type=_MESH)
                cp.start()
                cps.append(cp)
        for cp in cps:
            cp.wait()

    return pl.pallas_call(
        body, name=name,
        in_specs=[_ANY] * n, out_specs=[_ANY] * n,
        out_shape=[jax.ShapeDtypeStruct((3, *s.shape[1:]), s.dtype) for s in sums],
        scratch_shapes=[pltpu.SemaphoreType.DMA((n, 3)), pltpu.SemaphoreType.DMA((n, 3))],
    )(*sums)


_HBM = pl.BlockSpec(memory_space=pltpu.HBM)
_SEM = pl.BlockSpec(memory_space=pltpu.SEMAPHORE)
_EFFECT = pltpu.SideEffectType.DATAFLOW_SIDE_EFFECTING


def _in_hbm(a):
    return pltpu.with_memory_space_constraint(a, pltpu.HBM)


def split_start(name, bufs, plan, n_copies, after=None):
    nb = len(bufs)
    extra = [] if after is None else [after]

    def body(*refs):
        buf_refs = refs[:nb]
        send_sems, recv_sems = refs[nb + len(extra)], refs[nb + len(extra) + 1]
        token = refs[-1]
        copies = plan(buf_refs)
        assert len(copies) == n_copies
        for k, (src, dst, to) in enumerate(copies):
            pltpu.make_async_remote_copy(src_ref=src, dst_ref=dst, send_sem=send_sems.at[k], recv_sem=recv_sems.at[k],
                                         device_id=to, device_id_type=_MESH).start()
        token[...] = jnp.zeros_like(token)

    out = pl.pallas_call(
        body, name=name,
        out_shape=(pltpu.SemaphoreType.DMA((n_copies,)), pltpu.SemaphoreType.DMA((n_copies,)),
                   *[pltpu.HBM(b.shape, b.dtype) for b in bufs], jax.ShapeDtypeStruct((8, LANE), F32)),
        in_specs=[_HBM] * nb + [_ANY] * len(extra),
        out_specs=(_SEM, _SEM, *[_HBM] * nb, pl.BlockSpec(memory_space=pltpu.VMEM)),
        input_output_aliases={i: 2 + i for i in range(nb)},
        compiler_params=pltpu.CompilerParams(has_side_effects=_EFFECT),
    )(*[_in_hbm(b) for b in bufs], *extra)
    return out[0], out[1], list(out[2:2 + nb]), out[-1]


def split_wait(name, bufs, send_sems, recv_sems, plan, after):
    nb = len(bufs)

    def body(*refs):
        buf_refs = refs[:nb]
        s_sems, r_sems = refs[nb], refs[nb + 1]
        for k, (src, dst, to) in enumerate(plan(buf_refs)):
            cp = pltpu.make_async_remote_copy(src_ref=src, dst_ref=dst, send_sem=s_sems.at[k], recv_sem=r_sems.at[k],
                                              device_id=to, device_id_type=_MESH)
            cp.wait_send()
            cp.wait_recv()

    out = pl.pallas_call(
        body, name=name,
        out_shape=tuple(pltpu.HBM(b.shape, b.dtype) for b in bufs),
        in_specs=[_HBM] * nb + [_SEM, _SEM, _ANY],
        out_specs=tuple([_HBM] * nb),
        input_output_aliases={i: i for i in range(nb)},
        compiler_params=pltpu.CompilerParams(has_side_effects=_EFFECT),
    )(*bufs, send_sems, recv_sems, after)
    return list(out)


def _ag_own_plan(shapes, layouts):
    n = len(shapes)

    def plan(refs):
        x, y, c = _my_pos()
        targets = [(x, y, 1 - c), (1 - x, y, c), (x, 1 - y, c), (1 - x, 1 - y, c)]
        out = []
        for a in range(n):
            blk = _dev_block(refs[a], layouts[a], 4 * x + 2 * y + c, shapes[a][-1])
            out += [(blk, blk, to) for to in targets]
        return out

    return plan, 4 * n


def _ag_pass_plan(shapes, layouts):
    n = len(shapes)

    def plan(refs):
        x, y, c = _my_pos()
        out = []
        for a in range(n):
            for px, py in [(1 - x, y), (x, 1 - y), (1 - x, 1 - y)]:
                blk = _dev_block(refs[a], layouts[a], 4 * px + 2 * py + c, shapes[a][-1])
                out.append((blk, blk, (x, y, 1 - c)))
        return out

    return plan, 3 * n


def _rs_sibling_plan(layouts, n):
    def plan(refs):
        x, y, c = _my_pos()
        return [(_dev_block(refs[a], layouts[a], 2 * k + (1 - c), refs[n + a].shape[-1]), refs[n + a].at[k], (x, y, 1 - c))
                for a in range(n) for k in range(4)]

    return plan, 4 * n


def _rs_chips_plan(n):
    def plan(refs):
        x, y, c = _my_pos()
        return [(refs[a].at[2 * px + py], refs[n + a].at[j], (px, py, c))
                for a in range(n) for j, (px, py) in enumerate([(1 - x, y), (x, 1 - y), (1 - x, 1 - y)])]

    return plan, 3 * n


def place_own(name, shard, layout, dtype, dev, after):
    r, c = shard.shape
    tr = _ew_rows(r, c)
    if layout == LEAD:
        o_spec = pl.BlockSpec((None, tr, c), lambda i, dev_ref: (dev_ref[0], i, 0))
    else:
        o_spec = pl.BlockSpec((tr, c), lambda i, dev_ref: (i, _col_block(layout, dev_ref[0])))
    extra = [] if after is None else [after]

    def body(dev_ref, s_ref, *rest):
        rest[-1][...] = s_ref[...].astype(dtype)

    return pl.pallas_call(
        body, name=name,
        grid_spec=pltpu.PrefetchScalarGridSpec(
            num_scalar_prefetch=1, grid=(r // tr,),
            in_specs=[pl.BlockSpec((tr, c), lambda i, dev_ref: (i, 0))] + [_ANY] * len(extra),
            out_specs=o_spec),
        out_shape=jax.ShapeDtypeStruct(_gathered_shape(shard.shape, layout), dtype),
        compiler_params=_cparams(("parallel",)),
    )(dev, shard, *extra)


def _ew_rows(r, c):
    for t in (512, 256, 128, 64, 32, 16):
        if r % t == 0 and t * c * 4 <= 768 * 1024:
            return t
    raise ValueError((r, c))


def rs_add(name, grad, layout, recv, core):
    _, r, c = recv.shape
    tr = _ew_rows(r, c)
    if layout == LEAD:
        g_spec = pl.BlockSpec((None, tr, c), lambda k, i, core_ref: (2 * k + core_ref[0], i, 0))
    else:
        g_spec = pl.BlockSpec((tr, c), lambda k, i, core_ref: (i, _col_block(layout, 2 * k + core_ref[0])))

    def body(core_ref, g_ref, r_ref, o_ref):
        o_ref[...] = (g_ref[...].astype(F32) + r_ref[...].astype(F32)).astype(BF16)

    return pl.pallas_call(
        body, name=name,
        grid_spec=pltpu.PrefetchScalarGridSpec(
            num_scalar_prefetch=1, grid=(4, r // tr),
            in_specs=[g_spec, pl.BlockSpec((None, tr, c), lambda k, i, core_ref: (k, i, 0))],
            out_specs=pl.BlockSpec((None, tr, c), lambda k, i, core_ref: (k, i, 0))),
        out_shape=jax.ShapeDtypeStruct((4, r, c), BF16),
        compiler_params=_cparams(("parallel", "parallel")),
    )(core, grad, recv)


def _adamw(w, g, m, v):
    m = ADAM_B1 * m + (1.0 - ADAM_B1) * g
    v = ADAM_B2 * v + (1.0 - ADAM_B2) * jnp.square(g)
    m_hat = m / (1.0 - ADAM_B1 ** ADAM_STEP)
    v_hat = v / (1.0 - ADAM_B2 ** ADAM_STEP)
    delta = -ADAM_LR * (m_hat / (jnp.sqrt(v_hat) + ADAM_EPS) + ADAM_WD * w)
    return delta, m, v


def adamw_shard(name, w, m, v, sums, recv, chip):
    r, c = w.shape
    tr = _ew_rows(r, c)

    def body(chip_ref, w_ref, m_ref, v_ref, s_ref, r_ref, g_ref, d_ref, mo_ref, vo_ref):
        g = s_ref[0].astype(F32)
        for j in range(3):
            g = g + r_ref[j].astype(F32)
        d, mn, vn = _adamw(w_ref[...], g, m_ref[...], v_ref[...])
        g_ref[...] = g
        d_ref[...] = d
        mo_ref[...] = mn
        vo_ref[...] = vn

    spec = pl.BlockSpec((tr, c), lambda i, chip_ref: (i, 0))
    return pl.pallas_call(
        body, name=name,
        grid_spec=pltpu.PrefetchScalarGridSpec(
            num_scalar_prefetch=1, grid=(r // tr,),
            in_specs=[spec, spec, spec,
                      pl.BlockSpec((1, tr, c), lambda i, chip_ref: (chip_ref[0], i, 0)),
                      pl.BlockSpec((3, tr, c), lambda i, chip_ref: (0, i, 0))],
            out_specs=[spec] * 4),
        out_shape=[jax.ShapeDtypeStruct((r, c), F32)] * 4,
        compiler_params=_cparams(("parallel",)),
    )(chip, w, m, v, sums, recv)


def reduce_small(gathered):
    _, r, c = gathered.shape

    def body(g_ref, o_ref):
        acc = g_ref[0]
        for k in range(1, N_DEV):
            acc = acc + g_ref[k]
        o_ref[...] = acc

    return pl.pallas_call(body, name="reduce_small", out_shape=jax.ShapeDtypeStruct((r, c), F32))(gathered)


def adamw_small(ws, gs, ms, vs):
    n = len(ws)

    def body(*refs):
        w_r, g_r, m_r, v_r = refs[:n], refs[n:2 * n], refs[2 * n:3 * n], refs[3 * n:4 * n]
        d_o, m_o, v_o = refs[4 * n:5 * n], refs[5 * n:6 * n], refs[6 * n:7 * n]
        for a in range(n):
            d, mn, vn = _adamw(w_r[a][...], g_r[a][...], m_r[a][...], v_r[a][...])
            d_o[a][...] = d
            m_o[a][...] = mn
            v_o[a][...] = vn

    shapes = [jax.ShapeDtypeStruct(w.shape, F32) for w in ws]
    out = pl.pallas_call(body, name="adamw_small", out_shape=shapes * 3)(*ws, *gs, *ms, *vs)
    return out[:n], out[n:2 * n], out[2 * n:]


WEIGHTS = ["meta_tokens", "norm_ffn1_pre", "norm_ffn1_post", "ffn1_w_gu", "ffn1_w_down", "norm_mix_pre",
           "norm_mix_post", "w_in", "pool_w", "pool_scale", "w_pool_o", "q_a_norm", "w_q_b", "kv_a_norm", "w_kv_b",
           "w_mla_o", "w_out", "norm_ffn2_pre", "norm_ffn2_post", "ffn2_w_gu", "ffn2_w_down"]
BIG = ["ffn1_w_gu", "ffn1_w_down", "w_in", "pool_w", "w_pool_o", "w_q_b", "w_kv_b", "w_mla_o", "w_out",
       "ffn2_w_gu", "ffn2_w_down"]
COL_SHARDED = ("w_in", "w_q_b")
GATHERED = {"ffn1_w_gu": COLS_GU, "ffn2_w_gu": COLS_GU, "w_pool_o": COLS, "w_kv_b": COLS}
GAINS =["norm_ffn1_pre", "norm_ffn1_post", "norm_mix_pre", "norm_mix_post", "norm_ffn2_pre", "norm_ffn2_post"]
SMALL = GAINS + ["pool_scale", "q_a_norm", "kv_a_norm"]
Z_SRC = 1024 + 512 + 512 + QK_ROPE


def _full_from_gathered(name, g):
    _, r, c = g.shape
    if name == "pool_w":
        ng = len(POOL_WINDOWS)
        return g.reshape(N_DEV, ng, r // ng, c).transpose(1, 0, 2, 3).reshape(ng, POOL_GROUP, POOL_GROUP)
    if name in COL_SHARDED:
        return g.transpose(1, 0, 2).reshape(r, N_DEV * c)
    return g.reshape(N_DEV * r, c)


def _blocks_from_full(name, dw):
    if name == "pool_w":
        ng = len(POOL_WINDOWS)
        return dw.reshape(ng, N_DEV, POOL_GROUP // N_DEV, POOL_GROUP).transpose(1, 0, 2, 3).reshape(
            N_DEV, ng * POOL_GROUP // N_DEV, POOL_GROUP)
    k, n = dw.shape
    if name in COL_SHARDED:
        return dw.reshape(k, N_DEV, n // N_DEV).transpose(1, 0, 2)
    return dw.reshape(N_DEV, k // N_DEV, n)


def _to_internal(name, w):
    if name == "w_in":
        d = w.shape[0]
        return jnp.concatenate([w[:, :Z_SRC - QK_ROPE], w[:, Z_SRC:], w[:, Z_SRC - QK_ROPE:Z_SRC],
                                jnp.zeros((d, Z_COLS - Z_KR - QK_ROPE), w.dtype)], axis=1)
    if name == "w_q_b":
        r = w.shape[0]
        w3 = w.reshape(r, HEADS, QK_NOPE + QK_ROPE)
        return jnp.pad(w3, ((0, 0), (0, 0), (0, HEAD_W - QK_NOPE - QK_ROPE))).reshape(r, Q_COLS)
    return w


def _from_internal(name, dw):
    if name == "w_in":
        return jnp.concatenate([dw[:, :Z_SRC - QK_ROPE], dw[:, Z_KR:Z_KR + QK_ROPE], dw[:, Z_SRC - QK_ROPE:Z_KR]], axis=1)
    if name == "w_q_b":
        r = dw.shape[0]
        return dw.reshape(r, HEADS, HEAD_W)[:, :, :QK_NOPE + QK_ROPE].reshape(r, HEADS * (QK_NOPE + QK_ROPE))
    return dw


def _shard2d(a):
    return a.reshape(-1, a.shape[-1])


def kernel(x, meta_tokens, norm_ffn1_pre, norm_ffn1_post, ffn1_w_gu, ffn1_w_down, norm_mix_pre, norm_mix_post, w_in, pool_w, pool_scale, w_pool_o, q_a_norm, w_q_b, kv_a_norm, w_kv_b, w_mla_o, w_out, norm_ffn2_pre, norm_ffn2_post, ffn2_w_gu, ffn2_w_down, loss_target, m_meta_tokens, m_norm_ffn1_pre, m_norm_ffn1_post, m_ffn1_w_gu, m_ffn1_w_down, m_norm_mix_pre, m_norm_mix_post, m_w_in, m_pool_w, m_pool_scale, m_w_pool_o, m_q_a_norm, m_w_q_b, m_kv_a_norm, m_w_kv_b, m_w_mla_o, m_w_out, m_norm_ffn2_pre, m_norm_ffn2_post, m_ffn2_w_gu, m_ffn2_w_down, v_meta_tokens, v_norm_ffn1_pre, v_norm_ffn1_post, v_ffn1_w_gu, v_ffn1_w_down, v_norm_mix_pre, v_norm_mix_post, v_w_in, v_pool_w, v_pool_scale, v_w_pool_o, v_q_a_norm, v_w_q_b, v_kv_a_norm, v_w_kv_b, v_w_mla_o, v_w_out, v_norm_ffn2_pre, v_norm_ffn2_post, v_ffn2_w_gu, v_ffn2_w_down):
    given = dict(locals())
    w_in_dev = {n: given[n] for n in WEIGHTS}
    m_in = {n: given["m_" + n] for n in WEIGHTS}
    v_in = {n: given["v_" + n] for n in WEIGHTS}
    xi, yi, ci = _my_pos()
    dev = 4 * xi + 2 * yi + ci
    core = jnp.reshape(ci, (1,)).astype(jnp.int32)
    chip = jnp.reshape(2 * xi + yi, (1,)).astype(jnp.int32)
    d = x.shape[-1]

    shards = {n: _shard2d(w_in_dev[n]) for n in BIG}
    ex = _Exchange(shards, meta_tokens, dev, core)
    gain = {n: given[n] for n in SMALL}
    loss_blk, grad_x, front, gsmall = local_step(x[0], loss_target[0], gain, ex)

    out_g, out_d, out_m, out_v = {}, {}, {}, {}
    updated = []

    def finish(grp, after):
        names, sums, from_chips = ex.finish_grads(grp, after)
        for n, s, r in zip(names, sums, from_chips, strict=True):
            shp = w_in_dev[n].shape
            res = adamw_shard("adamw_" + n, shards[n], _shard2d(m_in[n]), _shard2d(v_in[n]), s, r, chip)
            out_g[n], out_d[n], out_m[n], out_v[n] = [t.reshape(shp) for t in res]
            updated.append(res[1])
        return res[0]

    finish("A2", finish("B", finish("C", grad_x)))

    tail = jnp.concatenate([gsmall["pool_scale"], gsmall["q_a_norm"], gsmall["kv_a_norm"]], axis=1)
    small = jnp.concatenate([gsmall[n] for n in GAINS] + [tail, jnp.broadcast_to(loss_blk[:1, :1], (1, d)),
                                                         front[PAD_ROWS:]], axis=0)
    (small_g,) = all_gather("ag_small", [small], [LEAD], after=updated)
    total = reduce_small(small_g)
    ng = len(GAINS)
    for i, n in enumerate(GAINS):
        out_g[n] = total[i:i + 1]
    o = 0
    for n in ("pool_scale", "q_a_norm", "kv_a_norm"):
        wdt = w_in_dev[n].shape[1]
        out_g[n] = total[ng:ng + 1, o:o + wdt]
        o += wdt
    loss = total[ng + 1, 0]
    mcols = meta_tokens.shape[1]
    out_g["meta_tokens"] = lax.dynamic_slice(total[ng + 2:ng + 2 + N_META], (0, dev * mcols), (N_META, mcols))
    names = ["meta_tokens"] + SMALL
    ds_, ms_, vs_ = adamw_small([w_in_dev[n] for n in names], [out_g[n] for n in names],
                                [m_in[n] for n in names], [v_in[n] for n in names])
    for n, dd, mm, vv in zip(names, ds_, ms_, vs_, strict=True):
        out_d[n], out_m[n], out_v[n] = dd, mm, vv
    finish("A1", ds_[0])

    return (loss, grad_x[None], *[out_g[n] for n in WEIGHTS], *[out_d[n] for n in WEIGHTS],
            *[out_m[n] for n in WEIGHTS], *[out_v[n] for n in WEIGHTS])


GROUPS = {"A1": ["ffn1_w_gu"], "A2": ["ffn1_w_down"],
          "B": ["w_in", "pool_w", "w_pool_o", "w_q_b", "w_kv_b", "w_mla_o", "w_out"],
          "C": ["ffn2_w_gu", "ffn2_w_down"]}


class _Exchange:
    def __init__(self, shards, meta_tokens, dev, core):
        self.shards, self.meta_tokens, self.core = shards, meta_tokens, core
        self.dev1 = jnp.reshape(dev, (1,)).astype(jnp.int32)
        self.w, self.meta_full, self.token = {}, None, None
        self._ag, self._rs = {}, {}

    def _ag_place(self, grp, after):
        names = GROUPS[grp] + (["meta_tokens"] if grp == "A1" else [])
        srcs = [self.meta_tokens if n == "meta_tokens" else self.shards[n] for n in names]
        lays = [GATHERED.get(n, LEAD) for n in names]
        shapes = [a.shape for a in srcs]
        lands = [place_own(f"place_{n}", a, lay, F32 if n == "meta_tokens" else BF16, self.dev1, after)
                 for n, a, lay in zip(names, srcs, lays, strict=True)]
        self._ag[grp] = dict(names=names, lays=lays, shapes=shapes, lands=lands)

    def _ag_own_start(self, grp, after=None):
        st = self._ag[grp]
        plan, cnt = _ag_own_plan(st["shapes"], st["lays"])
        ss, rs, bufs, self.token = split_start(f"ag{grp}_own_start", st["lands"], plan, cnt, after)
        st["own"] = (ss, rs, bufs, plan)

    def _ag_pass(self, grp, after):
        st = self._ag[grp]
        ss, rs, bufs, plan = st["own"]
        lands = split_wait(f"ag{grp}_own_wait", bufs, ss, rs, plan, after)
        plan, cnt = _ag_pass_plan(st["shapes"], st["lays"])
        ss, rs, lands, self.token = split_start(f"ag{grp}_pass_start", lands, plan, cnt)
        st["pass"] = (ss, rs, lands, plan)

    def _ag_finish(self, grp, after):
        st = self._ag[grp]
        ss, rs, lands, plan = st["pass"]
        lands = split_wait(f"ag{grp}_pass_wait", lands, ss, rs, plan, after)
        for n, g, lay in zip(st["names"], lands, st["lays"]):
            if n == "meta_tokens":
                self.meta_full = g.transpose(1, 0, 2).reshape(N_META, N_DEV * g.shape[-1])
            else:
                self.w[n] = g if lay != LEAD else _to_internal(n, _full_from_gathered(n, g))

    def grads(self, grp, gbig, after=None):
        names = GROUPS[grp]
        lays = [GATHERED.get(n, LEAD) for n in names]
        grads = [gbig[n] if lay != LEAD else _blocks_from_full(n, _from_internal(n, gbig[n]).astype(BF16))
                 for n, lay in zip(names, lays, strict=True)]
        lands = [lax.empty((4, *_shard_shape(g, lay)), BF16) for g, lay in zip(grads, lays, strict=True)]
        plan, cnt = _rs_sibling_plan(lays, len(names))
        ss, rs, bufs, self.token = split_start(f"rs{grp}_sibling_start", grads + lands, plan, cnt, after)
        self._rs[grp] = dict(names=names, lays=lays, sib=(ss, rs, bufs, plan))
        return self.token

    def _rs_mid(self, grp, after):
        st = self._rs[grp]
        n = len(st["names"])
        ss, rs, bufs, plan = st["sib"]
        bufs = split_wait(f"rs{grp}_sibling_wait", bufs, ss, rs, plan, after)
        sums = [rs_add(f"rs_add_{name}", g, lay, r, self.core)
                for name, g, lay, r in zip(st["names"], bufs[:n], st["lays"], bufs[n:], strict=True)]
        lands = [lax.empty((3, *s.shape[1:]), BF16) for s in sums]
        plan, cnt = _rs_chips_plan(n)
        ss, rs, bufs, self.token = split_start(f"rs{grp}_chips_start", sums + lands, plan, cnt)
        st["chips"] = (ss, rs, bufs, plan)

    def finish_grads(self, grp, after):
        st = self._rs[grp]
        n = len(st["names"])
        ss, rs, bufs, plan = st["chips"]
        bufs = split_wait(f"rs{grp}_chips_wait", bufs, ss, rs, plan, after)
        return st["names"], bufs[:n], bufs[n:]

    def point(self, name, after=None):
        if name == "start":
            self._ag_place("A1", None)
            self._ag_own_start("A1")
            first = self.token
            self._ag_place("A2", first)
            self._ag_place("B", first)
            self._ag_pass("A1", self._ag["B"]["lands"][-1])
            self._ag_own_start("A2", self.token)
            self._ag_own_start("B", self.token)
            self._ag_finish("A1", self.token)
        elif name == "ffn1_gu_done":
            self._ag_pass("A2", after)
            self._ag_finish("A2", self.token)
        elif name == "ffn1_fwd_done":
            self._ag_pass("B", after)
            self._ag_place("C", self.token)
            self._ag_own_start("C", self.token)
        elif name == "mix_pre_done":
            self._ag_finish("B", after)
        elif name == "flash_fwd_done":
            self._ag_pass("C", after)
        elif name == "ffn2_pre_done":
            self._ag_finish("C", after)
        elif name.startswith("rs") and name.endswith("_mid"):
            self._rs_mid(name[2:-4], after)


def local_step(x, target, gain, ex):
    d = x.shape[-1]
    ex.point("start")
    w = ex.w
    h0 = jnp.concatenate([jnp.zeros((PAD_ROWS, d), F32), ex.meta_full, x], axis=0)
    lp = h0.shape[0]
    tabs = rope_tables(lp)
    a1 = prenorm(h0, gain["norm_ffn1_pre"], after=ex.token)
    gu1, s1 = ffn_gu("ffn1", a1, w["ffn1_w_gu"])
    ex.point("ffn1_gu_done", s1)
    f1 = ffn_down("ffn1", s1, w["ffn1_w_down"])
    ex.point("ffn1_fwd_done", f1)
    h1, a2 = post_pre("post_pre1", f1, h0, gain["norm_ffn1_post"], 0.5, gain["norm_mix_pre"], after=ex.token)
    ex.point("mix_pre_done", a2)
    mix, sv = mixer_fwd(a2, w, tabs, gain["pool_scale"], gain["q_a_norm"], gain["kv_a_norm"], ex)
    h2, a3 = post_pre("post_pre2", mix, h1, gain["norm_mix_post"], 1.0, gain["norm_ffn2_pre"])
    ex.point("ffn2_pre_done", a3)
    gu2, s2 = ffn_gu("ffn2", a3, w["ffn2_w_gu"])
    f2 = ffn_down("ffn2", s2, w["ffn2_w_down"])
    dh3, loss_blk = post_loss(f2, h2, gain["norm_ffn2_post"], 0.5, target)

    gsmall = {}
    df2, gsmall["norm_ffn2_post"] = post_bwd(dh3, f2, gain["norm_ffn2_post"], 0.5)
    da3 = ffn_bwd("ffn2", df2, a3, gu2, s2, w["ffn2_w_gu"], w["ffn2_w_down"], lambda dwd: None,
                  lambda dwgu, dwd: ex.grads("C", {"ffn2_w_gu": dwgu, "ffn2_w_down": dwd}))
    dh2, dmix, gsmall["norm_ffn2_pre"], gsmall["norm_mix_post"] = pre_post_bwd(
        "pre_post_bwd2", da3, h2, gain["norm_ffn2_pre"], dh3, mix, gain["norm_mix_post"], 1.0)
    ex.point("rsC_mid", dmix)
    da2, gmix, gmix_small = mixer_bwd(dmix, a2, sv, w, tabs, gain["pool_scale"], gain["q_a_norm"], gain["kv_a_norm"],
                                      after=ex.token)
    gsmall.update(gmix_small)
    dh1, df1, gsmall["norm_mix_pre"], gsmall["norm_ffn1_post"] = pre_post_bwd(
        "pre_post_bwd1", da2, h1, gain["norm_mix_pre"], dh2, f1, gain["norm_ffn1_post"], 0.5,
        after=ex.grads("B", gmix))
    ex.point("rsB_mid", df1)
    def on_dwgu1(dwgu, dwd):
        ex.point("rsA2_mid", dwgu)
        return ex.grads("A1", {"ffn1_w_gu": dwgu}, after=ex.token)

    da1 = ffn_bwd("ffn1", df1, a1, gu1, s1, w["ffn1_w_gu"], w["ffn1_w_down"],
                  lambda dwd: ex.grads("A2", {"ffn1_w_down": dwd}), on_dwgu1, after=ex.token)
    ex.point("rsA1_mid", da1)
    grad_x, front, gsmall["norm_ffn1_pre"] = pre_bwd_first(da1, h0, gain["norm_ffn1_pre"], dh1, after=ex.token)
    return loss_blk, grad_x, front, gsmall
```

```python
import functools

import jax
import jax.numpy as jnp
import numpy as np
from jax import lax
from jax.experimental import pallas as pl
from jax.experimental.pallas import tpu as pltpu

F32 = jnp.float32
BF16 = jnp.bfloat16

N_META = 16
POOL_WINDOWS = (2, 4, 8, 16)
POOL_GROUP = 256
HEADS = 16
QK_NOPE = 128
QK_ROPE = 64
V_DIM = 128
ROPE_THETA = 10000.0
SOFTMAX_SCALE = (QK_NOPE + QK_ROPE) ** -0.5
EPS = 1e-6
ADAM_LR = 0.001
ADAM_B1 = 0.9
ADAM_B2 = 0.999
ADAM_EPS = 1e-08
ADAM_WD = 0.01
ADAM_STEP = 10

LANE = 128
FRONT = 128
PAD_ROWS = FRONT - N_META
HEAD_W = 256
GU_TILE = 1408
VMEM_LIMIT = 56 * 1024 * 1024
MESH_AXES = ("x", "y", "c")
N_DEV = 8


def _pick(n, cands):
    for c in cands:
        if n % c == 0:
            return c
    raise ValueError(f"no tile for {n} in {cands}")


def _cparams(sem=None):
    kw = dict(vmem_limit_bytes=VMEM_LIMIT)
    if sem is not None:
        kw["dimension_semantics"] = sem
    return pltpu.CompilerParams(**kw)


_DIMS = {"nn": (((1,), (0,)), ((), ())), "nt": (((1,), (1,)), ((), ())), "tn": (((0,), (0,)), ((), ()))}


def _behind(after):
    return [] if after is None else list(after) if isinstance(after, (list, tuple)) else [after]


def _behind_specs(after):
    return [pl.BlockSpec(memory_space=pl.ANY)] * len(_behind(after))


def _mm(name, form, a, b, tm, tn, outs, epi=None, extras=(), n_outer=False, after=None):
    if form == "tn":
        k, m = a.shape
        n = b.shape[1]
        a_blk, a_map = (k, tm), lambda i, j: (0, i)
        b_blk, b_map = (k, tn), lambda i, j: (0, j)
    elif form == "nn":
        m, k = a.shape
        n = b.shape[1]
        a_blk, a_map = (tm, k), lambda i, j: (i, 0)
        b_blk, b_map = (k, tn), lambda i, j: (0, j)
    else:
        m, k = a.shape
        n = b.shape[0]
        a_blk, a_map = (tm, k), lambda i, j: (i, 0)
        b_blk, b_map = (tn, k), lambda i, j: (j, 0)
    assert m % tm == 0 and n % tn == 0, (name, m, n, tm, tn)
    n_ex = len(extras)
    dn = _DIMS[form]
    if n_outer:
        grid = (n // tn, m // tm)

        def spec(blk, im):
            return pl.BlockSpec(blk, lambda gj, gi: im(gi, gj))
    else:
        grid = (m // tm, n // tn)
        spec = pl.BlockSpec

    behind = _behind(after)

    def body(a_ref, b_ref, *rest):
        ex, out_refs = rest[:n_ex], rest[n_ex + len(behind):]
        acc = lax.dot_general(a_ref[...].astype(BF16), b_ref[...].astype(BF16), dn, preferred_element_type=F32)
        res = epi(acc, *[e[...] for e in ex]) if epi is not None else (acc,)
        for r, o in zip(res, out_refs, strict=True):
            o[...] = r.astype(o.dtype)

    return pl.pallas_call(
        body,
        name=name,
        grid=grid,
        in_specs=[spec(a_blk, a_map), spec(b_blk, b_map)] + [spec(blk, im) for _, blk, im in extras] + _behind_specs(after),
        out_specs=[spec(blk, im) for _, _, blk, im in outs],
        out_shape=[jax.ShapeDtypeStruct(s, d) for s, d, _, _ in outs],
        compiler_params=_cparams(("parallel", "parallel")),
    )(a, b, *[e for e, _, _ in extras], *behind)


def _mm_plain(name, form, a, b, tm, tn, out_dtype, after=None):
    m = a.shape[1] if form == "tn" else a.shape[0]
    n = b.shape[0] if form == "nt" else b.shape[1]
    return _mm(name, form, a, b, tm, tn, [((m, n), out_dtype, (tm, tn), lambda i, j: (i, j))], after=after)[0]


def _rstd(x):
    return lax.rsqrt(jnp.mean(x * x, axis=-1, keepdims=True) + EPS)


def _norm_bwd(dy, x, gain):
    r = _rstd(x)
    dyg = dy * gain
    dx = r * (dyg - x * (r * r) * jnp.mean(dyg * x, axis=-1, keepdims=True))
    dgain = jnp.sum(dy * x * r, axis=0, keepdims=True)
    return dx, dgain


def _row_spec(tr, cols, col_block=0):
    return pl.BlockSpec((tr, cols), lambda i: (i, col_block))


def _vec_spec(cols, col_block=0):
    return pl.BlockSpec((1, cols), lambda i: (0, col_block))


def _row_tile(lp):
    return _pick(lp, (128,))


def _skip_behind(after, body):
    n = len(_behind(after))
    return body if n == 0 else (lambda *refs: body(*refs[n:]))


def prenorm(h, gain, after=None):
    lp, d = h.shape
    tr = _row_tile(lp)

    def body(h_ref, g_ref, a_ref):
        x = h_ref[...]
        a_ref[...] = (x * _rstd(x) * g_ref[...]).astype(BF16)

    return pl.pallas_call(
        _skip_behind(after, body), name="prenorm", grid=(lp // tr,),
        in_specs=_behind_specs(after) + [_row_spec(tr, d), _vec_spec(d)], out_specs=_row_spec(tr, d),
        out_shape=jax.ShapeDtypeStruct((lp, d), BF16), compiler_params=_cparams(("parallel",)),
    )(*_behind(after), h, gain)


def post_pre(name, f, h_in, g_post, coef, g_next, after=None):
    lp, d = f.shape
    tr = _row_tile(lp)

    def body(f_ref, h_ref, gp_ref, gn_ref, ho_ref, a_ref):
        fv = f_ref[...]
        ho = h_ref[...] + coef * (fv * _rstd(fv) * gp_ref[...])
        ho_ref[...] = ho
        a_ref[...] = (ho * _rstd(ho) * gn_ref[...]).astype(BF16)

    return pl.pallas_call(
        _skip_behind(after, body), name=name, grid=(lp // tr,),
        in_specs=_behind_specs(after) + [_row_spec(tr, d), _row_spec(tr, d), _vec_spec(d), _vec_spec(d)],
        out_specs=[_row_spec(tr, d), _row_spec(tr, d)],
        out_shape=[jax.ShapeDtypeStruct((lp, d), F32), jax.ShapeDtypeStruct((lp, d), BF16)],
        compiler_params=_cparams(("parallel",)),
    )(*_behind(after), f, h_in, g_post, g_next)


def post_loss(f, h_in, g_post, coef, target):
    lp, d = f.shape
    tr = _row_tile(lp)
    front_tiles = FRONT // tr

    def body(f_ref, h_ref, gp_ref, t_ref, dh_ref, loss_ref):
        i = pl.program_id(0)

        @pl.when(i == 0)
        def _():
            loss_ref[...] = jnp.zeros_like(loss_ref)

        @pl.when(i < front_tiles)
        def _():
            dh_ref[...] = jnp.zeros_like(dh_ref)

        @pl.when(i >= front_tiles)
        def _():
            fv = f_ref[...]
            ho = h_ref[...] + coef * (fv * _rstd(fv) * gp_ref[...])
            err = ho - t_ref[...]
            dh_ref[...] = err / d
            tok = jnp.mean(err * err, axis=-1, keepdims=True)
            loss_ref[...] += 0.5 * jnp.sum(tok)

    return pl.pallas_call(
        body, name="post_loss", grid=(lp // tr,),
        in_specs=[_row_spec(tr, d), _row_spec(tr, d), _vec_spec(d),
                  pl.BlockSpec((tr, d), lambda i: (jnp.maximum(i - front_tiles, 0), 0))],
        out_specs=[_row_spec(tr, d), pl.BlockSpec((8, LANE), lambda i: (0, 0))],
        out_shape=[jax.ShapeDtypeStruct((lp, d), F32), jax.ShapeDtypeStruct((8, LANE), F32)],
        compiler_params=_cparams(("arbitrary",)),
    )(f, h_in, g_post, target)


def post_bwd(dh_out, f, g_post, coef):
    lp, d = f.shape
    tr = _row_tile(lp)

    def body(dh_ref, f_ref, gp_ref, df_ref, dg_ref):
        @pl.when(pl.program_id(0) == 0)
        def _():
            dg_ref[...] = jnp.zeros_like(dg_ref)

        df, dg = _norm_bwd(coef * dh_ref[...], f_ref[...], gp_ref[...])
        df_ref[...] = df.astype(BF16)
        dg_ref[...] += dg

    return pl.pallas_call(
        body, name="post_bwd", grid=(lp // tr,),
        in_specs=[_row_spec(tr, d), _row_spec(tr, d), _vec_spec(d)],
        out_specs=[_row_spec(tr, d), _vec_spec(d)],
        out_shape=[jax.ShapeDtypeStruct((lp, d), BF16), jax.ShapeDtypeStruct((1, d), F32)],
        compiler_params=_cparams(("arbitrary",)),
    )(dh_out, f, g_post)


def pre_post_bwd(name, da, h_mid, g_pre, dh_out, f_prev, g_post_prev, coef_prev, after=None):
    lp, d = da.shape
    tr = _row_tile(lp)

    def body(da_ref, h_ref, gpre_ref, dho_ref, f_ref, gpost_ref, dh_ref, df_ref, dgpre_ref, dgpost_ref):
        @pl.when(pl.program_id(0) == 0)
        def _():
            dgpre_ref[...] = jnp.zeros_like(dgpre_ref)
            dgpost_ref[...] = jnp.zeros_like(dgpost_ref)

        dx, dgpre = _norm_bwd(da_ref[...], h_ref[...], gpre_ref[...])
        dh = dho_ref[...] + dx
        dh_ref[...] = dh
        dgpre_ref[...] += dgpre
        df, dgpost = _norm_bwd(coef_prev * dh, f_ref[...], gpost_ref[...])
        df_ref[...] = df.astype(BF16)
        dgpost_ref[...] += dgpost

    return pl.pallas_call(
        _skip_behind(after, body), name=name, grid=(lp // tr,),
        in_specs=_behind_specs(after) + [_row_spec(tr, d), _row_spec(tr, d), _vec_spec(d), _row_spec(tr, d),
                                         _row_spec(tr, d), _vec_spec(d)],
        out_specs=[_row_spec(tr, d), _row_spec(tr, d), _vec_spec(d), _vec_spec(d)],
        out_shape=[jax.ShapeDtypeStruct((lp, d), F32), jax.ShapeDtypeStruct((lp, d), BF16),
                   jax.ShapeDtypeStruct((1, d), F32), jax.ShapeDtypeStruct((1, d), F32)],
        compiler_params=_cparams(("arbitrary",)),
    )(*_behind(after), da, h_mid, g_pre, dh_out, f_prev, g_post_prev)


def pre_bwd_first(da, h0, g_pre, dh_out, after=None):
    lp, d = da.shape
    tr = _row_tile(lp)
    front_tiles = FRONT // tr
    assert front_tiles == 1

    def body(da_ref, h_ref, gpre_ref, dho_ref, gx_ref, front_ref, dgpre_ref):
        i = pl.program_id(0)

        @pl.when(i == 0)
        def _():
            dgpre_ref[...] = jnp.zeros_like(dgpre_ref)

        dx, dgpre = _norm_bwd(da_ref[...], h_ref[...], gpre_ref[...])
        dh = dho_ref[...] + dx
        dgpre_ref[...] += dgpre
        gx_ref[...] = dh

        @pl.when(i == 0)
        def _():
            front_ref[...] = dh

    return pl.pallas_call(
        _skip_behind(after, body), name="pre_bwd_first", grid=(lp // tr,),
        in_specs=_behind_specs(after) + [_row_spec(tr, d), _row_spec(tr, d), _vec_spec(d), _row_spec(tr, d)],
        out_specs=[pl.BlockSpec((tr, d), lambda i: (jnp.maximum(i - front_tiles, 0), 0)),
                   pl.BlockSpec((tr, d), lambda i: (0, 0)), _vec_spec(d)],
        out_shape=[jax.ShapeDtypeStruct((lp - FRONT, d), F32), jax.ShapeDtypeStruct((tr, d), F32),
                   jax.ShapeDtypeStruct((1, d), F32)],
        compiler_params=_cparams(("arbitrary",)),
    )(*_behind(after), da, h0, g_pre, dh_out)


def _m_tile(lp):
    return _pick(lp, (1056, 512, 256, 128))


def ffn_gu(tag, a, wgu_p):
    lp, d = a.shape
    f2 = wgu_p.shape[1]

    def epi(acc):
        g, u = acc[:, :GU_TILE], acc[:, GU_TILE:]
        return acc, g * jax.nn.sigmoid(g) * u

    tg = _pick(lp, (384, 256, 128))
    return _mm(tag + "_gu", "nn", a, wgu_p, tg, 2 * GU_TILE,
               [((lp, f2), F32, (tg, 2 * GU_TILE), lambda i, j: (i, j)),
                ((lp, f2 // 2), BF16, (tg, GU_TILE), lambda i, j: (i, j))], epi=epi, n_outer=True)


def ffn_down(tag, s, wd):
    return _mm_plain(tag + "_down", "nn", s, wd, _m_tile(s.shape[0]), 512, F32)


def ffn_bwd(tag, df, a, gu, s, wgu_p, wd, on_dwd, on_dwgu, after=None):
    lp, d = df.shape
    f2 = wgu_p.shape[1]
    tm = _m_tile(lp)

    def epi(acc, gu_t):
        g, u = gu_t[:, :GU_TILE], gu_t[:, GU_TILE:]
        sig = jax.nn.sigmoid(g)
        dg = acc * u * (sig * (1.0 + g * (1.0 - sig)))
        du = acc * (g * sig)
        return (jnp.concatenate([dg, du], axis=1),)

    ts = _pick(lp, (528, 256, 128))
    dgu = _mm(tag + "_ds", "nt", df, wd, ts, GU_TILE,
              [((lp, f2), BF16, (ts, 2 * GU_TILE), lambda i, j: (i, j))], epi=epi,
              extras=[(gu, (ts, 2 * GU_TILE), lambda i, j: (i, j))], n_outer=True, after=after)[0]
    dwd = _mm_plain(tag + "_dwd", "tn", s, df, 512, _pick(d, (1024,)), BF16, after=after)
    dwgu = _mm_plain(tag + "_dwgu", "tn", a, dgu, _pick(d, (1024,)), 1024, BF16, after=on_dwd(dwd))
    return _mm_plain(tag + "_da", "nt", dgu, wgu_p, _pick(lp, (528, 256, 128)), 256, F32, after=on_dwgu(dwgu, dwd))


Z_U, Z_CQ, Z_CKV, Z_GP, Z_GM, Z_KR, Z_COLS = 0, 1024, 1536, 2048, 4096, 6144, 6400
POOL_W = POOL_GROUP * len(POOL_WINDOWS)
HALO = 16


def _pool_counts(lp, w):
    pos = lax.broadcasted_iota(jnp.int32, (lp, 1), 0) - PAD_ROWS
    return jnp.clip(pos + 1, 1, w).astype(F32)


def _pool_diff(u_ref, pad_ref, lp, w):
    pad_ref[pl.ds(0, HALO), :] = jnp.zeros((HALO, POOL_GROUP), F32)
    pad_ref[pl.ds(HALO, lp), :] = u_ref[...]
    acc = pad_ref[pl.ds(HALO, lp), :]
    for s in range(1, w):
        acc = acc + pad_ref[pl.ds(HALO - s, lp), :]
    return acc / _pool_counts(lp, w) - u_ref[...]


def pool_fwd(z, pool_w, pool_scale):
    lp = z.shape[0]
    ng = len(POOL_WINDOWS)

    def body(u_ref, w_ref, sc_ref, o_ref, pad_ref):
        for g, w in enumerate(POOL_WINDOWS):
            @pl.when(pl.program_id(0) == g)
            def _(w=w):
                dd = _pool_diff(u_ref, pad_ref, lp, w)
                y = jnp.dot(dd.astype(BF16), w_ref[0], preferred_element_type=F32)
                o_ref[...] = (y * sc_ref[...]).astype(BF16)

    return pl.pallas_call(
        body, name="pool_fwd", grid=(ng,),
        in_specs=[pl.BlockSpec((lp, POOL_GROUP), lambda g: (0, g)),
                  pl.BlockSpec((1, POOL_GROUP, POOL_GROUP), lambda g: (g, 0, 0)),
                  pl.BlockSpec((1, POOL_GROUP), lambda g: (0, g))],
        out_specs=pl.BlockSpec((lp, POOL_GROUP), lambda g: (0, g)),
        out_shape=jax.ShapeDtypeStruct((lp, POOL_W), BF16),
        scratch_shapes=[pltpu.VMEM((lp + HALO, POOL_GROUP), F32)],
        compiler_params=_cparams(("parallel",)),
    )(z, pool_w, pool_scale)


def pool_bwd(dyp, z, pool_w, pool_scale):
    lp = z.shape[0]
    ng = len(POOL_WINDOWS)

    def body(dy_ref, u_ref, w_ref, sc_ref, du_ref, dw_ref, dsc_ref, pad_ref):
        for g, w in enumerate(POOL_WINDOWS):
            @pl.when(pl.program_id(0) == g)
            def _(w=w):
                dd = _pool_diff(u_ref, pad_ref, lp, w).astype(BF16)
                wg = w_ref[0]
                ypre = jnp.dot(dd, wg, preferred_element_type=F32)
                dy = dy_ref[...]
                dsc_ref[...] = jnp.sum(dy * ypre, axis=0, keepdims=True)
                dypre = (dy * sc_ref[...]).astype(BF16)
                dw_ref[0] = lax.dot_general(dd, dypre, _DIMS["tn"], preferred_element_type=F32)
                ddd = lax.dot_general(dypre, wg, _DIMS["nt"], preferred_element_type=F32)
                pad_ref[pl.ds(0, lp), :] = ddd / _pool_counts(lp, w)
                pad_ref[pl.ds(lp, HALO), :] = jnp.zeros((HALO, POOL_GROUP), F32)
                acc = -ddd
                for s in range(w):
                    acc = acc + pad_ref[pl.ds(s, lp), :]
                du_ref[...] = acc.astype(BF16)

    return pl.pallas_call(
        body, name="pool_bwd", grid=(ng,),
        in_specs=[pl.BlockSpec((lp, POOL_GROUP), lambda g: (0, g)),
                  pl.BlockSpec((lp, POOL_GROUP), lambda g: (0, g)),
                  pl.BlockSpec((1, POOL_GROUP, POOL_GROUP), lambda g: (g, 0, 0)),
                  pl.BlockSpec((1, POOL_GROUP), lambda g: (0, g))],
        out_specs=[pl.BlockSpec((lp, POOL_GROUP), lambda g: (0, g)),
                   pl.BlockSpec((1, POOL_GROUP, POOL_GROUP), lambda g: (g, 0, 0)),
                   pl.BlockSpec((1, POOL_GROUP), lambda g: (0, g))],
        out_shape=[jax.ShapeDtypeStruct((lp, POOL_W), BF16),
                   jax.ShapeDtypeStruct((ng, POOL_GROUP, POOL_GROUP), F32),
                   jax.ShapeDtypeStruct((1, POOL_W), F32)],
        scratch_shapes=[pltpu.VMEM((lp + HALO, POOL_GROUP), F32)],
        compiler_params=_cparams(("parallel",)),
    )(dyp, z, pool_w, pool_scale)


Q_COLS = HEADS * HEAD_W
ROPE_BLOCK = LANE


def rope_tables(lp):
    pos = jnp.maximum(jnp.arange(lp, dtype=F32) - PAD_ROWS, 0.0)
    inv = ROPE_THETA ** (-jnp.arange(0, QK_ROPE, 2, dtype=F32) / QK_ROPE)
    ang = pos[:, None] * inv[None, :]
    cos, sin, zero = jnp.cos(ang), jnp.sin(ang), jnp.zeros_like(ang)
    return jnp.stack([jnp.concatenate([cos, cos, zero, zero], axis=1),
                      jnp.concatenate([-sin, zero, zero, zero], axis=1),
                      jnp.concatenate([zero, sin, zero, zero], axis=1)])


def _rope(x, tabs):
    return x * tabs[0] + pltpu.roll(x, 96, 1) * tabs[1] + pltpu.roll(x, 32, 1) * tabs[2]


def _rope_bwd(g, tabs):
    return g * tabs[0] + pltpu.roll(g * tabs[1], 32, 1) + pltpu.roll(g * tabs[2], 96, 1)


def _tab_spec(tr):
    return pl.BlockSpec((3, tr, ROPE_BLOCK), lambda i: (0, i, 0))


def mla_prep(z, g_q, g_kv, tabs):
    lp = z.shape[0]
    tr = _row_tile(lp)
    r = g_q.shape[1]

    def body(cq_ref, ckv_ref, kr_ref, gq_ref, gkv_ref, tab_ref, qn_ref, kvn_ref, kpe_ref):
        cq, ckv = cq_ref[...], ckv_ref[...]
        qn_ref[...] = (cq * _rstd(cq) * gq_ref[...]).astype(BF16)
        kvn_ref[...] = (ckv * _rstd(ckv) * gkv_ref[...]).astype(BF16)
        kpe_ref[...] = _rope(kr_ref[...], tab_ref[...]).astype(BF16)

    return pl.pallas_call(
        body, name="mla_prep", grid=(lp // tr,),
        in_specs=[_row_spec(tr, r, Z_CQ // r), _row_spec(tr, r, Z_CKV // r), _row_spec(tr, ROPE_BLOCK, Z_KR // ROPE_BLOCK),
                  _vec_spec(r), _vec_spec(r), _tab_spec(tr)],
        out_specs=[_row_spec(tr, r), _row_spec(tr, r), _row_spec(tr, ROPE_BLOCK)],
        out_shape=[jax.ShapeDtypeStruct((lp, r), BF16), jax.ShapeDtypeStruct((lp, r), BF16),
                   jax.ShapeDtypeStruct((lp, ROPE_BLOCK), BF16)],
        compiler_params=_cparams(("parallel",)),
    )(z, z, z, g_q, g_kv, tabs)


def q_proj(qn, wq_p, tabs):
    lp = qn.shape[0]
    tm = _m_tile(lp)
    tn = 4 * HEAD_W

    def epi(acc, tab):
        acc = acc * SOFTMAX_SCALE
        parts = []
        for t in range(tn // HEAD_W):
            parts.append(acc[:, t * HEAD_W:t * HEAD_W + QK_NOPE])
            parts.append(_rope(acc[:, t * HEAD_W + QK_NOPE:(t + 1) * HEAD_W], tab))
        return (jnp.concatenate(parts, axis=1),)

    return _mm("q_proj", "nn", qn, wq_p, tm, tn, [((lp, Q_COLS), BF16, (tm, tn), lambda i, j: (i, j))], epi=epi,
               extras=[(tabs, (3, tm, ROPE_BLOCK), lambda i, j: (0, i, 0))])[0]


def kv_proj(kvn, wkv):
    return _mm_plain("kv_proj", "nn", kvn, wkv, _m_tile(kvn.shape[0]), 1024, BF16)


def q_rope_bwd(dq, tabs):
    lp = dq.shape[0]
    tr = _row_tile(lp)

    def body(dq_ref, tab_ref, o_ref):
        tab = tab_ref[...]
        for h in range(HEADS):
            nope = dq_ref[:, h * HEAD_W:h * HEAD_W + QK_NOPE] * SOFTMAX_SCALE
            o_ref[:, h * HEAD_W:h * HEAD_W + QK_NOPE] = nope.astype(BF16)
            o_ref[:, h * HEAD_W + QK_NOPE:(h + 1) * HEAD_W] = _rope_bwd(
                dq_ref[:, h * HEAD_W + QK_NOPE:(h + 1) * HEAD_W] * SOFTMAX_SCALE, tab).astype(BF16)

    return pl.pallas_call(
        body, name="q_rope_bwd", grid=(lp // tr,),
        in_specs=[_row_spec(tr, Q_COLS), _tab_spec(tr)], out_specs=_row_spec(tr, Q_COLS),
        out_shape=jax.ShapeDtypeStruct((lp, Q_COLS), BF16), compiler_params=_cparams(("parallel",)),
    )(dq, tabs)


def mla_prep_bwd(dqn, dkvn, dkpe_h, z, g_q, g_kv, tabs):
    lp = z.shape[0]
    tr = _row_tile(lp)
    r = g_q.shape[1]

    def body(dqn_ref, dkvn_ref, dkpe_ref, cq_ref, ckv_ref, gq_ref, gkv_ref, tab_ref,
             dcq_ref, dckv_ref, dkr_ref, dgq_ref, dgkv_ref):
        @pl.when(pl.program_id(0) == 0)
        def _():
            dgq_ref[...] = jnp.zeros_like(dgq_ref)
            dgkv_ref[...] = jnp.zeros_like(dgkv_ref)

        dcq, dgq = _norm_bwd(dqn_ref[...], cq_ref[...], gq_ref[...])
        dckv, dgkv = _norm_bwd(dkvn_ref[...], ckv_ref[...], gkv_ref[...])
        dcq_ref[...] = dcq.astype(BF16)
        dckv_ref[...] = dckv.astype(BF16)
        dgq_ref[...] += dgq
        dgkv_ref[...] += dgkv
        dkpe = dkpe_ref[0]
        for h in range(1, HEADS):
            dkpe = dkpe + dkpe_ref[h]
        dkr_ref[...] = _rope_bwd(dkpe, tab_ref[...]).astype(BF16)

    return pl.pallas_call(
        body, name="mla_prep_bwd", grid=(lp // tr,),
        in_specs=[_row_spec(tr, r), _row_spec(tr, r), pl.BlockSpec((HEADS, tr, ROPE_BLOCK), lambda i: (0, i, 0)),
                  _row_spec(tr, r, Z_CQ // r), _row_spec(tr, r, Z_CKV // r), _vec_spec(r), _vec_spec(r), _tab_spec(tr)],
        out_specs=[_row_spec(tr, r), _row_spec(tr, r), _row_spec(tr, ROPE_BLOCK), _vec_spec(r), _vec_spec(r)],
        out_shape=[jax.ShapeDtypeStruct((lp, r), BF16), jax.ShapeDtypeStruct((lp, r), BF16),
                   jax.ShapeDtypeStruct((lp, ROPE_BLOCK), BF16),
                   jax.ShapeDtypeStruct((1, r), F32), jax.ShapeDtypeStruct((1, r), F32)],
        compiler_params=_cparams(("arbitrary",)),
    )(dqn, dkvn, dkpe_h, z, z, g_q, g_kv, tabs)


def _attn_tile(lp):
    return _pick(lp, (528, 128))


def _scores(q, kcat, q_tile, k_tile, t, masked):
    s = lax.dot_general(q, kcat, _DIMS["nt"], preferred_element_type=F32)
    if not masked:
        return s
    qpos = q_tile * t + lax.broadcasted_iota(jnp.int32, (t, t), 0)
    kpos = k_tile * t + lax.broadcasted_iota(jnp.int32, (t, t), 1)
    return jnp.where((kpos <= qpos) & (kpos >= PAD_ROWS), s, jnp.float32(-1e30))


def _causal_pairs(nt, k_major):
    if k_major:
        pairs = [(qi, ki) for ki in range(nt) for qi in range(ki, nt)]
    else:
        pairs = [(qi, ki) for qi in range(nt) for ki in range(qi + 1)]
    return (jnp.asarray([p[0] for p in pairs], jnp.int32), jnp.asarray([p[1] for p in pairs], jnp.int32))


def _on_masked_or_not(q_tile, k_tile, fn):
    needs_mask = (q_tile == k_tile) | (k_tile == 0)

    @pl.when(needs_mask)
    def _():
        fn(True)

    @pl.when(jnp.logical_not(needs_mask))
    def _():
        fn(False)


def flash_fwd(q, kv, kpe):
    lp = q.shape[0]
    t = _attn_tile(lp)
    q_tab, k_tab = _causal_pairs(lp // t, k_major=False)

    def body(q_tab_ref, k_tab_ref, q_ref, kv_ref, kpe_ref, o32_ref, o16_ref, lse_ref, m_sc, l_sc, acc_sc):
        pair = pl.program_id(1)
        qi, ki = q_tab_ref[pair], k_tab_ref[pair]

        @pl.when(ki == 0)
        def _():
            m_sc[...] = jnp.full_like(m_sc, -jnp.inf)
            l_sc[...] = jnp.zeros_like(l_sc)
            acc_sc[...] = jnp.zeros_like(acc_sc)

        def step(masked):
            kvt = kv_ref[...]
            kcat = jnp.concatenate([kvt[:, :QK_NOPE], kpe_ref[...]], axis=1)
            s = _scores(q_ref[...], kcat, qi, ki, t, masked)
            m_prev = m_sc[...]
            m_new = jnp.maximum(m_prev, jnp.max(s, axis=1, keepdims=True))
            alpha = jnp.exp(m_prev - m_new)
            p = jnp.exp(s - m_new[:, :1])
            l_sc[...] = alpha * l_sc[...] + jnp.sum(p, axis=1, keepdims=True)
            acc_sc[...] = alpha * acc_sc[...] + jnp.dot(p.astype(BF16), kvt[:, QK_NOPE:], preferred_element_type=F32)
            m_sc[...] = m_new

        _on_masked_or_not(qi, ki, step)

        @pl.when(ki == qi)
        def _():
            l = l_sc[...]
            o = acc_sc[...] / l
            o32_ref[...] = o
            o16_ref[...] = o.astype(BF16)
            lse_ref[0] = m_sc[...] + jnp.log(l)

    return pl.pallas_call(
        body, name="flash_fwd",
        grid_spec=pltpu.PrefetchScalarGridSpec(
            num_scalar_prefetch=2, grid=(HEADS, q_tab.shape[0]),
            in_specs=[pl.BlockSpec((t, HEAD_W), lambda h, p, qt, kt: (qt[p], h)),
                      pl.BlockSpec((t, HEAD_W), lambda h, p, qt, kt: (kt[p], h)),
                      pl.BlockSpec((t, ROPE_BLOCK), lambda h, p, qt, kt: (kt[p], 0))],
            out_specs=[pl.BlockSpec((t, V_DIM), lambda h, p, qt, kt: (qt[p], h)),
                       pl.BlockSpec((t, V_DIM), lambda h, p, qt, kt: (qt[p], h)),
                       pl.BlockSpec((1, t, LANE), lambda h, p, qt, kt: (h, qt[p], 0))],
            scratch_shapes=[pltpu.VMEM((t, LANE), F32), pltpu.VMEM((t, LANE), F32), pltpu.VMEM((t, V_DIM), F32)]),
        out_shape=[jax.ShapeDtypeStruct((lp, HEADS * V_DIM), F32), jax.ShapeDtypeStruct((lp, HEADS * V_DIM), BF16),
                   jax.ShapeDtypeStruct((HEADS, lp, LANE), F32)],
        compiler_params=_cparams(("parallel", "arbitrary")),
    )(q_tab, k_tab, q, kv, kpe)


def flash_bwd(q, kv, kpe, o32, lse, do):
    lp = q.shape[0]
    t = _attn_tile(lp)
    nt = lp // t
    q_tab, k_tab = _causal_pairs(nt, k_major=True)

    def body(q_tab_ref, k_tab_ref, q_ref, kv_ref, kpe_ref, o_ref, lse_ref, do_ref, dq_ref, dkv_ref, dkpe_ref,
             dk_sc, dv_sc):
        pair = pl.program_id(1)
        qi, ki = q_tab_ref[pair], k_tab_ref[pair]

        @pl.when(pair == 0)
        def _():
            dq_ref[...] = jnp.zeros_like(dq_ref)

        @pl.when(qi == ki)
        def _():
            dk_sc[...] = jnp.zeros_like(dk_sc)
            dv_sc[...] = jnp.zeros_like(dv_sc)

        def step(masked):
            qt = q_ref[...]
            kvt = kv_ref[...]
            kcat = jnp.concatenate([kvt[:, :QK_NOPE], kpe_ref[...]], axis=1)
            s = _scores(qt, kcat, qi, ki, t, masked)
            p = jnp.exp(s - lse_ref[0][:, :1])
            do = do_ref[...]
            delta = jnp.sum(do * o_ref[...], axis=1, keepdims=True)
            do16 = do.astype(BF16)
            dv_sc[...] += lax.dot_general(p.astype(BF16), do16, _DIMS["tn"], preferred_element_type=F32)
            dp = lax.dot_general(do16, kvt[:, QK_NOPE:], _DIMS["nt"], preferred_element_type=F32)
            ds = (p * (dp - delta)).astype(BF16)
            dk_sc[...] += lax.dot_general(ds, qt, _DIMS["tn"], preferred_element_type=F32)
            row = pl.multiple_of(qi * t, t)
            dq_ref[pl.ds(row, t), :] += jnp.dot(ds, kcat, preferred_element_type=F32)

        _on_masked_or_not(qi, ki, step)

        @pl.when(qi == nt - 1)
        def _():
            dk = dk_sc[...]
            dkv_ref[...] = jnp.concatenate([dk[:, :QK_NOPE], dv_sc[...]], axis=1).astype(BF16)
            dkpe_ref[0] = dk[:, QK_NOPE:]

    qmap = lambda h, p, qt, kt: (qt[p], h)
    return pl.pallas_call(
        body, name="flash_bwd",
        grid_spec=pltpu.PrefetchScalarGridSpec(
            num_scalar_prefetch=2, grid=(HEADS, q_tab.shape[0]),
            in_specs=[pl.BlockSpec((t, HEAD_W), qmap),
                      pl.BlockSpec((t, HEAD_W), lambda h, p, qt, kt: (kt[p], h)),
                      pl.BlockSpec((t, ROPE_BLOCK), lambda h, p, qt, kt: (kt[p], 0)),
                      pl.BlockSpec((t, V_DIM), qmap),
                      pl.BlockSpec((1, t, LANE), lambda h, p, qt, kt: (h, qt[p], 0)),
                      pl.BlockSpec((t, V_DIM), qmap)],
            out_specs=[pl.BlockSpec((lp, HEAD_W), lambda h, p, qt, kt: (0, h)),
                       pl.BlockSpec((t, HEAD_W), lambda h, p, qt, kt: (kt[p], h)),
                       pl.BlockSpec((1, t, ROPE_BLOCK), lambda h, p, qt, kt: (h, kt[p], 0))],
            scratch_shapes=[pltpu.VMEM((t, HEAD_W), F32), pltpu.VMEM((t, V_DIM), F32)]),
        out_shape=[jax.ShapeDtypeStruct((lp, Q_COLS), F32), jax.ShapeDtypeStruct((lp, Q_COLS), BF16),
                   jax.ShapeDtypeStruct((HEADS, lp, ROPE_BLOCK), F32)],
        compiler_params=_cparams(("parallel", "arbitrary")),
    )(q_tab, k_tab, q, kv, kpe, o32, lse, do)


def _ij(i, j):
    return (i, j)


def mixer_fwd(a2, w, tabs, pool_scale, g_q, g_kv, ex):
    lp, d = a2.shape
    tm = _m_tile(lp)
    tn = 512
    z = _mm_plain("mix_in", "nn", a2, w["w_in"], tm, 1280, F32)
    yp = pool_fwd(z, w["pool_w"], pool_scale)
    qn, kvn, kpe = mla_prep(z, g_q, g_kv, tabs)
    q = q_proj(qn, w["w_q_b"], tabs)
    kv = kv_proj(kvn, w["w_kv_b"])
    o32, o16, lse = flash_fwd(q, kv, kpe)
    ex.point("flash_fwd_done", o16)
    y_pool = _mm_plain("pool_out", "nn", yp, w["w_pool_o"], tm, tn, F32, after=ex.token)

    def epi(acc, ypl, gp, gm):
        return jax.nn.sigmoid(gp) * ypl + jax.nn.sigmoid(gm) * acc, acc

    y, y_mla = _mm("mla_out_gate", "nn", o16, w["w_mla_o"], tm, tn,
                   [((lp, d), BF16, (tm, tn), _ij), ((lp, d), F32, (tm, tn), _ij)], epi=epi,
                   extras=[(y_pool, (tm, tn), _ij), (z, (tm, tn), lambda i, j: (i, Z_GP // tn + j)),
                           (z, (tm, tn), lambda i, j: (i, Z_GM // tn + j))])
    m = _mm_plain("mix_out", "nn", y, w["w_out"], tm, tn, F32)
    return m, dict(z=z, yp=yp, qn=qn, kvn=kvn, kpe=kpe, q=q, kv=kv, o32=o32, o16=o16, lse=lse,
                   y_pool=y_pool, y_mla=y_mla, y=y)


def mixer_bwd(dm, a2, sv, w, tabs, pool_scale, g_q, g_kv, after=None):
    lp, d = dm.shape
    tm = _m_tile(lp)
    tn = 512
    z = sv["z"]

    def epi(acc, ypl, yml, gp, gm):
        sp, sm = jax.nn.sigmoid(gp), jax.nn.sigmoid(gm)
        return acc * sp, acc * sm, acc * ypl * (sp * (1.0 - sp)), acc * yml * (sm * (1.0 - sm))

    dyp, dym, dgp, dgm = _mm(
        "gate_bwd", "nt", dm, w["w_out"], tm, tn, [((lp, d), BF16, (tm, tn), _ij)] * 4, epi=epi,
        extras=[(sv["y_pool"], (tm, tn), _ij), (sv["y_mla"], (tm, tn), _ij),
                (z, (tm, tn), lambda i, j: (i, Z_GP // tn + j)), (z, (tm, tn), lambda i, j: (i, Z_GM // tn + j))],
        after=after)
    g = {}
    g["w_out"] = _mm_plain("dw_out", "tn", sv["y"], dm, 1024, 1024, BF16, after=after)
    g["w_pool_o"] = _mm_plain("dw_pool_o", "tn", sv["yp"], dyp, 512, 1024, BF16)
    dypre = _mm_plain("pool_out_bwd", "nt", dyp, w["w_pool_o"], tm, tn, F32)
    du, g["pool_w"], d_pool_scale = pool_bwd(dypre, z, w["pool_w"], pool_scale)
    g["w_mla_o"] = _mm_plain("dw_mla_o", "tn", sv["o16"], dym, 1024, 1024, BF16)
    do = _mm_plain("mla_out_bwd", "nt", dym, w["w_mla_o"], tm, tn, F32)
    dq, dkv, dkpe_h = flash_bwd(sv["q"], sv["kv"], sv["kpe"], sv["o32"], sv["lse"], do)
    dql = q_rope_bwd(dq, tabs)
    g["w_q_b"] = _mm_plain("dw_q_b", "tn", sv["qn"], dql, 512, 1024, BF16)
    dqn = _mm_plain("q_proj_bwd", "nt", dql, w["w_q_b"], tm, 512, F32)
    g["w_kv_b"] = _mm_plain("dw_kv_b", "tn", sv["kvn"], dkv, 512, 1024, BF16)
    dkvn = _mm_plain("kv_proj_bwd", "nt", dkv, w["w_kv_b"], tm, 512, F32)
    dcq, dckv, dkr, d_gq, d_gkv = mla_prep_bwd(dqn, dkvn, dkpe_h, z, g_q, g_kv, tabs)
    dz = jnp.concatenate([du, dcq, dckv, dgp, dgm, dkr, jnp.zeros((lp, Z_COLS - Z_KR - ROPE_BLOCK), BF16)], axis=1)
    g["w_in"] = _mm_plain("dw_in", "tn", a2, dz, 1024, 1280, BF16)
    da2 = _mm_plain("mix_in_bwd", "nt", dz, w["w_in"], tm, tn, F32)
    return da2, g, dict(pool_scale=d_pool_scale, q_a_norm=d_gq, kv_a_norm=d_gkv)


_ANY = pl.BlockSpec(memory_space=pl.ANY)
_MESH = pl.DeviceIdType.MESH


def _my_pos():
    return lax.axis_index("x"), lax.axis_index("y"), lax.axis_index("c")


LEAD = "lead"
COLS = "cols"
COLS_GU = "cols_gu"


def _col_block(layout, dev):
    return dev if layout == COLS else 2 * (dev % 4) + dev // 4


def _dev_block(ref, layout, dev, cols):
    if layout == LEAD:
        return ref.at[dev]
    return ref.at[:, pl.ds(pl.multiple_of(_col_block(layout, dev) * cols, LANE), cols)]


def _gathered_shape(shard_shape, layout):
    if layout == LEAD:
        return (N_DEV, *shard_shape)
    return (shard_shape[0], N_DEV * shard_shape[1])


def all_gather(name, shards, layouts, after=None):
    n = len(shards)
    behind = _behind(after)

    def body(*refs):
        ins, outs = refs[:n], refs[n + len(behind):2 * n + len(behind)]
        send_sems, recv_sems, local_sems = refs[2 * n + len(behind):]
        x, y, c = _my_pos()
        me, sibling = (x, y, c), (x, y, 1 - c)
        chips = [(1 - x, y), (x, 1 - y), (1 - x, 1 - y)]

        def blk(a, px, py, pc):
            return _dev_block(outs[a], layouts[a], 4 * px + 2 * py + pc, shards[a].shape[-1])

        def copy(a, k, block, to, src=None):
            return pltpu.make_async_remote_copy(
                src_ref=blk(a, *block) if src is None else src, dst_ref=blk(a, *block),
                send_sem=send_sems.at[a, k], recv_sem=recv_sems.at[a, k], device_id=to, device_id_type=_MESH)

        mine = [pltpu.make_async_copy(ins[a], blk(a, *me), local_sems.at[a]) for a in range(n)]
        for cp in mine:
            cp.start()
        first = []
        for a in range(n):
            first.append(copy(a, 0, me, sibling, src=ins[a]))
            first += [copy(a, 1 + j, me, (*chip, c), src=ins[a]) for j, chip in enumerate(chips)]
        for cp in first:
            cp.start()
        passed = []
        for j, chip in enumerate(chips):
            for a in range(n):
                copy(a, 1 + j, (*chip, c), me).wait_recv()
                fwd = copy(a, 4 + j, (*chip, c), sibling)
                fwd.start()
                passed.append(fwd)
        for a in range(n):
            copy(a, 0, sibling, me).wait_recv()
            for j, chip in enumerate(chips):
                copy(a, 4 + j, (*chip, 1 - c), me).wait_recv()
        for cp in first + passed:
            cp.wait_send()
        for cp in mine:
            cp.wait()

    return pl.pallas_call(
        body, name=name,
        in_specs=[_ANY] * (n + len(behind)), out_specs=[_ANY] * n,
        out_shape=[jax.ShapeDtypeStruct(_gathered_shape(s.shape, lay), s.dtype)
                   for s, lay in zip(shards, layouts, strict=True)],
        scratch_shapes=[pltpu.SemaphoreType.DMA((n, 7)), pltpu.SemaphoreType.DMA((n, 7)), pltpu.SemaphoreType.DMA((n,))],
    )(*shards, *behind)


def _shard_shape(grad, layout):
    return grad.shape[1:] if layout == LEAD else (grad.shape[0], grad.shape[1] // N_DEV)


def rs_sibling(name, grads, layouts):
    n = len(grads)

    def body(*refs):
        ins, outs = refs[:n], refs[n:2 * n]
        send_sems, recv_sems = refs[2 * n:]
        x, y, c = _my_pos()
        cps = []
        for a in range(n):
            for k in range(4):
                cp = pltpu.make_async_remote_copy(
                    src_ref=_dev_block(ins[a], layouts[a], 2 * k + (1 - c), outs[a].shape[-1]), dst_ref=outs[a].at[k],
                    send_sem=send_sems.at[a, k], recv_sem=recv_sems.at[a, k],
                    device_id=(x, y, 1 - c), device_id_type=_MESH)
                cp.start()
                cps.append(cp)
        for cp in cps:
            cp.wait()

    return pl.pallas_call(
        body, name=name,
        in_specs=[_ANY] * n, out_specs=[_ANY] * n,
        out_shape=[jax.ShapeDtypeStruct((4, *_shard_shape(g, lay)), g.dtype) for g, lay in zip(grads, layouts, strict=True)],
        scratch_shapes=[pltpu.SemaphoreType.DMA((n, 4)), pltpu.SemaphoreType.DMA((n, 4))],
    )(*grads)


def rs_chips(name, sums):
    n = len(sums)

    def body(*refs):
        ins, outs = refs[:n], refs[n:2 * n]
        send_sems, recv_sems = refs[2 * n:]
        x, y, c = _my_pos()
        chips = [(1 - x, y), (x, 1 - y), (1 - x, 1 - y)]
        cps = []
        for a in range(n):
            for j, chip in enumerate(chips):
                cp = pltpu.make_async_remote_copy(
                    src_ref=ins[a].at[2 * chip[0] + chip[1]], dst_ref=outs[a].at[j],
                    send_sem=send_sems.at[a, j], recv_sem=recv_sems.at[a, j],
                    device_id=(*chip, c), device_id_type=_MESH)
                cp.start()
                cps.append(cp)
        for cp in cps:
            cp.wait()

    return pl.pallas_call(
        body, name=name,
        in_specs=[_ANY] * n, out_specs=[_ANY] * n,
        out_shape=[jax.ShapeDtypeStruct((3, *s.shape[1:]), s.dtype) for s in sums],
        scratch_shapes=[pltpu.SemaphoreType.DMA((n, 3)), pltpu.SemaphoreType.DMA((n, 3))],
    )(*sums)


_HBM = pl.BlockSpec(memory_space=pltpu.HBM)
_SEM = pl.BlockSpec(memory_space=pltpu.SEMAPHORE)
_EFFECT = pltpu.SideEffectType.DATAFLOW_SIDE_EFFECTING


def _in_hbm(a):
    return pltpu.with_memory_space_constraint(a, pltpu.HBM)


def split_start(name, bufs, plan, n_copies, after=None):
    nb = len(bufs)
    extra = [] if after is None else [after]

    def body(*refs):
        buf_refs = refs[:nb]
        send_sems, recv_sems = refs[nb + len(extra)], refs[nb + len(extra) + 1]
        token = refs[-1]
        copies = plan(buf_refs)
        assert len(copies) == n_copies
        for k, (src, dst, to) in enumerate(copies):
            pltpu.make_async_remote_copy(src_ref=src, dst_ref=dst, send_sem=send_sems.at[k], recv_sem=recv_sems.at[k],
                                         device_id=to, device_id_type=_MESH).start()
        token[...] = jnp.zeros_like(token)

    out = pl.pallas_call(
        body, name=name,
        out_shape=(pltpu.SemaphoreType.DMA((n_copies,)), pltpu.SemaphoreType.DMA((n_copies,)),
                   *[pltpu.HBM(b.shape, b.dtype) for b in bufs], jax.ShapeDtypeStruct((8, LANE), F32)),
        in_specs=[_HBM] * nb + [_ANY] * len(extra),
        out_specs=(_SEM, _SEM, *[_HBM] * nb, pl.BlockSpec(memory_space=pltpu.VMEM)),
        input_output_aliases={i: 2 + i for i in range(nb)},
        compiler_params=pltpu.CompilerParams(has_side_effects=_EFFECT),
    )(*[_in_hbm(b) for b in bufs], *extra)
    return out[0], out[1], list(out[2:2 + nb]), out[-1]


def split_wait(name, bufs, send_sems, recv_sems, plan, after):
    nb = len(bufs)

    def body(*refs):
        buf_refs = refs[:nb]
        s_sems, r_sems = refs[nb], refs[nb + 1]
        for k, (src, dst, to) in enumerate(plan(buf_refs)):
            cp = pltpu.make_async_remote_copy(src_ref=src, dst_ref=dst, send_sem=s_sems.at[k], recv_sem=r_sems.at[k],
                                              device_id=to, device_id_type=_MESH)
            cp.wait_send()
            cp.wait_recv()

    out = pl.pallas_call(
        body, name=name,
        out_shape=tuple(pltpu.HBM(b.shape, b.dtype) for b in bufs),
        in_specs=[_HBM] * nb + [_SEM, _SEM, _ANY],
        out_specs=tuple([_HBM] * nb),
        input_output_aliases={i: i for i in range(nb)},
        compiler_params=pltpu.CompilerParams(has_side_effects=_EFFECT),
    )(*bufs, send_sems, recv_sems, after)
    return list(out)


def _ag_own_plan(shapes, layouts):
    n = len(shapes)

    def plan(refs):
        x, y, c = _my_pos()
        targets = [(x, y, 1 - c), (1 - x, y, c), (x, 1 - y, c), (1 - x, 1 - y, c)]
        out = []
        for a in range(n):
            blk = _dev_block(refs[a], layouts[a], 4 * x + 2 * y + c, shapes[a][-1])
            out += [(blk, blk, to) for to in targets]
        return out

    return plan, 4 * n


def _ag_pass_plan(shapes, layouts):
    n = len(shapes)

    def plan(refs):
        x, y, c = _my_pos()
        out = []
        for a in range(n):
            for px, py in [(1 - x, y), (x, 1 - y), (1 - x, 1 - y)]:
                blk = _dev_block(refs[a], layouts[a], 4 * px + 2 * py + c, shapes[a][-1])
                out.append((blk, blk, (x, y, 1 - c)))
        return out

    return plan, 3 * n


def _rs_sibling_plan(layouts, n):
    def plan(refs):
        x, y, c = _my_pos()
        return [(_dev_block(refs[a], layouts[a], 2 * k + (1 - c), refs[n + a].shape[-1]), refs[n + a].at[k], (x, y, 1 - c))
                for a in range(n) for k in range(4)]

    return plan, 4 * n


def _rs_chips_plan(n):
    def plan(refs):
        x, y, c = _my_pos()
        return [(refs[a].at[2 * px + py], refs[n + a].at[j], (px, py, c))
                for a in range(n) for j, (px, py) in enumerate([(1 - x, y), (x, 1 - y), (1 - x, 1 - y)])]

    return plan, 3 * n


def place_own(name, shard, layout, dtype, dev, after):
    r, c = shard.shape
    tr = _ew_rows(r, c)
    if layout == LEAD:
        o_spec = pl.BlockSpec((None, tr, c), lambda i, dev_ref: (dev_ref[0], i, 0))
    else:
        o_spec = pl.BlockSpec((tr, c), lambda i, dev_ref: (i, _col_block(layout, dev_ref[0])))
    extra = [] if after is None else [after]

    def body(dev_ref, s_ref, *rest):
        rest[-1][...] = s_ref[...].astype(dtype)

    return pl.pallas_call(
        body, name=name,
        grid_spec=pltpu.PrefetchScalarGridSpec(
            num_scalar_prefetch=1, grid=(r // tr,),
            in_specs=[pl.BlockSpec((tr, c), lambda i, dev_ref: (i, 0))] + [_ANY] * len(extra),
            out_specs=o_spec),
        out_shape=jax.ShapeDtypeStruct(_gathered_shape(shard.shape, layout), dtype),
        compiler_params=_cparams(("parallel",)),
    )(dev, shard, *extra)


def _ew_rows(r, c):
    for t in (512, 256, 128, 64, 32, 16):
        if r % t == 0 and t * c * 4 <= 768 * 1024:
            return t
    raise ValueError((r, c))


def rs_add(name, grad, layout, recv, core):
    _, r, c = recv.shape
    tr = _ew_rows(r, c)
    if layout == LEAD:
        g_spec = pl.BlockSpec((None, tr, c), lambda k, i, core_ref: (2 * k + core_ref[0], i, 0))
    else:
        g_spec = pl.BlockSpec((tr, c), lambda k, i, core_ref: (i, _col_block(layout, 2 * k + core_ref[0])))

    def body(core_ref, g_ref, r_ref, o_ref):
        o_ref[...] = (g_ref[...].astype(F32) + r_ref[...].astype(F32)).astype(BF16)

    return pl.pallas_call(
        body, name=name,
        grid_spec=pltpu.PrefetchScalarGridSpec(
            num_scalar_prefetch=1, grid=(4, r // tr),
            in_specs=[g_spec, pl.BlockSpec((None, tr, c), lambda k, i, core_ref: (k, i, 0))],
            out_specs=pl.BlockSpec((None, tr, c), lambda k, i, core_ref: (k, i, 0))),
        out_shape=jax.ShapeDtypeStruct((4, r, c), BF16),
        compiler_params=_cparams(("parallel", "parallel")),
    )(core, grad, recv)


def _adamw(w, g, m, v):
    m = ADAM_B1 * m + (1.0 - ADAM_B1) * g
    v = ADAM_B2 * v + (1.0 - ADAM_B2) * jnp.square(g)
    m_hat = m / (1.0 - ADAM_B1 ** ADAM_STEP)
    v_hat = v / (1.0 - ADAM_B2 ** ADAM_STEP)
    delta = -ADAM_LR * (m_hat / (jnp.sqrt(v_hat) + ADAM_EPS) + ADAM_WD * w)
    return delta, m, v


def adamw_shard(name, w, m, v, sums, recv, chip):
    r, c = w.shape
    tr = _ew_rows(r, c)

    def body(chip_ref, w_ref, m_ref, v_ref, s_ref, r_ref, g_ref, d_ref, mo_ref, vo_ref):
        g = s_ref[0].astype(F32)
        for j in range(3):
            g = g + r_ref[j].astype(F32)
        d, mn, vn = _adamw(w_ref[...], g, m_ref[...], v_ref[...])
        g_ref[...] = g
        d_ref[...] = d
        mo_ref[...] = mn
        vo_ref[...] = vn

    spec = pl.BlockSpec((tr, c), lambda i, chip_ref: (i, 0))
    return pl.pallas_call(
        body, name=name,
        grid_spec=pltpu.PrefetchScalarGridSpec(
            num_scalar_prefetch=1, grid=(r // tr,),
            in_specs=[spec, spec, spec,
                      pl.BlockSpec((1, tr, c), lambda i, chip_ref: (chip_ref[0], i, 0)),
                      pl.BlockSpec((3, tr, c), lambda i, chip_ref: (0, i, 0))],
            out_specs=[spec] * 4),
        out_shape=[jax.ShapeDtypeStruct((r, c), F32)] * 4,
        compiler_params=_cparams(("parallel",)),
    )(chip, w, m, v, sums, recv)


def reduce_small(gathered):
    _, r, c = gathered.shape

    def body(g_ref, o_ref):
        acc = g_ref[0]
        for k in range(1, N_DEV):
            acc = acc + g_ref[k]
        o_ref[...] = acc

    return pl.pallas_call(body, name="reduce_small", out_shape=jax.ShapeDtypeStruct((r, c), F32))(gathered)


def adamw_small(ws, gs, ms, vs):
    n = len(ws)

    def body(*refs):
        w_r, g_r, m_r, v_r = refs[:n], refs[n:2 * n], refs[2 * n:3 * n], refs[3 * n:4 * n]
        d_o, m_o, v_o = refs[4 * n:5 * n], refs[5 * n:6 * n], refs[6 * n:7 * n]
        for a in range(n):
            d, mn, vn = _adamw(w_r[a][...], g_r[a][...], m_r[a][...], v_r[a][...])
            d_o[a][...] = d
            m_o[a][...] = mn
            v_o[a][...] = vn

    shapes = [jax.ShapeDtypeStruct(w.shape, F32) for w in ws]
    out = pl.pallas_call(body, name="adamw_small", out_shape=shapes * 3)(*ws, *gs, *ms, *vs)
    return out[:n], out[n:2 * n], out[2 * n:]


WEIGHTS = ["meta_tokens", "norm_ffn1_pre", "norm_ffn1_post", "ffn1_w_gu", "ffn1_w_down", "norm_mix_pre",
           "norm_mix_post", "w_in", "pool_w", "pool_scale", "w_pool_o", "q_a_norm", "w_q_b", "kv_a_norm", "w_kv_b",
           "w_mla_o", "w_out", "norm_ffn2_pre", "norm_ffn2_post", "ffn2_w_gu", "ffn2_w_down"]
BIG = ["ffn1_w_gu", "ffn1_w_down", "w_in", "pool_w", "w_pool_o", "w_q_b", "w_kv_b", "w_mla_o", "w_out",
       "ffn2_w_gu", "ffn2_w_down"]
COL_SHARDED = ("w_in", "w_q_b")
GATHERED = {"ffn1_w_gu": COLS_GU, "ffn2_w_gu": COLS_GU, "w_pool_o": COLS, "w_kv_b": COLS}
GAINS =["norm_ffn1_pre", "norm_ffn1_post", "norm_mix_pre", "norm_mix_post", "norm_ffn2_pre", "norm_ffn2_post"]
SMALL = GAINS + ["pool_scale", "q_a_norm", "kv_a_norm"]
Z_SRC = 1024 + 512 + 512 + QK_ROPE


def _full_from_gathered(name, g):
    _, r, c = g.shape
    if name == "pool_w":
        ng = len(POOL_WINDOWS)
        return g.reshape(N_DEV, ng, r // ng, c).transpose(1, 0, 2, 3).reshape(ng, POOL_GROUP, POOL_GROUP)
    if name in COL_SHARDED:
        return g.transpose(1, 0, 2).reshape(r, N_DEV * c)
    return g.reshape(N_DEV * r, c)


def _blocks_from_full(name, dw):
    if name == "pool_w":
        ng = len(POOL_WINDOWS)
        return dw.reshape(ng, N_DEV, POOL_GROUP // N_DEV, POOL_GROUP).transpose(1, 0, 2, 3).reshape(
            N_DEV, ng * POOL_GROUP // N_DEV, POOL_GROUP)
    k, n = dw.shape
    if name in COL_SHARDED:
        return dw.reshape(k, N_DEV, n // N_DEV).transpose(1, 0, 2)
    return dw.reshape(N_DEV, k // N_DEV, n)


def _to_internal(name, w):
    if name == "w_in":
        d = w.shape[0]
        return jnp.concatenate([w[:, :Z_SRC - QK_ROPE], w[:, Z_SRC:], w[:, Z_SRC - QK_ROPE:Z_SRC],
                                jnp.zeros((d, Z_COLS - Z_KR - QK_ROPE), w.dtype)], axis=1)
    if name == "w_q_b":
        r = w.shape[0]
        w3 = w.reshape(r, HEADS, QK_NOPE + QK_ROPE)
        return jnp.pad(w3, ((0, 0), (0, 0), (0, HEAD_W - QK_NOPE - QK_ROPE))).reshape(r, Q_COLS)
    return w


def _from_internal(name, dw):
    if name == "w_in":
        return jnp.concatenate([dw[:, :Z_SRC - QK_ROPE], dw[:, Z_KR:Z_KR + QK_ROPE], dw[:, Z_SRC - QK_ROPE:Z_KR]], axis=1)
    if name == "w_q_b":
        r = dw.shape[0]
        return dw.reshape(r, HEADS, HEAD_W)[:, :, :QK_NOPE + QK_ROPE].reshape(r, HEADS * (QK_NOPE + QK_ROPE))
    return dw


def _shard2d(a):
    return a.reshape(-1, a.shape[-1])


def kernel(x, meta_tokens, norm_ffn1_pre, norm_ffn1_post, ffn1_w_gu, ffn1_w_down, norm_mix_pre, norm_mix_post, w_in, pool_w, pool_scale, w_pool_o, q_a_norm, w_q_b, kv_a_norm, w_kv_b, w_mla_o, w_out, norm_ffn2_pre, norm_ffn2_post, ffn2_w_gu, ffn2_w_down, loss_target, m_meta_tokens, m_norm_ffn1_pre, m_norm_ffn1_post, m_ffn1_w_gu, m_ffn1_w_down, m_norm_mix_pre, m_norm_mix_post, m_w_in, m_pool_w, m_pool_scale, m_w_pool_o, m_q_a_norm, m_w_q_b, m_kv_a_norm, m_w_kv_b, m_w_mla_o, m_w_out, m_norm_ffn2_pre, m_norm_ffn2_post, m_ffn2_w_gu, m_ffn2_w_down, v_meta_tokens, v_norm_ffn1_pre, v_norm_ffn1_post, v_ffn1_w_gu, v_ffn1_w_down, v_norm_mix_pre, v_norm_mix_post, v_w_in, v_pool_w, v_pool_scale, v_w_pool_o, v_q_a_norm, v_w_q_b, v_kv_a_norm, v_w_kv_b, v_w_mla_o, v_w_out, v_norm_ffn2_pre, v_norm_ffn2_post, v_ffn2_w_gu, v_ffn2_w_down):
    given = dict(locals())
    w_in_dev = {n: given[n] for n in WEIGHTS}
    m_in = {n: given["m_" + n] for n in WEIGHTS}
    v_in = {n: given["v_" + n] for n in WEIGHTS}
    xi, yi, ci = _my_pos()
    dev = 4 * xi + 2 * yi + ci
    core = jnp.reshape(ci, (1,)).astype(jnp.int32)
    chip = jnp.reshape(2 * xi + yi, (1,)).astype(jnp.int32)
    d = x.shape[-1]

    shards = {n: _shard2d(w_in_dev[n]) for n in BIG}
    ex = _Exchange(shards, meta_tokens, dev, core)
    gain = {n: given[n] for n in SMALL}
    loss_blk, grad_x, front, gsmall = local_step(x[0], loss_target[0], gain, ex)

    out_g, out_d, out_m, out_v = {}, {}, {}, {}
    updated = []

    def finish(grp, after):
        names, sums, from_chips = ex.finish_grads(grp, after)
        for n, s, r in zip(names, sums, from_chips, strict=True):
            shp = w_in_dev[n].shape
            res = adamw_shard("adamw_" + n, shards[n], _shard2d(m_in[n]), _shard2d(v_in[n]), s, r, chip)
            out_g[n], out_d[n], out_m[n], out_v[n] = [t.reshape(shp) for t in res]
            updated.append(res[1])
        return res[0]

    finish("A2", finish("B", finish("C", grad_x)))

    tail = jnp.concatenate([gsmall["pool_scale"], gsmall["q_a_norm"], gsmall["kv_a_norm"]], axis=1)
    small = jnp.concatenate([gsmall[n] for n in GAINS] + [tail, jnp.broadcast_to(loss_blk[:1, :1], (1, d)),
                                                         front[PAD_ROWS:]], axis=0)
    (small_g,) = all_gather("ag_small", [small], [LEAD], after=updated)
    total = reduce_small(small_g)
    ng = len(GAINS)
    for i, n in enumerate(GAINS):
        out_g[n] = total[i:i + 1]
    o = 0
    for n in ("pool_scale", "q_a_norm", "kv_a_norm"):
        wdt = w_in_dev[n].shape[1]
        out_g[n] = total[ng:ng + 1, o:o + wdt]
        o += wdt
    loss = total[ng + 1, 0]
    mcols = meta_tokens.shape[1]
    out_g["meta_tokens"] = lax.dynamic_slice(total[ng + 2:ng + 2 + N_META], (0, dev * mcols), (N_META, mcols))
    names = ["meta_tokens"] + SMALL
    ds_, ms_, vs_ = adamw_small([w_in_dev[n] for n in names], [out_g[n] for n in names],
                                [m_in[n] for n in names], [v_in[n] for n in names])
    for n, dd, mm, vv in zip(names, ds_, ms_, vs_, strict=True):
        out_d[n], out_m[n], out_v[n] = dd, mm, vv
    finish("A1", ds_[0])

    return (loss, grad_x[None], *[out_g[n] for n in WEIGHTS], *[out_d[n] for n in WEIGHTS],
            *[out_m[n] for n in WEIGHTS], *[out_v[n] for n in WEIGHTS])


GROUPS = {"A1": ["ffn1_w_gu"], "A2": ["ffn1_w_down"],
          "B": ["w_in", "pool_w", "w_pool_o", "w_q_b", "w_kv_b", "w_mla_o", "w_out"],
          "C": ["ffn2_w_gu", "ffn2_w_down"]}


class _Exchange:
    def __init__(self, shards, meta_tokens, dev, core):
        self.shards, self.meta_tokens, self.core = shards, meta_tokens, core
        self.dev1 = jnp.reshape(dev, (1,)).astype(jnp.int32)
        self.w, self.meta_full, self.token = {}, None, None
        self._ag, self._rs = {}, {}

    def _ag_place(self, grp, after):
        names = GROUPS[grp] + (["meta_tokens"] if grp == "A1" else [])
        srcs = [self.meta_tokens if n == "meta_tokens" else self.shards[n] for n in names]
        lays = [GATHERED.get(n, LEAD) for n in names]
        shapes = [a.shape for a in srcs]
        lands = [place_own(f"place_{n}", a, lay, F32 if n == "meta_tokens" else BF16, self.dev1, after)
                 for n, a, lay in zip(names, srcs, lays, strict=True)]
        self._ag[grp] = dict(names=names, lays=lays, shapes=shapes, lands=lands)

    def _ag_own_start(self, grp, after=None):
        st = self._ag[grp]
        plan, cnt = _ag_own_plan(st["shapes"], st["lays"])
        ss, rs, bufs, self.token = split_start(f"ag{grp}_own_start", st["lands"], plan, cnt, after)
        st["own"] = (ss, rs, bufs, plan)

    def _ag_pass(self, grp, after):
        st = self._ag[grp]
        ss, rs, bufs, plan = st["own"]
        lands = split_wait(f"ag{grp}_own_wait", bufs, ss, rs, plan, after)
        plan, cnt = _ag_pass_plan(st["shapes"], st["lays"])
        ss, rs, lands, self.token = split_start(f"ag{grp}_pass_start", lands, plan, cnt)
        st["pass"] = (ss, rs, lands, plan)

    def _ag_finish(self, grp, after):
        st = self._ag[grp]
        ss, rs, lands, plan = st["pass"]
        lands = split_wait(f"ag{grp}_pass_wait", lands, ss, rs, plan, after)
        for n, g, lay in zip(st["names"], lands, st["lays"]):
            if n == "meta_tokens":
                self.meta_full = g.transpose(1, 0, 2).reshape(N_META, N_DEV * g.shape[-1])
            else:
                self.w[n] = g if lay != LEAD else _to_internal(n, _full_from_gathered(n, g))

    def grads(self, grp, gbig, after=None):
        names = GROUPS[grp]
        lays = [GATHERED.get(n, LEAD) for n in names]
        grads = [gbig[n] if lay != LEAD else _blocks_from_full(n, _from_internal(n, gbig[n]).astype(BF16))
                 for n, lay in zip(names, lays, strict=True)]
        lands = [lax.empty((4, *_shard_shape(g, lay)), BF16) for g, lay in zip(grads, lays, strict=True)]
        plan, cnt = _rs_sibling_plan(lays, len(names))
        ss, rs, bufs, self.token = split_start(f"rs{grp}_sibling_start", grads + lands, plan, cnt, after)
        self._rs[grp] = dict(names=names, lays=lays, sib=(ss, rs, bufs, plan))
        return self.token

    def _rs_mid(self, grp, after):
        st = self._rs[grp]
        n = len(st["names"])
        ss, rs, bufs, plan = st["sib"]
        bufs = split_wait(f"rs{grp}_sibling_wait", bufs, ss, rs, plan, after)
        sums = [rs_add(f"rs_add_{name}", g, lay, r, self.core)
                for name, g, lay, r in zip(st["names"], bufs[:n], st["lays"], bufs[n:], strict=True)]
        lands = [lax.empty((3, *s.shape[1:]), BF16) for s in sums]
        plan, cnt = _rs_chips_plan(n)
        ss, rs, bufs, self.token = split_start(f"rs{grp}_chips_start", sums + lands, plan, cnt)
        st["chips"] = (ss, rs, bufs, plan)

    def finish_grads(self, grp, after):
        st = self._rs[grp]
        n = len(st["names"])
        ss, rs, bufs, plan = st["chips"]
        bufs = split_wait(f"rs{grp}_chips_wait", bufs, ss, rs, plan, after)
        return st["names"], bufs[:n], bufs[n:]

    def point(self, name, after=None):
        if name == "start":
            self._ag_place("A1", None)
            self._ag_own_start("A1")
            first = self.token
            self._ag_place("A2", first)
            self._ag_place("B", first)
            self._ag_pass("A1", self._ag["B"]["lands"][-1])
            self._ag_own_start("A2", self.token)
            self._ag_own_start("B", self.token)
            self._ag_finish("A1", self.token)
        elif name == "ffn1_gu_done":
            self._ag_pass("A2", after)
            self._ag_finish("A2", self.token)
        elif name == "ffn1_fwd_done":
            self._ag_pass("B", after)
            self._ag_place("C", self.token)
            self._ag_own_start("C", self.token)
        elif name == "mix_pre_done":
            self._ag_finish("B", after)
        elif name == "flash_fwd_done":
            self._ag_pass("C", after)
        elif name == "ffn2_pre_done":
            self._ag_finish("C", after)
        elif name.startswith("rs") and name.endswith("_mid"):
            self._rs_mid(name[2:-4], after)


def local_step(x, target, gain, ex):
    d = x.shape[-1]
    ex.point("start")
    w = ex.w
    h0 = jnp.concatenate([jnp.zeros((PAD_ROWS, d), F32), ex.meta_full, x], axis=0)
    lp = h0.shape[0]
    tabs = rope_tables(lp)
    a1 = prenorm(h0, gain["norm_ffn1_pre"], after=ex.token)
    gu1, s1 = ffn_gu("ffn1", a1, w["ffn1_w_gu"])
    ex.point("ffn1_gu_done", s1)
    f1 = ffn_down("ffn1", s1, w["ffn1_w_down"])
    ex.point("ffn1_fwd_done", f1)
    h1, a2 = post_pre("post_pre1", f1, h0, gain["norm_ffn1_post"], 0.5, gain["norm_mix_pre"], after=ex.token)
    ex.point("mix_pre_done", a2)
    mix, sv = mixer_fwd(a2, w, tabs, gain["pool_scale"], gain["q_a_norm"], gain["kv_a_norm"], ex)
    h2, a3 = post_pre("post_pre2", mix, h1, gain["norm_mix_post"], 1.0, gain["norm_ffn2_pre"])
    ex.point("ffn2_pre_done", a3)
    gu2, s2 = ffn_gu("ffn2", a3, w["ffn2_w_gu"])
    f2 = ffn_down("ffn2", s2, w["ffn2_w_down"])
    dh3, loss_blk = post_loss(f2, h2, gain["norm_ffn2_post"], 0.5, target)

    gsmall = {}
    df2, gsmall["norm_ffn2_post"] = post_bwd(dh3, f2, gain["norm_ffn2_post"], 0.5)
    da3 = ffn_bwd("ffn2", df2, a3, gu2, s2, w["ffn2_w_gu"], w["ffn2_w_down"], lambda dwd: None,
                  lambda dwgu, dwd: ex.grads("C", {"ffn2_w_gu": dwgu, "ffn2_w_down": dwd}))
    dh2, dmix, gsmall["norm_ffn2_pre"], gsmall["norm_mix_post"] = pre_post_bwd(
        "pre_post_bwd2", da3, h2, gain["norm_ffn2_pre"], dh3, mix, gain["norm_mix_post"], 1.0)
    ex.point("rsC_mid", dmix)
    da2, gmix, gmix_small = mixer_bwd(dmix, a2, sv, w, tabs, gain["pool_scale"], gain["q_a_norm"], gain["kv_a_norm"],
                                      after=ex.token)
    gsmall.update(gmix_small)
    dh1, df1, gsmall["norm_mix_pre"], gsmall["norm_ffn1_post"] = pre_post_bwd(
        "pre_post_bwd1", da2, h1, gain["norm_mix_pre"], dh2, f1, gain["norm_ffn1_post"], 0.5,
        after=ex.grads("B", gmix))
    ex.point("rsB_mid", df1)
    def on_dwgu1(dwgu, dwd):
        ex.point("rsA2_mid", dwgu)
        return ex.grads("A1", {"ffn1_w_gu": dwgu}, after=ex.token)

    da1 = ffn_bwd("ffn1", df1, a1, gu1, s1, w["ffn1_w_gu"], w["ffn1_w_down"],
                  lambda dwd: ex.grads("A2", {"ffn1_w_down": dwd}), on_dwgu1, after=ex.token)
    ex.point("rsA1_mid", da1)
    grad_x, front, gsmall["norm_ffn1_pre"] = pre_bwd_first(da1, h0, gain["norm_ffn1_pre"], dh1, after=ex.token)
    return loss_blk, grad_x, front, gsmall
```

```python
import functools

import jax
import jax.numpy as jnp
import numpy as np
from jax import lax
from jax.experimental import pallas as pl
from jax.experimental.pallas import tpu as pltpu

F32 = jnp.float32
BF16 = jnp.bfloat16

N_META = 16
POOL_WINDOWS = (2, 4, 8, 16)
POOL_GROUP = 256
HEADS = 16
QK_NOPE = 128
QK_ROPE = 64
V_DIM = 128
ROPE_THETA = 10000.0
SOFTMAX_SCALE = (QK_NOPE + QK_ROPE) ** -0.5
EPS = 1e-6
ADAM_LR = 0.001
ADAM_B1 = 0.9
ADAM_B2 = 0.999
ADAM_EPS = 1e-08
ADAM_WD = 0.01
ADAM_STEP = 10

LANE = 128
FRONT = 128
PAD_ROWS = FRONT - N_META
HEAD_W = 256
GU_TILE = 1408
VMEM_LIMIT = 56 * 1024 * 1024
MESH_AXES = ("x", "y", "c")
N_DEV = 8


def _pick(n, cands):
    for c in cands:
        if n % c == 0:
            return c
    raise ValueError(f"no tile for {n} in {cands}")


def _cparams(sem=None):
    kw = dict(vmem_limit_bytes=VMEM_LIMIT)
    if sem is not None:
        kw["dimension_semantics"] = sem
    return pltpu.CompilerParams(**kw)


_DIMS = {"nn": (((1,), (0,)), ((), ())), "nt": (((1,), (1,)), ((), ())), "tn": (((0,), (0,)), ((), ()))}


def _behind(after):
    return [] if after is None else list(after) if isinstance(after, (list, tuple)) else [after]


def _behind_specs(after):
    return [pl.BlockSpec(memory_space=pl.ANY)] * len(_behind(after))


def _mm(name, form, a, b, tm, tn, outs, epi=None, extras=(), n_outer=False, after=None):
    if form == "tn":
        k, m = a.shape
        n = b.shape[1]
        a_blk, a_map = (k, tm), lambda i, j: (0, i)
        b_blk, b_map = (k, tn), lambda i, j: (0, j)
    elif form == "nn":
        m, k = a.shape
        n = b.shape[1]
        a_blk, a_map = (tm, k), lambda i, j: (i, 0)
        b_blk, b_map = (k, tn), lambda i, j: (0, j)
    else:
        m, k = a.shape
        n = b.shape[0]
        a_blk, a_map = (tm, k), lambda i, j: (i, 0)
        b_blk, b_map = (tn, k), lambda i, j: (j, 0)
    assert m % tm == 0 and n % tn == 0, (name, m, n, tm, tn)
    n_ex = len(extras)
    dn = _DIMS[form]
    if n_outer:
        grid = (n // tn, m // tm)

        def spec(blk, im):
            return pl.BlockSpec(blk, lambda gj, gi: im(gi, gj))
    else:
        grid = (m // tm, n // tn)
        spec = pl.BlockSpec

    behind = _behind(after)

    def body(a_ref, b_ref, *rest):
        ex, out_refs = rest[:n_ex], rest[n_ex + len(behind):]
        acc = lax.dot_general(a_ref[...].astype(BF16), b_ref[...].astype(BF16), dn, preferred_element_type=F32)
        res = epi(acc, *[e[...] for e in ex]) if epi is not None else (acc,)
        for r, o in zip(res, out_refs, strict=True):
            o[...] = r.astype(o.dtype)

    return pl.pallas_call(
        body,
        name=name,
        grid=grid,
        in_specs=[spec(a_blk, a_map), spec(b_blk, b_map)] + [spec(blk, im) for _, blk, im in extras] + _behind_specs(after),
        out_specs=[spec(blk, im) for _, _, blk, im in outs],
        out_shape=[jax.ShapeDtypeStruct(s, d) for s, d, _, _ in outs],
        compiler_params=_cparams(("parallel", "parallel")),
    )(a, b, *[e for e, _, _ in extras], *behind)


def _mm_plain(name, form, a, b, tm, tn, out_dtype, after=None):
    m = a.shape[1] if form == "tn" else a.shape[0]
    n = b.shape[0] if form == "nt" else b.shape[1]
    return _mm(name, form, a, b, tm, tn, [((m, n), out_dtype, (tm, tn), lambda i, j: (i, j))], after=after)[0]


def _rstd(x):
    return lax.rsqrt(jnp.mean(x * x, axis=-1, keepdims=True) + EPS)


def _norm_bwd(dy, x, gain):
    r = _rstd(x)
    dyg = dy * gain
    dx = r * (dyg - x * (r * r) * jnp.mean(dyg * x, axis=-1, keepdims=True))
    dgain = jnp.sum(dy * x * r, axis=0, keepdims=True)
    return dx, dgain


def _row_spec(tr, cols, col_block=0):
    return pl.BlockSpec((tr, cols), lambda i: (i, col_block))


def _vec_spec(cols, col_block=0):
    return pl.BlockSpec((1, cols), lambda i: (0, col_block))


def _row_tile(lp):
    return _pick(lp, (128,))


def _skip_behind(after, body):
    n = len(_behind(after))
    return body if n == 0 else (lambda *refs: body(*refs[n:]))


def prenorm(h, gain, after=None):
    lp, d = h.shape
    tr = _row_tile(lp)

    def body(h_ref, g_ref, a_ref):
        x = h_ref[...]
        a_ref[...] = (x * _rstd(x) * g_ref[...]).astype(BF16)

    return pl.pallas_call(
        _skip_behind(after, body), name="prenorm", grid=(lp // tr,),
        in_specs=_behind_specs(after) + [_row_spec(tr, d), _vec_spec(d)], out_specs=_row_spec(tr, d),
        out_shape=jax.ShapeDtypeStruct((lp, d), BF16), compiler_params=_cparams(("parallel",)),
    )(*_behind(after), h, gain)


def post_pre(name, f, h_in, g_post, coef, g_next, after=None):
    lp, d = f.shape
    tr = _row_tile(lp)

    def body(f_ref, h_ref, gp_ref, gn_ref, ho_ref, a_ref):
        fv = f_ref[...]
        ho = h_ref[...] + coef * (fv * _rstd(fv) * gp_ref[...])
        ho_ref[...] = ho
        a_ref[...] = (ho * _rstd(ho) * gn_ref[...]).astype(BF16)

    return pl.pallas_call(
        _skip_behind(after, body), name=name, grid=(lp // tr,),
        in_specs=_behind_specs(after) + [_row_spec(tr, d), _row_spec(tr, d), _vec_spec(d), _vec_spec(d)],
        out_specs=[_row_spec(tr, d), _row_spec(tr, d)],
        out_shape=[jax.ShapeDtypeStruct((lp, d), F32), jax.ShapeDtypeStruct((lp, d), BF16)],
        compiler_params=_cparams(("parallel",)),
    )(*_behind(after), f, h_in, g_post, g_next)


def post_loss(f, h_in, g_post, coef, target):
    lp, d = f.shape
    tr = _row_tile(lp)
    front_tiles = FRONT // tr

    def body(f_ref, h_ref, gp_ref, t_ref, dh_ref, loss_ref):
        i = pl.program_id(0)

        @pl.when(i == 0)
        def _():
            loss_ref[...] = jnp.zeros_like(loss_ref)

        @pl.when(i < front_tiles)
        def _():
            dh_ref[...] = jnp.zeros_like(dh_ref)

        @pl.when(i >= front_tiles)
        def _():
            fv = f_ref[...]
            ho = h_ref[...] + coef * (fv * _rstd(fv) * gp_ref[...])
            err = ho - t_ref[...]
            dh_ref[...] = err / d
            tok = jnp.mean(err * err, axis=-1, keepdims=True)
            loss_ref[...] += 0.5 * jnp.sum(tok)

    return pl.pallas_call(
        body, name="post_loss", grid=(lp // tr,),
        in_specs=[_row_spec(tr, d), _row_spec(tr, d), _vec_spec(d),
                  pl.BlockSpec((tr, d), lambda i: (jnp.maximum(i - front_tiles, 0), 0))],
        out_specs=[_row_spec(tr, d), pl.BlockSpec((8, LANE), lambda i: (0, 0))],
        out_shape=[jax.ShapeDtypeStruct((lp, d), F32), jax.ShapeDtypeStruct((8, LANE), F32)],
        compiler_params=_cparams(("arbitrary",)),
    )(f, h_in, g_post, target)


def post_bwd(dh_out, f, g_post, coef):
    lp, d = f.shape
    tr = _row_tile(lp)

    def body(dh_ref, f_ref, gp_ref, df_ref, dg_ref):
        @pl.when(pl.program_id(0) == 0)
        def _():
            dg_ref[...] = jnp.zeros_like(dg_ref)

        df, dg = _norm_bwd(coef * dh_ref[...], f_ref[...], gp_ref[...])
        df_ref[...] = df.astype(BF16)
        dg_ref[...] += dg

    return pl.pallas_call(
        body, name="post_bwd", grid=(lp // tr,),
        in_specs=[_row_spec(tr, d), _row_spec(tr, d), _vec_spec(d)],
        out_specs=[_row_spec(tr, d), _vec_spec(d)],
        out_shape=[jax.ShapeDtypeStruct((lp, d), BF16), jax.ShapeDtypeStruct((1, d), F32)],
        compiler_params=_cparams(("arbitrary",)),
    )(dh_out, f, g_post)


def pre_post_bwd(name, da, h_mid, g_pre, dh_out, f_prev, g_post_prev, coef_prev, after=None):
    lp, d = da.shape
    tr = _row_tile(lp)

    def body(da_ref, h_ref, gpre_ref, dho_ref, f_ref, gpost_ref, dh_ref, df_ref, dgpre_ref, dgpost_ref):
        @pl.when(pl.program_id(0) == 0)
        def _():
            dgpre_ref[...] = jnp.zeros_like(dgpre_ref)
            dgpost_ref[...] = jnp.zeros_like(dgpost_ref)

        dx, dgpre = _norm_bwd(da_ref[...], h_ref[...], gpre_ref[...])
        dh = dho_ref[...] + dx
        dh_ref[...] = dh
        dgpre_ref[...] += dgpre
        df, dgpost = _norm_bwd(coef_prev * dh, f_ref[...], gpost_ref[...])
        df_ref[...] = df.astype(BF16)
        dgpost_ref[...] += dgpost

    return pl.pallas_call(
        _skip_behind(after, body), name=name, grid=(lp // tr,),
        in_specs=_behind_specs(after) + [_row_spec(tr, d), _row_spec(tr, d), _vec_spec(d), _row_spec(tr, d),
                                         _row_spec(tr, d), _vec_spec(d)],
        out_specs=[_row_spec(tr, d), _row_spec(tr, d), _vec_spec(d), _vec_spec(d)],
        out_shape=[jax.ShapeDtypeStruct((lp, d), F32), jax.ShapeDtypeStruct((lp, d), BF16),
                   jax.ShapeDtypeStruct((1, d), F32), jax.ShapeDtypeStruct((1, d), F32)],
        compiler_params=_cparams(("arbitrary",)),
    )(*_behind(after), da, h_mid, g_pre, dh_out, f_prev, g_post_prev)


def pre_bwd_first(da, h0, g_pre, dh_out, after=None):
    lp, d = da.shape
    tr = _row_tile(lp)
    front_tiles = FRONT // tr
    assert front_tiles == 1

    def body(da_ref, h_ref, gpre_ref, dho_ref, gx_ref, front_ref, dgpre_ref):
        i = pl.program_id(0)

        @pl.when(i == 0)
        def _():
            dgpre_ref[...] = jnp.zeros_like(dgpre_ref)

        dx, dgpre = _norm_bwd(da_ref[...], h_ref[...], gpre_ref[...])
        dh = dho_ref[...] + dx
        dgpre_ref[...] += dgpre
        gx_ref[...] = dh

        @pl.when(i == 0)
        def _():
            front_ref[...] = dh

    return pl.pallas_call(
        _skip_behind(after, body), name="pre_bwd_first", grid=(lp // tr,),
        in_specs=_behind_specs(after) + [_row_spec(tr, d), _row_spec(tr, d), _vec_spec(d), _row_spec(tr, d)],
        out_specs=[pl.BlockSpec((tr, d), lambda i: (jnp.maximum(i - front_tiles, 0), 0)),
                   pl.BlockSpec((tr, d), lambda i: (0, 0)), _vec_spec(d)],
        out_shape=[jax.ShapeDtypeStruct((lp - FRONT, d), F32), jax.ShapeDtypeStruct((tr, d), F32),
                   jax.ShapeDtypeStruct((1, d), F32)],
        compiler_params=_cparams(("arbitrary",)),
    )(*_behind(after), da, h0, g_pre, dh_out)


def _m_tile(lp):
    return _pick(lp, (1056, 512, 256, 128))


def ffn_gu(tag, a, wgu_p):
    lp, d = a.shape
    f2 = wgu_p.shape[1]

    def epi(acc):
        g, u = acc[:, :GU_TILE], acc[:, GU_TILE:]
        return acc, g * jax.nn.sigmoid(g) * u

    tg = _pick(lp, (384, 256, 128))
    return _mm(tag + "_gu", "nn", a, wgu_p, tg, 2 * GU_TILE,
               [((lp, f2), F32, (tg, 2 * GU_TILE), lambda i, j: (i, j)),
                ((lp, f2 // 2), BF16, (tg, GU_TILE), lambda i, j: (i, j))], epi=epi, n_outer=True)


def ffn_down(tag, s, wd):
    return _mm_plain(tag + "_down", "nn", s, wd, _m_tile(s.shape[0]), 512, F32)


def ffn_bwd(tag, df, a, gu, s, wgu_p, wd, on_dwd, on_dwgu, after=None):
    lp, d = df.shape
    f2 = wgu_p.shape[1]
    tm = _m_tile(lp)

    def epi(acc, gu_t):
        g, u = gu_t[:, :GU_TILE], gu_t[:, GU_TILE:]
        sig = jax.nn.sigmoid(g)
        dg = acc * u * (sig * (1.0 + g * (1.0 - sig)))
        du = acc * (g * sig)
        return (jnp.concatenate([dg, du], axis=1),)

    ts = _pick(lp, (528, 256, 128))
    dgu = _mm(tag + "_ds", "nt", df, wd, ts, GU_TILE,
              [((lp, f2), BF16, (ts, 2 * GU_TILE), lambda i, j: (i, j))], epi=epi,
              extras=[(gu, (ts, 2 * GU_TILE), lambda i, j: (i, j))], n_outer=True, after=after)[0]
    dwd = _mm_plain(tag + "_dwd", "tn", s, df, 512, _pick(d, (1024,)), BF16, after=after)
    dwgu = _mm_plain(tag + "_dwgu", "tn", a, dgu, _pick(d, (1024,)), 1024, BF16, after=on_dwd(dwd))
    return _mm_plain(tag + "_da", "nt", dgu, wgu_p, _pick(lp, (528, 256, 128)), 512, F32, after=on_dwgu(dwgu, dwd))


Z_U, Z_CQ, Z_CKV, Z_GP, Z_GM, Z_KR, Z_COLS = 0, 1024, 1536, 2048, 4096, 6144, 6400
POOL_W = POOL_GROUP * len(POOL_WINDOWS)
HALO = 16


def _pool_counts(lp, w):
    pos = lax.broadcasted_iota(jnp.int32, (lp, 1), 0) - PAD_ROWS
    return jnp.clip(pos + 1, 1, w).astype(F32)


def _pool_diff(u_ref, pad_ref, lp, w):
    pad_ref[pl.ds(0, HALO), :] = jnp.zeros((HALO, POOL_GROUP), F32)
    pad_ref[pl.ds(HALO, lp), :] = u_ref[...]
    acc = pad_ref[pl.ds(HALO, lp), :]
    for s in range(1, w):
        acc = acc + pad_ref[pl.ds(HALO - s, lp), :]
    return acc / _pool_counts(lp, w) - u_ref[...]


def pool_fwd(z, pool_w, pool_scale):
    lp = z.shape[0]
    ng = len(POOL_WINDOWS)

    def body(u_ref, w_ref, sc_ref, o_ref, pad_ref):
        for g, w in enumerate(POOL_WINDOWS):
            @pl.when(pl.program_id(0) == g)
            def _(w=w):
                dd = _pool_diff(u_ref, pad_ref, lp, w)
                y = jnp.dot(dd.astype(BF16), w_ref[0], preferred_element_type=F32)
                o_ref[...] = (y * sc_ref[...]).astype(BF16)

    return pl.pallas_call(
        body, name="pool_fwd", grid=(ng,),
        in_specs=[pl.BlockSpec((lp, POOL_GROUP), lambda g: (0, g)),
                  pl.BlockSpec((1, POOL_GROUP, POOL_GROUP), lambda g: (g, 0, 0)),
                  pl.BlockSpec((1, POOL_GROUP), lambda g: (0, g))],
        out_specs=pl.BlockSpec((lp, POOL_GROUP), lambda g: (0, g)),
        out_shape=jax.ShapeDtypeStruct((lp, POOL_W), BF16),
        scratch_shapes=[pltpu.VMEM((lp + HALO, POOL_GROUP), F32)],
        compiler_params=_cparams(("parallel",)),
    )(z, pool_w, pool_scale)


def pool_bwd(dyp, z, pool_w, pool_scale):
    lp = z.shape[0]
    ng = len(POOL_WINDOWS)

    def body(dy_ref, u_ref, w_ref, sc_ref, du_ref, dw_ref, dsc_ref, pad_ref):
        for g, w in enumerate(POOL_WINDOWS):
            @pl.when(pl.program_id(0) == g)
            def _(w=w):
                dd = _pool_diff(u_ref, pad_ref, lp, w).astype(BF16)
                wg = w_ref[0]
                ypre = jnp.dot(dd, wg, preferred_element_type=F32)
                dy = dy_ref[...]
                dsc_ref[...] = jnp.sum(dy * ypre, axis=0, keepdims=True)
                dypre = (dy * sc_ref[...]).astype(BF16)
                dw_ref[0] = lax.dot_general(dd, dypre, _DIMS["tn"], preferred_element_type=F32)
                ddd = lax.dot_general(dypre, wg, _DIMS["nt"], preferred_element_type=F32)
                pad_ref[pl.ds(0, lp), :] = ddd / _pool_counts(lp, w)
                pad_ref[pl.ds(lp, HALO), :] = jnp.zeros((HALO, POOL_GROUP), F32)
                acc = -ddd
                for s in range(w):
                    acc = acc + pad_ref[pl.ds(s, lp), :]
                du_ref[...] = acc.astype(BF16)

    return pl.pallas_call(
        body, name="pool_bwd", grid=(ng,),
        in_specs=[pl.BlockSpec((lp, POOL_GROUP), lambda g: (0, g)),
                  pl.BlockSpec((lp, POOL_GROUP), lambda g: (0, g)),
                  pl.BlockSpec((1, POOL_GROUP, POOL_GROUP), lambda g: (g, 0, 0)),
                  pl.BlockSpec((1, POOL_GROUP), lambda g: (0, g))],
        out_specs=[pl.BlockSpec((lp, POOL_GROUP), lambda g: (0, g)),
                   pl.BlockSpec((1, POOL_GROUP, POOL_GROUP), lambda g: (g, 0, 0)),
                   pl.BlockSpec((1, POOL_GROUP), lambda g: (0, g))],
        out_shape=[jax.ShapeDtypeStruct((lp, POOL_W), BF16),
                   jax.ShapeDtypeStruct((ng, POOL_GROUP, POOL_GROUP), F32),
                   jax.ShapeDtypeStruct((1, POOL_W), F32)],
        scratch_shapes=[pltpu.VMEM((lp + HALO, POOL_GROUP), F32)],
        compiler_params=_cparams(("parallel",)),
    )(dyp, z, pool_w, pool_scale)


Q_COLS = HEADS * HEAD_W
ROPE_BLOCK = LANE


def rope_tables(lp):
    pos = jnp.maximum(jnp.arange(lp, dtype=F32) - PAD_ROWS, 0.0)
    inv = ROPE_THETA ** (-jnp.arange(0, QK_ROPE, 2, dtype=F32) / QK_ROPE)
    ang = pos[:, None] * inv[None, :]
    cos, sin, zero = jnp.cos(ang), jnp.sin(ang), jnp.zeros_like(ang)
    return jnp.stack([jnp.concatenate([cos, cos, zero, zero], axis=1),
                      jnp.concatenate([-sin, zero, zero, zero], axis=1),
                      jnp.concatenate([zero, sin, zero, zero], axis=1)])


def _rope(x, tabs):
    return x * tabs[0] + pltpu.roll(x, 96, 1) * tabs[1] + pltpu.roll(x, 32, 1) * tabs[2]


def _rope_bwd(g, tabs):
    return g * tabs[0] + pltpu.roll(g * tabs[1], 32, 1) + pltpu.roll(g * tabs[2], 96, 1)


def _tab_spec(tr):
    return pl.BlockSpec((3, tr, ROPE_BLOCK), lambda i: (0, i, 0))


def mla_prep(z, g_q, g_kv, tabs):
    lp = z.shape[0]
    tr = _row_tile(lp)
    r = g_q.shape[1]

    def body(cq_ref, ckv_ref, kr_ref, gq_ref, gkv_ref, tab_ref, qn_ref, kvn_ref, kpe_ref):
        cq, ckv = cq_ref[...], ckv_ref[...]
        qn_ref[...] = (cq * _rstd(cq) * gq_ref[...]).astype(BF16)
        kvn_ref[...] = (ckv * _rstd(ckv) * gkv_ref[...]).astype(BF16)
        kpe_ref[...] = _rope(kr_ref[...], tab_ref[...]).astype(BF16)

    return pl.pallas_call(
        body, name="mla_prep", grid=(lp // tr,),
        in_specs=[_row_spec(tr, r, Z_CQ // r), _row_spec(tr, r, Z_CKV // r), _row_spec(tr, ROPE_BLOCK, Z_KR // ROPE_BLOCK),
                  _vec_spec(r), _vec_spec(r), _tab_spec(tr)],
        out_specs=[_row_spec(tr, r), _row_spec(tr, r), _row_spec(tr, ROPE_BLOCK)],
        out_shape=[jax.ShapeDtypeStruct((lp, r), BF16), jax.ShapeDtypeStruct((lp, r), BF16),
                   jax.ShapeDtypeStruct((lp, ROPE_BLOCK), BF16)],
        compiler_params=_cparams(("parallel",)),
    )(z, z, z, g_q, g_kv, tabs)


def q_proj(qn, wq_p, tabs):
    lp = qn.shape[0]
    tm = _m_tile(lp)
    tn = 4 * HEAD_W

    def epi(acc, tab):
        acc = acc * SOFTMAX_SCALE
        parts = []
        for t in range(tn // HEAD_W):
            parts.append(acc[:, t * HEAD_W:t * HEAD_W + QK_NOPE])
            parts.append(_rope(acc[:, t * HEAD_W + QK_NOPE:(t + 1) * HEAD_W], tab))
        return (jnp.concatenate(parts, axis=1),)

    return _mm("q_proj", "nn", qn, wq_p, tm, tn, [((lp, Q_COLS), BF16, (tm, tn), lambda i, j: (i, j))], epi=epi,
               extras=[(tabs, (3, tm, ROPE_BLOCK), lambda i, j: (0, i, 0))])[0]


def kv_proj(kvn, wkv):
    return _mm_plain("kv_proj", "nn", kvn, wkv, _m_tile(kvn.shape[0]), 1024, BF16)


def q_rope_bwd(dq, tabs):
    lp = dq.shape[0]
    tr = _row_tile(lp)

    def body(dq_ref, tab_ref, o_ref):
        tab = tab_ref[...]
        for h in range(HEADS):
            nope = dq_ref[:, h * HEAD_W:h * HEAD_W + QK_NOPE] * SOFTMAX_SCALE
            o_ref[:, h * HEAD_W:h * HEAD_W + QK_NOPE] = nope.astype(BF16)
            o_ref[:, h * HEAD_W + QK_NOPE:(h + 1) * HEAD_W] = _rope_bwd(
                dq_ref[:, h * HEAD_W + QK_NOPE:(h + 1) * HEAD_W] * SOFTMAX_SCALE, tab).astype(BF16)

    return pl.pallas_call(
        body, name="q_rope_bwd", grid=(lp // tr,),
        in_specs=[_row_spec(tr, Q_COLS), _tab_spec(tr)], out_specs=_row_spec(tr, Q_COLS),
        out_shape=jax.ShapeDtypeStruct((lp, Q_COLS), BF16), compiler_params=_cparams(("parallel",)),
    )(dq, tabs)


def mla_prep_bwd(dqn, dkvn, dkpe_h, z, g_q, g_kv, tabs):
    lp = z.shape[0]
    tr = _row_tile(lp)
    r = g_q.shape[1]

    def body(dqn_ref, dkvn_ref, dkpe_ref, cq_ref, ckv_ref, gq_ref, gkv_ref, tab_ref,
             dcq_ref, dckv_ref, dkr_ref, dgq_ref, dgkv_ref):
        @pl.when(pl.program_id(0) == 0)
        def _():
            dgq_ref[...] = jnp.zeros_like(dgq_ref)
            dgkv_ref[...] = jnp.zeros_like(dgkv_ref)

        dcq, dgq = _norm_bwd(dqn_ref[...], cq_ref[...], gq_ref[...])
        dckv, dgkv = _norm_bwd(dkvn_ref[...], ckv_ref[...], gkv_ref[...])
        dcq_ref[...] = dcq.astype(BF16)
        dckv_ref[...] = dckv.astype(BF16)
        dgq_ref[...] += dgq
        dgkv_ref[...] += dgkv
        dkpe = dkpe_ref[0]
        for h in range(1, HEADS):
            dkpe = dkpe + dkpe_ref[h]
        dkr_ref[...] = _rope_bwd(dkpe, tab_ref[...]).astype(BF16)

    return pl.pallas_call(
        body, name="mla_prep_bwd", grid=(lp // tr,),
        in_specs=[_row_spec(tr, r), _row_spec(tr, r), pl.BlockSpec((HEADS, tr, ROPE_BLOCK), lambda i: (0, i, 0)),
                  _row_spec(tr, r, Z_CQ // r), _row_spec(tr, r, Z_CKV // r), _vec_spec(r), _vec_spec(r), _tab_spec(tr)],
        out_specs=[_row_spec(tr, r), _row_spec(tr, r), _row_spec(tr, ROPE_BLOCK), _vec_spec(r), _vec_spec(r)],
        out_shape=[jax.ShapeDtypeStruct((lp, r), BF16), jax.ShapeDtypeStruct((lp, r), BF16),
                   jax.ShapeDtypeStruct((lp, ROPE_BLOCK), BF16),
                   jax.ShapeDtypeStruct((1, r), F32), jax.ShapeDtypeStruct((1, r), F32)],
        compiler_params=_cparams(("arbitrary",)),
    )(dqn, dkvn, dkpe_h, z, z, g_q, g_kv, tabs)


def _attn_tile(lp):
    return _pick(lp, (528, 128))


def _scores(q, kcat, q_tile, k_tile, t, masked):
    s = lax.dot_general(q, kcat, _DIMS["nt"], preferred_element_type=F32)
    if not masked:
        return s
    qpos = q_tile * t + lax.broadcasted_iota(jnp.int32, (t, t), 0)
    kpos = k_tile * t + lax.broadcasted_iota(jnp.int32, (t, t), 1)
    return jnp.where((kpos <= qpos) & (kpos >= PAD_ROWS), s, jnp.float32(-1e30))


def _causal_pairs(nt, k_major):
    if k_major:
        pairs = [(qi, ki) for ki in range(nt) for qi in range(ki, nt)]
    else:
        pairs = [(qi, ki) for qi in range(nt) for ki in range(qi + 1)]
    return (jnp.asarray([p[0] for p in pairs], jnp.int32), jnp.asarray([p[1] for p in pairs], jnp.int32))


def _on_masked_or_not(q_tile, k_tile, fn):
    needs_mask = (q_tile == k_tile) | (k_tile == 0)

    @pl.when(needs_mask)
    def _():
        fn(True)

    @pl.when(jnp.logical_not(needs_mask))
    def _():
        fn(False)


def flash_fwd(q, kv, kpe):
    lp = q.shape[0]
    t = _attn_tile(lp)
    q_tab, k_tab = _causal_pairs(lp // t, k_major=False)

    def body(q_tab_ref, k_tab_ref, q_ref, kv_ref, kpe_ref, o32_ref, o16_ref, lse_ref, m_sc, l_sc, acc_sc):
        pair = pl.program_id(1)
        qi, ki = q_tab_ref[pair], k_tab_ref[pair]

        @pl.when(ki == 0)
        def _():
            m_sc[...] = jnp.full_like(m_sc, -jnp.inf)
            l_sc[...] = jnp.zeros_like(l_sc)
            acc_sc[...] = jnp.zeros_like(acc_sc)

        def step(masked):
            kvt = kv_ref[...]
            kcat = jnp.concatenate([kvt[:, :QK_NOPE], kpe_ref[...]], axis=1)
            s = _scores(q_ref[...], kcat, qi, ki, t, masked)
            m_prev = m_sc[...]
            m_new = jnp.maximum(m_prev, jnp.max(s, axis=1, keepdims=True))
            alpha = jnp.exp(m_prev - m_new)
            p = jnp.exp(s - m_new[:, :1])
            l_sc[...] = alpha * l_sc[...] + jnp.sum(p, axis=1, keepdims=True)
            acc_sc[...] = alpha * acc_sc[...] + jnp.dot(p.astype(BF16), kvt[:, QK_NOPE:], preferred_element_type=F32)
            m_sc[...] = m_new

        _on_masked_or_not(qi, ki, step)

        @pl.when(ki == qi)
        def _():
            l = l_sc[...]
            o = acc_sc[...] / l
            o32_ref[...] = o
            o16_ref[...] = o.astype(BF16)
            lse_ref[0] = m_sc[...] + jnp.log(l)

    return pl.pallas_call(
        body, name="flash_fwd",
        grid_spec=pltpu.PrefetchScalarGridSpec(
            num_scalar_prefetch=2, grid=(HEADS, q_tab.shape[0]),
            in_specs=[pl.BlockSpec((t, HEAD_W), lambda h, p, qt, kt: (qt[p], h)),
                      pl.BlockSpec((t, HEAD_W), lambda h, p, qt, kt: (kt[p], h)),
                      pl.BlockSpec((t, ROPE_BLOCK), lambda h, p, qt, kt: (kt[p], 0))],
            out_specs=[pl.BlockSpec((t, V_DIM), lambda h, p, qt, kt: (qt[p], h)),
                       pl.BlockSpec((t, V_DIM), lambda h, p, qt, kt: (qt[p], h)),
                       pl.BlockSpec((1, t, LANE), lambda h, p, qt, kt: (h, qt[p], 0))],
            scratch_shapes=[pltpu.VMEM((t, LANE), F32), pltpu.VMEM((t, LANE), F32), pltpu.VMEM((t, V_DIM), F32)]),
        out_shape=[jax.ShapeDtypeStruct((lp, HEADS * V_DIM), F32), jax.ShapeDtypeStruct((lp, HEADS * V_DIM), BF16),
                   jax.ShapeDtypeStruct((HEADS, lp, LANE), F32)],
        compiler_params=_cparams(("parallel", "arbitrary")),
    )(q_tab, k_tab, q, kv, kpe)


def flash_bwd(q, kv, kpe, o32, lse, do):
    lp = q.shape[0]
    t = _attn_tile(lp)
    nt = lp // t
    q_tab, k_tab = _causal_pairs(nt, k_major=True)

    def body(q_tab_ref, k_tab_ref, q_ref, kv_ref, kpe_ref, o_ref, lse_ref, do_ref, dq_ref, dkv_ref, dkpe_ref,
             dk_sc, dv_sc):
        pair = pl.program_id(1)
        qi, ki = q_tab_ref[pair], k_tab_ref[pair]

        @pl.when(pair == 0)
        def _():
            dq_ref[...] = jnp.zeros_like(dq_ref)

        @pl.when(qi == ki)
        def _():
            dk_sc[...] = jnp.zeros_like(dk_sc)
            dv_sc[...] = jnp.zeros_like(dv_sc)

        def step(masked):
            qt = q_ref[...]
            kvt = kv_ref[...]
            kcat = jnp.concatenate([kvt[:, :QK_NOPE], kpe_ref[...]], axis=1)
            s = _scores(qt, kcat, qi, ki, t, masked)
            p = jnp.exp(s - lse_ref[0][:, :1])
            do = do_ref[...]
            delta = jnp.sum(do * o_ref[...], axis=1, keepdims=True)
            do16 = do.astype(BF16)
            dv_sc[...] += lax.dot_general(p.astype(BF16), do16, _DIMS["tn"], preferred_element_type=F32)
            dp = lax.dot_general(do16, kvt[:, QK_NOPE:], _DIMS["nt"], preferred_element_type=F32)
            ds = (p * (dp - delta)).astype(BF16)
            dk_sc[...] += lax.dot_general(ds, qt, _DIMS["tn"], preferred_element_type=F32)
            row = pl.multiple_of(qi * t, t)
            dq_ref[pl.ds(row, t), :] += jnp.dot(ds, kcat, preferred_element_type=F32)

        _on_masked_or_not(qi, ki, step)

        @pl.when(qi == nt - 1)
        def _():
            dk = dk_sc[...]
            dkv_ref[...] = jnp.concatenate([dk[:, :QK_NOPE], dv_sc[...]], axis=1).astype(BF16)
            dkpe_ref[0] = dk[:, QK_NOPE:]

    qmap = lambda h, p, qt, kt: (qt[p], h)
    return pl.pallas_call(
        body, name="flash_bwd",
        grid_spec=pltpu.PrefetchScalarGridSpec(
            num_scalar_prefetch=2, grid=(HEADS, q_tab.shape[0]),
            in_specs=[pl.BlockSpec((t, HEAD_W), qmap),
                      pl.BlockSpec((t, HEAD_W), lambda h, p, qt, kt: (kt[p], h)),
                      pl.BlockSpec((t, ROPE_BLOCK), lambda h, p, qt, kt: (kt[p], 0)),
                      pl.BlockSpec((t, V_DIM), qmap),
                      pl.BlockSpec((1, t, LANE), lambda h, p, qt, kt: (h, qt[p], 0)),
                      pl.BlockSpec((t, V_DIM), qmap)],
            out_specs=[pl.BlockSpec((lp, HEAD_W), lambda h, p, qt, kt: (0, h)),
                       pl.BlockSpec((t, HEAD_W), lambda h, p, qt, kt: (kt[p], h)),
                       pl.BlockSpec((1, t, ROPE_BLOCK), lambda h, p, qt, kt: (h, kt[p], 0))],
            scratch_shapes=[pltpu.VMEM((t, HEAD_W), F32), pltpu.VMEM((t, V_DIM), F32)]),
        out_shape=[jax.ShapeDtypeStruct((lp, Q_COLS), F32), jax.ShapeDtypeStruct((lp, Q_COLS), BF16),
                   jax.ShapeDtypeStruct((HEADS, lp, ROPE_BLOCK), F32)],
        compiler_params=_cparams(("parallel", "arbitrary")),
    )(q_tab, k_tab, q, kv, kpe, o32, lse, do)


def _ij(i, j):
    return (i, j)


def mixer_fwd(a2, w, tabs, pool_scale, g_q, g_kv, ex):
    lp, d = a2.shape
    tm = _m_tile(lp)
    tn = 512
    z = _mm_plain("mix_in", "nn", a2, w["w_in"], tm, 1280, F32)
    yp = pool_fwd(z, w["pool_w"], pool_scale)
    qn, kvn, kpe = mla_prep(z, g_q, g_kv, tabs)
    q = q_proj(qn, w["w_q_b"], tabs)
    kv = kv_proj(kvn, w["w_kv_b"])
    o32, o16, lse = flash_fwd(q, kv, kpe)
    ex.point("flash_fwd_done", o16)
    y_pool = _mm_plain("pool_out", "nn", yp, w["w_pool_o"], tm, tn, F32, after=ex.token)

    def epi(acc, ypl, gp, gm):
        return jax.nn.sigmoid(gp) * ypl + jax.nn.sigmoid(gm) * acc, acc

    y, y_mla = _mm("mla_out_gate", "nn", o16, w["w_mla_o"], tm, tn,
                   [((lp, d), BF16, (tm, tn), _ij), ((lp, d), F32, (tm, tn), _ij)], epi=epi,
                   extras=[(y_pool, (tm, tn), _ij), (z, (tm, tn), lambda i, j: (i, Z_GP // tn + j)),
                           (z, (tm, tn), lambda i, j: (i, Z_GM // tn + j))])
    m = _mm_plain("mix_out", "nn", y, w["w_out"], tm, tn, F32)
    return m, dict(z=z, yp=yp, qn=qn, kvn=kvn, kpe=kpe, q=q, kv=kv, o32=o32, o16=o16, lse=lse,
                   y_pool=y_pool, y_mla=y_mla, y=y)


def mixer_bwd(dm, a2, sv, w, tabs, pool_scale, g_q, g_kv, after=None):
    lp, d = dm.shape
    tm = _m_tile(lp)
    tn = 512
    z = sv["z"]

    def epi(acc, ypl, yml, gp, gm):
        sp, sm = jax.nn.sigmoid(gp), jax.nn.sigmoid(gm)
        return acc * sp, acc * sm, acc * ypl * (sp * (1.0 - sp)), acc * yml * (sm * (1.0 - sm))

    dyp, dym, dgp, dgm = _mm(
        "gate_bwd", "nt", dm, w["w_out"], tm, tn, [((lp, d), BF16, (tm, tn), _ij)] * 4, epi=epi,
        extras=[(sv["y_pool"], (tm, tn), _ij), (sv["y_mla"], (tm, tn), _ij),
                (z, (tm, tn), lambda i, j: (i, Z_GP // tn + j)), (z, (tm, tn), lambda i, j: (i, Z_GM // tn + j))],
        after=after)
    g = {}
    g["w_out"] = _mm_plain("dw_out", "tn", sv["y"], dm, 1024, 1024, BF16, after=after)
    g["w_pool_o"] = _mm_plain("dw_pool_o", "tn", sv["yp"], dyp, 512, 1024, BF16)
    dypre = _mm_plain("pool_out_bwd", "nt", dyp, w["w_pool_o"], tm, tn, F32)
    du, g["pool_w"], d_pool_scale = pool_bwd(dypre, z, w["pool_w"], pool_scale)
    g["w_mla_o"] = _mm_plain("dw_mla_o", "tn", sv["o16"], dym, 1024, 1024, BF16)
    do = _mm_plain("mla_out_bwd", "nt", dym, w["w_mla_o"], tm, tn, F32)
    dq, dkv, dkpe_h = flash_bwd(sv["q"], sv["kv"], sv["kpe"], sv["o32"], sv["lse"], do)
    dql = q_rope_bwd(dq, tabs)
    g["w_q_b"] = _mm_plain("dw_q_b", "tn", sv["qn"], dql, 512, 1024, BF16)
    dqn = _mm_plain("q_proj_bwd", "nt", dql, w["w_q_b"], tm, 512, F32)
    g["w_kv_b"] = _mm_plain("dw_kv_b", "tn", sv["kvn"], dkv, 512, 1024, BF16)
    dkvn = _mm_plain("kv_proj_bwd", "nt", dkv, w["w_kv_b"], tm, 512, F32)
    dcq, dckv, dkr, d_gq, d_gkv = mla_prep_bwd(dqn, dkvn, dkpe_h, z, g_q, g_kv, tabs)
    dz = jnp.concatenate([du, dcq, dckv, dgp, dgm, dkr, jnp.zeros((lp, Z_COLS - Z_KR - ROPE_BLOCK), BF16)], axis=1)
    g["w_in"] = _mm_plain("dw_in", "tn", a2, dz, 1024, 1280, BF16)
    da2 = _mm_plain("mix_in_bwd", "nt", dz, w["w_in"], tm, tn, F32)
    return da2, g, dict(pool_scale=d_pool_scale, q_a_norm=d_gq, kv_a_norm=d_gkv)


_ANY = pl.BlockSpec(memory_space=pl.ANY)
_MESH = pl.DeviceIdType.MESH


def _my_pos():
    return lax.axis_index("x"), lax.axis_index("y"), lax.axis_index("c")


LEAD = "lead"
COLS = "cols"
COLS_GU = "cols_gu"


def _col_block(layout, dev):
    return dev if layout == COLS else 2 * (dev % 4) + dev // 4


def _dev_block(ref, layout, dev, cols):
    if layout == LEAD:
        return ref.at[dev]
    return ref.at[:, pl.ds(pl.multiple_of(_col_block(layout, dev) * cols, LANE), cols)]


def _gathered_shape(shard_shape, layout):
    if layout == LEAD:
        return (N_DEV, *shard_shape)
    return (shard_shape[0], N_DEV * shard_shape[1])


def all_gather(name, shards, layouts, after=None):
    n = len(shards)
    behind = _behind(after)

    def body(*refs):
        ins, outs = refs[:n], refs[n + len(behind):2 * n + len(behind)]
        send_sems, recv_sems, local_sems = refs[2 * n + len(behind):]
        x, y, c = _my_pos()
        me, sibling = (x, y, c), (x, y, 1 - c)
        chips = [(1 - x, y), (x, 1 - y), (1 - x, 1 - y)]

        def blk(a, px, py, pc):
            return _dev_block(outs[a], layouts[a], 4 * px + 2 * py + pc, shards[a].shape[-1])

        def copy(a, k, block, to, src=None):
            return pltpu.make_async_remote_copy(
                src_ref=blk(a, *block) if src is None else src, dst_ref=blk(a, *block),
                send_sem=send_sems.at[a, k], recv_sem=recv_sems.at[a, k], device_id=to, device_id_type=_MESH)

        mine = [pltpu.make_async_copy(ins[a], blk(a, *me), local_sems.at[a]) for a in range(n)]
        for cp in mine:
            cp.start()
        first = []
        for a in range(n):
            first.append(copy(a, 0, me, sibling, src=ins[a]))
            first += [copy(a, 1 + j, me, (*chip, c), src=ins[a]) for j, chip in enumerate(chips)]
        for cp in first:
            cp.start()
        passed = []
        for j, chip in enumerate(chips):
            for a in range(n):
                copy(a, 1 + j, (*chip, c), me).wait_recv()
                fwd = copy(a, 4 + j, (*chip, c), sibling)
                fwd.start()
                passed.append(fwd)
        for a in range(n):
            copy(a, 0, sibling, me).wait_recv()
            for j, chip in enumerate(chips):
                copy(a, 4 + j, (*chip, 1 - c), me).wait_recv()
        for cp in first + passed:
            cp.wait_send()
        for cp in mine:
            cp.wait()

    return pl.pallas_call(
        body, name=name,
        in_specs=[_ANY] * (n + len(behind)), out_specs=[_ANY] * n,
        out_shape=[jax.ShapeDtypeStruct(_gathered_shape(s.shape, lay), s.dtype)
                   for s, lay in zip(shards, layouts, strict=True)],
        scratch_shapes=[pltpu.SemaphoreType.DMA((n, 7)), pltpu.SemaphoreType.DMA((n, 7)), pltpu.SemaphoreType.DMA((n,))],
    )(*shards, *behind)


def _shard_shape(grad, layout):
    return grad.shape[1:] if layout == LEAD else (grad.shape[0], grad.shape[1] // N_DEV)


def rs_sibling(name, grads, layouts):
    n = len(grads)

    def body(*refs):
        ins, outs = refs[:n], refs[n:2 * n]
        send_sems, recv_sems = refs[2 * n:]
        x, y, c = _my_pos()
        cps = []
        for a in range(n):
            for k in range(4):
                cp = pltpu.make_async_remote_copy(
                    src_ref=_dev_block(ins[a], layouts[a], 2 * k + (1 - c), outs[a].shape[-1]), dst_ref=outs[a].at[k],
                    send_sem=send_sems.at[a, k], recv_sem=recv_sems.at[a, k],
                    device_id=(x, y, 1 - c), device_id_type=_MESH)
                cp.start()
                cps.append(cp)
        for cp in cps:
            cp.wait()

    return pl.pallas_call(
        body, name=name,
        in_specs=[_ANY] * n, out_specs=[_ANY] * n,
        out_shape=[jax.ShapeDtypeStruct((4, *_shard_shape(g, lay)), g.dtype) for g, lay in zip(grads, layouts, strict=True)],
        scratch_shapes=[pltpu.SemaphoreType.DMA((n, 4)), pltpu.SemaphoreType.DMA((n, 4))],
    )(*grads)


def rs_chips(name, sums):
    n = len(sums)

    def body(*refs):
        ins, outs = refs[:n], refs[n:2 * n]
        send_sems, recv_sems = refs[2 * n:]
        x, y, c = _my_pos()
        chips = [(1 - x, y), (x, 1 - y), (1 - x, 1 - y)]
        cps = []
        for a in range(n):
            for j, chip in enumerate(chips):
                cp = pltpu.make_async_remote_copy(
                    src_ref=ins[a].at[2 * chip[0] + chip[1]], dst_ref=outs[a].at[j],
                    send_sem=send_sems.at[a, j], recv_sem=recv_sems.at[a, j],
                    device_id=(*chip, c), device_id_type=_MESH)
                cp.start()
                cps.append(cp)
        for cp in cps:
            cp.wait()

    return pl.pallas_call(
        body, name=name,
        in_specs=[_ANY] * n, out_specs=[_ANY] * n,
        out_shape=[jax.ShapeDtypeStruct((3, *s.shape[1:]), s.dtype) for s in sums],
        scratch_shapes=[pltpu.SemaphoreType.DMA((n, 3)), pltpu.SemaphoreType.DMA((n, 3))],
    )(*sums)


_HBM = pl.BlockSpec(memory_space=pltpu.HBM)
_SEM = pl.BlockSpec(memory_space=pltpu.SEMAPHORE)
_EFFECT = pltpu.SideEffectType.DATAFLOW_SIDE_EFFECTING


def _in_hbm(a):
    return pltpu.with_memory_space_constraint(a, pltpu.HBM)


def split_start(name, bufs, plan, n_copies, after=None):
    nb = len(bufs)
    extra = [] if after is None else [after]

    def body(*refs):
        buf_refs = refs[:nb]
        send_sems, recv_sems = refs[nb + len(extra)], refs[nb + len(extra) + 1]
        token = refs[-1]
        copies = plan(buf_refs)
        assert len(copies) == n_copies
        for k, (src, dst, to) in enumerate(copies):
            pltpu.make_async_remote_copy(src_ref=src, dst_ref=dst, send_sem=send_sems.at[k], recv_sem=recv_sems.at[k],
                                         device_id=to, device_id_type=_MESH).start()
        token[...] = jnp.zeros_like(token)

    out = pl.pallas_call(
        body, name=name,
        out_shape=(pltpu.SemaphoreType.DMA((n_copies,)), pltpu.SemaphoreType.DMA((n_copies,)),
                   *[pltpu.HBM(b.shape, b.dtype) for b in bufs], jax.ShapeDtypeStruct((8, LANE), F32)),
        in_specs=[_HBM] * nb + [_ANY] * len(extra),
        out_specs=(_SEM, _SEM, *[_HBM] * nb, pl.BlockSpec(memory_space=pltpu.VMEM)),
        input_output_aliases={i: 2 + i for i in range(nb)},
        compiler_params=pltpu.CompilerParams(has_side_effects=_EFFECT),
    )(*[_in_hbm(b) for b in bufs], *extra)
    return out[0], out[1], list(out[2:2 + nb]), out[-1]


def split_wait(name, bufs, send_sems, recv_sems, plan, after):
    nb = len(bufs)

    def body(*refs):
        buf_refs = refs[:nb]
        s_sems, r_sems = refs[nb], refs[nb + 1]
        for k, (src, dst, to) in enumerate(plan(buf_refs)):
            cp = pltpu.make_async_remote_copy(src_ref=src, dst_ref=dst, send_sem=s_sems.at[k], recv_sem=r_sems.at[k],
                                              device_id=to, device_id_type=_MESH)
            cp.wait_send()
            cp.wait_recv()

    out = pl.pallas_call(
        body, name=name,
        out_shape=tuple(pltpu.HBM(b.shape, b.dtype) for b in bufs),
        in_specs=[_HBM] * nb + [_SEM, _SEM, _ANY],
        out_specs=tuple([_HBM] * nb),
        input_output_aliases={i: i for i in range(nb)},
        compiler_params=pltpu.CompilerParams(has_side_effects=_EFFECT),
    )(*bufs, send_sems, recv_sems, after)
    return list(out)


def _ag_own_plan(shapes, layouts):
    n = len(shapes)

    def plan(refs):
        x, y, c = _my_pos()
        targets = [(x, y, 1 - c), (1 - x, y, c), (x, 1 - y, c), (1 - x, 1 - y, c)]
        out = []
        for a in range(n):
            blk = _dev_block(refs[a], layouts[a], 4 * x + 2 * y + c, shapes[a][-1])
            out += [(blk, blk, to) for to in targets]
        return out

    return plan, 4 * n


def _ag_pass_plan(shapes, layouts):
    n = len(shapes)

    def plan(refs):
        x, y, c = _my_pos()
        out = []
        for a in range(n):
            for px, py in [(1 - x, y), (x, 1 - y), (1 - x, 1 - y)]:
                blk = _dev_block(refs[a], layouts[a], 4 * px + 2 * py + c, shapes[a][-1])
                out.append((blk, blk, (x, y, 1 - c)))
        return out

    return plan, 3 * n


def _rs_sibling_plan(layouts, n):
    def plan(refs):
        x, y, c = _my_pos()
        return [(_dev_block(refs[a], layouts[a], 2 * k + (1 - c), refs[n + a].shape[-1]), refs[n + a].at[k], (x, y, 1 - c))
                for a in range(n) for k in range(4)]

    return plan, 4 * n


def _rs_chips_plan(n):
    def plan(refs):
        x, y, c = _my_pos()
        return [(refs[a].at[2 * px + py], refs[n + a].at[j], (px, py, c))
                for a in range(n) for j, (px, py) in enumerate([(1 - x, y), (x, 1 - y), (1 - x, 1 - y)])]

    return plan, 3 * n


def place_own(name, shard, layout, dtype, dev, after):
    r, c = shard.shape
    tr = _ew_rows(r, c)
    if layout == LEAD:
        o_spec = pl.BlockSpec((None, tr, c), lambda i, dev_ref: (dev_ref[0], i, 0))
    else:
        o_spec = pl.BlockSpec((tr, c), lambda i, dev_ref: (i, _col_block(layout, dev_ref[0])))
    extra = [] if after is None else [after]

    def body(dev_ref, s_ref, *rest):
        rest[-1][...] = s_ref[...].astype(dtype)

    return pl.pallas_call(
        body, name=name,
        grid_spec=pltpu.PrefetchScalarGridSpec(
            num_scalar_prefetch=1, grid=(r // tr,),
            in_specs=[pl.BlockSpec((tr, c), lambda i, dev_ref: (i, 0))] + [_ANY] * len(extra),
            out_specs=o_spec),
        out_shape=jax.ShapeDtypeStruct(_gathered_shape(shard.shape, layout), dtype),
        compiler_params=_cparams(("parallel",)),
    )(dev, shard, *extra)


def _ew_rows(r, c, itemsize=4):
    for t in (1024, 512, 256, 128, 64, 32, 16):
        if r % t == 0 and t * c * itemsize <= 768 * 1024:
            return t
    raise ValueError((r, c))


def rs_add(name, grad, layout, recv, core):
    _, r, c = recv.shape
    tr = _ew_rows(r, c, itemsize=2)
    if layout == LEAD:
        g_spec = pl.BlockSpec((None, tr, c), lambda k, i, core_ref: (2 * k + core_ref[0], i, 0))
    else:
        g_spec = pl.BlockSpec((tr, c), lambda k, i, core_ref: (i, _col_block(layout, 2 * k + core_ref[0])))

    def body(core_ref, g_ref, r_ref, o_ref):
        o_ref[...] = (g_ref[...].astype(F32) + r_ref[...].astype(F32)).astype(BF16)

    return pl.pallas_call(
        body, name=name,
        grid_spec=pltpu.PrefetchScalarGridSpec(
            num_scalar_prefetch=1, grid=(4, r // tr),
            in_specs=[g_spec, pl.BlockSpec((None, tr, c), lambda k, i, core_ref: (k, i, 0))],
            out_specs=pl.BlockSpec((None, tr, c), lambda k, i, core_ref: (k, i, 0))),
        out_shape=jax.ShapeDtypeStruct((4, r, c), BF16),
        compiler_params=_cparams(("parallel", "parallel")),
    )(core, grad, recv)


def _adamw(w, g, m, v):
    m = ADAM_B1 * m + (1.0 - ADAM_B1) * g
    v = ADAM_B2 * v + (1.0 - ADAM_B2) * jnp.square(g)
    m_hat = m / (1.0 - ADAM_B1 ** ADAM_STEP)
    v_hat = v / (1.0 - ADAM_B2 ** ADAM_STEP)
    delta = -ADAM_LR * (m_hat / (jnp.sqrt(v_hat) + ADAM_EPS) + ADAM_WD * w)
    return delta, m, v


def adamw_shard(name, w, m, v, sums, recv, chip):
    r, c = w.shape
    tr = _ew_rows(r, c)

    def body(chip_ref, w_ref, m_ref, v_ref, s_ref, r_ref, g_ref, d_ref, mo_ref, vo_ref):
        g = s_ref[0].astype(F32)
        for j in range(3):
            g = g + r_ref[j].astype(F32)
        d, mn, vn = _adamw(w_ref[...], g, m_ref[...], v_ref[...])
        g_ref[...] = g
        d_ref[...] = d
        mo_ref[...] = mn
        vo_ref[...] = vn

    spec = pl.BlockSpec((tr, c), lambda i, chip_ref: (i, 0))
    return pl.pallas_call(
        body, name=name,
        grid_spec=pltpu.PrefetchScalarGridSpec(
            num_scalar_prefetch=1, grid=(r // tr,),
            in_specs=[spec, spec, spec,
                      pl.BlockSpec((1, tr, c), lambda i, chip_ref: (chip_ref[0], i, 0)),
                      pl.BlockSpec((3, tr, c), lambda i, chip_ref: (0, i, 0))],
            out_specs=[spec] * 4),
        out_shape=[jax.ShapeDtypeStruct((r, c), F32)] * 4,
        compiler_params=_cparams(("parallel",)),
    )(chip, w, m, v, sums, recv)


def reduce_small(gathered):
    _, r, c = gathered.shape

    def body(g_ref, o_ref):
        acc = g_ref[0]
        for k in range(1, N_DEV):
            acc = acc + g_ref[k]
        o_ref[...] = acc

    return pl.pallas_call(body, name="reduce_small", out_shape=jax.ShapeDtypeStruct((r, c), F32))(gathered)


def adamw_small(ws, gs, ms, vs):
    n = len(ws)

    def body(*refs):
        w_r, g_r, m_r, v_r = refs[:n], refs[n:2 * n], refs[2 * n:3 * n], refs[3 * n:4 * n]
        d_o, m_o, v_o = refs[4 * n:5 * n], refs[5 * n:6 * n], refs[6 * n:7 * n]
        for a in range(n):
            d, mn, vn = _adamw(w_r[a][...], g_r[a][...], m_r[a][...], v_r[a][...])
            d_o[a][...] = d
            m_o[a][...] = mn
            v_o[a][...] = vn

    shapes = [jax.ShapeDtypeStruct(w.shape, F32) for w in ws]
    out = pl.pallas_call(body, name="adamw_small", out_shape=shapes * 3)(*ws, *gs, *ms, *vs)
    return out[:n], out[n:2 * n], out[2 * n:]


WEIGHTS = ["meta_tokens", "norm_ffn1_pre", "norm_ffn1_post", "ffn1_w_gu", "ffn1_w_down", "norm_mix_pre",
           "norm_mix_post", "w_in", "pool_w", "pool_scale", "w_pool_o", "q_a_norm", "w_q_b", "kv_a_norm", "w_kv_b",
           "w_mla_o", "w_out", "norm_ffn2_pre", "norm_ffn2_post", "ffn2_w_gu", "ffn2_w_down"]
BIG = ["ffn1_w_gu", "ffn1_w_down", "w_in", "pool_w", "w_pool_o", "w_q_b", "w_kv_b", "w_mla_o", "w_out",
       "ffn2_w_gu", "ffn2_w_down"]
COL_SHARDED = ("w_in", "w_q_b")
GATHERED = {"ffn1_w_gu": COLS_GU, "ffn2_w_gu": COLS_GU, "w_pool_o": COLS, "w_kv_b": COLS}
GAINS =["norm_ffn1_pre", "norm_ffn1_post", "norm_mix_pre", "norm_mix_post", "norm_ffn2_pre", "norm_ffn2_post"]
SMALL = GAINS + ["pool_scale", "q_a_norm", "kv_a_norm"]
Z_SRC = 1024 + 512 + 512 + QK_ROPE


def _full_from_gathered(name, g):
    _, r, c = g.shape
    if name == "pool_w":
        ng = len(POOL_WINDOWS)
        return g.reshape(N_DEV, ng, r // ng, c).transpose(1, 0, 2, 3).reshape(ng, POOL_GROUP, POOL_GROUP)
    if name in COL_SHARDED:
        return g.transpose(1, 0, 2).reshape(r, N_DEV * c)
    return g.reshape(N_DEV * r, c)


def _blocks_from_full(name, dw):
    if name == "pool_w":
        ng = len(POOL_WINDOWS)
        return dw.reshape(ng, N_DEV, POOL_GROUP // N_DEV, POOL_GROUP).transpose(1, 0, 2, 3).reshape(
            N_DEV, ng * POOL_GROUP // N_DEV, POOL_GROUP)
    k, n = dw.shape
    if name in COL_SHARDED:
        return dw.reshape(k, N_DEV, n // N_DEV).transpose(1, 0, 2)
    return dw.reshape(N_DEV, k // N_DEV, n)


def _to_internal(name, w):
    if name == "w_in":
        d = w.shape[0]
        return jnp.concatenate([w[:, :Z_SRC - QK_ROPE], w[:, Z_SRC:], w[:, Z_SRC - QK_ROPE:Z_SRC],
                                jnp.zeros((d, Z_COLS - Z_KR - QK_ROPE), w.dtype)], axis=1)
    if name == "w_q_b":
        r = w.shape[0]
        w3 = w.reshape(r, HEADS, QK_NOPE + QK_ROPE)
        return jnp.pad(w3, ((0, 0), (0, 0), (0, HEAD_W - QK_NOPE - QK_ROPE))).reshape(r, Q_COLS)
    return w


def _from_internal(name, dw):
    if name == "w_in":
        return jnp.concatenate([dw[:, :Z_SRC - QK_ROPE], dw[:, Z_KR:Z_KR + QK_ROPE], dw[:, Z_SRC - QK_ROPE:Z_KR]], axis=1)
    if name == "w_q_b":
        r = dw.shape[0]
        return dw.reshape(r, HEADS, HEAD_W)[:, :, :QK_NOPE + QK_ROPE].reshape(r, HEADS * (QK_NOPE + QK_ROPE))
    return dw


def _shard2d(a):
    return a.reshape(-1, a.shape[-1])


def kernel(x, meta_tokens, norm_ffn1_pre, norm_ffn1_post, ffn1_w_gu, ffn1_w_down, norm_mix_pre, norm_mix_post, w_in, pool_w, pool_scale, w_pool_o, q_a_norm, w_q_b, kv_a_norm, w_kv_b, w_mla_o, w_out, norm_ffn2_pre, norm_ffn2_post, ffn2_w_gu, ffn2_w_down, loss_target, m_meta_tokens, m_norm_ffn1_pre, m_norm_ffn1_post, m_ffn1_w_gu, m_ffn1_w_down, m_norm_mix_pre, m_norm_mix_post, m_w_in, m_pool_w, m_pool_scale, m_w_pool_o, m_q_a_norm, m_w_q_b, m_kv_a_norm, m_w_kv_b, m_w_mla_o, m_w_out, m_norm_ffn2_pre, m_norm_ffn2_post, m_ffn2_w_gu, m_ffn2_w_down, v_meta_tokens, v_norm_ffn1_pre, v_norm_ffn1_post, v_ffn1_w_gu, v_ffn1_w_down, v_norm_mix_pre, v_norm_mix_post, v_w_in, v_pool_w, v_pool_scale, v_w_pool_o, v_q_a_norm, v_w_q_b, v_kv_a_norm, v_w_kv_b, v_w_mla_o, v_w_out, v_norm_ffn2_pre, v_norm_ffn2_post, v_ffn2_w_gu, v_ffn2_w_down):
    given = dict(locals())
    w_in_dev = {n: given[n] for n in WEIGHTS}
    m_in = {n: given["m_" + n] for n in WEIGHTS}
    v_in = {n: given["v_" + n] for n in WEIGHTS}
    xi, yi, ci = _my_pos()
    dev = 4 * xi + 2 * yi + ci
    core = jnp.reshape(ci, (1,)).astype(jnp.int32)
    chip = jnp.reshape(2 * xi + yi, (1,)).astype(jnp.int32)
    d = x.shape[-1]

    shards = {n: _shard2d(w_in_dev[n]) for n in BIG}
    ex = _Exchange(shards, meta_tokens, dev, core)
    gain = {n: given[n] for n in SMALL}
    loss_blk, grad_x, front, gsmall = local_step(x[0], loss_target[0], gain, ex)

    out_g, out_d, out_m, out_v = {}, {}, {}, {}
    updated = []

    def finish(grp, after):
        names, sums, from_chips = ex.finish_grads(grp, after)
        for n, s, r in zip(names, sums, from_chips, strict=True):
            shp = w_in_dev[n].shape
            res = adamw_shard("adamw_" + n, shards[n], _shard2d(m_in[n]), _shard2d(v_in[n]), s, r, chip)
            out_g[n], out_d[n], out_m[n], out_v[n] = [t.reshape(shp) for t in res]
            updated.append(res[1])
        return res[0]

    finish("A2", finish("B", finish("C", grad_x)))

    tail = jnp.concatenate([gsmall["pool_scale"], gsmall["q_a_norm"], gsmall["kv_a_norm"]], axis=1)
    small = jnp.concatenate([gsmall[n] for n in GAINS] + [tail, jnp.broadcast_to(loss_blk[:1, :1], (1, d)),
                                                         front[PAD_ROWS:]], axis=0)
    (small_g,) = all_gather("ag_small", [small], [LEAD], after=updated)
    total = reduce_small(small_g)
    ng = len(GAINS)
    for i, n in enumerate(GAINS):
        out_g[n] = total[i:i + 1]
    o = 0
    for n in ("pool_scale", "q_a_norm", "kv_a_norm"):
        wdt = w_in_dev[n].shape[1]
        out_g[n] = total[ng:ng + 1, o:o + wdt]
        o += wdt
    loss = total[ng + 1, 0]
    mcols = meta_tokens.shape[1]
    out_g["meta_tokens"] = lax.dynamic_slice(total[ng + 2:ng + 2 + N_META], (0, dev * mcols), (N_META, mcols))
    names = ["meta_tokens"] + SMALL
    ds_, ms_, vs_ = adamw_small([w_in_dev[n] for n in names], [out_g[n] for n in names],
                                [m_in[n] for n in names], [v_in[n] for n in names])
    for n, dd, mm, vv in zip(names, ds_, ms_, vs_, strict=True):
        out_d[n], out_m[n], out_v[n] = dd, mm, vv
    finish("A1", ds_[0])

    return (loss, grad_x[None], *[out_g[n] for n in WEIGHTS], *[out_d[n] for n in WEIGHTS],
            *[out_m[n] for n in WEIGHTS], *[out_v[n] for n in WEIGHTS])


GROUPS = {"A1": ["ffn1_w_gu"], "A2": ["ffn1_w_down"],
          "B": ["w_in", "pool_w", "w_pool_o", "w_q_b", "w_kv_b", "w_mla_o", "w_out"],
          "C": ["ffn2_w_gu", "ffn2_w_down"]}


class _Exchange:
    def __init__(self, shards, meta_tokens, dev, core):
        self.shards, self.meta_tokens, self.core = shards, meta_tokens, core
        self.dev1 = jnp.reshape(dev, (1,)).astype(jnp.int32)
        self.w, self.meta_full, self.token = {}, None, None
        self._ag, self._rs = {}, {}

    def _ag_place(self, grp, after):
        names = GROUPS[grp] + (["meta_tokens"] if grp == "A1" else [])
        srcs = [self.meta_tokens if n == "meta_tokens" else self.shards[n] for n in names]
        lays = [GATHERED.get(n, LEAD) for n in names]
        shapes = [a.shape for a in srcs]
        lands = [place_own(f"place_{n}", a, lay, F32 if n == "meta_tokens" else BF16, self.dev1, after)
                 for n, a, lay in zip(names, srcs, lays, strict=True)]
        self._ag[grp] = dict(names=names, lays=lays, shapes=shapes, lands=lands)

    def _ag_own_start(self, grp, after=None):
        st = self._ag[grp]
        plan, cnt = _ag_own_plan(st["shapes"], st["lays"])
        ss, rs, bufs, self.token = split_start(f"ag{grp}_own_start", st["lands"], plan, cnt, after)
        st["own"] = (ss, rs, bufs, plan)

    def _ag_pass(self, grp, after):
        st = self._ag[grp]
        ss, rs, bufs, plan = st["own"]
        lands = split_wait(f"ag{grp}_own_wait", bufs, ss, rs, plan, after)
        plan, cnt = _ag_pass_plan(st["shapes"], st["lays"])
        ss, rs, lands, self.token = split_start(f"ag{grp}_pass_start", lands, plan, cnt)
        st["pass"] = (ss, rs, lands, plan)

    def _ag_finish(self, grp, after):
        st = self._ag[grp]
        ss, rs, lands, plan = st["pass"]
        lands = split_wait(f"ag{grp}_pass_wait", lands, ss, rs, plan, after)
        for n, g, lay in zip(st["names"], lands, st["lays"]):
            if n == "meta_tokens":
                self.meta_full = g.transpose(1, 0, 2).reshape(N_META, N_DEV * g.shape[-1])
            else:
                self.w[n] = g if lay != LEAD else _to_internal(n, _full_from_gathered(n, g))

    def grads(self, grp, gbig, after=None):
        names = GROUPS[grp]
        lays = [GATHERED.get(n, LEAD) for n in names]
        grads = [gbig[n] if lay != LEAD else _blocks_from_full(n, _from_internal(n, gbig[n]).astype(BF16))
                 for n, lay in zip(names, lays, strict=True)]
        lands = [lax.empty((4, *_shard_shape(g, lay)), BF16) for g, lay in zip(grads, lays, strict=True)]
        plan, cnt = _rs_sibling_plan(lays, len(names))
        ss, rs, bufs, self.token = split_start(f"rs{grp}_sibling_start", grads + lands, plan, cnt, after)
        self._rs[grp] = dict(names=names, lays=lays, sib=(ss, rs, bufs, plan))
        return self.token

    def _rs_mid(self, grp, after):
        st = self._rs[grp]
        n = len(st["names"])
        ss, rs, bufs, plan = st["sib"]
        bufs = split_wait(f"rs{grp}_sibling_wait", bufs, ss, rs, plan, after)
        sums = [rs_add(f"rs_add_{name}", g, lay, r, self.core)
                for name, g, lay, r in zip(st["names"], bufs[:n], st["lays"], bufs[n:], strict=True)]
        lands = [lax.empty((3, *s.shape[1:]), BF16) for s in sums]
        plan, cnt = _rs_chips_plan(n)
        ss, rs, bufs, self.token = split_start(f"rs{grp}_chips_start", sums + lands, plan, cnt)
        st["chips"] = (ss, rs, bufs, plan)

    def finish_grads(self, grp, after):
        st = self._rs[grp]
        n = len(st["names"])
        ss, rs, bufs, plan = st["chips"]
        bufs = split_wait(f"rs{grp}_chips_wait", bufs, ss, rs, plan, after)
        return st["names"], bufs[:n], bufs[n:]

    def point(self, name, after=None):
        if name == "start":
            self._ag_place("A1", None)
            self._ag_own_start("A1")
            first = self.token
            self._ag_place("A2", first)
            self._ag_place("B", first)
            self._ag_pass("A1", self._ag["B"]["lands"][-1])
            self._ag_own_start("A2", self.token)
            self._ag_own_start("B", self.token)
            self._ag_finish("A1", self.token)
        elif name == "ffn1_gu_done":
            self._ag_pass("A2", after)
            self._ag_finish("A2", self.token)
        elif name == "ffn1_fwd_done":
            self._ag_pass("B", after)
            self._ag_place("C", self.token)
            self._ag_own_start("C", self.token)
        elif name == "mix_pre_done":
            self._ag_finish("B", after)
        elif name == "flash_fwd_done":
            self._ag_pass("C", after)
        elif name == "ffn2_pre_done":
            self._ag_finish("C", after)
        elif name.startswith("rs") and name.endswith("_mid"):
            self._rs_mid(name[2:-4], after)


def local_step(x, target, gain, ex):
    d = x.shape[-1]
    ex.point("start")
    w = ex.w
    h0 = jnp.concatenate([jnp.zeros((PAD_ROWS, d), F32), ex.meta_full, x], axis=0)
    lp = h0.shape[0]
    tabs = rope_tables(lp)
    a1 = prenorm(h0, gain["norm_ffn1_pre"], after=ex.token)
    gu1, s1 = ffn_gu("ffn1", a1, w["ffn1_w_gu"])
    ex.point("ffn1_gu_done", s1)
    f1 = ffn_down("ffn1", s1, w["ffn1_w_down"])
    ex.point("ffn1_fwd_done", f1)
    h1, a2 = post_pre("post_pre1", f1, h0, gain["norm_ffn1_post"], 0.5, gain["norm_mix_pre"], after=ex.token)
    ex.point("mix_pre_done", a2)
    mix, sv = mixer_fwd(a2, w, tabs, gain["pool_scale"], gain["q_a_norm"], gain["kv_a_norm"], ex)
    h2, a3 = post_pre("post_pre2", mix, h1, gain["norm_mix_post"], 1.0, gain["norm_ffn2_pre"])
    ex.point("ffn2_pre_done", a3)
    gu2, s2 = ffn_gu("ffn2", a3, w["ffn2_w_gu"])
    f2 = ffn_down("ffn2", s2, w["ffn2_w_down"])
    dh3, loss_blk = post_loss(f2, h2, gain["norm_ffn2_post"], 0.5, target)

    gsmall = {}
    df2, gsmall["norm_ffn2_post"] = post_bwd(dh3, f2, gain["norm_ffn2_post"], 0.5)
    da3 = ffn_bwd("ffn2", df2, a3, gu2, s2, w["ffn2_w_gu"], w["ffn2_w_down"], lambda dwd: None,
                  lambda dwgu, dwd: ex.grads("C", {"ffn2_w_gu": dwgu, "ffn2_w_down": dwd}))
    dh2, dmix, gsmall["norm_ffn2_pre"], gsmall["norm_mix_post"] = pre_post_bwd(
        "pre_post_bwd2", da3, h2, gain["norm_ffn2_pre"], dh3, mix, gain["norm_mix_post"], 1.0)
    ex.point("rsC_mid", dmix)
    da2, gmix, gmix_small = mixer_bwd(dmix, a2, sv, w, tabs, gain["pool_scale"], gain["q_a_norm"], gain["kv_a_norm"],
                                      after=ex.token)
    gsmall.update(gmix_small)
    dh1, df1, gsmall["norm_mix_pre"], gsmall["norm_ffn1_post"] = pre_post_bwd(
        "pre_post_bwd1", da2, h1, gain["norm_mix_pre"], dh2, f1, gain["norm_ffn1_post"], 0.5,
        after=ex.grads("B", gmix))
    ex.point("rsB_mid", df1)
    def on_dwgu1(dwgu, dwd):
        ex.point("rsA2_mid", dwgu)
        return ex.grads("A1", {"ffn1_w_gu": dwgu}, after=ex.token)

    da1 = ffn_bwd("ffn1", df1, a1, gu1, s1, w["ffn1_w_gu"], w["ffn1_w_down"],
                  lambda dwd: ex.grads("A2", {"ffn1_w_down": dwd}), on_dwgu1, after=ex.token)
    ex.point("rsA1_mid", da1)
    grad_x, front, gsmall["norm_ffn1_pre"] = pre_bwd_first(da1, h0, gain["norm_ffn1_pre"], dh1, after=ex.token)
    return loss_blk, grad_x, front, gsmall
```

```python
import functools

import jax
import jax.numpy as jnp
import numpy as np
from jax import lax
from jax.experimental import pallas as pl
from jax.experimental.pallas import tpu as pltpu

F32 = jnp.float32
BF16 = jnp.bfloat16

N_META = 16
POOL_WINDOWS = (2, 4, 8, 16)
POOL_GROUP = 256
HEADS = 16
QK_NOPE = 128
QK_ROPE = 64
V_DIM = 128
ROPE_THETA = 10000.0
SOFTMAX_SCALE = (QK_NOPE + QK_ROPE) ** -0.5
EPS = 1e-6
ADAM_LR = 0.001
ADAM_B1 = 0.9
ADAM_B2 = 0.999
ADAM_EPS = 1e-08
ADAM_WD = 0.01
ADAM_STEP = 10

LANE = 128
FRONT = 128
PAD_ROWS = FRONT - N_META
HEAD_W = 256
GU_TILE = 1408
VMEM_LIMIT = 56 * 1024 * 1024
MESH_AXES = ("x", "y", "c")
N_DEV = 8


def _pick(n, cands):
    for c in cands:
        if n % c == 0:
            return c
    raise ValueError(f"no tile for {n} in {cands}")


def _cparams(sem=None):
    kw = dict(vmem_limit_bytes=VMEM_LIMIT)
    if sem is not None:
        kw["dimension_semantics"] = sem
    return pltpu.CompilerParams(**kw)


_DIMS = {"nn": (((1,), (0,)), ((), ())), "nt": (((1,), (1,)), ((), ())), "tn": (((0,), (0,)), ((), ()))}


def _behind(after):
    return [] if after is None else list(after) if isinstance(after, (list, tuple)) else [after]


def _behind_specs(after):
    return [pl.BlockSpec(memory_space=pl.ANY)] * len(_behind(after))


def _mm(name, form, a, b, tm, tn, outs, epi=None, extras=(), n_outer=False, after=None):
    if form == "tn":
        k, m = a.shape
        n = b.shape[1]
        a_blk, a_map = (k, tm), lambda i, j: (0, i)
        b_blk, b_map = (k, tn), lambda i, j: (0, j)
    elif form == "nn":
        m, k = a.shape
        n = b.shape[1]
        a_blk, a_map = (tm, k), lambda i, j: (i, 0)
        b_blk, b_map = (k, tn), lambda i, j: (0, j)
    else:
        m, k = a.shape
        n = b.shape[0]
        a_blk, a_map = (tm, k), lambda i, j: (i, 0)
        b_blk, b_map = (tn, k), lambda i, j: (j, 0)
    assert m % tm == 0 and n % tn == 0, (name, m, n, tm, tn)
    n_ex = len(extras)
    dn = _DIMS[form]
    if n_outer:
        grid = (n // tn, m // tm)

        def spec(blk, im):
            return pl.BlockSpec(blk, lambda gj, gi: im(gi, gj))
    else:
        grid = (m // tm, n // tn)
        spec = pl.BlockSpec

    behind = _behind(after)

    def body(a_ref, b_ref, *rest):
        ex, out_refs = rest[:n_ex], rest[n_ex + len(behind):]
        acc = lax.dot_general(a_ref[...].astype(BF16), b_ref[...].astype(BF16), dn, preferred_element_type=F32)
        res = epi(acc, *[e[...] for e in ex]) if epi is not None else (acc,)
        for r, o in zip(res, out_refs, strict=True):
            o[...] = r.astype(o.dtype)

    return pl.pallas_call(
        body,
        name=name,
        grid=grid,
        in_specs=[spec(a_blk, a_map), spec(b_blk, b_map)] + [spec(blk, im) for _, blk, im in extras] + _behind_specs(after),
        out_specs=[spec(blk, im) for _, _, blk, im in outs],
        out_shape=[jax.ShapeDtypeStruct(s, d) for s, d, _, _ in outs],
        compiler_params=_cparams(("parallel", "parallel")),
    )(a, b, *[e for e, _, _ in extras], *behind)


def _mm_plain(name, form, a, b, tm, tn, out_dtype, after=None):
    m = a.shape[1] if form == "tn" else a.shape[0]
    n = b.shape[0] if form == "nt" else b.shape[1]
    return _mm(name, form, a, b, tm, tn, [((m, n), out_dtype, (tm, tn), lambda i, j: (i, j))], after=after)[0]


def _rstd(x):
    return lax.rsqrt(jnp.mean(x * x, axis=-1, keepdims=True) + EPS)


def _norm_bwd(dy, x, gain):
    r = _rstd(x)
    dyg = dy * gain
    dx = r * (dyg - x * (r * r) * jnp.mean(dyg * x, axis=-1, keepdims=True))
    dgain = jnp.sum(dy * x * r, axis=0, keepdims=True)
    return dx, dgain


def _row_spec(tr, cols, col_block=0):
    return pl.BlockSpec((tr, cols), lambda i: (i, col_block))


def _vec_spec(cols, col_block=0):
    return pl.BlockSpec((1, cols), lambda i: (0, col_block))


def _row_tile(lp):
    return _pick(lp, (128,))


def _skip_behind(after, body):
    n = len(_behind(after))
    return body if n == 0 else (lambda *refs: body(*refs[n:]))


def prenorm(h, gain, after=None):
    lp, d = h.shape
    tr = _row_tile(lp)

    def body(h_ref, g_ref, a_ref):
        x = h_ref[...]
        a_ref[...] = (x * _rstd(x) * g_ref[...]).astype(BF16)

    return pl.pallas_call(
        _skip_behind(after, body), name="prenorm", grid=(lp // tr,),
        in_specs=_behind_specs(after) + [_row_spec(tr, d), _vec_spec(d)], out_specs=_row_spec(tr, d),
        out_shape=jax.ShapeDtypeStruct((lp, d), BF16), compiler_params=_cparams(("parallel",)),
    )(*_behind(after), h, gain)


def post_pre(name, f, h_in, g_post, coef, g_next, after=None):
    lp, d = f.shape
    tr = _row_tile(lp)

    def body(f_ref, h_ref, gp_ref, gn_ref, ho_ref, a_ref):
        fv = f_ref[...]
        ho = h_ref[...] + coef * (fv * _rstd(fv) * gp_ref[...])
        ho_ref[...] = ho
        a_ref[...] = (ho * _rstd(ho) * gn_ref[...]).astype(BF16)

    return pl.pallas_call(
        _skip_behind(after, body), name=name, grid=(lp // tr,),
        in_specs=_behind_specs(after) + [_row_spec(tr, d), _row_spec(tr, d), _vec_spec(d), _vec_spec(d)],
        out_specs=[_row_spec(tr, d), _row_spec(tr, d)],
        out_shape=[jax.ShapeDtypeStruct((lp, d), F32), jax.ShapeDtypeStruct((lp, d), BF16)],
        compiler_params=_cparams(("parallel",)),
    )(*_behind(after), f, h_in, g_post, g_next)


def post_loss(f, h_in, g_post, coef, target):
    lp, d = f.shape
    tr = _row_tile(lp)
    front_tiles = FRONT // tr

    def body(f_ref, h_ref, gp_ref, t_ref, dh_ref, loss_ref):
        i = pl.program_id(0)

        @pl.when(i == 0)
        def _():
            loss_ref[...] = jnp.zeros_like(loss_ref)

        @pl.when(i < front_tiles)
        def _():
            dh_ref[...] = jnp.zeros_like(dh_ref)

        @pl.when(i >= front_tiles)
        def _():
            fv = f_ref[...]
            ho = h_ref[...] + coef * (fv * _rstd(fv) * gp_ref[...])
            err = ho - t_ref[...]
            dh_ref[...] = err / d
            tok = jnp.mean(err * err, axis=-1, keepdims=True)
            loss_ref[...] += 0.5 * jnp.sum(tok)

    return pl.pallas_call(
        body, name="post_loss", grid=(lp // tr,),
        in_specs=[_row_spec(tr, d), _row_spec(tr, d), _vec_spec(d),
                  pl.BlockSpec((tr, d), lambda i: (jnp.maximum(i - front_tiles, 0), 0))],
        out_specs=[_row_spec(tr, d), pl.BlockSpec((8, LANE), lambda i: (0, 0))],
        out_shape=[jax.ShapeDtypeStruct((lp, d), F32), jax.ShapeDtypeStruct((8, LANE), F32)],
        compiler_params=_cparams(("arbitrary",)),
    )(f, h_in, g_post, target)


def post_bwd(dh_out, f, g_post, coef):
    lp, d = f.shape
    tr = _row_tile(lp)

    def body(dh_ref, f_ref, gp_ref, df_ref, dg_ref):
        @pl.when(pl.program_id(0) == 0)
        def _():
            dg_ref[...] = jnp.zeros_like(dg_ref)

        df, dg = _norm_bwd(coef * dh_ref[...], f_ref[...], gp_ref[...])
        df_ref[...] = df.astype(BF16)
        dg_ref[...] += dg

    return pl.pallas_call(
        body, name="post_bwd", grid=(lp // tr,),
        in_specs=[_row_spec(tr, d), _row_spec(tr, d), _vec_spec(d)],
        out_specs=[_row_spec(tr, d), _vec_spec(d)],
        out_shape=[jax.ShapeDtypeStruct((lp, d), BF16), jax.ShapeDtypeStruct((1, d), F32)],
        compiler_params=_cparams(("arbitrary",)),
    )(dh_out, f, g_post)


def pre_post_bwd(name, da, h_mid, g_pre, dh_out, f_prev, g_post_prev, coef_prev, after=None):
    lp, d = da.shape
    tr = _row_tile(lp)

    def body(da_ref, h_ref, gpre_ref, dho_ref, f_ref, gpost_ref, dh_ref, df_ref, dgpre_ref, dgpost_ref):
        @pl.when(pl.program_id(0) == 0)
        def _():
            dgpre_ref[...] = jnp.zeros_like(dgpre_ref)
            dgpost_ref[...] = jnp.zeros_like(dgpost_ref)

        dx, dgpre = _norm_bwd(da_ref[...], h_ref[...], gpre_ref[...])
        dh = dho_ref[...] + dx
        dh_ref[...] = dh
        dgpre_ref[...] += dgpre
        df, dgpost = _norm_bwd(coef_prev * dh, f_ref[...], gpost_ref[...])
        df_ref[...] = df.astype(BF16)
        dgpost_ref[...] += dgpost

    return pl.pallas_call(
        _skip_behind(after, body), name=name, grid=(lp // tr,),
        in_specs=_behind_specs(after) + [_row_spec(tr, d), _row_spec(tr, d), _vec_spec(d), _row_spec(tr, d),
                                         _row_spec(tr, d), _vec_spec(d)],
        out_specs=[_row_spec(tr, d), _row_spec(tr, d), _vec_spec(d), _vec_spec(d)],
        out_shape=[jax.ShapeDtypeStruct((lp, d), F32), jax.ShapeDtypeStruct((lp, d), BF16),
                   jax.ShapeDtypeStruct((1, d), F32), jax.ShapeDtypeStruct((1, d), F32)],
        compiler_params=_cparams(("arbitrary",)),
    )(*_behind(after), da, h_mid, g_pre, dh_out, f_prev, g_post_prev)


def pre_bwd_first(da, h0, g_pre, dh_out, after=None):
    lp, d = da.shape
    tr = _row_tile(lp)
    front_tiles = FRONT // tr
    assert front_tiles == 1

    def body(da_ref, h_ref, gpre_ref, dho_ref, gx_ref, front_ref, dgpre_ref):
        i = pl.program_id(0)

        @pl.when(i == 0)
        def _():
            dgpre_ref[...] = jnp.zeros_like(dgpre_ref)

        dx, dgpre = _norm_bwd(da_ref[...], h_ref[...], gpre_ref[...])
        dh = dho_ref[...] + dx
        dgpre_ref[...] += dgpre
        gx_ref[...] = dh

        @pl.when(i == 0)
        def _():
            front_ref[...] = dh

    return pl.pallas_call(
        _skip_behind(after, body), name="pre_bwd_first", grid=(lp // tr,),
        in_specs=_behind_specs(after) + [_row_spec(tr, d), _row_spec(tr, d), _vec_spec(d), _row_spec(tr, d)],
        out_specs=[pl.BlockSpec((tr, d), lambda i: (jnp.maximum(i - front_tiles, 0), 0)),
                   pl.BlockSpec((tr, d), lambda i: (0, 0)), _vec_spec(d)],
        out_shape=[jax.ShapeDtypeStruct((lp - FRONT, d), F32), jax.ShapeDtypeStruct((tr, d), F32),
                   jax.ShapeDtypeStruct((1, d), F32)],
        compiler_params=_cparams(("arbitrary",)),
    )(*_behind(after), da, h0, g_pre, dh_out)


def _m_tile(lp):
    return _pick(lp, (1056, 512, 256, 128))


def ffn_gu(tag, a, wgu_p):
    lp, d = a.shape
    f2 = wgu_p.shape[1]

    def epi(acc):
        g, u = acc[:, :GU_TILE], acc[:, GU_TILE:]
        return acc, g * jax.nn.sigmoid(g) * u

    tg = _pick(lp, (528, 256, 128))
    return _mm(tag + "_gu", "nn", a, wgu_p, tg, 2 * GU_TILE,
               [((lp, f2), F32, (tg, 2 * GU_TILE), lambda i, j: (i, j)),
                ((lp, f2 // 2), BF16, (tg, GU_TILE), lambda i, j: (i, j))], epi=epi, n_outer=True)


def ffn_down(tag, s, wd):
    return _mm_plain(tag + "_down", "nn", s, wd, _m_tile(s.shape[0]), 512, F32)


def ffn_bwd(tag, df, a, gu, s, wgu_p, wd, on_dwd, on_dwgu, after=None):
    lp, d = df.shape
    f2 = wgu_p.shape[1]
    tm = _m_tile(lp)

    def epi(acc, gu_t):
        g, u = gu_t[:, :GU_TILE], gu_t[:, GU_TILE:]
        sig = jax.nn.sigmoid(g)
        dg = acc * u * (sig * (1.0 + g * (1.0 - sig)))
        du = acc * (g * sig)
        return (jnp.concatenate([dg, du], axis=1),)

    ts = _pick(lp, (528, 256, 128))
    dgu = _mm(tag + "_ds", "nt", df, wd, ts, GU_TILE,
              [((lp, f2), BF16, (ts, 2 * GU_TILE), lambda i, j: (i, j))], epi=epi,
              extras=[(gu, (ts, 2 * GU_TILE), lambda i, j: (i, j))], n_outer=True, after=after)[0]
    dwd = _mm_plain(tag + "_dwd", "tn", s, df, 512, _pick(d, (1024,)), BF16, after=after)
    dwgu = _mm_plain(tag + "_dwgu", "tn", a, dgu, _pick(d, (1024,)), 1024, BF16, after=on_dwd(dwd))
    return _mm_plain(tag + "_da", "nt", dgu, wgu_p, _pick(lp, (528, 256, 128)), 512, F32, after=on_dwgu(dwgu, dwd))


Z_U, Z_CQ, Z_CKV, Z_GP, Z_GM, Z_KR, Z_COLS = 0, 1024, 1536, 2048, 4096, 6144, 6400
POOL_W = POOL_GROUP * len(POOL_WINDOWS)
HALO = 16


def _pool_counts(lp, w):
    pos = lax.broadcasted_iota(jnp.int32, (lp, 1), 0) - PAD_ROWS
    return jnp.clip(pos + 1, 1, w).astype(F32)


def _pool_diff(u_ref, pad_ref, lp, w):
    pad_ref[pl.ds(0, HALO), :] = jnp.zeros((HALO, POOL_GROUP), F32)
    pad_ref[pl.ds(HALO, lp), :] = u_ref[...]
    acc = pad_ref[pl.ds(HALO, lp), :]
    for s in range(1, w):
        acc = acc + pad_ref[pl.ds(HALO - s, lp), :]
    return acc / _pool_counts(lp, w) - u_ref[...]


def pool_fwd(z, pool_w, pool_scale):
    lp = z.shape[0]
    ng = len(POOL_WINDOWS)

    def body(u_ref, w_ref, sc_ref, o_ref, pad_ref):
        for g, w in enumerate(POOL_WINDOWS):
            @pl.when(pl.program_id(0) == g)
            def _(w=w):
                dd = _pool_diff(u_ref, pad_ref, lp, w)
                y = jnp.dot(dd.astype(BF16), w_ref[0], preferred_element_type=F32)
                o_ref[...] = (y * sc_ref[...]).astype(BF16)

    return pl.pallas_call(
        body, name="pool_fwd", grid=(ng,),
        in_specs=[pl.BlockSpec((lp, POOL_GROUP), lambda g: (0, g)),
                  pl.BlockSpec((1, POOL_GROUP, POOL_GROUP), lambda g: (g, 0, 0)),
                  pl.BlockSpec((1, POOL_GROUP), lambda g: (0, g))],
        out_specs=pl.BlockSpec((lp, POOL_GROUP), lambda g: (0, g)),
        out_shape=jax.ShapeDtypeStruct((lp, POOL_W), BF16),
        scratch_shapes=[pltpu.VMEM((lp + HALO, POOL_GROUP), F32)],
        compiler_params=_cparams(("parallel",)),
    )(z, pool_w, pool_scale)


def pool_bwd(dyp, z, pool_w, pool_scale):
    lp = z.shape[0]
    ng = len(POOL_WINDOWS)

    def body(dy_ref, u_ref, w_ref, sc_ref, du_ref, dw_ref, dsc_ref, pad_ref):
        for g, w in enumerate(POOL_WINDOWS):
            @pl.when(pl.program_id(0) == g)
            def _(w=w):
                dd = _pool_diff(u_ref, pad_ref, lp, w).astype(BF16)
                wg = w_ref[0]
                ypre = jnp.dot(dd, wg, preferred_element_type=F32)
                dy = dy_ref[...]
                dsc_ref[...] = jnp.sum(dy * ypre, axis=0, keepdims=True)
                dypre = (dy * sc_ref[...]).astype(BF16)
                dw_ref[0] = lax.dot_general(dd, dypre, _DIMS["tn"], preferred_element_type=F32)
                ddd = lax.dot_general(dypre, wg, _DIMS["nt"], preferred_element_type=F32)
                pad_ref[pl.ds(0, lp), :] = ddd / _pool_counts(lp, w)
                pad_ref[pl.ds(lp, HALO), :] = jnp.zeros((HALO, POOL_GROUP), F32)
                acc = -ddd
                for s in range(w):
                    acc = acc + pad_ref[pl.ds(s, lp), :]
                du_ref[...] = acc.astype(BF16)

    return pl.pallas_call(
        body, name="pool_bwd", grid=(ng,),
        in_specs=[pl.BlockSpec((lp, POOL_GROUP), lambda g: (0, g)),
                  pl.BlockSpec((lp, POOL_GROUP), lambda g: (0, g)),
                  pl.BlockSpec((1, POOL_GROUP, POOL_GROUP), lambda g: (g, 0, 0)),
                  pl.BlockSpec((1, POOL_GROUP), lambda g: (0, g))],
        out_specs=[pl.BlockSpec((lp, POOL_GROUP), lambda g: (0, g)),
                   pl.BlockSpec((1, POOL_GROUP, POOL_GROUP), lambda g: (g, 0, 0)),
                   pl.BlockSpec((1, POOL_GROUP), lambda g: (0, g))],
        out_shape=[jax.ShapeDtypeStruct((lp, POOL_W), BF16),
                   jax.ShapeDtypeStruct((ng, POOL_GROUP, POOL_GROUP), F32),
                   jax.ShapeDtypeStruct((1, POOL_W), F32)],
        scratch_shapes=[pltpu.VMEM((lp + HALO, POOL_GROUP), F32)],
        compiler_params=_cparams(("parallel",)),
    )(dyp, z, pool_w, pool_scale)


Q_COLS = HEADS * HEAD_W
ROPE_BLOCK = LANE


def rope_tables(lp):
    pos = jnp.maximum(jnp.arange(lp, dtype=F32) - PAD_ROWS, 0.0)
    inv = ROPE_THETA ** (-jnp.arange(0, QK_ROPE, 2, dtype=F32) / QK_ROPE)
    ang = pos[:, None] * inv[None, :]
    cos, sin, zero = jnp.cos(ang), jnp.sin(ang), jnp.zeros_like(ang)
    return jnp.stack([jnp.concatenate([cos, cos, zero, zero], axis=1),
                      jnp.concatenate([-sin, zero, zero, zero], axis=1),
                      jnp.concatenate([zero, sin, zero, zero], axis=1)])


def _rope(x, tabs):
    return x * tabs[0] + pltpu.roll(x, 96, 1) * tabs[1] + pltpu.roll(x, 32, 1) * tabs[2]


def _rope_bwd(g, tabs):
    return g * tabs[0] + pltpu.roll(g * tabs[1], 32, 1) + pltpu.roll(g * tabs[2], 96, 1)


def _tab_spec(tr):
    return pl.BlockSpec((3, tr, ROPE_BLOCK), lambda i: (0, i, 0))


def mla_prep(z, g_q, g_kv, tabs):
    lp = z.shape[0]
    tr = _row_tile(lp)
    r = g_q.shape[1]

    def body(cq_ref, ckv_ref, kr_ref, gq_ref, gkv_ref, tab_ref, qn_ref, kvn_ref, kpe_ref):
        cq, ckv = cq_ref[...], ckv_ref[...]
        qn_ref[...] = (cq * _rstd(cq) * gq_ref[...]).astype(BF16)
        kvn_ref[...] = (ckv * _rstd(ckv) * gkv_ref[...]).astype(BF16)
        kpe_ref[...] = _rope(kr_ref[...], tab_ref[...]).astype(BF16)

    return pl.pallas_call(
        body, name="mla_prep", grid=(lp // tr,),
        in_specs=[_row_spec(tr, r, Z_CQ // r), _row_spec(tr, r, Z_CKV // r), _row_spec(tr, ROPE_BLOCK, Z_KR // ROPE_BLOCK),
                  _vec_spec(r), _vec_spec(r), _tab_spec(tr)],
        out_specs=[_row_spec(tr, r), _row_spec(tr, r), _row_spec(tr, ROPE_BLOCK)],
        out_shape=[jax.ShapeDtypeStruct((lp, r), BF16), jax.ShapeDtypeStruct((lp, r), BF16),
                   jax.ShapeDtypeStruct((lp, ROPE_BLOCK), BF16)],
        compiler_params=_cparams(("parallel",)),
    )(z, z, z, g_q, g_kv, tabs)


def q_proj(qn, wq_p, tabs):
    lp = qn.shape[0]
    tm = _m_tile(lp)
    tn = 4 * HEAD_W

    def epi(acc, tab):
        acc = acc * SOFTMAX_SCALE
        parts = []
        for t in range(tn // HEAD_W):
            parts.append(acc[:, t * HEAD_W:t * HEAD_W + QK_NOPE])
            parts.append(_rope(acc[:, t * HEAD_W + QK_NOPE:(t + 1) * HEAD_W], tab))
        return (jnp.concatenate(parts, axis=1),)

    return _mm("q_proj", "nn", qn, wq_p, tm, tn, [((lp, Q_COLS), BF16, (tm, tn), lambda i, j: (i, j))], epi=epi,
               extras=[(tabs, (3, tm, ROPE_BLOCK), lambda i, j: (0, i, 0))])[0]


def kv_proj(kvn, wkv):
    return _mm_plain("kv_proj", "nn", kvn, wkv, _m_tile(kvn.shape[0]), 1024, BF16)


def q_rope_bwd(dq, tabs):
    lp = dq.shape[0]
    tr = _row_tile(lp)

    def body(dq_ref, tab_ref, o_ref):
        tab = tab_ref[...]
        for h in range(HEADS):
            nope = dq_ref[:, h * HEAD_W:h * HEAD_W + QK_NOPE] * SOFTMAX_SCALE
            o_ref[:, h * HEAD_W:h * HEAD_W + QK_NOPE] = nope.astype(BF16)
            o_ref[:, h * HEAD_W + QK_NOPE:(h + 1) * HEAD_W] = _rope_bwd(
                dq_ref[:, h * HEAD_W + QK_NOPE:(h + 1) * HEAD_W] * SOFTMAX_SCALE, tab).astype(BF16)

    return pl.pallas_call(
        body, name="q_rope_bwd", grid=(lp // tr,),
        in_specs=[_row_spec(tr, Q_COLS), _tab_spec(tr)], out_specs=_row_spec(tr, Q_COLS),
        out_shape=jax.ShapeDtypeStruct((lp, Q_COLS), BF16), compiler_params=_cparams(("parallel",)),
    )(dq, tabs)


def mla_prep_bwd(dqn, dkvn, dkpe_h, z, g_q, g_kv, tabs):
    lp = z.shape[0]
    tr = _row_tile(lp)
    r = g_q.shape[1]

    def body(dqn_ref, dkvn_ref, dkpe_ref, cq_ref, ckv_ref, gq_ref, gkv_ref, tab_ref,
             dcq_ref, dckv_ref, dkr_ref, dgq_ref, dgkv_ref):
        @pl.when(pl.program_id(0) == 0)
        def _():
            dgq_ref[...] = jnp.zeros_like(dgq_ref)
            dgkv_ref[...] = jnp.zeros_like(dgkv_ref)

        dcq, dgq = _norm_bwd(dqn_ref[...], cq_ref[...], gq_ref[...])
        dckv, dgkv = _norm_bwd(dkvn_ref[...], ckv_ref[...], gkv_ref[...])
        dcq_ref[...] = dcq.astype(BF16)
        dckv_ref[...] = dckv.astype(BF16)
        dgq_ref[...] += dgq
        dgkv_ref[...] += dgkv
        dkpe = dkpe_ref[0]
        for h in range(1, HEADS):
            dkpe = dkpe + dkpe_ref[h]
        dkr_ref[...] = _rope_bwd(dkpe, tab_ref[...]).astype(BF16)

    return pl.pallas_call(
        body, name="mla_prep_bwd", grid=(lp // tr,),
        in_specs=[_row_spec(tr, r), _row_spec(tr, r), pl.BlockSpec((HEADS, tr, ROPE_BLOCK), lambda i: (0, i, 0)),
                  _row_spec(tr, r, Z_CQ // r), _row_spec(tr, r, Z_CKV // r), _vec_spec(r), _vec_spec(r), _tab_spec(tr)],
        out_specs=[_row_spec(tr, r), _row_spec(tr, r), _row_spec(tr, ROPE_BLOCK), _vec_spec(r), _vec_spec(r)],
        out_shape=[jax.ShapeDtypeStruct((lp, r), BF16), jax.ShapeDtypeStruct((lp, r), BF16),
                   jax.ShapeDtypeStruct((lp, ROPE_BLOCK), BF16),
                   jax.ShapeDtypeStruct((1, r), F32), jax.ShapeDtypeStruct((1, r), F32)],
        compiler_params=_cparams(("arbitrary",)),
    )(dqn, dkvn, dkpe_h, z, z, g_q, g_kv, tabs)


def _attn_tile(lp):
    return _pick(lp, (528, 128))


def _scores(q, kcat, q_tile, k_tile, t, masked):
    s = lax.dot_general(q, kcat, _DIMS["nt"], preferred_element_type=F32)
    if not masked:
        return s
    qpos = q_tile * t + lax.broadcasted_iota(jnp.int32, (t, t), 0)
    kpos = k_tile * t + lax.broadcasted_iota(jnp.int32, (t, t), 1)
    return jnp.where((kpos <= qpos) & (kpos >= PAD_ROWS), s, jnp.float32(-1e30))


def _causal_pairs(nt, k_major):
    if k_major:
        pairs = [(qi, ki) for ki in range(nt) for qi in range(ki, nt)]
    else:
        pairs = [(qi, ki) for qi in range(nt) for ki in range(qi + 1)]
    return (jnp.asarray([p[0] for p in pairs], jnp.int32), jnp.asarray([p[1] for p in pairs], jnp.int32))


def _on_masked_or_not(q_tile, k_tile, fn):
    needs_mask = (q_tile == k_tile) | (k_tile == 0)

    @pl.when(needs_mask)
    def _():
        fn(True)

    @pl.when(jnp.logical_not(needs_mask))
    def _():
        fn(False)


def flash_fwd(q, kv, kpe):
    lp = q.shape[0]
    t = _attn_tile(lp)
    q_tab, k_tab = _causal_pairs(lp // t, k_major=False)

    def body(q_tab_ref, k_tab_ref, q_ref, kv_ref, kpe_ref, o32_ref, o16_ref, lse_ref, m_sc, l_sc, acc_sc):
        pair = pl.program_id(1)
        qi, ki = q_tab_ref[pair], k_tab_ref[pair]

        @pl.when(ki == 0)
        def _():
            m_sc[...] = jnp.full_like(m_sc, -jnp.inf)
            l_sc[...] = jnp.zeros_like(l_sc)
            acc_sc[...] = jnp.zeros_like(acc_sc)

        def step(masked):
            kvt = kv_ref[...]
            kcat = jnp.concatenate([kvt[:, :QK_NOPE], kpe_ref[...]], axis=1)
            s = _scores(q_ref[...], kcat, qi, ki, t, masked)
            m_prev = m_sc[...]
            m_new = jnp.maximum(m_prev, jnp.max(s, axis=1, keepdims=True))
            alpha = jnp.exp(m_prev - m_new)
            p = jnp.exp(s - m_new[:, :1])
            l_sc[...] = alpha * l_sc[...] + jnp.sum(p, axis=1, keepdims=True)
            acc_sc[...] = alpha * acc_sc[...] + jnp.dot(p.astype(BF16), kvt[:, QK_NOPE:], preferred_element_type=F32)
            m_sc[...] = m_new

        _on_masked_or_not(qi, ki, step)

        @pl.when(ki == qi)
        def _():
            l = l_sc[...]
            o = acc_sc[...] / l
            o32_ref[...] = o
            o16_ref[...] = o.astype(BF16)
            lse_ref[0] = m_sc[...] + jnp.log(l)

    return pl.pallas_call(
        body, name="flash_fwd",
        grid_spec=pltpu.PrefetchScalarGridSpec(
            num_scalar_prefetch=2, grid=(HEADS, q_tab.shape[0]),
            in_specs=[pl.BlockSpec((t, HEAD_W), lambda h, p, qt, kt: (qt[p], h)),
                      pl.BlockSpec((t, HEAD_W), lambda h, p, qt, kt: (kt[p], h)),
                      pl.BlockSpec((t, ROPE_BLOCK), lambda h, p, qt, kt: (kt[p], 0))],
            out_specs=[pl.BlockSpec((t, V_DIM), lambda h, p, qt, kt: (qt[p], h)),
                       pl.BlockSpec((t, V_DIM), lambda h, p, qt, kt: (qt[p], h)),
                       pl.BlockSpec((1, t, LANE), lambda h, p, qt, kt: (h, qt[p], 0))],
            scratch_shapes=[pltpu.VMEM((t, LANE), F32), pltpu.VMEM((t, LANE), F32), pltpu.VMEM((t, V_DIM), F32)]),
        out_shape=[jax.ShapeDtypeStruct((lp, HEADS * V_DIM), F32), jax.ShapeDtypeStruct((lp, HEADS * V_DIM), BF16),
                   jax.ShapeDtypeStruct((HEADS, lp, LANE), F32)],
        compiler_params=_cparams(("parallel", "arbitrary")),
    )(q_tab, k_tab, q, kv, kpe)


def flash_bwd(q, kv, kpe, o32, lse, do):
    lp = q.shape[0]
    t = _attn_tile(lp)
    nt = lp // t
    q_tab, k_tab = _causal_pairs(nt, k_major=True)

    def body(q_tab_ref, k_tab_ref, q_ref, kv_ref, kpe_ref, o_ref, lse_ref, do_ref, dq_ref, dkv_ref, dkpe_ref,
             dk_sc, dv_sc):
        pair = pl.program_id(1)
        qi, ki = q_tab_ref[pair], k_tab_ref[pair]

        @pl.when(pair == 0)
        def _():
            dq_ref[...] = jnp.zeros_like(dq_ref)

        @pl.when(qi == ki)
        def _():
            dk_sc[...] = jnp.zeros_like(dk_sc)
            dv_sc[...] = jnp.zeros_like(dv_sc)

        def step(masked):
            qt = q_ref[...]
            kvt = kv_ref[...]
            kcat = jnp.concatenate([kvt[:, :QK_NOPE], kpe_ref[...]], axis=1)
            s = _scores(qt, kcat, qi, ki, t, masked)
            p = jnp.exp(s - lse_ref[0][:, :1])
            do = do_ref[...]
            delta = jnp.sum(do * o_ref[...], axis=1, keepdims=True)
            do16 = do.astype(BF16)
            dv_sc[...] += lax.dot_general(p.astype(BF16), do16, _DIMS["tn"], preferred_element_type=F32)
            dp = lax.dot_general(do16, kvt[:, QK_NOPE:], _DIMS["nt"], preferred_element_type=F32)
            ds = (p * (dp - delta)).astype(BF16)
            dk_sc[...] += lax.dot_general(ds, qt, _DIMS["tn"], preferred_element_type=F32)
            row = pl.multiple_of(qi * t, t)
            dq_ref[pl.ds(row, t), :] += jnp.dot(ds, kcat, preferred_element_type=F32)

        _on_masked_or_not(qi, ki, step)

        @pl.when(qi == nt - 1)
        def _():
            dk = dk_sc[...]
            dkv_ref[...] = jnp.concatenate([dk[:, :QK_NOPE], dv_sc[...]], axis=1).astype(BF16)
            dkpe_ref[0] = dk[:, QK_NOPE:]

    qmap = lambda h, p, qt, kt: (qt[p], h)
    return pl.pallas_call(
        body, name="flash_bwd",
        grid_spec=pltpu.PrefetchScalarGridSpec(
            num_scalar_prefetch=2, grid=(HEADS, q_tab.shape[0]),
            in_specs=[pl.BlockSpec((t, HEAD_W), qmap),
                      pl.BlockSpec((t, HEAD_W), lambda h, p, qt, kt: (kt[p], h)),
                      pl.BlockSpec((t, ROPE_BLOCK), lambda h, p, qt, kt: (kt[p], 0)),
                      pl.BlockSpec((t, V_DIM), qmap),
                      pl.BlockSpec((1, t, LANE), lambda h, p, qt, kt: (h, qt[p], 0)),
                      pl.BlockSpec((t, V_DIM), qmap)],
            out_specs=[pl.BlockSpec((lp, HEAD_W), lambda h, p, qt, kt: (0, h)),
                       pl.BlockSpec((t, HEAD_W), lambda h, p, qt, kt: (kt[p], h)),
                       pl.BlockSpec((1, t, ROPE_BLOCK), lambda h, p, qt, kt: (h, kt[p], 0))],
            scratch_shapes=[pltpu.VMEM((t, HEAD_W), F32), pltpu.VMEM((t, V_DIM), F32)]),
        out_shape=[jax.ShapeDtypeStruct((lp, Q_COLS), F32), jax.ShapeDtypeStruct((lp, Q_COLS), BF16),
                   jax.ShapeDtypeStruct((HEADS, lp, ROPE_BLOCK), F32)],
        compiler_params=_cparams(("parallel", "arbitrary")),
    )(q_tab, k_tab, q, kv, kpe, o32, lse, do)


def _ij(i, j):
    return (i, j)


def mixer_fwd(a2, w, tabs, pool_scale, g_q, g_kv, ex):
    lp, d = a2.shape
    tm = _m_tile(lp)
    tn = 512
    z = _mm_plain("mix_in", "nn", a2, w["w_in"], tm, 1280, F32)
    yp = pool_fwd(z, w["pool_w"], pool_scale)
    qn, kvn, kpe = mla_prep(z, g_q, g_kv, tabs)
    q = q_proj(qn, w["w_q_b"], tabs)
    kv = kv_proj(kvn, w["w_kv_b"])
    o32, o16, lse = flash_fwd(q, kv, kpe)
    ex.point("flash_fwd_done", o16)
    y_pool = _mm_plain("pool_out", "nn", yp, w["w_pool_o"], tm, tn, F32, after=ex.token)

    def epi(acc, ypl, gp, gm):
        return jax.nn.sigmoid(gp) * ypl + jax.nn.sigmoid(gm) * acc, acc

    y, y_mla = _mm("mla_out_gate", "nn", o16, w["w_mla_o"], tm, tn,
                   [((lp, d), BF16, (tm, tn), _ij), ((lp, d), F32, (tm, tn), _ij)], epi=epi,
                   extras=[(y_pool, (tm, tn), _ij), (z, (tm, tn), lambda i, j: (i, Z_GP // tn + j)),
                           (z, (tm, tn), lambda i, j: (i, Z_GM // tn + j))])
    m = _mm_plain("mix_out", "nn", y, w["w_out"], tm, tn, F32)
    return m, dict(z=z, yp=yp, qn=qn, kvn=kvn, kpe=kpe, q=q, kv=kv, o32=o32, o16=o16, lse=lse,
                   y_pool=y_pool, y_mla=y_mla, y=y)


def mixer_bwd(dm, a2, sv, w, tabs, pool_scale, g_q, g_kv, after=None):
    lp, d = dm.shape
    tm = _m_tile(lp)
    tn = 512
    z = sv["z"]

    def epi(acc, ypl, yml, gp, gm):
        sp, sm = jax.nn.sigmoid(gp), jax.nn.sigmoid(gm)
        return acc * sp, acc * sm, acc * ypl * (sp * (1.0 - sp)), acc * yml * (sm * (1.0 - sm))

    dyp, dym, dgp, dgm = _mm(
        "gate_bwd", "nt", dm, w["w_out"], tm, tn, [((lp, d), BF16, (tm, tn), _ij)] * 4, epi=epi,
        extras=[(sv["y_pool"], (tm, tn), _ij), (sv["y_mla"], (tm, tn), _ij),
                (z, (tm, tn), lambda i, j: (i, Z_GP // tn + j)), (z, (tm, tn), lambda i, j: (i, Z_GM // tn + j))],
        after=after)
    g = {}
    g["w_out"] = _mm_plain("dw_out", "tn", sv["y"], dm, 1024, 1024, BF16, after=after)
    g["w_pool_o"] = _mm_plain("dw_pool_o", "tn", sv["yp"], dyp, 512, 1024, BF16)
    dypre = _mm_plain("pool_out_bwd", "nt", dyp, w["w_pool_o"], tm, tn, F32)
    du, g["pool_w"], d_pool_scale = pool_bwd(dypre, z, w["pool_w"], pool_scale)
    g["w_mla_o"] = _mm_plain("dw_mla_o", "tn", sv["o16"], dym, 1024, 1024, BF16)
    do = _mm_plain("mla_out_bwd", "nt", dym, w["w_mla_o"], tm, tn, F32)
    dq, dkv, dkpe_h = flash_bwd(sv["q"], sv["kv"], sv["kpe"], sv["o32"], sv["lse"], do)
    dql = q_rope_bwd(dq, tabs)
    g["w_q_b"] = _mm_plain("dw_q_b", "tn", sv["qn"], dql, 512, 1024, BF16)
    dqn = _mm_plain("q_proj_bwd", "nt", dql, w["w_q_b"], tm, 512, F32)
    g["w_kv_b"] = _mm_plain("dw_kv_b", "tn", sv["kvn"], dkv, 512, 1024, BF16)
    dkvn = _mm_plain("kv_proj_bwd", "nt", dkv, w["w_kv_b"], tm, 512, F32)
    dcq, dckv, dkr, d_gq, d_gkv = mla_prep_bwd(dqn, dkvn, dkpe_h, z, g_q, g_kv, tabs)
    dz = jnp.concatenate([du, dcq, dckv, dgp, dgm, dkr, jnp.zeros((lp, Z_COLS - Z_KR - ROPE_BLOCK), BF16)], axis=1)
    g["w_in"] = _mm_plain("dw_in", "tn", a2, dz, 1024, 1280, BF16)
    da2 = _mm_plain("mix_in_bwd", "nt", dz, w["w_in"], tm, tn, F32)
    return da2, g, dict(pool_scale=d_pool_scale, q_a_norm=d_gq, kv_a_norm=d_gkv)


_ANY = pl.BlockSpec(memory_space=pl.ANY)
_MESH = pl.DeviceIdType.MESH


def _my_pos():
    return lax.axis_index("x"), lax.axis_index("y"), lax.axis_index("c")


LEAD = "lead"
COLS = "cols"
COLS_GU = "cols_gu"


def _col_block(layout, dev):
    return dev if layout == COLS else 2 * (dev % 4) + dev // 4


def _dev_block(ref, layout, dev, cols):
    if layout == LEAD:
        return ref.at[dev]
    return ref.at[:, pl.ds(pl.multiple_of(_col_block(layout, dev) * cols, LANE), cols)]


def _gathered_shape(shard_shape, layout):
    if layout == LEAD:
        return (N_DEV, *shard_shape)
    return (shard_shape[0], N_DEV * shard_shape[1])


def all_gather(name, shards, layouts, after=None):
    n = len(shards)
    behind = _behind(after)

    def body(*refs):
        ins, outs = refs[:n], refs[n + len(behind):2 * n + len(behind)]
        send_sems, recv_sems, local_sems = refs[2 * n + len(behind):]
        x, y, c = _my_pos()
        me, sibling = (x, y, c), (x, y, 1 - c)
        chips = [(1 - x, y), (x, 1 - y), (1 - x, 1 - y)]

        def blk(a, px, py, pc):
            return _dev_block(outs[a], layouts[a], 4 * px + 2 * py + pc, shards[a].shape[-1])

        def copy(a, k, block, to, src=None):
            return pltpu.make_async_remote_copy(
                src_ref=blk(a, *block) if src is None else src, dst_ref=blk(a, *block),
                send_sem=send_sems.at[a, k], recv_sem=recv_sems.at[a, k], device_id=to, device_id_type=_MESH)

        mine = [pltpu.make_async_copy(ins[a], blk(a, *me), local_sems.at[a]) for a in range(n)]
        for cp in mine:
            cp.start()
        first = []
        for a in range(n):
            first.append(copy(a, 0, me, sibling, src=ins[a]))
            first += [copy(a, 1 + j, me, (*chip, c), src=ins[a]) for j, chip in enumerate(chips)]
        for cp in first:
            cp.start()
        passed = []
        for j, chip in enumerate(chips):
            for a in range(n):
                copy(a, 1 + j, (*chip, c), me).wait_recv()
                fwd = copy(a, 4 + j, (*chip, c), sibling)
                fwd.start()
                passed.append(fwd)
        for a in range(n):
            copy(a, 0, sibling, me).wait_recv()
            for j, chip in enumerate(chips):
                copy(a, 4 + j, (*chip, 1 - c), me).wait_recv()
        for cp in first + passed:
            cp.wait_send()
        for cp in mine:
            cp.wait()

    return pl.pallas_call(
        body, name=name,
        in_specs=[_ANY] * (n + len(behind)), out_specs=[_ANY] * n,
        out_shape=[jax.ShapeDtypeStruct(_gathered_shape(s.shape, lay), s.dtype)
                   for s, lay in zip(shards, layouts, strict=True)],
        scratch_shapes=[pltpu.SemaphoreType.DMA((n, 7)), pltpu.SemaphoreType.DMA((n, 7)), pltpu.SemaphoreType.DMA((n,))],
    )(*shards, *behind)


def _shard_shape(grad, layout):
    return grad.shape[1:] if layout == LEAD else (grad.shape[0], grad.shape[1] // N_DEV)


def rs_sibling(name, grads, layouts):
    n = len(grads)

    def body(*refs):
        ins, outs = refs[:n], refs[n:2 * n]
        send_sems, recv_sems = refs[2 * n:]
        x, y, c = _my_pos()
        cps = []
        for a in range(n):
            for k in range(4):
                cp = pltpu.make_async_remote_copy(
                    src_ref=_dev_block(ins[a], layouts[a], 2 * k + (1 - c), outs[a].shape[-1]), dst_ref=outs[a].at[k],
                    send_sem=send_sems.at[a, k], recv_sem=recv_sems.at[a, k],
                    device_id=(x, y, 1 - c), device_id_type=_MESH)
                cp.start()
                cps.append(cp)
        for cp in cps:
            cp.wait()

    return pl.pallas_call(
        body, name=name,
        in_specs=[_ANY] * n, out_specs=[_ANY] * n,
        out_shape=[jax.ShapeDtypeStruct((4, *_shard_shape(g, lay)), g.dtype) for g, lay in zip(grads, layouts, strict=True)],
        scratch_shapes=[pltpu.SemaphoreType.DMA((n, 4)), pltpu.SemaphoreType.DMA((n, 4))],
    )(*grads)


def rs_chips(name, sums):
    n = len(sums)

    def body(*refs):
        ins, outs = refs[:n], refs[n:2 * n]
        send_sems, recv_sems = refs[2 * n:]
        x, y, c = _my_pos()
        chips = [(1 - x, y), (x, 1 - y), (1 - x, 1 - y)]
        cps = []
        for a in range(n):
            for j, chip in enumerate(chips):
                cp = pltpu.make_async_remote_copy(
                    src_ref=ins[a].at[2 * chip[0] + chip[1]], dst_ref=outs[a].at[j],
                    send_sem=send_sems.at[a, j], recv_sem=recv_sems.at[a, j],
                    device_id=(*chip, c), device_id_type=_MESH)
                cp.start()
                cps.append(cp)
        for cp in cps:
            cp.wait()

    return pl.pallas_call(
        body, name=name,
        in_specs=[_ANY] * n, out_specs=[_ANY] * n,
        out_shape=[jax.ShapeDtypeStruct((3, *s.shape[1:]), s.dtype) for s in sums],
        scratch_shapes=[pltpu.SemaphoreType.DMA((n, 3)), pltpu.SemaphoreType.DMA((n, 3))],
    )(*sums)


_HBM = pl.BlockSpec(memory_space=pltpu.HBM)
_SEM = pl.BlockSpec(memory_space=pltpu.SEMAPHORE)
_EFFECT = pltpu.SideEffectType.DATAFLOW_SIDE_EFFECTING


def _in_hbm(a):
    return pltpu.with_memory_space_constraint(a, pltpu.HBM)


def split_start(name, bufs, plan, n_copies, after=None):
    nb = len(bufs)
    extra = [] if after is None else [after]

    def body(*refs):
        buf_refs = refs[:nb]
        send_sems, recv_sems = refs[nb + len(extra)], refs[nb + len(extra) + 1]
        token = refs[-1]
        copies = plan(buf_refs)
        assert len(copies) == n_copies
        for k, (src, dst, to) in enumerate(copies):
            pltpu.make_async_remote_copy(src_ref=src, dst_ref=dst, send_sem=send_sems.at[k], recv_sem=recv_sems.at[k],
                                         device_id=to, device_id_type=_MESH).start()
        token[...] = jnp.zeros_like(token)

    out = pl.pallas_call(
        body, name=name,
        out_shape=(pltpu.SemaphoreType.DMA((n_copies,)), pltpu.SemaphoreType.DMA((n_copies,)),
                   *[pltpu.HBM(b.shape, b.dtype) for b in bufs], jax.ShapeDtypeStruct((8, LANE), F32)),
        in_specs=[_HBM] * nb + [_ANY] * len(extra),
        out_specs=(_SEM, _SEM, *[_HBM] * nb, pl.BlockSpec(memory_space=pltpu.VMEM)),
        input_output_aliases={i: 2 + i for i in range(nb)},
        compiler_params=pltpu.CompilerParams(has_side_effects=_EFFECT),
    )(*[_in_hbm(b) for b in bufs], *extra)
    return out[0], out[1], list(out[2:2 + nb]), out[-1]


def split_wait(name, bufs, send_sems, recv_sems, plan, after):
    nb = len(bufs)

    def body(*refs):
        buf_refs = refs[:nb]
        s_sems, r_sems = refs[nb], refs[nb + 1]
        for k, (src, dst, to) in enumerate(plan(buf_refs)):
            cp = pltpu.make_async_remote_copy(src_ref=src, dst_ref=dst, send_sem=s_sems.at[k], recv_sem=r_sems.at[k],
                                              device_id=to, device_id_type=_MESH)
            cp.wait_send()
            cp.wait_recv()

    out = pl.pallas_call(
        body, name=name,
        out_shape=tuple(pltpu.HBM(b.shape, b.dtype) for b in bufs),
        in_specs=[_HBM] * nb + [_SEM, _SEM, _ANY],
        out_specs=tuple([_HBM] * nb),
        input_output_aliases={i: i for i in range(nb)},
        compiler_params=pltpu.CompilerParams(has_side_effects=_EFFECT),
    )(*bufs, send_sems, recv_sems, after)
    return list(out)


def _ag_own_plan(shapes, layouts):
    n = len(shapes)

    def plan(refs):
        x, y, c = _my_pos()
        targets = [(x, y, 1 - c), (1 - x, y, c), (x, 1 - y, c), (1 - x, 1 - y, c)]
        out = []
        for a in range(n):
            blk = _dev_block(refs[a], layouts[a], 4 * x + 2 * y + c, shapes[a][-1])
            out += [(blk, blk, to) for to in targets]
        return out

    return plan, 4 * n


def _ag_pass_plan(shapes, layouts):
    n = len(shapes)

    def plan(refs):
        x, y, c = _my_pos()
        out = []
        for a in range(n):
            for px, py in [(1 - x, y), (x, 1 - y), (1 - x, 1 - y)]:
                blk = _dev_block(refs[a], layouts[a], 4 * px + 2 * py + c, shapes[a][-1])
                out.append((blk, blk, (x, y, 1 - c)))
        return out

    return plan, 3 * n


def _rs_sibling_plan(layouts, n):
    def plan(refs):
        x, y, c = _my_pos()
        return [(_dev_block(refs[a], layouts[a], 2 * k + (1 - c), refs[n + a].shape[-1]), refs[n + a].at[k], (x, y, 1 - c))
                for a in range(n) for k in range(4)]

    return plan, 4 * n


def _rs_chips_plan(n):
    def plan(refs):
        x, y, c = _my_pos()
        return [(refs[a].at[2 * px + py], refs[n + a].at[j], (px, py, c))
                for a in range(n) for j, (px, py) in enumerate([(1 - x, y), (x, 1 - y), (1 - x, 1 - y)])]

    return plan, 3 * n


def place_own(name, shard, layout, dtype, dev, after):
    r, c = shard.shape
    tr = _ew_rows(r, c)
    if layout == LEAD:
        o_spec = pl.BlockSpec((None, tr, c), lambda i, dev_ref: (dev_ref[0], i, 0))
    else:
        o_spec = pl.BlockSpec((tr, c), lambda i, dev_ref: (i, _col_block(layout, dev_ref[0])))
    extra = [] if after is None else [after]

    def body(dev_ref, s_ref, *rest):
        rest[-1][...] = s_ref[...].astype(dtype)

    return pl.pallas_call(
        body, name=name,
        grid_spec=pltpu.PrefetchScalarGridSpec(
            num_scalar_prefetch=1, grid=(r // tr,),
            in_specs=[pl.BlockSpec((tr, c), lambda i, dev_ref: (i, 0))] + [_ANY] * len(extra),
            out_specs=o_spec),
        out_shape=jax.ShapeDtypeStruct(_gathered_shape(shard.shape, layout), dtype),
        compiler_params=_cparams(("parallel",)),
    )(dev, shard, *extra)


def _ew_rows(r, c, itemsize=4):
    for t in (1024, 512, 256, 128, 64, 32, 16):
        if r % t == 0 and t * c * itemsize <= 768 * 1024:
            return t
    raise ValueError((r, c))


def rs_add(name, grad, layout, recv, core):
    _, r, c = recv.shape
    tr = _ew_rows(r, c, itemsize=2)
    if layout == LEAD:
        g_spec = pl.BlockSpec((None, tr, c), lambda k, i, core_ref: (2 * k + core_ref[0], i, 0))
    else:
        g_spec = pl.BlockSpec((tr, c), lambda k, i, core_ref: (i, _col_block(layout, 2 * k + core_ref[0])))

    def body(core_ref, g_ref, r_ref, o_ref):
        o_ref[...] = (g_ref[...].astype(F32) + r_ref[...].astype(F32)).astype(BF16)

    return pl.pallas_call(
        body, name=name,
        grid_spec=pltpu.PrefetchScalarGridSpec(
            num_scalar_prefetch=1, grid=(4, r // tr),
            in_specs=[g_spec, pl.BlockSpec((None, tr, c), lambda k, i, core_ref: (k, i, 0))],
            out_specs=pl.BlockSpec((None, tr, c), lambda k, i, core_ref: (k, i, 0))),
        out_shape=jax.ShapeDtypeStruct((4, r, c), BF16),
        compiler_params=_cparams(("parallel", "parallel")),
    )(core, grad, recv)


def _adamw(w, g, m, v):
    m = ADAM_B1 * m + (1.0 - ADAM_B1) * g
    v = ADAM_B2 * v + (1.0 - ADAM_B2) * jnp.square(g)
    m_hat = m / (1.0 - ADAM_B1 ** ADAM_STEP)
    v_hat = v / (1.0 - ADAM_B2 ** ADAM_STEP)
    delta = -ADAM_LR * (m_hat / (jnp.sqrt(v_hat) + ADAM_EPS) + ADAM_WD * w)
    return delta, m, v


def adamw_shard(name, w, m, v, sums, recv, chip):
    r, c = w.shape
    tr = _ew_rows(r, c)

    def body(chip_ref, w_ref, m_ref, v_ref, s_ref, r_ref, g_ref, d_ref, mo_ref, vo_ref):
        g = s_ref[0].astype(F32)
        for j in range(3):
            g = g + r_ref[j].astype(F32)
        d, mn, vn = _adamw(w_ref[...], g, m_ref[...], v_ref[...])
        g_ref[...] = g
        d_ref[...] = d
        mo_ref[...] = mn
        vo_ref[...] = vn

    spec = pl.BlockSpec((tr, c), lambda i, chip_ref: (i, 0))
    return pl.pallas_call(
        body, name=name,
        grid_spec=pltpu.PrefetchScalarGridSpec(
            num_scalar_prefetch=1, grid=(r // tr,),
            in_specs=[spec, spec, spec,
                      pl.BlockSpec((1, tr, c), lambda i, chip_ref: (chip_ref[0], i, 0)),
                      pl.BlockSpec((3, tr, c), lambda i, chip_ref: (0, i, 0))],
            out_specs=[spec] * 4),
        out_shape=[jax.ShapeDtypeStruct((r, c), F32)] * 4,
        compiler_params=_cparams(("parallel",)),
    )(chip, w, m, v, sums, recv)


def reduce_small(gathered):
    _, r, c = gathered.shape

    def body(g_ref, o_ref):
        acc = g_ref[0]
        for k in range(1, N_DEV):
            acc = acc + g_ref[k]
        o_ref[...] = acc

    return pl.pallas_call(body, name="reduce_small", out_shape=jax.ShapeDtypeStruct((r, c), F32))(gathered)


def adamw_small(ws, gs, ms, vs):
    n = len(ws)

    def body(*refs):
        w_r, g_r, m_r, v_r = refs[:n], refs[n:2 * n], refs[2 * n:3 * n], refs[3 * n:4 * n]
        d_o, m_o, v_o = refs[4 * n:5 * n], refs[5 * n:6 * n], refs[6 * n:7 * n]
        for a in range(n):
            d, mn, vn = _adamw(w_r[a][...], g_r[a][...], m_r[a][...], v_r[a][...])
            d_o[a][...] = d
            m_o[a][...] = mn
            v_o[a][...] = vn

    shapes = [jax.ShapeDtypeStruct(w.shape, F32) for w in ws]
    out = pl.pallas_call(body, name="adamw_small", out_shape=shapes * 3)(*ws, *gs, *ms, *vs)
    return out[:n], out[n:2 * n], out[2 * n:]


WEIGHTS = ["meta_tokens", "norm_ffn1_pre", "norm_ffn1_post", "ffn1_w_gu", "ffn1_w_down", "norm_mix_pre",
           "norm_mix_post", "w_in", "pool_w", "pool_scale", "w_pool_o", "q_a_norm", "w_q_b", "kv_a_norm", "w_kv_b",
           "w_mla_o", "w_out", "norm_ffn2_pre", "norm_ffn2_post", "ffn2_w_gu", "ffn2_w_down"]
BIG = ["ffn1_w_gu", "ffn1_w_down", "w_in", "pool_w", "w_pool_o", "w_q_b", "w_kv_b", "w_mla_o", "w_out",
       "ffn2_w_gu", "ffn2_w_down"]
COL_SHARDED = ("w_in", "w_q_b")
GATHERED = {"ffn1_w_gu": COLS_GU, "ffn2_w_gu": COLS_GU, "w_pool_o": COLS, "w_kv_b": COLS}
GAINS =["norm_ffn1_pre", "norm_ffn1_post", "norm_mix_pre", "norm_mix_post", "norm_ffn2_pre", "norm_ffn2_post"]
SMALL = GAINS + ["pool_scale", "q_a_norm", "kv_a_norm"]
Z_SRC = 1024 + 512 + 512 + QK_ROPE


def _full_from_gathered(name, g):
    _, r, c = g.shape
    if name == "pool_w":
        ng = len(POOL_WINDOWS)
        return g.reshape(N_DEV, ng, r // ng, c).transpose(1, 0, 2, 3).reshape(ng, POOL_GROUP, POOL_GROUP)
    if name in COL_SHARDED:
        return g.transpose(1, 0, 2).reshape(r, N_DEV * c)
    return g.reshape(N_DEV * r, c)


def _blocks_from_full(name, dw):
    if name == "pool_w":
        ng = len(POOL_WINDOWS)
        return dw.reshape(ng, N_DEV, POOL_GROUP // N_DEV, POOL_GROUP).transpose(1, 0, 2, 3).reshape(
            N_DEV, ng * POOL_GROUP // N_DEV, POOL_GROUP)
    k, n = dw.shape
    if name in COL_SHARDED:
        return dw.reshape(k, N_DEV, n // N_DEV).transpose(1, 0, 2)
    return dw.reshape(N_DEV, k // N_DEV, n)


def _to_internal(name, w):
    if name == "w_in":
        d = w.shape[0]
        return jnp.concatenate([w[:, :Z_SRC - QK_ROPE], w[:, Z_SRC:], w[:, Z_SRC - QK_ROPE:Z_SRC],
                                jnp.zeros((d, Z_COLS - Z_KR - QK_ROPE), w.dtype)], axis=1)
    if name == "w_q_b":
        r = w.shape[0]
        w3 = w.reshape(r, HEADS, QK_NOPE + QK_ROPE)
        return jnp.pad(w3, ((0, 0), (0, 0), (0, HEAD_W - QK_NOPE - QK_ROPE))).reshape(r, Q_COLS)
    return w


def _from_internal(name, dw):
    if name == "w_in":
        return jnp.concatenate([dw[:, :Z_SRC - QK_ROPE], dw[:, Z_KR:Z_KR + QK_ROPE], dw[:, Z_SRC - QK_ROPE:Z_KR]], axis=1)
    if name == "w_q_b":
        r = dw.shape[0]
        return dw.reshape(r, HEADS, HEAD_W)[:, :, :QK_NOPE + QK_ROPE].reshape(r, HEADS * (QK_NOPE + QK_ROPE))
    return dw


def _shard2d(a):
    return a.reshape(-1, a.shape[-1])


def kernel(x, meta_tokens, norm_ffn1_pre, norm_ffn1_post, ffn1_w_gu, ffn1_w_down, norm_mix_pre, norm_mix_post, w_in, pool_w, pool_scale, w_pool_o, q_a_norm, w_q_b, kv_a_norm, w_kv_b, w_mla_o, w_out, norm_ffn2_pre, norm_ffn2_post, ffn2_w_gu, ffn2_w_down, loss_target, m_meta_tokens, m_norm_ffn1_pre, m_norm_ffn1_post, m_ffn1_w_gu, m_ffn1_w_down, m_norm_mix_pre, m_norm_mix_post, m_w_in, m_pool_w, m_pool_scale, m_w_pool_o, m_q_a_norm, m_w_q_b, m_kv_a_norm, m_w_kv_b, m_w_mla_o, m_w_out, m_norm_ffn2_pre, m_norm_ffn2_post, m_ffn2_w_gu, m_ffn2_w_down, v_meta_tokens, v_norm_ffn1_pre, v_norm_ffn1_post, v_ffn1_w_gu, v_ffn1_w_down, v_norm_mix_pre, v_norm_mix_post, v_w_in, v_pool_w, v_pool_scale, v_w_pool_o, v_q_a_norm, v_w_q_b, v_kv_a_norm, v_w_kv_b, v_w_mla_o, v_w_out, v_norm_ffn2_pre, v_norm_ffn2_post, v_ffn2_w_gu, v_ffn2_w_down):
    given = dict(locals())
    w_in_dev = {n: given[n] for n in WEIGHTS}
    m_in = {n: given["m_" + n] for n in WEIGHTS}
    v_in = {n: given["v_" + n] for n in WEIGHTS}
    xi, yi, ci = _my_pos()
    dev = 4 * xi + 2 * yi + ci
    core = jnp.reshape(ci, (1,)).astype(jnp.int32)
    chip = jnp.reshape(2 * xi + yi, (1,)).astype(jnp.int32)
    d = x.shape[-1]

    shards = {n: _shard2d(w_in_dev[n]) for n in BIG}
    ex = _Exchange(shards, meta_tokens, dev, core)
    gain = {n: given[n] for n in SMALL}
    loss_blk, grad_x, front, gsmall = local_step(x[0], loss_target[0], gain, ex)

    out_g, out_d, out_m, out_v = {}, {}, {}, {}
    updated = []

    def finish(grp, after):
        names, sums, from_chips = ex.finish_grads(grp, after)
        for n, s, r in zip(names, sums, from_chips, strict=True):
            shp = w_in_dev[n].shape
            res = adamw_shard("adamw_" + n, shards[n], _shard2d(m_in[n]), _shard2d(v_in[n]), s, r, chip)
            out_g[n], out_d[n], out_m[n], out_v[n] = [t.reshape(shp) for t in res]
            updated.append(res[1])
        return res[0]

    finish("A2", finish("B", finish("C", grad_x)))

    tail = jnp.concatenate([gsmall["pool_scale"], gsmall["q_a_norm"], gsmall["kv_a_norm"]], axis=1)
    small = jnp.concatenate([gsmall[n] for n in GAINS] + [tail, jnp.broadcast_to(loss_blk[:1, :1], (1, d)),
                                                         front[PAD_ROWS:]], axis=0)
    (small_g,) = all_gather("ag_small", [small], [LEAD], after=updated)
    total = reduce_small(small_g)
    ng = len(GAINS)
    for i, n in enumerate(GAINS):
        out_g[n] = total[i:i + 1]
    o = 0
    for n in ("pool_scale", "q_a_norm", "kv_a_norm"):
        wdt = w_in_dev[n].shape[1]
        out_g[n] = total[ng:ng + 1, o:o + wdt]
        o += wdt
    loss = total[ng + 1, 0]
    mcols = meta_tokens.shape[1]
    out_g["meta_tokens"] = lax.dynamic_slice(total[ng + 2:ng + 2 + N_META], (0, dev * mcols), (N_META, mcols))
    names = ["meta_tokens"] + SMALL
    ds_, ms_, vs_ = adamw_small([w_in_dev[n] for n in names], [out_g[n] for n in names],
                                [m_in[n] for n in names], [v_in[n] for n in names])
    for n, dd, mm, vv in zip(names, ds_, ms_, vs_, strict=True):
        out_d[n], out_m[n], out_v[n] = dd, mm, vv
    finish("A1", ds_[0])

    return (loss, grad_x[None], *[out_g[n] for n in WEIGHTS], *[out_d[n] for n in WEIGHTS],
            *[out_m[n] for n in WEIGHTS], *[out_v[n] for n in WEIGHTS])


GROUPS = {"A1": ["ffn1_w_gu"], "A2": ["ffn1_w_down"],
          "B": ["w_in", "pool_w", "w_pool_o", "w_q_b", "w_kv_b", "w_mla_o", "w_out"],
          "C": ["ffn2_w_gu", "ffn2_w_down"]}


class _Exchange:
    def __init__(self, shards, meta_tokens, dev, core):
        self.shards, self.meta_tokens, self.core = shards, meta_tokens, core
        self.dev1 = jnp.reshape(dev, (1,)).astype(jnp.int32)
        self.w, self.meta_full, self.token = {}, None, None
        self._ag, self._rs = {}, {}

    def _ag_place(self, grp, after):
        names = GROUPS[grp] + (["meta_tokens"] if grp == "A1" else [])
        srcs = [self.meta_tokens if n == "meta_tokens" else self.shards[n] for n in names]
        lays = [GATHERED.get(n, LEAD) for n in names]
        shapes = [a.shape for a in srcs]
        lands = [place_own(f"place_{n}", a, lay, F32 if n == "meta_tokens" else BF16, self.dev1, after)
                 for n, a, lay in zip(names, srcs, lays, strict=True)]
        self._ag[grp] = dict(names=names, lays=lays, shapes=shapes, lands=lands)

    def _ag_own_start(self, grp, after=None):
        st = self._ag[grp]
        plan, cnt = _ag_own_plan(st["shapes"], st["lays"])
        ss, rs, bufs, self.token = split_start(f"ag{grp}_own_start", st["lands"], plan, cnt, after)
        st["own"] = (ss, rs, bufs, plan)

    def _ag_pass(self, grp, after):
        st = self._ag[grp]
        ss, rs, bufs, plan = st["own"]
        lands = split_wait(f"ag{grp}_own_wait", bufs, ss, rs, plan, after)
        plan, cnt = _ag_pass_plan(st["shapes"], st["lays"])
        ss, rs, lands, self.token = split_start(f"ag{grp}_pass_start", lands, plan, cnt)
        st["pass"] = (ss, rs, lands, plan)

    def _ag_finish(self, grp, after):
        st = self._ag[grp]
        ss, rs, lands, plan = st["pass"]
        lands = split_wait(f"ag{grp}_pass_wait", lands, ss, rs, plan, after)
        for n, g, lay in zip(st["names"], lands, st["lays"]):
            if n == "meta_tokens":
                self.meta_full = g.transpose(1, 0, 2).reshape(N_META, N_DEV * g.shape[-1])
            else:
                self.w[n] = g if lay != LEAD else _to_internal(n, _full_from_gathered(n, g))

    def grads(self, grp, gbig, after=None):
        names = GROUPS[grp]
        lays = [GATHERED.get(n, LEAD) for n in names]
        grads = [gbig[n] if lay != LEAD else _blocks_from_full(n, _from_internal(n, gbig[n]).astype(BF16))
                 for n, lay in zip(names, lays, strict=True)]
        lands = [lax.empty((4, *_shard_shape(g, lay)), BF16) for g, lay in zip(grads, lays, strict=True)]
        plan, cnt = _rs_sibling_plan(lays, len(names))
        ss, rs, bufs, self.token = split_start(f"rs{grp}_sibling_start", grads + lands, plan, cnt, after)
        self._rs[grp] = dict(names=names, lays=lays, sib=(ss, rs, bufs, plan))
        return self.token

    def _rs_mid(self, grp, after):
        st = self._rs[grp]
        n = len(st["names"])
        ss, rs, bufs, plan = st["sib"]
        bufs = split_wait(f"rs{grp}_sibling_wait", bufs, ss, rs, plan, after)
        sums = [rs_add(f"rs_add_{name}", g, lay, r, self.core)
                for name, g, lay, r in zip(st["names"], bufs[:n], st["lays"], bufs[n:], strict=True)]
        lands = [lax.empty((3, *s.shape[1:]), BF16) for s in sums]
        plan, cnt = _rs_chips_plan(n)
        ss, rs, bufs, self.token = split_start(f"rs{grp}_chips_start", sums + lands, plan, cnt)
        st["chips"] = (ss, rs, bufs, plan)

    def finish_grads(self, grp, after):
        st = self._rs[grp]
        n = len(st["names"])
        ss, rs, bufs, plan = st["chips"]
        bufs = split_wait(f"rs{grp}_chips_wait", bufs, ss, rs, plan, after)
        return st["names"], bufs[:n], bufs[n:]

    def point(self, name, after=None):
        if name == "start":
            self._ag_place("A1", None)
            self._ag_own_start("A1")
            first = self.token
            self._ag_place("A2", first)
            self._ag_place("B", first)
            self._ag_pass("A1", self._ag["B"]["lands"][-1])
            self._ag_own_start("A2", self.token)
            self._ag_own_start("B", self.token)
            self._ag_finish("A1", self.token)
        elif name == "ffn1_gu_done":
            self._ag_pass("A2", after)
            self._ag_finish("A2", self.token)
        elif name == "ffn1_fwd_done":
            self._ag_pass("B", after)
            self._ag_place("C", self.token)
            self._ag_own_start("C", self.token)
        elif name == "mix_pre_done":
            self._ag_finish("B", after)
        elif name == "flash_fwd_done":
            self._ag_pass("C", after)
        elif name == "ffn2_pre_done":
            self._ag_finish("C", after)
        elif name.startswith("rs") and name.endswith("_mid"):
            self._rs_mid(name[2:-4], after)


def local_step(x, target, gain, ex):
    d = x.shape[-1]
    ex.point("start")
    w = ex.w
    h0 = jnp.concatenate([jnp.zeros((PAD_ROWS, d), F32), ex.meta_full, x], axis=0)
    lp = h0.shape[0]
    tabs = rope_tables(lp)
    a1 = prenorm(h0, gain["norm_ffn1_pre"], after=ex.token)
    gu1, s1 = ffn_gu("ffn1", a1, w["ffn1_w_gu"])
    ex.point("ffn1_gu_done", s1)
    f1 = ffn_down("ffn1", s1, w["ffn1_w_down"])
    ex.point("ffn1_fwd_done", f1)
    h1, a2 = post_pre("post_pre1", f1, h0, gain["norm_ffn1_post"], 0.5, gain["norm_mix_pre"], after=ex.token)
    ex.point("mix_pre_done", a2)
    mix, sv = mixer_fwd(a2, w, tabs, gain["pool_scale"], gain["q_a_norm"], gain["kv_a_norm"], ex)
    h2, a3 = post_pre("post_pre2", mix, h1, gain["norm_mix_post"], 1.0, gain["norm_ffn2_pre"])
    ex.point("ffn2_pre_done", a3)
    gu2, s2 = ffn_gu("ffn2", a3, w["ffn2_w_gu"])
    f2 = ffn_down("ffn2", s2, w["ffn2_w_down"])
    dh3, loss_blk = post_loss(f2, h2, gain["norm_ffn2_post"], 0.5, target)

    gsmall = {}
    df2, gsmall["norm_ffn2_post"] = post_bwd(dh3, f2, gain["norm_ffn2_post"], 0.5)
    da3 = ffn_bwd("ffn2", df2, a3, gu2, s2, w["ffn2_w_gu"], w["ffn2_w_down"], lambda dwd: None,
                  lambda dwgu, dwd: ex.grads("C", {"ffn2_w_gu": dwgu, "ffn2_w_down": dwd}))
    dh2, dmix, gsmall["norm_ffn2_pre"], gsmall["norm_mix_post"] = pre_post_bwd(
        "pre_post_bwd2", da3, h2, gain["norm_ffn2_pre"], dh3, mix, gain["norm_mix_post"], 1.0)
    ex.point("rsC_mid", dmix)
    da2, gmix, gmix_small = mixer_bwd(dmix, a2, sv, w, tabs, gain["pool_scale"], gain["q_a_norm"], gain["kv_a_norm"],
                                      after=ex.token)
    gsmall.update(gmix_small)
    dh1, df1, gsmall["norm_mix_pre"], gsmall["norm_ffn1_post"] = pre_post_bwd(
        "pre_post_bwd1", da2, h1, gain["norm_mix_pre"], dh2, f1, gain["norm_ffn1_post"], 0.5,
        after=ex.grads("B", gmix))
    ex.point("rsB_mid", df1)
    def on_dwgu1(dwgu, dwd):
        ex.point("rsA2_mid", dwgu)
        return ex.grads("A1", {"ffn1_w_gu": dwgu}, after=ex.token)

    da1 = ffn_bwd("ffn1", df1, a1, gu1, s1, w["ffn1_w_gu"], w["ffn1_w_down"],
                  lambda dwd: ex.grads("A2", {"ffn1_w_down": dwd}), on_dwgu1, after=ex.token)
    ex.point("rsA1_mid", da1)
    grad_x, front, gsmall["norm_ffn1_pre"] = pre_bwd_first(da1, h0, gain["norm_ffn1_pre"], dh1, after=ex.token)
    return loss_blk, grad_x, front, gsmall
```
